```python
import functools
import jax, jax.numpy as jnp
from jax import lax
import numpy as np

D_MODEL = 1024
BATCH = 8
SEQ = 2048
DEPTH = 1
DEC_BATCH = 128
DEC_SEQ = 8
PAST_LEN = 8192
PAGE_SIZE = 128

N_Q_HEADS = 8
N_KV_HEADS = 2
HEAD_DIM = 64
GQA_GROUP = N_Q_HEADS // N_KV_HEADS
WINDOW = 128
ATTN_BLOCK = WINDOW
ROPE_THETA = 10000.0
N_RET_HEADS = 4
RET_KEY_DIM = 128
RET_VAL_DIM = 256
RET_CHUNK = 128
RET_THETA = 10000.0
D_FF = ((8 * D_MODEL // 3 + 127) // 128) * 128
DEEPNORM_ALPHA = (2.0 * DEPTH) ** 0.25
DEEPNORM_BETA = (8.0 * DEPTH) ** -0.25
LN_EPS = 1e-5
GN_EPS = 1e-6
NEG_INF = -1e30
Q_W = N_Q_HEADS * HEAD_DIM
KV_W = N_KV_HEADS * HEAD_DIM
RQ_W = N_RET_HEADS * RET_KEY_DIM
RV_W = N_RET_HEADS * RET_VAL_DIM
IN_SPLITS = (Q_W, KV_W, KV_W, RQ_W, RQ_W, RV_W, RV_W, D_MODEL, D_MODEL)
VALUE_GROUPS = (2, 5)
N_IN = sum(IN_SPLITS)

kernel_name = 'hybrid_swa_sink_retention_macaron_deepnorm_step'


def layer_norm(x, g, b):
    xf = x.astype(jnp.float32)
    mu = jnp.mean(xf, axis=-1, keepdims=True)
    xc = xf - mu
    var = jnp.mean(xc * xc, axis=-1, keepdims=True)
    return (xc * lax.rsqrt(var + LN_EPS) * g.astype(jnp.float32) + b.astype(jnp.float32)).astype(x.dtype)


def group_norm(o):
    mu = jnp.mean(o, axis=-1, keepdims=True)
    oc = o - mu
    var = jnp.mean(oc * oc, axis=-1, keepdims=True)
    return oc * lax.rsqrt(var + GN_EPS)


def swiglu(x, wi, wo):
    g, u = jnp.split(x @ wi, 2, axis=-1)
    return (jax.nn.silu(g) * u) @ wo


def attn_inv_freq():
    return 1.0 / (ROPE_THETA ** (jnp.arange(0, HEAD_DIM, 2, dtype=jnp.float32) / HEAD_DIM))


def ret_inv_freq():
    return 1.0 / (RET_THETA ** jnp.linspace(0.0, 1.0, RET_KEY_DIM // 2, dtype=jnp.float32))


def rope(x, pos, inv_freq):
    ang = pos.astype(jnp.float32)[:, None] * inv_freq[None, :]
    c = jnp.cos(ang)[:, None, :]
    s = jnp.sin(ang)[:, None, :]
    xf = x.astype(jnp.float32)
    x1, x2 = jnp.split(xf, 2, axis=-1)
    return jnp.concatenate([x1 * c - x2 * s, x2 * c + x1 * s], axis=-1).astype(x.dtype)


def window_mask(qpos, kpos):
    return (kpos <= qpos) & (qpos - kpos < WINDOW) & (kpos >= 0)


def sink_softmax(s, mask, sinks):
    s = jnp.where(mask, s, NEG_INF)
    sink = jnp.broadcast_to(sinks.astype(jnp.float32).reshape(N_KV_HEADS, GQA_GROUP, 1, 1), s.shape[:-1] + (1,))
    p = jax.nn.softmax(jnp.concatenate([s, sink], axis=-1), axis=-1)
    return p[..., :-1]


def swa_prompt(q, k, v, sinks):
    B, S = q.shape[0], q.shape[1]
    L = ATTN_BLOCK
    nb = S // L
    qb = q.reshape(B, nb, L, N_KV_HEADS, GQA_GROUP, HEAD_DIM)
    kb = k.reshape(B, nb, L, N_KV_HEADS, HEAD_DIM)
    vb = v.reshape(B, nb, L, N_KV_HEADS, HEAD_DIM)
    pad = ((0, 0), (1, 0), (0, 0), (0, 0), (0, 0))
    k_band = jnp.concatenate([jnp.pad(kb[:, :-1], pad), kb], axis=2)
    v_band = jnp.concatenate([jnp.pad(vb[:, :-1], pad), vb], axis=2)
    s = jnp.einsum('bnqkgd,bnmkd->bnkgqm', qb, k_band).astype(jnp.float32) * (HEAD_DIM ** -0.5)
    blk = jnp.arange(nb, dtype=jnp.int32)[:, None] * L
    qpos = blk + jnp.arange(L, dtype=jnp.int32)[None, :]
    kpos = blk - L + jnp.arange(2 * L, dtype=jnp.int32)[None, :]
    mask = window_mask(qpos[:, :, None], kpos[:, None, :])[None, :, None, None]
    p = sink_softmax(s, mask, sinks)
    o = jnp.einsum('bnkgqm,bnmkd->bnqkgd', p.astype(v.dtype), v_band).reshape(B, S, Q_W)
    w = min(WINDOW, S)
    return o, k[:, S - w:], v[:, S - w:]


def swa_sample(q, k, v, sinks, cache_k, cache_v):
    B, T = q.shape[0], q.shape[1]
    W = cache_k.shape[1]
    k_all = jnp.concatenate([cache_k.astype(k.dtype), k], axis=1)
    v_all = jnp.concatenate([cache_v.astype(v.dtype), v], axis=1)
    qpos = PAST_LEN + jnp.arange(T, dtype=jnp.int32)
    kpos = jnp.concatenate([PAST_LEN - W + jnp.arange(W, dtype=jnp.int32), qpos])
    qg = q.reshape(B, T, N_KV_HEADS, GQA_GROUP, HEAD_DIM)
    s = jnp.einsum('bqkgd,bmkd->bkgqm', qg, k_all).astype(jnp.float32) * (HEAD_DIM ** -0.5)
    mask = window_mask(qpos[:, None], kpos[None, :])
    p = sink_softmax(s, mask, sinks)
    o = jnp.einsum('bkgqm,bmkd->bqkgd', p.astype(v.dtype), v_all).reshape(B, T, Q_W)
    return o, k_all[:, -W:], v_all[:, -W:]


def retention(q, k, v, state0, chunk):
    B, S, H, DK = q.shape
    DV = v.shape[-1]
    nc = S // chunk
    f32 = jnp.float32
    to_chunks = lambda t: t.astype(f32).reshape(B, nc, chunk, H, t.shape[-1]).transpose(1, 0, 3, 2, 4)
    qc, kc, vc = to_chunks(q), to_chunks(k), to_chunks(v)
    log_g = jnp.log(1.0 - 2.0 ** (-5.0 - jnp.arange(H, dtype=f32)))
    i = jnp.arange(chunk, dtype=f32)
    causal = i[:, None] >= i[None, :]
    diff = jnp.where(causal, i[:, None] - i[None, :], 0.0)
    decay_mask = jnp.exp(diff[None] * log_g[:, None, None]) * causal[None]
    q_decay = jnp.exp((i[None, :] + 1.0) * log_g[:, None])[..., None]
    k_decay = jnp.exp((chunk - 1.0 - i[None, :]) * log_g[:, None])[..., None]
    chunk_decay = jnp.exp(chunk * log_g)[:, None, None]

    def step(R, inp):
        qi, ki, vi = inp
        inner = jnp.einsum('bhid,bhjd->bhij', qi, ki) * decay_mask
        o = jnp.einsum('bhij,bhje->bhie', inner, vi) + jnp.einsum('bhid,bhde->bhie', qi * q_decay, R)
        R = R * chunk_decay + jnp.einsum('bhjd,bhje->bhde', ki * k_decay, vi)
        return R, o

    R, o = lax.scan(step, state0.astype(f32), (qc, kc, vc))
    o = o.transpose(1, 0, 3, 2, 4).reshape(B, S, H, DV)
    return o, R


def trunk_layer(x, pos, attn_fn, ret_state0, ret_chunk, ln_g, ln_b, ffn_wi, ffn_wo, w_in, sinks, w_br_attn, w_br_ret, w_o):
    B, S, _ = x.shape
    x = layer_norm(DEEPNORM_ALPHA * x + 0.5 * swiglu(x, ffn_wi[0], ffn_wo[0]), ln_g[0], ln_b[0])
    offs = [int(o) for o in np.cumsum(IN_SPLITS)[:-1]]
    q_a, k_a, v_a, q_r, k_r, v_r, g_r, gate_a, gate_b = jnp.split(x @ w_in, offs, axis=-1)
    fa = attn_inv_freq()
    q_a = rope(q_a.reshape(B, S, N_Q_HEADS, HEAD_DIM), pos, fa)
    k_a = rope(k_a.reshape(B, S, N_KV_HEADS, HEAD_DIM), pos, fa)
    v_a = v_a.reshape(B, S, N_KV_HEADS, HEAD_DIM)
    o_a, new_k, new_v = attn_fn(q_a, k_a, v_a, sinks)
    fr = ret_inv_freq()
    q_r = rope(q_r.reshape(B, S, N_RET_HEADS, RET_KEY_DIM), pos, fr)
    k_r = rope(k_r.reshape(B, S, N_RET_HEADS, RET_KEY_DIM), pos, fr) * (RET_KEY_DIM ** -0.5)
    v_r = v_r.reshape(B, S, N_RET_HEADS, RET_VAL_DIM)
    o_r, new_state = retention(q_r, k_r, v_r, ret_state0, ret_chunk)
    o_r = group_norm(o_r).reshape(B, S, RV_W).astype(x.dtype) * jax.nn.silu(g_r)
    merged = jax.nn.sigmoid(gate_a) * (o_a @ w_br_attn) + jax.nn.sigmoid(gate_b) * (o_r @ w_br_ret)
    x = layer_norm(DEEPNORM_ALPHA * x + merged @ w_o, ln_g[1], ln_b[1])
    x = layer_norm(DEEPNORM_ALPHA * x + 0.5 * swiglu(x, ffn_wi[1], ffn_wo[1]), ln_g[2], ln_b[2])
    return x, new_k, new_v, new_state


def setup_inputs(seed: int = 0) -> dict:
    key = jax.random.key(seed)
    ks = jax.random.split(key, 16)
    f32 = jnp.float32
    W = min(WINDOW, PAST_LEN)
    x_prompt = jax.random.normal(ks[0], (BATCH, SEQ, D_MODEL), f32)
    x_sample = jax.random.normal(ks[1], (DEC_BATCH, DEC_SEQ, D_MODEL), f32)
    cache_k = jax.random.normal(ks[2], (DEPTH, DEC_BATCH, W, N_KV_HEADS, HEAD_DIM), f32)
    cache_v = jax.random.normal(ks[3], (DEPTH, DEC_BATCH, W, N_KV_HEADS, HEAD_DIM), f32) * DEEPNORM_BETA
    state_ret = jax.random.normal(ks[4], (DEPTH, DEC_BATCH, N_RET_HEADS, RET_KEY_DIM, RET_VAL_DIM), f32) * 0.5
    ln_g = 1.0 + 0.02 * jax.random.normal(ks[5], (DEPTH, 3, D_MODEL), f32)
    ln_b = 0.02 * jax.random.normal(ks[6], (DEPTH, 3, D_MODEL), f32)
    ffn_wi = jax.random.normal(ks[7], (DEPTH, 2, D_MODEL, 2 * D_FF), f32) * (D_MODEL ** -0.5)
    ffn_wo = jax.random.normal(ks[8], (DEPTH, 2, D_FF, D_MODEL), f32) * (D_FF ** -0.5) * DEEPNORM_BETA
    col_scale = jnp.concatenate([jnp.full((w,), DEEPNORM_BETA if gi in VALUE_GROUPS else 1.0, f32) for gi, w in enumerate(IN_SPLITS)])
    w_in = jax.random.normal(ks[9], (DEPTH, D_MODEL, N_IN), f32) * (D_MODEL ** -0.5) * col_scale
    attn_sinks = jax.random.normal(ks[10], (DEPTH, N_Q_HEADS), f32)
    w_br_attn = jax.random.normal(ks[11], (DEPTH, Q_W, D_MODEL), f32) * (Q_W ** -0.5)
    w_br_ret = jax.random.normal(ks[12], (DEPTH, RV_W, D_MODEL), f32) * (RV_W ** -0.5)
    w_o = jax.random.normal(ks[13], (DEPTH, D_MODEL, D_MODEL), f32) * (D_MODEL ** -0.5) * DEEPNORM_BETA
    return {'x_prompt': x_prompt, 'x_sample': x_sample, 'cache_k': cache_k, 'cache_v': cache_v,
            'state_ret': state_ret, 'ln_g': ln_g, 'ln_b': ln_b, 'ffn_wi': ffn_wi, 'ffn_wo': ffn_wo,
            'w_in': w_in, 'attn_sinks': attn_sinks, 'w_br_attn': w_br_attn, 'w_br_ret': w_br_ret, 'w_o': w_o}


def reference(x_prompt, x_sample, cache_k, cache_v, state_ret, ln_g, ln_b, ffn_wi, ffn_wo, w_in, attn_sinks, w_br_attn, w_br_ret, w_o):
    B, S, _ = x_prompt.shape
    T = x_sample.shape[1]
    pos_p = jnp.arange(S, dtype=jnp.int32)
    pos_s = PAST_LEN + jnp.arange(T, dtype=jnp.int32)
    y_p, y_s = x_prompt, x_sample
    kp_l, vp_l, rp_l, ks_l, vs_l, rs_l = [], [], [], [], [], []
    for l in range(DEPTH):
        lw = (ln_g[l], ln_b[l], ffn_wi[l], ffn_wo[l], w_in[l], attn_sinks[l], w_br_attn[l], w_br_ret[l], w_o[l])
        zero_state = jnp.zeros((B, N_RET_HEADS, RET_KEY_DIM, RET_VAL_DIM), jnp.float32)
        y_p, kp, vp, rp = trunk_layer(y_p, pos_p, swa_prompt, zero_state, min(RET_CHUNK, S), *lw)
        sample_attn = functools.partial(swa_sample, cache_k=cache_k[l], cache_v=cache_v[l])
        y_s, ks_, vs_, rs_ = trunk_layer(y_s, pos_s, sample_attn, state_ret[l], T, *lw)
        kp_l.append(kp); vp_l.append(vp); rp_l.append(rp)
        ks_l.append(ks_); vs_l.append(vs_); rs_l.append(rs_.astype(state_ret.dtype))
    return (y_p, y_s, jnp.stack(kp_l), jnp.stack(vp_l), jnp.stack(rp_l), jnp.stack(ks_l), jnp.stack(vs_l), jnp.stack(rs_l))
```

```python
import functools
import math

import jax
import jax.numpy as jnp
from jax import lax
from jax.experimental import pallas as pl
from jax.experimental.pallas import tpu as pltpu

F32 = jnp.float32
BF16 = jnp.bfloat16

PAST_LEN = 8192
N_Q_HEADS = 8
N_KV_HEADS = 2
HEAD_DIM = 64
GQA_GROUP = N_Q_HEADS // N_KV_HEADS
WINDOW = 128
ROPE_THETA = 10000.0
N_RET_HEADS = 4
RET_KEY_DIM = 128
RET_VAL_DIM = 256
RET_CHUNK = 128
RET_THETA = 10000.0
LN_EPS = 1e-5
GN_EPS = 1e-6
NEG_INF = -1e30

Q_W = N_Q_HEADS * HEAD_DIM
KV_W = N_KV_HEADS * HEAD_DIM
RQ_W = N_RET_HEADS * RET_KEY_DIM
RV_W = N_RET_HEADS * RET_VAL_DIM

LANES = 128
VMEM_LIMIT_BYTES = 56 * 1024 * 1024

TOKEN_TILE = 512
FF_CHUNK = 256
ATTN_Q_TILE = 512
RET_TILE = 512
SAMPLE_BT = 8


def _dot(a, b):
    return jnp.dot(a, b, preferred_element_type=F32)


def _dot_nt(a, b):
    return lax.dot_general(a, b, (((1,), (1,)), ((), ())), preferred_element_type=F32)


def _layer_norm(y, g, b):
    mu = jnp.mean(y, axis=-1, keepdims=True)
    yc = y - mu
    var = jnp.mean(yc * yc, axis=-1, keepdims=True)
    return yc * lax.rsqrt(var + LN_EPS) * g + b


def _cparams(n_axes, semantics=None):
    return pltpu.CompilerParams(
        dimension_semantics=semantics or ("arbitrary",) * n_axes,
        vmem_limit_bytes=VMEM_LIMIT_BYTES)


def _const_spec(shape):
    nd = len(shape)
    return pl.BlockSpec(shape, lambda *_: (0,) * nd, pipeline_mode=pl.Buffered(1))


def _ffn_ln_kernel(alpha, d_ff, x_ref, wi_ref, wo_ref, g_ref, b_ref, o_ref, acc_ref):
    x = x_ref[...]
    xb = x.astype(BF16)
    for c in range(d_ff // FF_CHUNK):
        lo = c * FF_CHUNK
        gate = _dot(xb, wi_ref[:, lo:lo + FF_CHUNK])
        up = _dot(xb, wi_ref[:, d_ff + lo:d_ff + lo + FF_CHUNK])
        act = (jax.nn.silu(gate) * up).astype(BF16)
        part = _dot(act, wo_ref[lo:lo + FF_CHUNK, :])
        if c == 0:
            acc_ref[...] = part
        else:
            acc_ref[...] += part
    y = alpha * x + 0.5 * acc_ref[...]
    o_ref[...] = _layer_norm(y, g_ref[...], b_ref[...])


def _ffn_ln(x, wi, wo, g, b, alpha):
    t, d = x.shape
    d_ff = wo.shape[0]
    assert t % TOKEN_TILE == 0 and d_ff % FF_CHUNK == 0
    return pl.pallas_call(
        functools.partial(_ffn_ln_kernel, alpha, d_ff),
        grid=(t // TOKEN_TILE,),
        in_specs=[pl.BlockSpec((TOKEN_TILE, d), lambda i: (i, 0)),
                  _const_spec(wi.shape), _const_spec(wo.shape),
                  _const_spec((1, d)), _const_spec((1, d))],
        out_specs=pl.BlockSpec((TOKEN_TILE, d), lambda i: (i, 0)),
        out_shape=jax.ShapeDtypeStruct((t, d), F32),
        scratch_shapes=[pltpu.VMEM((TOKEN_TILE, d), F32)],
        compiler_params=_cparams(1),
        name="ffn_ln",
    )(x, wi, wo, g.reshape(1, d), b.reshape(1, d))


def _rope_tables(pos, dim, theta_fn):
    half = dim // 2
    inv_freq = theta_fn(half)
    ang = pos.astype(F32)[:, None] * inv_freq[None, :]
    reps = LANES // half
    cos = jnp.tile(jnp.cos(ang), (1, reps))
    sign = jnp.tile(jnp.concatenate([-jnp.ones((half,), F32), jnp.ones((half,), F32)]), LANES // dim)
    sin = jnp.tile(jnp.sin(ang), (1, reps)) * sign[None, :]
    return cos, sin


def _attn_inv_freq(half):
    return 1.0 / (ROPE_THETA ** (jnp.arange(0, 2 * half, 2, dtype=F32) / (2 * half)))


def _ret_inv_freq(half):
    return 1.0 / (RET_THETA ** jnp.linspace(0.0, 1.0, half, dtype=F32))


def _rope_block(x, cos, sin, dim):
    half = dim // 2
    if dim == LANES:
        rot = pltpu.roll(x, half, 1)
    else:
        lane = lax.broadcasted_iota(jnp.int32, x.shape, 1)
        first_half = (lane % dim) < half
        rot = jnp.where(first_half, pltpu.roll(x, LANES - half, 1), pltpu.roll(x, half, 1))
    return x * cos + rot * sin


def _proj_rope_kernel(x_ref, w_ref, cosa_ref, sina_ref, cosr_ref, sinr_ref,
                      qa_ref, ka_ref, va_ref, qr_ref, kr_ref, vr_ref, gr_ref, ga_ref, gb_ref):
    xb = x_ref[...].astype(BF16)
    cosa, sina = cosa_ref[...], sina_ref[...]
    cosr, sinr = cosr_ref[...], sinr_ref[...]
    off = 0

    y = _dot(xb, w_ref[:, off:off + Q_W])
    for j in range(Q_W // LANES):
        qa_ref[:, j * LANES:(j + 1) * LANES] = _rope_block(
            y[:, j * LANES:(j + 1) * LANES], cosa, sina, HEAD_DIM).astype(qa_ref.dtype)
    off += Q_W
    y = _dot(xb, w_ref[:, off:off + 2 * KV_W])
    ka_ref[...] = _rope_block(y[:, :KV_W], cosa, sina, HEAD_DIM)
    va_ref[...] = y[:, KV_W:]
    off += 2 * KV_W
    y = _dot(xb, w_ref[:, off:off + RQ_W])
    for j in range(RQ_W // LANES):
        qr_ref[:, j * LANES:(j + 1) * LANES] = _rope_block(
            y[:, j * LANES:(j + 1) * LANES], cosr, sinr, RET_KEY_DIM).astype(qr_ref.dtype)
    off += RQ_W
    y = _dot(xb, w_ref[:, off:off + RQ_W])
    for j in range(RQ_W // LANES):
        kr = _rope_block(y[:, j * LANES:(j + 1) * LANES], cosr, sinr, RET_KEY_DIM)
        kr_ref[:, j * LANES:(j + 1) * LANES] = (kr * (RET_KEY_DIM ** -0.5)).astype(kr_ref.dtype)
    off += RQ_W
    for ref in (vr_ref, gr_ref, ga_ref, gb_ref):
        width = ref.shape[1]
        for c in range(width // 512):
            ref[:, c * 512:(c + 1) * 512] = _dot(
                xb, w_ref[:, off + c * 512:off + (c + 1) * 512]).astype(ref.dtype)
        off += width


def _proj_rope(x1, w_in, tables, table_blocks):
    t, d = x1.shape
    n_in = w_in.shape[1]
    d_model = (n_in - Q_W - 2 * KV_W - 2 * RQ_W - 2 * RV_W) // 2
    tm = TOKEN_TILE
    row = lambda i: (i, 0)
    tab = lambda i: (i % table_blocks, 0)
    widths = [(Q_W, BF16), (KV_W, F32), (KV_W, F32), (RQ_W, BF16), (RQ_W, BF16),
              (RV_W, BF16), (RV_W, BF16), (d_model, BF16), (d_model, BF16)]
    return pl.pallas_call(
        _proj_rope_kernel,
        grid=(t // tm,),
        in_specs=[pl.BlockSpec((tm, d), row), _const_spec(w_in.shape)]
                 + [pl.BlockSpec((tm, LANES), tab)] * 4,
        out_specs=[pl.BlockSpec((tm, w), row) for w, _ in widths],
        out_shape=[jax.ShapeDtypeStruct((t, w), dt) for w, dt in widths],
        compiler_params=_cparams(1),
        name="proj_rope",
    )(x1, w_in, *tables)


def _kv_lane_variants(x):
    lane = lax.broadcasted_iota(jnp.int32, x.shape, 1)
    lo = lane < HEAD_DIM
    swapped = pltpu.roll(x, HEAD_DIM, 1)
    zero = jnp.zeros_like(x)
    kv0 = (jnp.where(lo, x, zero).astype(BF16), jnp.where(lo, zero, swapped).astype(BF16))
    kv1 = (jnp.where(lo, swapped, zero).astype(BF16), jnp.where(lo, zero, x).astype(BF16))
    return kv0, kv1


def _attn_prompt_kernel(sinks_ref, q_ref, kc_ref, kp_ref, vc_ref, vp_ref, o_ref):
    n = pl.program_id(1)
    L = WINDOW
    k_full = jnp.concatenate([kp_ref[...], kc_ref[...]], axis=0)
    v_full = jnp.concatenate([vp_ref[...], vc_ref[...]], axis=0)
    k_var = _kv_lane_variants(k_full)
    v_var = _kv_lane_variants(v_full)
    qi = lax.broadcasted_iota(jnp.int32, (L, 2 * L), 0)
    mi = lax.broadcasted_iota(jnp.int32, (L, 2 * L), 1)
    band = (mi > qi) & (mi <= qi + L)
    scale = HEAD_DIM ** -0.5
    for i in range(q_ref.shape[0] // L):
        mask = band
        if i == 0:
            mask = band & ((mi >= L) | (n > 0))
        rows = slice(i * L, (i + 1) * L)
        keys = slice(i * L, i * L + 2 * L)
        for j in range(Q_W // LANES):
            kv = (2 * j) // GQA_GROUP
            q2 = q_ref[rows, j * LANES:(j + 1) * LANES] * scale
            out = None
            for par in range(2):
                s = _dot_nt(q2, k_var[kv][par][keys])
                s = jnp.where(mask, s, NEG_INF)
                sink = sinks_ref[2 * j + par]
                m = jnp.maximum(jnp.max(s, axis=-1, keepdims=True), sink)
                e = jnp.exp(s - m)
                denom = jnp.sum(e, axis=-1, keepdims=True) + jnp.exp(sink - m)
                r = _dot(e.astype(BF16), v_var[kv][par][keys]) / denom
                out = r if out is None else out + r
            o_ref[rows, j * LANES:(j + 1) * LANES] = out.astype(o_ref.dtype)


def _attn_prompt(qa, ka, va, sinks, batch, seq):
    tq = ATTN_Q_TILE
    nq = seq // tq
    per = tq // WINDOW
    cur = lambda b, n: (b * nq + n, 0)
    prev = lambda b, n: ((b * nq + n) * per - jnp.minimum(n, 1), 0)
    return pl.pallas_call(
        _attn_prompt_kernel,
        grid=(batch, nq),
        in_specs=[pl.BlockSpec(memory_space=pltpu.SMEM),
                  pl.BlockSpec((tq, Q_W), cur),
                  pl.BlockSpec((tq, KV_W), cur), pl.BlockSpec((WINDOW, KV_W), prev),
                  pl.BlockSpec((tq, KV_W), cur), pl.BlockSpec((WINDOW, KV_W), prev)],
        out_specs=pl.BlockSpec((tq, Q_W), cur),
        out_shape=jax.ShapeDtypeStruct(qa.shape, BF16),
        compiler_params=_cparams(2),
        name="attn_prompt",
    )(sinks, qa, ka, ka, va, va)


def _attn_sample_kernel(sink_rows_ref, q_ref, kn_ref, vn_ref, ck_ref, cv_ref,
                        o_ref, nk_ref, nv_ref):
    bt, t_new, _ = q_ref.shape
    w = ck_ref.shape[1]
    n_rows = N_Q_HEADS * t_new
    lane = lax.broadcasted_iota(jnp.int32, (t_new, LANES), 1)
    lane_lo = lane < HEAD_DIM
    r_i = lax.broadcasted_iota(jnp.int32, (n_rows, w + t_new), 0) % t_new
    m_i = lax.broadcasted_iota(jnp.int32, (n_rows, w + t_new), 1)
    mask = ((m_i < w) & (m_i > r_i + (w - WINDOW))) | ((m_i >= w) & (m_i - w <= r_i))
    sink = sink_rows_ref[...]
    scale = HEAD_DIM ** -0.5
    for b in range(bt):
        kc, kn = ck_ref[b], kn_ref[b]
        vc, vn = cv_ref[b], vn_ref[b]
        k_all = jnp.concatenate([kc, kn], axis=0).astype(BF16)
        v_all = jnp.concatenate([vc, vn], axis=0).astype(BF16)
        q = q_ref[b].astype(F32)
        pieces = []
        for h in range(N_Q_HEADS):
            q2 = q[:, (h // 2) * LANES:(h // 2 + 1) * LANES]
            src_lo = h % 2 == 0
            dst_lo = h // GQA_GROUP == 0
            if src_lo != dst_lo:
                q2 = pltpu.roll(q2, HEAD_DIM, 1)
            pieces.append(jnp.where(lane_lo == dst_lo, q2, 0.0))
        q_rows = (jnp.concatenate(pieces, axis=0) * scale).astype(BF16)
        s = _dot_nt(q_rows, k_all)
        s = jnp.where(mask, s, NEG_INF)
        m = jnp.maximum(jnp.max(s, axis=-1, keepdims=True), sink)
        e = jnp.exp(s - m)
        denom = jnp.sum(e, axis=-1, keepdims=True) + jnp.exp(sink - m)
        pv = _dot(e.astype(BF16), v_all) / denom
        for j in range(Q_W // LANES):
            out = None
            for par in range(2):
                h = 2 * j + par
                dst_lo = h // GQA_GROUP == 0
                piece = jnp.where(lane_lo == dst_lo, pv[h * t_new:(h + 1) * t_new], 0.0)
                if dst_lo != (par == 0):
                    piece = pltpu.roll(piece, HEAD_DIM, 1)
                out = piece if out is None else out + piece
            o_ref[b, :, j * LANES:(j + 1) * LANES] = out.astype(o_ref.dtype)
        nk_ref[b, :w - t_new, :] = kc[t_new:, :]
        nk_ref[b, w - t_new:, :] = kn
        nv_ref[b, :w - t_new, :] = vc[t_new:, :]
        nv_ref[b, w - t_new:, :] = vn


def _attn_sample(qa, ka, va, sinks, cache_k, cache_v):
    db, t_new, _ = qa.shape
    w = cache_k.shape[1]
    bt = SAMPLE_BT
    sink_rows = jnp.repeat(sinks.astype(F32), t_new).reshape(N_Q_HEADS * t_new, 1)
    blk = lambda *s: pl.BlockSpec((bt,) + s, lambda i: (i, 0, 0))
    return pl.pallas_call(
        _attn_sample_kernel,
        grid=(db // bt,),
        in_specs=[_const_spec(sink_rows.shape), blk(t_new, Q_W), blk(t_new, KV_W), blk(t_new, KV_W),
                  blk(w, KV_W), blk(w, KV_W)],
        out_specs=[blk(t_new, Q_W), blk(w, KV_W), blk(w, KV_W)],
        out_shape=[jax.ShapeDtypeStruct((db, t_new, Q_W), BF16),
                   jax.ShapeDtypeStruct((db, w, KV_W), F32),
                   jax.ShapeDtypeStruct((db, w, KV_W), F32)],
        compiler_params=_cparams(1),
        name="attn_sample",
    )(sink_rows, qa, ka, va, cache_k, cache_v)


def _ret_log_gamma(h):
    return math.log(1.0 - 2.0 ** (-5.0 - h))


def _ret_decays(chunk, h):
    lg = _ret_log_gamma(h)
    i = lax.broadcasted_iota(jnp.int32, (chunk, chunk), 0)
    j = lax.broadcasted_iota(jnp.int32, (chunk, chunk), 1)
    causal = i >= j
    diff = jnp.where(causal, i - j, 0).astype(F32)
    decay_mask = jnp.where(causal, jnp.exp(diff * lg), 0.0)
    col = lax.broadcasted_iota(jnp.int32, (chunk, 1), 0).astype(F32)
    q_decay = jnp.exp((col + 1.0) * lg)
    k_decay = jnp.exp((chunk - 1.0 - col) * lg)
    return decay_mask, q_decay, k_decay, math.exp(chunk * lg)


def _ret_chunk(q, k, v, g, state, decays):
    decay_mask, q_decay, k_decay, chunk_decay = decays
    inner = _dot_nt(q, k) * decay_mask
    o = _dot(inner.astype(BF16), v) + _dot(q, state.astype(BF16)) * q_decay
    kd_t = (k.astype(F32) * k_decay).T.astype(BF16)
    new_state = state * chunk_decay + _dot(kd_t, v)
    mu = jnp.mean(o, axis=-1, keepdims=True)
    oc = o - mu
    var = jnp.mean(oc * oc, axis=-1, keepdims=True)
    normed = oc * lax.rsqrt(var + GN_EPS)
    return normed * jax.nn.silu(g.astype(F32)), new_state


def _ret_prompt_kernel(q_ref, k_ref, v_ref, g_ref, o_ref, st_ref, state_ref):
    n = pl.program_id(1)
    C = RET_CHUNK

    @pl.when(n == 0)
    def _():
        state_ref[...] = jnp.zeros_like(state_ref)

    for h in range(N_RET_HEADS):
        decays = _ret_decays(C, h)
        state = state_ref[h]
        kcols = slice(h * RET_KEY_DIM, (h + 1) * RET_KEY_DIM)
        vcols = slice(h * RET_VAL_DIM, (h + 1) * RET_VAL_DIM)
        for c in range(q_ref.shape[0] // C):
            rows = slice(c * C, (c + 1) * C)
            out, state = _ret_chunk(q_ref[rows, kcols], k_ref[rows, kcols], v_ref[rows, vcols],
                                    g_ref[rows, vcols], state, decays)
            o_ref[rows, vcols] = out.astype(o_ref.dtype)
        state_ref[h] = state

    @pl.when(n == pl.num_programs(1) - 1)
    def _():
        st_ref[0] = state_ref[...]


def _ret_prompt(qr, kr, vr, gr, batch, seq):
    ts = RET_TILE
    ns = seq // ts
    row = lambda b, n: (b * ns + n, 0)
    return pl.pallas_call(
        _ret_prompt_kernel,
        grid=(batch, ns),
        in_specs=[pl.BlockSpec((ts, RQ_W), row), pl.BlockSpec((ts, RQ_W), row),
                  pl.BlockSpec((ts, RV_W), row), pl.BlockSpec((ts, RV_W), row)],
        out_specs=[pl.BlockSpec((ts, RV_W), row),
                   pl.BlockSpec((1, N_RET_HEADS, RET_KEY_DIM, RET_VAL_DIM), lambda b, n: (b, 0, 0, 0))],
        out_shape=[jax.ShapeDtypeStruct((batch * seq, RV_W), BF16),
                   jax.ShapeDtypeStruct((batch, N_RET_HEADS, RET_KEY_DIM, RET_VAL_DIM), F32)],
        scratch_shapes=[pltpu.VMEM((N_RET_HEADS, RET_KEY_DIM, RET_VAL_DIM), F32)],
        compiler_params=_cparams(2),
        name="ret_prompt",
    )(qr, kr, vr, gr)


def _ret_sample_kernel(q_ref, k_ref, v_ref, g_ref, st_ref, o_ref, nst_ref):
    bt, t_new, _ = q_ref.shape
    for h in range(N_RET_HEADS):
        decays = _ret_decays(t_new, h)
        kcols = slice(h * RET_KEY_DIM, (h + 1) * RET_KEY_DIM)
        vcols = slice(h * RET_VAL_DIM, (h + 1) * RET_VAL_DIM)
        for b in range(bt):
            out, state = _ret_chunk(q_ref[b, :, kcols], k_ref[b, :, kcols], v_ref[b, :, vcols],
                                    g_ref[b, :, vcols], st_ref[b, h], decays)
            o_ref[b, :, vcols] = out.astype(o_ref.dtype)
            nst_ref[b, h] = state


def _ret_sample(qr, kr, vr, gr, state):
    db, t_new, _ = qr.shape
    bt = SAMPLE_BT
    blk = lambda *s: pl.BlockSpec((bt,) + s, lambda i: (i,) + (0,) * len(s))
    st = blk(N_RET_HEADS, RET_KEY_DIM, RET_VAL_DIM)
    return pl.pallas_call(
        _ret_sample_kernel,
        grid=(db // bt,),
        in_specs=[blk(t_new, RQ_W), blk(t_new, RQ_W), blk(t_new, RV_W), blk(t_new, RV_W), st],
        out_specs=[blk(t_new, RV_W), st],
        out_shape=[jax.ShapeDtypeStruct((db, t_new, RV_W), BF16),
                   jax.ShapeDtypeStruct(state.shape, state.dtype)],
        compiler_params=_cparams(1),
        name="ret_sample",
    )(qr, kr, vr, gr, state)


def _merge_ln_kernel(alpha, x_ref, oa_ref, orr_ref, ga_ref, gb_ref, wba_ref, wbr_ref, wo_ref,
                     g_ref, b_ref, o_ref):
    ya = _dot(oa_ref[...], wba_ref[...])
    yr = _dot(orr_ref[...], wbr_ref[...])
    merged = (jax.nn.sigmoid(ga_ref[...].astype(F32)) * ya
              + jax.nn.sigmoid(gb_ref[...].astype(F32)) * yr)
    z = _dot(merged.astype(BF16), wo_ref[...])
    o_ref[...] = _layer_norm(alpha * x_ref[...] + z, g_ref[...], b_ref[...])


def _merge_ln(x1, oa, orr, ga, gb, wba, wbr, wo, g, b, alpha):
    t, d = x1.shape
    tm = TOKEN_TILE
    row = lambda i: (i, 0)
    return pl.pallas_call(
        functools.partial(_merge_ln_kernel, alpha),
        grid=(t // tm,),
        in_specs=[pl.BlockSpec((tm, d), row), pl.BlockSpec((tm, Q_W), row), pl.BlockSpec((tm, RV_W), row),
                  pl.BlockSpec((tm, d), row), pl.BlockSpec((tm, d), row),
                  _const_spec(wba.shape), _const_spec(wbr.shape), _const_spec(wo.shape),
                  _const_spec((1, d)), _const_spec((1, d))],
        out_specs=pl.BlockSpec((tm, d), row),
        out_shape=jax.ShapeDtypeStruct((t, d), F32),
        compiler_params=_cparams(1),
        name="merge_ln",
    )(x1, oa, orr, ga, gb, wba, wbr, wo, g.reshape(1, d), b.reshape(1, d))


def _layer(x, pos, mixers, alpha, ln_g, ln_b, wi, wo, w_in, wba, wbr, w_o):
    x1 = _ffn_ln(x, wi[0], wo[0], ln_g[0], ln_b[0], alpha)
    period = pos.shape[0]
    tables = _rope_tables(pos, HEAD_DIM, _attn_inv_freq) + _rope_tables(pos, RET_KEY_DIM, _ret_inv_freq)
    if period < TOKEN_TILE:
        tables = tuple(jnp.tile(tb, (TOKEN_TILE // period, 1)) for tb in tables)
        period = TOKEN_TILE
    qa, ka, va, qr, kr, vr, gr, ga, gb = _proj_rope(x1, w_in, tables, period // TOKEN_TILE)
    oa, orr, extras = mixers(qa, ka, va, qr, kr, vr, gr)
    x2 = _merge_ln(x1, oa, orr, ga, gb, wba, wbr, w_o, ln_g[1], ln_b[1], alpha)
    x3 = _ffn_ln(x2, wi[1], wo[1], ln_g[2], ln_b[2], alpha)
    return x3, extras


def kernel(x_prompt, x_sample, cache_k, cache_v, state_ret, ln_g, ln_b, ffn_wi, ffn_wo, w_in,
           attn_sinks, w_br_attn, w_br_ret, w_o):
    B, S, D = x_prompt.shape
    DB, T, _ = x_sample.shape
    depth = ln_g.shape[0]
    alpha = (2.0 * depth) ** 0.25
    W = cache_k.shape[2]
    pos_p = jnp.arange(S, dtype=jnp.int32)
    pos_s = PAST_LEN + jnp.arange(T, dtype=jnp.int32)

    y_p = x_prompt.reshape(B * S, D)
    y_s = x_sample.reshape(DB * T, D)
    outs = [[] for _ in range(6)]
    for l in range(depth):
        weights = (ffn_wi[l].astype(BF16), ffn_wo[l].astype(BF16), w_in[l].astype(BF16),
                   w_br_attn[l].astype(BF16), w_br_ret[l].astype(BF16), w_o[l].astype(BF16))
        sinks = attn_sinks[l].astype(F32)

        def prompt_mixers(qa, ka, va, qr, kr, vr, gr):
            oa = _attn_prompt(qa, ka, va, sinks, B, S)
            orr, st = _ret_prompt(qr, kr, vr, gr, B, S)
            w = min(WINDOW, S)
            new_k = ka.reshape(B, S, N_KV_HEADS, HEAD_DIM)[:, S - w:]
            new_v = va.reshape(B, S, N_KV_HEADS, HEAD_DIM)[:, S - w:]
            return oa, orr, (new_k, new_v, st)

        def sample_mixers(qa, ka, va, qr, kr, vr, gr):
            r3 = lambda a: a.reshape(DB, T, a.shape[-1])
            oa, nk, nv = _attn_sample(r3(qa), r3(ka), r3(va), sinks,
                                      cache_k[l].reshape(DB, W, KV_W), cache_v[l].reshape(DB, W, KV_W))
            orr, st = _ret_sample(r3(qr), r3(kr), r3(vr), r3(gr), state_ret[l])
            shape5 = (DB, W, N_KV_HEADS, HEAD_DIM)
            return (oa.reshape(DB * T, Q_W), orr.reshape(DB * T, RV_W),
                    (nk.reshape(shape5), nv.reshape(shape5), st))

        y_p, ex_p = _layer(y_p, pos_p, prompt_mixers, alpha, ln_g[l], ln_b[l], *weights)
        y_s, ex_s = _layer(y_s, pos_s, sample_mixers, alpha, ln_g[l], ln_b[l], *weights)
        for lst, val in zip(outs, ex_p + ex_s):
            lst.append(val)
    return (y_p.reshape(B, S, D), y_s.reshape(DB, T, D)) + tuple(jnp.stack(o) for o in outs)
```

```python
import functools
import math

import jax
import jax.numpy as jnp
from jax import lax
from jax.experimental import pallas as pl
from jax.experimental.pallas import tpu as pltpu

F32 = jnp.float32
BF16 = jnp.bfloat16

PAST_LEN = 8192
N_Q_HEADS = 8
N_KV_HEADS = 2
HEAD_DIM = 64
GQA_GROUP = N_Q_HEADS // N_KV_HEADS
WINDOW = 128
ROPE_THETA = 10000.0
N_RET_HEADS = 4
RET_KEY_DIM = 128
RET_VAL_DIM = 256
RET_CHUNK = 128
RET_THETA = 10000.0
LN_EPS = 1e-5
GN_EPS = 1e-6
NEG_INF = -1e30

Q_W = N_Q_HEADS * HEAD_DIM
KV_W = N_KV_HEADS * HEAD_DIM
RQ_W = N_RET_HEADS * RET_KEY_DIM
RV_W = N_RET_HEADS * RET_VAL_DIM

LANES = 128
VMEM_LIMIT_BYTES = 56 * 1024 * 1024

TOKEN_TILE = 512
FF_CHUNK = 256
ATTN_Q_TILE = 512
RET_TILE = 512
SAMPLE_BT = 8


def _dot(a, b):
    return jnp.dot(a, b, preferred_element_type=F32)


def _dot_nt(a, b):
    return lax.dot_general(a, b, (((1,), (1,)), ((), ())), preferred_element_type=F32)


def _layer_norm(y, g, b):
    mu = jnp.mean(y, axis=-1, keepdims=True)
    yc = y - mu
    var = jnp.mean(yc * yc, axis=-1, keepdims=True)
    return yc * lax.rsqrt(var + LN_EPS) * g + b


def _cparams(n_axes, semantics=None):
    return pltpu.CompilerParams(
        dimension_semantics=semantics or ("arbitrary",) * n_axes,
        vmem_limit_bytes=VMEM_LIMIT_BYTES)


def _const_spec(shape):
    nd = len(shape)
    return pl.BlockSpec(shape, lambda *_: (0,) * nd, pipeline_mode=pl.Buffered(1))


def _segment_specs(n_prompt_tiles, tm, width):
    prompt = pl.BlockSpec((tm, width), lambda i: (jnp.minimum(i, n_prompt_tiles - 1), 0))
    sample = pl.BlockSpec((tm, width), lambda i: (jnp.maximum(i - n_prompt_tiles, 0), 0))
    return prompt, sample


def _store_segment(is_prompt, prompt_ref, sample_ref, value):
    @pl.when(is_prompt)
    def _():
        prompt_ref[...] = value.astype(prompt_ref.dtype)

    @pl.when(jnp.logical_not(is_prompt))
    def _():
        sample_ref[...] = value.astype(sample_ref.dtype)


def _ffn_ln_kernel(alpha, d_ff, ln_row, n_prompt_tiles,
                   xp_ref, xs_ref, wi_ref, wo_ref, g_ref, b_ref, op_ref, os_ref, acc_ref):
    is_prompt = pl.program_id(0) < n_prompt_tiles
    x = jnp.where(is_prompt, xp_ref[...], xs_ref[...])
    xb = x.astype(BF16)
    for c in range(d_ff // FF_CHUNK):
        lo = c * FF_CHUNK
        gate = _dot(xb, wi_ref[:, lo:lo + FF_CHUNK])
        up = _dot(xb, wi_ref[:, d_ff + lo:d_ff + lo + FF_CHUNK])
        act = (jax.nn.silu(gate) * up).astype(BF16)
        part = _dot(act, wo_ref[lo:lo + FF_CHUNK, :])
        if c == 0:
            acc_ref[...] = part
        else:
            acc_ref[...] += part
    y = alpha * x + 0.5 * acc_ref[...]
    y = _layer_norm(y, g_ref[ln_row:ln_row + 1, :], b_ref[ln_row:ln_row + 1, :])
    _store_segment(is_prompt, op_ref, os_ref, y)


def _ffn_ln(xp, xs, wi, wo, half, ln_g, ln_b, ln_row, alpha):
    d = xp.shape[1]
    d_ff = wo.shape[1]
    tm = TOKEN_TILE
    assert xp.shape[0] % tm == 0 and xs.shape[0] % tm == 0 and d_ff % FF_CHUNK == 0
    n_p, n_s = xp.shape[0] // tm, xs.shape[0] // tm
    seg_p, seg_s = _segment_specs(n_p, tm, d)
    pick = lambda i: (half, 0, 0)
    return pl.pallas_call(
        functools.partial(_ffn_ln_kernel, alpha, d_ff, ln_row, n_p),
        grid=(n_p + n_s,),
        in_specs=[seg_p, seg_s,
                  pl.BlockSpec((None,) + wi.shape[1:], pick, pipeline_mode=pl.Buffered(1)),
                  pl.BlockSpec((None,) + wo.shape[1:], pick, pipeline_mode=pl.Buffered(1)),
                  _const_spec(ln_g.shape), _const_spec(ln_b.shape)],
        out_specs=[seg_p, seg_s],
        out_shape=[jax.ShapeDtypeStruct(xp.shape, F32), jax.ShapeDtypeStruct(xs.shape, F32)],
        scratch_shapes=[pltpu.VMEM((tm, d), F32)],
        compiler_params=_cparams(1),
        name="ffn_ln",
    )(xp, xs, wi, wo, ln_g, ln_b)


def _rope_tables(pos, dim, theta_fn):
    half = dim // 2
    inv_freq = theta_fn(half)
    ang = pos.astype(F32)[:, None] * inv_freq[None, :]
    reps = LANES // half
    cos = jnp.tile(jnp.cos(ang), (1, reps))
    sign = jnp.tile(jnp.concatenate([-jnp.ones((half,), F32), jnp.ones((half,), F32)]), LANES // dim)
    sin = jnp.tile(jnp.sin(ang), (1, reps)) * sign[None, :]
    return cos, sin


def _attn_inv_freq(half):
    return 1.0 / (ROPE_THETA ** (jnp.arange(0, 2 * half, 2, dtype=F32) / (2 * half)))


def _ret_inv_freq(half):
    return 1.0 / (RET_THETA ** jnp.linspace(0.0, 1.0, half, dtype=F32))


def _rope_block(x, cos, sin, dim):
    half = dim // 2
    if dim == LANES:
        rot = pltpu.roll(x, half, 1)
    else:
        lane = lax.broadcasted_iota(jnp.int32, x.shape, 1)
        first_half = (lane % dim) < half
        rot = jnp.where(first_half, pltpu.roll(x, LANES - half, 1), pltpu.roll(x, half, 1))
    return x * cos + rot * sin


def _proj_rope_kernel(x_ref, w_ref, cosa_ref, sina_ref, cosr_ref, sinr_ref,
                      qa_ref, ka_ref, va_ref, qr_ref, kr_ref, vr_ref, gr_ref, ga_ref, gb_ref):
    xb = x_ref[...].astype(BF16)
    cosa, sina = cosa_ref[...], sina_ref[...]
    cosr, sinr = cosr_ref[...], sinr_ref[...]
    off = 0

    y = _dot(xb, w_ref[:, off:off + Q_W])
    for j in range(Q_W // LANES):
        qa_ref[:, j * LANES:(j + 1) * LANES] = _rope_block(
            y[:, j * LANES:(j + 1) * LANES], cosa, sina, HEAD_DIM).astype(qa_ref.dtype)
    off += Q_W
    y = _dot(xb, w_ref[:, off:off + 2 * KV_W])
    ka_ref[...] = _rope_block(y[:, :KV_W], cosa, sina, HEAD_DIM)
    va_ref[...] = y[:, KV_W:]
    off += 2 * KV_W
    y = _dot(xb, w_ref[:, off:off + RQ_W])
    for j in range(RQ_W // LANES):
        qr_ref[:, j * LANES:(j + 1) * LANES] = _rope_block(
            y[:, j * LANES:(j + 1) * LANES], cosr, sinr, RET_KEY_DIM).astype(qr_ref.dtype)
    off += RQ_W
    y = _dot(xb, w_ref[:, off:off + RQ_W])
    for j in range(RQ_W // LANES):
        kr = _rope_block(y[:, j * LANES:(j + 1) * LANES], cosr, sinr, RET_KEY_DIM)
        kr_ref[:, j * LANES:(j + 1) * LANES] = (kr * (RET_KEY_DIM ** -0.5)).astype(kr_ref.dtype)
    off += RQ_W
    for ref in (vr_ref, gr_ref, ga_ref, gb_ref):
        width = ref.shape[1]
        for c in range(width // 512):
            ref[:, c * 512:(c + 1) * 512] = _dot(
                xb, w_ref[:, off + c * 512:off + (c + 1) * 512]).astype(ref.dtype)
        off += width


def _proj_rope(x1, w_in, tables, table_blocks, mixer_dtype):
    t, d = x1.shape
    n_in = w_in.shape[1]
    d_model = (n_in - Q_W - 2 * KV_W - 2 * RQ_W - 2 * RV_W) // 2
    tm = TOKEN_TILE
    row = lambda i: (i, 0)
    tab = lambda i: (i % table_blocks, 0)
    widths = [(Q_W, mixer_dtype), (KV_W, F32), (KV_W, F32), (RQ_W, mixer_dtype), (RQ_W, mixer_dtype),
              (RV_W, mixer_dtype), (RV_W, mixer_dtype), (d_model, BF16), (d_model, BF16)]
    return pl.pallas_call(
        _proj_rope_kernel,
        grid=(t // tm,),
        in_specs=[pl.BlockSpec((tm, d), row), _const_spec(w_in.shape)]
                 + [pl.BlockSpec((tm, LANES), tab)] * 4,
        out_specs=[pl.BlockSpec((tm, w), row) for w, _ in widths],
        out_shape=[jax.ShapeDtypeStruct((t, w), dt) for w, dt in widths],
        compiler_params=_cparams(1),
        name="proj_rope",
    )(x1, w_in, *tables)


def _kv_lane_variants(x):
    lane = lax.broadcasted_iota(jnp.int32, x.shape, 1)
    lo = lane < HEAD_DIM
    swapped = pltpu.roll(x, HEAD_DIM, 1)
    zero = jnp.zeros_like(x)
    kv0 = (jnp.where(lo, x, zero).astype(BF16), jnp.where(lo, zero, swapped).astype(BF16))
    kv1 = (jnp.where(lo, swapped, zero).astype(BF16), jnp.where(lo, zero, x).astype(BF16))
    return kv0, kv1


def _attn_prompt_kernel(sinks_ref, q_ref, kc_ref, kp_ref, vc_ref, vp_ref, o_ref):
    n = pl.program_id(1)
    L = WINDOW
    k_full = jnp.concatenate([kp_ref[...], kc_ref[...]], axis=0)
    v_full = jnp.concatenate([vp_ref[...], vc_ref[...]], axis=0)
    k_var = _kv_lane_variants(k_full)
    v_var = _kv_lane_variants(v_full)
    qi = lax.broadcasted_iota(jnp.int32, (L, 2 * L), 0)
    mi = lax.broadcasted_iota(jnp.int32, (L, 2 * L), 1)
    band = (mi > qi) & (mi <= qi + L)
    scale = HEAD_DIM ** -0.5
    for i in range(q_ref.shape[0] // L):
        mask = band
        if i == 0:
            mask = band & ((mi >= L) | (n > 0))
        rows = slice(i * L, (i + 1) * L)
        keys = slice(i * L, i * L + 2 * L)
        for j in range(Q_W // LANES):
            kv = (2 * j) // GQA_GROUP
            q2 = q_ref[rows, j * LANES:(j + 1) * LANES] * scale
            out = None
            for par in range(2):
                s = _dot_nt(q2, k_var[kv][par][keys])
                s = jnp.where(mask, s, NEG_INF)
                sink = sinks_ref[2 * j + par]
                m = jnp.maximum(jnp.max(s, axis=-1, keepdims=True), sink)
                e = jnp.exp(s - m)
                denom = jnp.sum(e, axis=-1, keepdims=True) + jnp.exp(sink - m)
                r = _dot(e.astype(BF16), v_var[kv][par][keys]) / denom
                out = r if out is None else out + r
            o_ref[rows, j * LANES:(j + 1) * LANES] = out.astype(o_ref.dtype)


def _attn_prompt(qa, ka, va, sinks, batch, seq):
    tq = ATTN_Q_TILE
    nq = seq // tq
    per = tq // WINDOW
    cur = lambda b, n: (b * nq + n, 0)
    prev = lambda b, n: ((b * nq + n) * per - jnp.minimum(n, 1), 0)
    return pl.pallas_call(
        _attn_prompt_kernel,
        grid=(batch, nq),
        in_specs=[pl.BlockSpec(memory_space=pltpu.SMEM),
                  pl.BlockSpec((tq, Q_W), cur),
                  pl.BlockSpec((tq, KV_W), cur), pl.BlockSpec((WINDOW, KV_W), prev),
                  pl.BlockSpec((tq, KV_W), cur), pl.BlockSpec((WINDOW, KV_W), prev)],
        out_specs=pl.BlockSpec((tq, Q_W), cur),
        out_shape=jax.ShapeDtypeStruct(qa.shape, BF16),
        compiler_params=_cparams(2),
        name="attn_prompt",
    )(sinks, qa, ka, ka, va, va)


def _attn_sample_kernel(sink_rows_ref, q_ref, kn_ref, vn_ref, ck_ref, cv_ref,
                        o_ref, nk_ref, nv_ref):
    bt, t_new, _ = q_ref.shape
    w = ck_ref.shape[1]
    n_rows = N_Q_HEADS * t_new
    lane_lo = lax.broadcasted_iota(jnp.int32, (bt * t_new, LANES), 1) < HEAD_DIM
    r_i = lax.broadcasted_iota(jnp.int32, (n_rows, w + t_new), 0) % t_new
    m_i = lax.broadcasted_iota(jnp.int32, (n_rows, w + t_new), 1)
    mask = ((m_i < w) & (m_i > r_i + (w - WINDOW))) | ((m_i >= w) & (m_i - w <= r_i))
    sink = sink_rows_ref[...]
    scale = HEAD_DIM ** -0.5
    kc, kn = ck_ref[...], kn_ref[...]
    vc, vn = cv_ref[...], vn_ref[...]
    k_all = jnp.concatenate([kc, kn], axis=1).astype(BF16)
    v_all = jnp.concatenate([vc, vn], axis=1).astype(BF16)
    q = q_ref[...].reshape(bt * t_new, Q_W)
    pieces = []
    for h in range(N_Q_HEADS):
        q2 = q[:, (h // 2) * LANES:(h // 2 + 1) * LANES]
        src_lo = h % 2 == 0
        dst_lo = h // GQA_GROUP == 0
        if src_lo != dst_lo:
            q2 = pltpu.roll(q2, HEAD_DIM, 1)
        pieces.append(jnp.where(lane_lo == dst_lo, q2, 0.0).reshape(bt, t_new, LANES))
    q_rows = (jnp.concatenate(pieces, axis=1) * scale).astype(BF16)
    s = jnp.einsum("bqd,bkd->bqk", q_rows, k_all, preferred_element_type=F32)
    s = jnp.where(mask, s, NEG_INF)
    m = jnp.maximum(jnp.max(s, axis=-1, keepdims=True), sink)
    e = jnp.exp(s - m)
    denom = jnp.sum(e, axis=-1, keepdims=True) + jnp.exp(sink - m)
    pv = jnp.einsum("bqk,bkd->bqd", e.astype(BF16), v_all, preferred_element_type=F32) / denom
    for j in range(Q_W // LANES):
        out = None
        for par in range(2):
            h = 2 * j + par
            dst_lo = h // GQA_GROUP == 0
            piece = pv[:, h * t_new:(h + 1) * t_new, :].reshape(bt * t_new, LANES)
            piece = jnp.where(lane_lo == dst_lo, piece, 0.0)
            if dst_lo != (par == 0):
                piece = pltpu.roll(piece, HEAD_DIM, 1)
            out = piece if out is None else out + piece
        o_ref[:, :, j * LANES:(j + 1) * LANES] = out.reshape(bt, t_new, LANES).astype(o_ref.dtype)
    nk_ref[:, :w - t_new, :] = kc[:, t_new:, :]
    nk_ref[:, w - t_new:, :] = kn
    nv_ref[:, :w - t_new, :] = vc[:, t_new:, :]
    nv_ref[:, w - t_new:, :] = vn


def _attn_sample(qa, ka, va, sinks, cache_k, cache_v):
    db, t_new, _ = qa.shape
    w = cache_k.shape[1]
    bt = SAMPLE_BT
    sink_rows = jnp.repeat(sinks.astype(F32), t_new).reshape(N_Q_HEADS * t_new, 1)
    blk = lambda *s: pl.BlockSpec((bt,) + s, lambda i: (i, 0, 0))
    return pl.pallas_call(
        _attn_sample_kernel,
        grid=(db // bt,),
        in_specs=[_const_spec(sink_rows.shape), blk(t_new, Q_W), blk(t_new, KV_W), blk(t_new, KV_W),
                  blk(w, KV_W), blk(w, KV_W)],
        out_specs=[blk(t_new, Q_W), blk(w, KV_W), blk(w, KV_W)],
        out_shape=[jax.ShapeDtypeStruct((db, t_new, Q_W), F32),
                   jax.ShapeDtypeStruct((db, w, KV_W), F32),
                   jax.ShapeDtypeStruct((db, w, KV_W), F32)],
        compiler_params=_cparams(1),
        name="attn_sample",
    )(sink_rows, qa, ka, va, cache_k, cache_v)


def _ret_log_gamma(h):
    return math.log(1.0 - 2.0 ** (-5.0 - h))


def _ret_decays(chunk, h):
    lg = _ret_log_gamma(h)
    i = lax.broadcasted_iota(jnp.int32, (chunk, chunk), 0)
    j = lax.broadcasted_iota(jnp.int32, (chunk, chunk), 1)
    causal = i >= j
    diff = jnp.where(causal, i - j, 0).astype(F32)
    decay_mask = jnp.where(causal, jnp.exp(diff * lg), 0.0)
    col = lax.broadcasted_iota(jnp.int32, (chunk, 1), 0).astype(F32)
    q_decay = jnp.exp((col + 1.0) * lg)
    k_decay = jnp.exp((chunk - 1.0 - col) * lg)
    return decay_mask, q_decay, k_decay, math.exp(chunk * lg)


def _ret_chunk(q, k, v, g, state, decays):
    decay_mask, q_decay, k_decay, chunk_decay = decays
    q, v = q.astype(BF16), v.astype(BF16)
    inner = _dot_nt(q, k.astype(BF16)) * decay_mask
    o = _dot(inner.astype(BF16), v) + _dot(q, state.astype(BF16)) * q_decay
    kd_t = (k.astype(F32) * k_decay).T.astype(BF16)
    new_state = state * chunk_decay + _dot(kd_t, v)
    mu = jnp.mean(o, axis=-1, keepdims=True)
    oc = o - mu
    var = jnp.mean(oc * oc, axis=-1, keepdims=True)
    normed = oc * lax.rsqrt(var + GN_EPS)
    return normed * jax.nn.silu(g.astype(F32)), new_state


def _ret_prompt_kernel(q_ref, k_ref, v_ref, g_ref, o_ref, st_ref, state_ref):
    n = pl.program_id(1)
    C = RET_CHUNK

    @pl.when(n == 0)
    def _():
        state_ref[...] = jnp.zeros_like(state_ref)

    for h in range(N_RET_HEADS):
        decays = _ret_decays(C, h)
        state = state_ref[h]
        kcols = slice(h * RET_KEY_DIM, (h + 1) * RET_KEY_DIM)
        vcols = slice(h * RET_VAL_DIM, (h + 1) * RET_VAL_DIM)
        for c in range(q_ref.shape[0] // C):
            rows = slice(c * C, (c + 1) * C)
            out, state = _ret_chunk(q_ref[rows, kcols], k_ref[rows, kcols], v_ref[rows, vcols],
                                    g_ref[rows, vcols], state, decays)
            o_ref[rows, vcols] = out.astype(o_ref.dtype)
        state_ref[h] = state

    @pl.when(n == pl.num_programs(1) - 1)
    def _():
        st_ref[0] = state_ref[...]


def _ret_prompt(qr, kr, vr, gr, batch, seq):
    ts = RET_TILE
    ns = seq // ts
    row = lambda b, n: (b * ns + n, 0)
    return pl.pallas_call(
        _ret_prompt_kernel,
        grid=(batch, ns),
        in_specs=[pl.BlockSpec((ts, RQ_W), row), pl.BlockSpec((ts, RQ_W), row),
                  pl.BlockSpec((ts, RV_W), row), pl.BlockSpec((ts, RV_W), row)],
        out_specs=[pl.BlockSpec((ts, RV_W), row),
                   pl.BlockSpec((1, N_RET_HEADS, RET_KEY_DIM, RET_VAL_DIM), lambda b, n: (b, 0, 0, 0))],
        out_shape=[jax.ShapeDtypeStruct((batch * seq, RV_W), BF16),
                   jax.ShapeDtypeStruct((batch, N_RET_HEADS, RET_KEY_DIM, RET_VAL_DIM), F32)],
        scratch_shapes=[pltpu.VMEM((N_RET_HEADS, RET_KEY_DIM, RET_VAL_DIM), F32)],
        compiler_params=_cparams(2),
        name="ret_prompt",
    )(qr, kr, vr, gr)


def _ret_sample_kernel(q_ref, k_ref, v_ref, g_ref, st_ref, o_ref, nst_ref):
    bt, t_new, _ = q_ref.shape
    for h in range(N_RET_HEADS):
        decays = _ret_decays(t_new, h)
        kcols = slice(h * RET_KEY_DIM, (h + 1) * RET_KEY_DIM)
        vcols = slice(h * RET_VAL_DIM, (h + 1) * RET_VAL_DIM)
        for b in range(bt):
            out, state = _ret_chunk(q_ref[b, :, kcols], k_ref[b, :, kcols], v_ref[b, :, vcols],
                                    g_ref[b, :, vcols], st_ref[b, h], decays)
            o_ref[b, :, vcols] = out.astype(o_ref.dtype)
            nst_ref[b, h] = state


def _ret_sample(qr, kr, vr, gr, state):
    db, t_new, _ = qr.shape
    bt = SAMPLE_BT
    blk = lambda *s: pl.BlockSpec((bt,) + s, lambda i: (i,) + (0,) * len(s))
    st = blk(N_RET_HEADS, RET_KEY_DIM, RET_VAL_DIM)
    return pl.pallas_call(
        _ret_sample_kernel,
        grid=(db // bt,),
        in_specs=[blk(t_new, RQ_W), blk(t_new, RQ_W), blk(t_new, RV_W), blk(t_new, RV_W), st],
        out_specs=[blk(t_new, RV_W), st],
        out_shape=[jax.ShapeDtypeStruct((db, t_new, RV_W), F32),
                   jax.ShapeDtypeStruct(state.shape, state.dtype)],
        compiler_params=_cparams(1),
        name="ret_sample",
    )(qr, kr, vr, gr, state)


def _merge_ln_kernel(alpha, ln_row, n_prompt_tiles,
                     xp_ref, xs_ref, oap_ref, oas_ref, orp_ref, ors_ref, gap_ref, gas_ref, gbp_ref, gbs_ref,
                     wba_ref, wbr_ref, wo_ref, g_ref, b_ref, op_ref, os_ref):
    is_prompt = pl.program_id(0) < n_prompt_tiles

    def pick(p_ref, s_ref, dtype):
        return jnp.where(is_prompt, p_ref[...].astype(dtype), s_ref[...].astype(dtype))

    ya = _dot(pick(oap_ref, oas_ref, BF16), wba_ref[...])
    yr = _dot(pick(orp_ref, ors_ref, BF16), wbr_ref[...])
    merged = (jax.nn.sigmoid(pick(gap_ref, gas_ref, F32)) * ya
              + jax.nn.sigmoid(pick(gbp_ref, gbs_ref, F32)) * yr)
    z = _dot(merged.astype(BF16), wo_ref[...])
    y = _layer_norm(alpha * pick(xp_ref, xs_ref, F32) + z,
                    g_ref[ln_row:ln_row + 1, :], b_ref[ln_row:ln_row + 1, :])
    _store_segment(is_prompt, op_ref, os_ref, y)


def _merge_ln(x1, oa, orr, ga, gb, wba, wbr, wo, ln_g, ln_b, ln_row, alpha):
    d = x1[0].shape[1]
    tm = TOKEN_TILE
    n_p, n_s = x1[0].shape[0] // tm, x1[1].shape[0] // tm
    in_specs, args = [], []
    for pair in (x1, oa, orr, ga, gb):
        in_specs += list(_segment_specs(n_p, tm, pair[0].shape[1]))
        args += list(pair)
    seg_p, seg_s = _segment_specs(n_p, tm, d)
    return pl.pallas_call(
        functools.partial(_merge_ln_kernel, alpha, ln_row, n_p),
        grid=(n_p + n_s,),
        in_specs=in_specs + [_const_spec(wba.shape), _const_spec(wbr.shape), _const_spec(wo.shape),
                             _const_spec(ln_g.shape), _const_spec(ln_b.shape)],
        out_specs=[seg_p, seg_s],
        out_shape=[jax.ShapeDtypeStruct(x1[0].shape, F32), jax.ShapeDtypeStruct(x1[1].shape, F32)],
        compiler_params=_cparams(1),
        name="merge_ln",
    )(*args, wba, wbr, wo, ln_g, ln_b)


def _rope_tables_for(pos):
    tables = _rope_tables(pos, HEAD_DIM, _attn_inv_freq) + _rope_tables(pos, RET_KEY_DIM, _ret_inv_freq)
    period = pos.shape[0]
    if period < TOKEN_TILE:
        tables = tuple(jnp.tile(tb, (TOKEN_TILE // period, 1)) for tb in tables)
        period = TOKEN_TILE
    return tables, period // TOKEN_TILE


def kernel(x_prompt, x_sample, cache_k, cache_v, state_ret, ln_g, ln_b, ffn_wi, ffn_wo, w_in,
           attn_sinks, w_br_attn, w_br_ret, w_o):
    B, S, D = x_prompt.shape
    DB, T, _ = x_sample.shape
    depth = ln_g.shape[0]
    alpha = (2.0 * depth) ** 0.25
    W = cache_k.shape[2]
    tables_p, tab_blocks_p = _rope_tables_for(jnp.arange(S, dtype=jnp.int32))
    tables_s, tab_blocks_s = _rope_tables_for(PAST_LEN + jnp.arange(T, dtype=jnp.int32))

    y_p = x_prompt.reshape(B * S, D)
    y_s = x_sample.reshape(DB * T, D)
    outs = [[] for _ in range(6)]
    for l in range(depth):
        wi, wo = ffn_wi[l].astype(BF16), ffn_wo[l].astype(BF16)
        w_in_l = w_in[l].astype(BF16)
        wba, wbr, w_o_l = w_br_attn[l].astype(BF16), w_br_ret[l].astype(BF16), w_o[l].astype(BF16)
        sinks = attn_sinks[l].astype(F32)
        g_l, b_l = ln_g[l], ln_b[l]

        x1_p, x1_s = _ffn_ln(y_p, y_s, wi, wo, 0, g_l, b_l, 0, alpha)
        qa_p, ka_p, va_p, qr_p, kr_p, vr_p, gr_p, ga_p, gb_p = _proj_rope(
            x1_p, w_in_l, tables_p, tab_blocks_p, BF16)
        qa_s, ka_s, va_s, qr_s, kr_s, vr_s, gr_s, ga_s, gb_s = _proj_rope(
            x1_s, w_in_l, tables_s, tab_blocks_s, F32)

        oa_p = _attn_prompt(qa_p, ka_p, va_p, sinks, B, S)
        or_p, st_p = _ret_prompt(qr_p, kr_p, vr_p, gr_p, B, S)
        w = min(WINDOW, S)
        nk_p = ka_p.reshape(B, S, N_KV_HEADS, HEAD_DIM)[:, S - w:]
        nv_p = va_p.reshape(B, S, N_KV_HEADS, HEAD_DIM)[:, S - w:]

        r3 = lambda a: a.reshape(DB, T, a.shape[-1])
        oa_s, nk_s, nv_s = _attn_sample(r3(qa_s), r3(ka_s), r3(va_s), sinks,
                                        cache_k[l].reshape(DB, W, KV_W), cache_v[l].reshape(DB, W, KV_W))
        or_s, st_s = _ret_sample(r3(qr_s), r3(kr_s), r3(vr_s), r3(gr_s), state_ret[l])
        oa_s, or_s = oa_s.reshape(DB * T, Q_W), or_s.reshape(DB * T, RV_W)
        shape5 = (DB, W, N_KV_HEADS, HEAD_DIM)

        x2_p, x2_s = _merge_ln((x1_p, x1_s), (oa_p, oa_s), (or_p, or_s), (ga_p, ga_s), (gb_p, gb_s),
                               wba, wbr, w_o_l, g_l, b_l, 1, alpha)
        y_p, y_s = _ffn_ln(x2_p, x2_s, wi, wo, 1, g_l, b_l, 2, alpha)
        for lst, val in zip(outs, (nk_p, nv_p, st_p, nk_s.reshape(shape5), nv_s.reshape(shape5), st_s)):
            lst.append(val)
    return (y_p.reshape(B, S, D), y_s.reshape(DB, T, D)) + tuple(jnp.stack(o) for o in outs)
```

```python
import functools
import math

import jax
import jax.numpy as jnp
from jax import lax
from jax.experimental import pallas as pl
from jax.experimental.pallas import tpu as pltpu

F32 = jnp.float32
BF16 = jnp.bfloat16

PAST_LEN = 8192
N_Q_HEADS = 8
N_KV_HEADS = 2
HEAD_DIM = 64
GQA_GROUP = N_Q_HEADS // N_KV_HEADS
WINDOW = 128
ROPE_THETA = 10000.0
N_RET_HEADS = 4
RET_KEY_DIM = 128
RET_VAL_DIM = 256
RET_CHUNK = 128
RET_THETA = 10000.0
LN_EPS = 1e-5
GN_EPS = 1e-6
NEG_INF = -1e30

Q_W = N_Q_HEADS * HEAD_DIM
KV_W = N_KV_HEADS * HEAD_DIM
RQ_W = N_RET_HEADS * RET_KEY_DIM
RV_W = N_RET_HEADS * RET_VAL_DIM

LANES = 128
VMEM_LIMIT_BYTES = 56 * 1024 * 1024

TOKEN_TILE = 512
FF_CHUNK = 256
ATTN_Q_TILE = 512
RET_TILE = 512
SAMPLE_BT = 8


def _dot(a, b):
    return jnp.dot(a, b, preferred_element_type=F32)


def _dot_nt(a, b):
    return lax.dot_general(a, b, (((1,), (1,)), ((), ())), preferred_element_type=F32)


def _layer_norm(y, g, b):
    mu = jnp.mean(y, axis=-1, keepdims=True)
    yc = y - mu
    var = jnp.mean(yc * yc, axis=-1, keepdims=True)
    return yc * lax.rsqrt(var + LN_EPS) * g + b


def _cparams(n_axes, semantics=None):
    return pltpu.CompilerParams(
        dimension_semantics=semantics or ("arbitrary",) * n_axes,
        vmem_limit_bytes=VMEM_LIMIT_BYTES)


def _const_spec(shape):
    nd = len(shape)
    return pl.BlockSpec(shape, lambda *_: (0,) * nd, pipeline_mode=pl.Buffered(1))


def _segment_specs(n_prompt_tiles, tm, width, tile_of):
    prompt = pl.BlockSpec((tm, width), lambda i: (jnp.minimum(tile_of(i), n_prompt_tiles - 1), 0))
    sample = pl.BlockSpec((tm, width), lambda i: (jnp.maximum(tile_of(i) - n_prompt_tiles, 0), 0))
    return prompt, sample


def _store_segment(is_prompt, prompt_ref, sample_ref, value):
    @pl.when(is_prompt)
    def _():
        prompt_ref[...] = value.astype(prompt_ref.dtype)

    @pl.when(jnp.logical_not(is_prompt))
    def _():
        sample_ref[...] = value.astype(sample_ref.dtype)


def _ffn_chunk(xb, w_gate, w_up, w_out):
    act = (jax.nn.silu(_dot(xb, w_gate)) * _dot(xb, w_up)).astype(BF16)
    return _dot(act, w_out)


def _ffn_ln_kernel(alpha, n_chunks, ln_row, n_prompt_tiles,
                   xp_ref, xs_ref, wg_ref, wu_ref, wo_ref, g_ref, b_ref, op_ref, os_ref,
                   acc_ref, wi_bf_ref, wo_bf_ref):
    step = pl.program_id(0)
    tile = jnp.maximum(step - (n_chunks - 1), 0)
    is_prompt = tile < n_prompt_tiles
    fc = FF_CHUNK

    def finish(x):
        y = alpha * x + 0.5 * acc_ref[...]
        y = _layer_norm(y, g_ref[ln_row:ln_row + 1, :], b_ref[ln_row:ln_row + 1, :])
        _store_segment(is_prompt, op_ref, os_ref, y)

    @pl.when(step < n_chunks)
    def _():
        w_gate, w_up, w_out = wg_ref[...].astype(BF16), wu_ref[...].astype(BF16), wo_ref[...].astype(BF16)
        wi_bf_ref[step, :, :fc] = w_gate
        wi_bf_ref[step, :, fc:] = w_up
        wo_bf_ref[step] = w_out
        x = xp_ref[...]
        part = _ffn_chunk(x.astype(BF16), w_gate, w_up, w_out)

        @pl.when(step == 0)
        def _():
            acc_ref[...] = part

        @pl.when(step > 0)
        def _():
            acc_ref[...] += part

        @pl.when(step == n_chunks - 1)
        def _():
            finish(x)

    @pl.when(step >= n_chunks)
    def _():
        x = jnp.where(is_prompt, xp_ref[...], xs_ref[...])
        xb = x.astype(BF16)
        for c in range(n_chunks):
            part = _ffn_chunk(xb, wi_bf_ref[c, :, :fc], wi_bf_ref[c, :, fc:], wo_bf_ref[c])
            if c == 0:
                acc_ref[...] = part
            else:
                acc_ref[...] += part
        finish(x)


def _ffn_ln(xp, xs, wi, wo, half, ln_g, ln_b, ln_row, alpha):
    d = xp.shape[1]
    d_ff = wo.shape[1]
    tm, fc = TOKEN_TILE, FF_CHUNK
    assert xp.shape[0] % tm == 0 and xs.shape[0] % tm == 0 and d_ff % fc == 0
    n_p, n_s = xp.shape[0] // tm, xs.shape[0] // tm
    n_chunks = d_ff // fc
    tile_of = lambda i: jnp.maximum(i - (n_chunks - 1), 0)
    chunk_of = lambda i: jnp.minimum(i, n_chunks - 1)
    seg_p, seg_s = _segment_specs(n_p, tm, d, tile_of)
    return pl.pallas_call(
        functools.partial(_ffn_ln_kernel, alpha, n_chunks, ln_row, n_p),
        grid=(n_chunks - 1 + n_p + n_s,),
        in_specs=[seg_p, seg_s,
                  pl.BlockSpec((None, d, fc), lambda i: (half, 0, chunk_of(i))),
                  pl.BlockSpec((None, d, fc), lambda i: (half, 0, n_chunks + chunk_of(i))),
                  pl.BlockSpec((None, fc, d), lambda i: (half, chunk_of(i), 0)),
                  _const_spec(ln_g.shape), _const_spec(ln_b.shape)],
        out_specs=[seg_p, seg_s],
        out_shape=[jax.ShapeDtypeStruct(xp.shape, F32), jax.ShapeDtypeStruct(xs.shape, F32)],
        scratch_shapes=[pltpu.VMEM((tm, d), F32),
                        pltpu.VMEM((n_chunks, d, 2 * fc), BF16),
                        pltpu.VMEM((n_chunks, fc, d), BF16)],
        compiler_params=_cparams(1),
        name="ffn_ln",
    )(xp, xs, wi, wi, wo, ln_g, ln_b)


def _rope_tables(pos, dim, theta_fn):
    half = dim // 2
    inv_freq = theta_fn(half)
    ang = pos.astype(F32)[:, None] * inv_freq[None, :]
    reps = LANES // half
    cos = jnp.tile(jnp.cos(ang), (1, reps))
    sign = jnp.tile(jnp.concatenate([-jnp.ones((half,), F32), jnp.ones((half,), F32)]), LANES // dim)
    sin = jnp.tile(jnp.sin(ang), (1, reps)) * sign[None, :]
    return cos, sin


def _attn_inv_freq(half):
    return 1.0 / (ROPE_THETA ** (jnp.arange(0, 2 * half, 2, dtype=F32) / (2 * half)))


def _ret_inv_freq(half):
    return 1.0 / (RET_THETA ** jnp.linspace(0.0, 1.0, half, dtype=F32))


def _rope_block(x, cos, sin, dim):
    half = dim // 2
    if dim == LANES:
        rot = pltpu.roll(x, half, 1)
    else:
        lane = lax.broadcasted_iota(jnp.int32, x.shape, 1)
        first_half = (lane % dim) < half
        rot = jnp.where(first_half, pltpu.roll(x, LANES - half, 1), pltpu.roll(x, half, 1))
    return x * cos + rot * sin


def _proj_rope_kernel(x_ref, w_ref, cosa_ref, sina_ref, cosr_ref, sinr_ref,
                      qa_ref, ka_ref, va_ref, qr_ref, kr_ref, vr_ref, gr_ref, ga_ref, gb_ref):
    xb = x_ref[...].astype(BF16)
    cosa, sina = cosa_ref[...], sina_ref[...]
    cosr, sinr = cosr_ref[...], sinr_ref[...]
    off = 0

    y = _dot(xb, w_ref[:, off:off + Q_W])
    for j in range(Q_W // LANES):
        qa_ref[:, j * LANES:(j + 1) * LANES] = _rope_block(
            y[:, j * LANES:(j + 1) * LANES], cosa, sina, HEAD_DIM).astype(qa_ref.dtype)
    off += Q_W
    y = _dot(xb, w_ref[:, off:off + 2 * KV_W])
    ka_ref[...] = _rope_block(y[:, :KV_W], cosa, sina, HEAD_DIM)
    va_ref[...] = y[:, KV_W:]
    off += 2 * KV_W
    y = _dot(xb, w_ref[:, off:off + RQ_W])
    for j in range(RQ_W // LANES):
        qr_ref[:, j * LANES:(j + 1) * LANES] = _rope_block(
            y[:, j * LANES:(j + 1) * LANES], cosr, sinr, RET_KEY_DIM).astype(qr_ref.dtype)
    off += RQ_W
    y = _dot(xb, w_ref[:, off:off + RQ_W])
    for j in range(RQ_W // LANES):
        kr = _rope_block(y[:, j * LANES:(j + 1) * LANES], cosr, sinr, RET_KEY_DIM)
        kr_ref[:, j * LANES:(j + 1) * LANES] = (kr * (RET_KEY_DIM ** -0.5)).astype(kr_ref.dtype)
    off += RQ_W
    for ref in (vr_ref, gr_ref, ga_ref, gb_ref):
        width = ref.shape[1]
        for c in range(width // 512):
            ref[:, c * 512:(c + 1) * 512] = _dot(
                xb, w_ref[:, off + c * 512:off + (c + 1) * 512]).astype(ref.dtype)
        off += width


def _proj_rope(x1, w_in, tables, table_blocks, mixer_dtype):
    t, d = x1.shape
    n_in = w_in.shape[1]
    d_model = (n_in - Q_W - 2 * KV_W - 2 * RQ_W - 2 * RV_W) // 2
    tm = TOKEN_TILE
    row = lambda i: (i, 0)
    tab = lambda i: (i % table_blocks, 0)
    widths = [(Q_W, mixer_dtype), (KV_W, F32), (KV_W, F32), (RQ_W, mixer_dtype), (RQ_W, mixer_dtype),
              (RV_W, mixer_dtype), (RV_W, mixer_dtype), (d_model, BF16), (d_model, BF16)]
    return pl.pallas_call(
        _proj_rope_kernel,
        grid=(t // tm,),
        in_specs=[pl.BlockSpec((tm, d), row), _const_spec(w_in.shape)]
                 + [pl.BlockSpec((tm, LANES), tab)] * 4,
        out_specs=[pl.BlockSpec((tm, w), row) for w, _ in widths],
        out_shape=[jax.ShapeDtypeStruct((t, w), dt) for w, dt in widths],
        compiler_params=_cparams(1),
        name="proj_rope",
    )(x1, w_in, *tables)


def _kv_lane_variants(x):
    lane = lax.broadcasted_iota(jnp.int32, x.shape, 1)
    lo = lane < HEAD_DIM
    swapped = pltpu.roll(x, HEAD_DIM, 1)
    zero = jnp.zeros_like(x)
    kv0 = (jnp.where(lo, x, zero).astype(BF16), jnp.where(lo, zero, swapped).astype(BF16))
    kv1 = (jnp.where(lo, swapped, zero).astype(BF16), jnp.where(lo, zero, x).astype(BF16))
    return kv0, kv1


def _attn_prompt_kernel(sinks_ref, q_ref, kc_ref, kp_ref, vc_ref, vp_ref, o_ref):
    n = pl.program_id(1)
    L = WINDOW
    k_full = jnp.concatenate([kp_ref[...], kc_ref[...]], axis=0)
    v_full = jnp.concatenate([vp_ref[...], vc_ref[...]], axis=0)
    k_var = _kv_lane_variants(k_full)
    v_var = _kv_lane_variants(v_full)
    qi = lax.broadcasted_iota(jnp.int32, (L, 2 * L), 0)
    mi = lax.broadcasted_iota(jnp.int32, (L, 2 * L), 1)
    band = (mi > qi) & (mi <= qi + L)
    scale = HEAD_DIM ** -0.5
    for i in range(q_ref.shape[0] // L):
        mask = band
        if i == 0:
            mask = band & ((mi >= L) | (n > 0))
        rows = slice(i * L, (i + 1) * L)
        keys = slice(i * L, i * L + 2 * L)
        for j in range(Q_W // LANES):
            kv = (2 * j) // GQA_GROUP
            q2 = q_ref[rows, j * LANES:(j + 1) * LANES] * scale
            out = None
            for par in range(2):
                s = _dot_nt(q2, k_var[kv][par][keys])
                s = jnp.where(mask, s, NEG_INF)
                sink = sinks_ref[2 * j + par]
                m = jnp.maximum(jnp.max(s, axis=-1, keepdims=True), sink)
                e = jnp.exp(s - m)
                denom = jnp.sum(e, axis=-1, keepdims=True) + jnp.exp(sink - m)
                r = _dot(e.astype(BF16), v_var[kv][par][keys]) / denom
                out = r if out is None else out + r
            o_ref[rows, j * LANES:(j + 1) * LANES] = out.astype(o_ref.dtype)


def _attn_prompt(qa, ka, va, sinks, batch, seq):
    tq = ATTN_Q_TILE
    nq = seq // tq
    per = tq // WINDOW
    cur = lambda b, n: (b * nq + n, 0)
    prev = lambda b, n: ((b * nq + n) * per - jnp.minimum(n, 1), 0)
    return pl.pallas_call(
        _attn_prompt_kernel,
        grid=(batch, nq),
        in_specs=[pl.BlockSpec(memory_space=pltpu.SMEM),
                  pl.BlockSpec((tq, Q_W), cur),
                  pl.BlockSpec((tq, KV_W), cur), pl.BlockSpec((WINDOW, KV_W), prev),
                  pl.BlockSpec((tq, KV_W), cur), pl.BlockSpec((WINDOW, KV_W), prev)],
        out_specs=pl.BlockSpec((tq, Q_W), cur),
        out_shape=jax.ShapeDtypeStruct(qa.shape, BF16),
        compiler_params=_cparams(2),
        name="attn_prompt",
    )(sinks, qa, ka, ka, va, va)


def _attn_sample_kernel(sink_rows_ref, q_ref, kn_ref, vn_ref, ck_ref, cv_ref,
                        o_ref, nk_ref, nv_ref):
    bt, t_new, _ = q_ref.shape
    w = ck_ref.shape[1]
    n_rows = N_Q_HEADS * t_new
    lane_lo = lax.broadcasted_iota(jnp.int32, (bt * t_new, LANES), 1) < HEAD_DIM
    r_i = lax.broadcasted_iota(jnp.int32, (n_rows, w + t_new), 0) % t_new
    m_i = lax.broadcasted_iota(jnp.int32, (n_rows, w + t_new), 1)
    mask = ((m_i < w) & (m_i > r_i + (w - WINDOW))) | ((m_i >= w) & (m_i - w <= r_i))
    sink = sink_rows_ref[...]
    scale = HEAD_DIM ** -0.5
    kc, kn = ck_ref[...], kn_ref[...]
    vc, vn = cv_ref[...], vn_ref[...]
    k_all = jnp.concatenate([kc, kn], axis=1).astype(BF16)
    v_all = jnp.concatenate([vc, vn], axis=1).astype(BF16)
    q = q_ref[...].reshape(bt * t_new, Q_W)
    pieces = []
    for h in range(N_Q_HEADS):
        q2 = q[:, (h // 2) * LANES:(h // 2 + 1) * LANES]
        src_lo = h % 2 == 0
        dst_lo = h // GQA_GROUP == 0
        if src_lo != dst_lo:
            q2 = pltpu.roll(q2, HEAD_DIM, 1)
        pieces.append(jnp.where(lane_lo == dst_lo, q2, 0.0).reshape(bt, t_new, LANES))
    q_rows = (jnp.concatenate(pieces, axis=1) * scale).astype(BF16)
    s = jnp.einsum("bqd,bkd->bqk", q_rows, k_all, preferred_element_type=F32)
    s = jnp.where(mask, s, NEG_INF)
    m = jnp.maximum(jnp.max(s, axis=-1, keepdims=True), sink)
    e = jnp.exp(s - m)
    denom = jnp.sum(e, axis=-1, keepdims=True) + jnp.exp(sink - m)
    pv = jnp.einsum("bqk,bkd->bqd", e.astype(BF16), v_all, preferred_element_type=F32) / denom
    for j in range(Q_W // LANES):
        out = None
        for par in range(2):
            h = 2 * j + par
            dst_lo = h // GQA_GROUP == 0
            piece = pv[:, h * t_new:(h + 1) * t_new, :].reshape(bt * t_new, LANES)
            piece = jnp.where(lane_lo == dst_lo, piece, 0.0)
            if dst_lo != (par == 0):
                piece = pltpu.roll(piece, HEAD_DIM, 1)
            out = piece if out is None else out + piece
        o_ref[:, :, j * LANES:(j + 1) * LANES] = out.reshape(bt, t_new, LANES).astype(o_ref.dtype)
    nk_ref[:, :w - t_new, :] = kc[:, t_new:, :]
    nk_ref[:, w - t_new:, :] = kn
    nv_ref[:, :w - t_new, :] = vc[:, t_new:, :]
    nv_ref[:, w - t_new:, :] = vn


def _attn_sample(qa, ka, va, sinks, cache_k, cache_v):
    db, t_new, _ = qa.shape
    w = cache_k.shape[1]
    bt = SAMPLE_BT
    sink_rows = jnp.repeat(sinks.astype(F32), t_new).reshape(N_Q_HEADS * t_new, 1)
    blk = lambda *s: pl.BlockSpec((bt,) + s, lambda i: (i, 0, 0))
    return pl.pallas_call(
        _attn_sample_kernel,
        grid=(db // bt,),
        in_specs=[_const_spec(sink_rows.shape), blk(t_new, Q_W), blk(t_new, KV_W), blk(t_new, KV_W),
                  blk(w, KV_W), blk(w, KV_W)],
        out_specs=[blk(t_new, Q_W), blk(w, KV_W), blk(w, KV_W)],
        out_shape=[jax.ShapeDtypeStruct((db, t_new, Q_W), F32),
                   jax.ShapeDtypeStruct((db, w, KV_W), F32),
                   jax.ShapeDtypeStruct((db, w, KV_W), F32)],
        compiler_params=_cparams(1),
        name="attn_sample",
    )(sink_rows, qa, ka, va, cache_k, cache_v)


def _ret_log_gamma(h):
    return math.log(1.0 - 2.0 ** (-5.0 - h))


def _ret_decays(chunk, h):
    lg = _ret_log_gamma(h)
    i = lax.broadcasted_iota(jnp.int32, (chunk, chunk), 0)
    j = lax.broadcasted_iota(jnp.int32, (chunk, chunk), 1)
    causal = i >= j
    diff = jnp.where(causal, i - j, 0).astype(F32)
    decay_mask = jnp.where(causal, jnp.exp(diff * lg), 0.0)
    col = lax.broadcasted_iota(jnp.int32, (chunk, 1), 0).astype(F32)
    q_decay = jnp.exp((col + 1.0) * lg)
    k_decay = jnp.exp((chunk - 1.0 - col) * lg)
    return decay_mask, q_decay, k_decay, math.exp(chunk * lg)


def _ret_chunk(q, k, v, g, state, decays):
    decay_mask, q_decay, k_decay, chunk_decay = decays
    q, v = q.astype(BF16), v.astype(BF16)
    inner = _dot_nt(q, k.astype(BF16)) * decay_mask
    o = _dot(inner.astype(BF16), v) + _dot(q, state.astype(BF16)) * q_decay
    kd_t = (k.astype(F32) * k_decay).T.astype(BF16)
    new_state = state * chunk_decay + _dot(kd_t, v)
    mu = jnp.mean(o, axis=-1, keepdims=True)
    oc = o - mu
    var = jnp.mean(oc * oc, axis=-1, keepdims=True)
    normed = oc * lax.rsqrt(var + GN_EPS)
    return normed * jax.nn.silu(g.astype(F32)), new_state


def _ret_prompt_kernel(q_ref, k_ref, v_ref, g_ref, o_ref, st_ref, state_ref):
    n = pl.program_id(1)
    C = RET_CHUNK

    @pl.when(n == 0)
    def _():
        state_ref[...] = jnp.zeros_like(state_ref)

    for h in range(N_RET_HEADS):
        decays = _ret_decays(C, h)
        state = state_ref[h]
        kcols = slice(h * RET_KEY_DIM, (h + 1) * RET_KEY_DIM)
        vcols = slice(h * RET_VAL_DIM, (h + 1) * RET_VAL_DIM)
        for c in range(q_ref.shape[0] // C):
            rows = slice(c * C, (c + 1) * C)
            out, state = _ret_chunk(q_ref[rows, kcols], k_ref[rows, kcols], v_ref[rows, vcols],
                                    g_ref[rows, vcols], state, decays)
            o_ref[rows, vcols] = out.astype(o_ref.dtype)
        state_ref[h] = state

    @pl.when(n == pl.num_programs(1) - 1)
    def _():
        st_ref[0] = state_ref[...]


def _ret_prompt(qr, kr, vr, gr, batch, seq):
    ts = RET_TILE
    ns = seq // ts
    row = lambda b, n: (b * ns + n, 0)
    return pl.pallas_call(
        _ret_prompt_kernel,
        grid=(batch, ns),
        in_specs=[pl.BlockSpec((ts, RQ_W), row), pl.BlockSpec((ts, RQ_W), row),
                  pl.BlockSpec((ts, RV_W), row), pl.BlockSpec((ts, RV_W), row)],
        out_specs=[pl.BlockSpec((ts, RV_W), row),
                   pl.BlockSpec((1, N_RET_HEADS, RET_KEY_DIM, RET_VAL_DIM), lambda b, n: (b, 0, 0, 0))],
        out_shape=[jax.ShapeDtypeStruct((batch * seq, RV_W), BF16),
                   jax.ShapeDtypeStruct((batch, N_RET_HEADS, RET_KEY_DIM, RET_VAL_DIM), F32)],
        scratch_shapes=[pltpu.VMEM((N_RET_HEADS, RET_KEY_DIM, RET_VAL_DIM), F32)],
        compiler_params=_cparams(2),
        name="ret_prompt",
    )(qr, kr, vr, gr)


def _ret_sample_kernel(q_ref, k_ref, v_ref, g_ref, st_ref, o_ref, nst_ref):
    bt, t_new, _ = q_ref.shape
    for h in range(N_RET_HEADS):
        decays = _ret_decays(t_new, h)
        kcols = slice(h * RET_KEY_DIM, (h + 1) * RET_KEY_DIM)
        vcols = slice(h * RET_VAL_DIM, (h + 1) * RET_VAL_DIM)
        for b in range(bt):
            out, state = _ret_chunk(q_ref[b, :, kcols], k_ref[b, :, kcols], v_ref[b, :, vcols],
                                    g_ref[b, :, vcols], st_ref[b, h], decays)
            o_ref[b, :, vcols] = out.astype(o_ref.dtype)
            nst_ref[b, h] = state


def _ret_sample(qr, kr, vr, gr, state):
    db, t_new, _ = qr.shape
    bt = SAMPLE_BT
    blk = lambda *s: pl.BlockSpec((bt,) + s, lambda i: (i,) + (0,) * len(s))
    st = blk(N_RET_HEADS, RET_KEY_DIM, RET_VAL_DIM)
    return pl.pallas_call(
        _ret_sample_kernel,
        grid=(db // bt,),
        in_specs=[blk(t_new, RQ_W), blk(t_new, RQ_W), blk(t_new, RV_W), blk(t_new, RV_W), st],
        out_specs=[blk(t_new, RV_W), st],
        out_shape=[jax.ShapeDtypeStruct((db, t_new, RV_W), F32),
                   jax.ShapeDtypeStruct(state.shape, state.dtype)],
        compiler_params=_cparams(1),
        name="ret_sample",
    )(qr, kr, vr, gr, state)


def _merge_ln_kernel(alpha, ln_row, x_ref, oa_ref, orr_ref, ga_ref, gb_ref, wba_ref, wbr_ref, wo_ref,
                     g_ref, b_ref, o_ref):
    ya = _dot(oa_ref[...].astype(BF16), wba_ref[...])
    yr = _dot(orr_ref[...].astype(BF16), wbr_ref[...])
    merged = (jax.nn.sigmoid(ga_ref[...].astype(F32)) * ya
              + jax.nn.sigmoid(gb_ref[...].astype(F32)) * yr)
    z = _dot(merged.astype(BF16), wo_ref[...])
    o_ref[...] = _layer_norm(alpha * x_ref[...] + z,
                             g_ref[ln_row:ln_row + 1, :], b_ref[ln_row:ln_row + 1, :])


def _merge_ln(x1, oa, orr, ga, gb, wba, wbr, wo, ln_g, ln_b, ln_row, alpha):
    t, d = x1.shape
    tm = TOKEN_TILE
    row = lambda i: (i, 0)
    return pl.pallas_call(
        functools.partial(_merge_ln_kernel, alpha, ln_row),
        grid=(t // tm,),
        in_specs=[pl.BlockSpec((tm, d), row), pl.BlockSpec((tm, Q_W), row), pl.BlockSpec((tm, RV_W), row),
                  pl.BlockSpec((tm, d), row), pl.BlockSpec((tm, d), row),
                  _const_spec(wba.shape), _const_spec(wbr.shape), _const_spec(wo.shape),
                  _const_spec(ln_g.shape), _const_spec(ln_b.shape)],
        out_specs=pl.BlockSpec((tm, d), row),
        out_shape=jax.ShapeDtypeStruct((t, d), F32),
        compiler_params=_cparams(1),
        name="merge_ln",
    )(x1, oa, orr, ga, gb, wba, wbr, wo, ln_g, ln_b)


def _rope_tables_for(pos):
    tables = _rope_tables(pos, HEAD_DIM, _attn_inv_freq) + _rope_tables(pos, RET_KEY_DIM, _ret_inv_freq)
    period = pos.shape[0]
    if period < TOKEN_TILE:
        tables = tuple(jnp.tile(tb, (TOKEN_TILE // period, 1)) for tb in tables)
        period = TOKEN_TILE
    return tables, period // TOKEN_TILE


def kernel(x_prompt, x_sample, cache_k, cache_v, state_ret, ln_g, ln_b, ffn_wi, ffn_wo, w_in,
           attn_sinks, w_br_attn, w_br_ret, w_o):
    B, S, D = x_prompt.shape
    DB, T, _ = x_sample.shape
    depth = ln_g.shape[0]
    alpha = (2.0 * depth) ** 0.25
    W = cache_k.shape[2]
    tables_p, tab_blocks_p = _rope_tables_for(jnp.arange(S, dtype=jnp.int32))
    tables_s, tab_blocks_s = _rope_tables_for(PAST_LEN + jnp.arange(T, dtype=jnp.int32))

    y_p = x_prompt.reshape(B * S, D)
    y_s = x_sample.reshape(DB * T, D)
    outs = [[] for _ in range(6)]
    for l in range(depth):
        wi, wo = ffn_wi[l], ffn_wo[l]
        w_in_l = w_in[l].astype(BF16)
        wba, wbr, w_o_l = w_br_attn[l].astype(BF16), w_br_ret[l].astype(BF16), w_o[l].astype(BF16)
        sinks = attn_sinks[l].astype(F32)
        g_l, b_l = ln_g[l], ln_b[l]

        x1_p, x1_s = _ffn_ln(y_p, y_s, wi, wo, 0, g_l, b_l, 0, alpha)
        qa_p, ka_p, va_p, qr_p, kr_p, vr_p, gr_p, ga_p, gb_p = _proj_rope(
            x1_p, w_in_l, tables_p, tab_blocks_p, BF16)
        qa_s, ka_s, va_s, qr_s, kr_s, vr_s, gr_s, ga_s, gb_s = _proj_rope(
            x1_s, w_in_l, tables_s, tab_blocks_s, F32)

        oa_p = _attn_prompt(qa_p, ka_p, va_p, sinks, B, S)
        or_p, st_p = _ret_prompt(qr_p, kr_p, vr_p, gr_p, B, S)
        w = min(WINDOW, S)
        nk_p = ka_p.reshape(B, S, N_KV_HEADS, HEAD_DIM)[:, S - w:]
        nv_p = va_p.reshape(B, S, N_KV_HEADS, HEAD_DIM)[:, S - w:]

        r3 = lambda a: a.reshape(DB, T, a.shape[-1])
        oa_s, nk_s, nv_s = _attn_sample(r3(qa_s), r3(ka_s), r3(va_s), sinks,
                                        cache_k[l].reshape(DB, W, KV_W), cache_v[l].reshape(DB, W, KV_W))
        or_s, st_s = _ret_sample(r3(qr_s), r3(kr_s), r3(vr_s), r3(gr_s), state_ret[l])
        oa_s, or_s = oa_s.reshape(DB * T, Q_W), or_s.reshape(DB * T, RV_W)
        shape5 = (DB, W, N_KV_HEADS, HEAD_DIM)

        x2_p = _merge_ln(x1_p, oa_p, or_p, ga_p, gb_p, wba, wbr, w_o_l, g_l, b_l, 1, alpha)
        x2_s = _merge_ln(x1_s, oa_s, or_s, ga_s, gb_s, wba, wbr, w_o_l, g_l, b_l, 1, alpha)
        y_p, y_s = _ffn_ln(x2_p, x2_s, wi, wo, 1, g_l, b_l, 2, alpha)
        for lst, val in zip(outs, (nk_p, nv_p, st_p, nk_s.reshape(shape5), nv_s.reshape(shape5), st_s)):
            lst.append(val)
    return (y_p.reshape(B, S, D), y_s.reshape(DB, T, D)) + tuple(jnp.stack(o) for o in outs)
```

```python
import functools
import math

import jax
import jax.numpy as jnp
from jax import lax
from jax.experimental import pallas as pl
from jax.experimental.pallas import tpu as pltpu

F32 = jnp.float32
BF16 = jnp.bfloat16

PAST_LEN = 8192
N_Q_HEADS = 8
N_KV_HEADS = 2
HEAD_DIM = 64
GQA_GROUP = N_Q_HEADS // N_KV_HEADS
WINDOW = 128
ROPE_THETA = 10000.0
N_RET_HEADS = 4
RET_KEY_DIM = 128
RET_VAL_DIM = 256
RET_CHUNK = 128
RET_THETA = 10000.0
LN_EPS = 1e-5
GN_EPS = 1e-6
NEG_INF = -1e30

Q_W = N_Q_HEADS * HEAD_DIM
KV_W = N_KV_HEADS * HEAD_DIM
RQ_W = N_RET_HEADS * RET_KEY_DIM
RV_W = N_RET_HEADS * RET_VAL_DIM

LANES = 128
VMEM_LIMIT_BYTES = 56 * 1024 * 1024

TOKEN_TILE = 512
FF_CHUNK = 256
ATTN_Q_TILE = 512
RET_TILE = 512
SAMPLE_BT = 8


def _dot(a, b):
    return jnp.dot(a, b, preferred_element_type=F32)


def _dot_nt(a, b):
    return lax.dot_general(a, b, (((1,), (1,)), ((), ())), preferred_element_type=F32)


def _layer_norm(y, g, b):
    mu = jnp.mean(y, axis=-1, keepdims=True)
    yc = y - mu
    var = jnp.mean(yc * yc, axis=-1, keepdims=True)
    return yc * lax.rsqrt(var + LN_EPS) * g + b


def _cparams(n_axes, semantics=None):
    return pltpu.CompilerParams(
        dimension_semantics=semantics or ("arbitrary",) * n_axes,
        vmem_limit_bytes=VMEM_LIMIT_BYTES)


def _const_spec(shape):
    nd = len(shape)
    return pl.BlockSpec(shape, lambda *_: (0,) * nd, pipeline_mode=pl.Buffered(1))


def _segment_specs(n_prompt_tiles, tm, width, tile_of):
    prompt = pl.BlockSpec((tm, width), lambda i: (jnp.minimum(tile_of(i), n_prompt_tiles - 1), 0))
    sample = pl.BlockSpec((tm, width), lambda i: (jnp.maximum(tile_of(i) - n_prompt_tiles, 0), 0))
    return prompt, sample


def _store_segment(is_prompt, prompt_ref, sample_ref, value):
    @pl.when(is_prompt)
    def _():
        prompt_ref[...] = value.astype(prompt_ref.dtype)

    @pl.when(jnp.logical_not(is_prompt))
    def _():
        sample_ref[...] = value.astype(sample_ref.dtype)


def _ffn_chunk(xb, w_gate, w_up, w_out):
    act = (jax.nn.silu(_dot(xb, w_gate)) * _dot(xb, w_up)).astype(BF16)
    return _dot(act, w_out)


def _ffn_ln_kernel(alpha, n_chunks, ln_row, n_prompt_tiles,
                   xp_ref, xs_ref, wg_ref, wu_ref, wo_ref, g_ref, b_ref, op_ref, os_ref,
                   acc_ref, wi_bf_ref, wo_bf_ref):
    step = pl.program_id(0)
    tile = jnp.maximum(step - (n_chunks - 1), 0)
    is_prompt = tile < n_prompt_tiles
    fc = FF_CHUNK

    def finish(x):
        y = alpha * x + 0.5 * acc_ref[...]
        y = _layer_norm(y, g_ref[ln_row:ln_row + 1, :], b_ref[ln_row:ln_row + 1, :])
        _store_segment(is_prompt, op_ref, os_ref, y)

    @pl.when(step < n_chunks)
    def _():
        w_gate, w_up, w_out = wg_ref[...].astype(BF16), wu_ref[...].astype(BF16), wo_ref[...].astype(BF16)
        wi_bf_ref[step, :, :fc] = w_gate
        wi_bf_ref[step, :, fc:] = w_up
        wo_bf_ref[step] = w_out
        x = xp_ref[...]
        part = _ffn_chunk(x.astype(BF16), w_gate, w_up, w_out)

        @pl.when(step == 0)
        def _():
            acc_ref[...] = part

        @pl.when(step > 0)
        def _():
            acc_ref[...] += part

        @pl.when(step == n_chunks - 1)
        def _():
            finish(x)

    @pl.when(step >= n_chunks)
    def _():
        x = jnp.where(is_prompt, xp_ref[...], xs_ref[...])
        xb = x.astype(BF16)
        for c in range(n_chunks):
            part = _ffn_chunk(xb, wi_bf_ref[c, :, :fc], wi_bf_ref[c, :, fc:], wo_bf_ref[c])
            if c == 0:
                acc_ref[...] = part
            else:
                acc_ref[...] += part
        finish(x)


def _ffn_ln(xp, xs, wi, wo, half, ln_g, ln_b, ln_row, alpha):
    d = xp.shape[1]
    d_ff = wo.shape[1]
    tm, fc = TOKEN_TILE, FF_CHUNK
    assert xp.shape[0] % tm == 0 and xs.shape[0] % tm == 0 and d_ff % fc == 0
    n_p, n_s = xp.shape[0] // tm, xs.shape[0] // tm
    n_chunks = d_ff // fc
    tile_of = lambda i: jnp.maximum(i - (n_chunks - 1), 0)
    chunk_of = lambda i: jnp.minimum(i, n_chunks - 1)
    seg_p, seg_s = _segment_specs(n_p, tm, d, tile_of)
    return pl.pallas_call(
        functools.partial(_ffn_ln_kernel, alpha, n_chunks, ln_row, n_p),
        grid=(n_chunks - 1 + n_p + n_s,),
        in_specs=[seg_p, seg_s,
                  pl.BlockSpec((None, d, fc), lambda i: (half, 0, chunk_of(i))),
                  pl.BlockSpec((None, d, fc), lambda i: (half, 0, n_chunks + chunk_of(i))),
                  pl.BlockSpec((None, fc, d), lambda i: (half, chunk_of(i), 0)),
                  _const_spec(ln_g.shape), _const_spec(ln_b.shape)],
        out_specs=[seg_p, seg_s],
        out_shape=[jax.ShapeDtypeStruct(xp.shape, F32), jax.ShapeDtypeStruct(xs.shape, F32)],
        scratch_shapes=[pltpu.VMEM((tm, d), F32),
                        pltpu.VMEM((n_chunks, d, 2 * fc), BF16),
                        pltpu.VMEM((n_chunks, fc, d), BF16)],
        compiler_params=_cparams(1),
        name="ffn_ln",
    )(xp, xs, wi, wi, wo, ln_g, ln_b)


def _rope_tables(pos, dim, theta_fn):
    half = dim // 2
    inv_freq = theta_fn(half)
    ang = pos.astype(F32)[:, None] * inv_freq[None, :]
    reps = LANES // half
    cos = jnp.tile(jnp.cos(ang), (1, reps))
    sign = jnp.tile(jnp.concatenate([-jnp.ones((half,), F32), jnp.ones((half,), F32)]), LANES // dim)
    sin = jnp.tile(jnp.sin(ang), (1, reps)) * sign[None, :]
    return cos, sin


def _attn_inv_freq(half):
    return 1.0 / (ROPE_THETA ** (jnp.arange(0, 2 * half, 2, dtype=F32) / (2 * half)))


def _ret_inv_freq(half):
    return 1.0 / (RET_THETA ** jnp.linspace(0.0, 1.0, half, dtype=F32))


def _rope_block(x, cos, sin, dim):
    half = dim // 2
    if dim == LANES:
        rot = pltpu.roll(x, half, 1)
    else:
        lane = lax.broadcasted_iota(jnp.int32, x.shape, 1)
        first_half = (lane % dim) < half
        rot = jnp.where(first_half, pltpu.roll(x, LANES - half, 1), pltpu.roll(x, half, 1))
    return x * cos + rot * sin


def _proj_rope_kernel(x_ref, w_ref, cosa_ref, sina_ref, cosr_ref, sinr_ref,
                      qa_ref, ka_ref, va_ref, qr_ref, kr_ref, vr_ref, gr_ref, ga_ref, gb_ref):
    xb = x_ref[...].astype(BF16)
    cosa, sina = cosa_ref[...], sina_ref[...]
    cosr, sinr = cosr_ref[...], sinr_ref[...]
    off = 0

    y = _dot(xb, w_ref[:, off:off + Q_W])
    for j in range(Q_W // LANES):
        qa_ref[:, j * LANES:(j + 1) * LANES] = _rope_block(
            y[:, j * LANES:(j + 1) * LANES], cosa, sina, HEAD_DIM).astype(qa_ref.dtype)
    off += Q_W
    y = _dot(xb, w_ref[:, off:off + 2 * KV_W])
    ka_ref[...] = _rope_block(y[:, :KV_W], cosa, sina, HEAD_DIM)
    va_ref[...] = y[:, KV_W:]
    off += 2 * KV_W
    y = _dot(xb, w_ref[:, off:off + RQ_W])
    for j in range(RQ_W // LANES):
        qr_ref[:, j * LANES:(j + 1) * LANES] = _rope_block(
            y[:, j * LANES:(j + 1) * LANES], cosr, sinr, RET_KEY_DIM).astype(qr_ref.dtype)
    off += RQ_W
    y = _dot(xb, w_ref[:, off:off + RQ_W])
    for j in range(RQ_W // LANES):
        kr = _rope_block(y[:, j * LANES:(j + 1) * LANES], cosr, sinr, RET_KEY_DIM)
        kr_ref[:, j * LANES:(j + 1) * LANES] = (kr * (RET_KEY_DIM ** -0.5)).astype(kr_ref.dtype)
    off += RQ_W
    for ref in (vr_ref, gr_ref, ga_ref, gb_ref):
        width = ref.shape[1]
        for c in range(width // 512):
            ref[:, c * 512:(c + 1) * 512] = _dot(
                xb, w_ref[:, off + c * 512:off + (c + 1) * 512]).astype(ref.dtype)
        off += width


def _proj_rope(x1, w_in, tables, table_blocks, mixer_dtype):
    t, d = x1.shape
    n_in = w_in.shape[1]
    d_model = (n_in - Q_W - 2 * KV_W - 2 * RQ_W - 2 * RV_W) // 2
    tm = TOKEN_TILE
    row = lambda i: (i, 0)
    tab = lambda i: (i % table_blocks, 0)
    widths = [(Q_W, mixer_dtype), (KV_W, F32), (KV_W, F32), (RQ_W, mixer_dtype), (RQ_W, mixer_dtype),
              (RV_W, mixer_dtype), (RV_W, mixer_dtype), (d_model, BF16), (d_model, BF16)]
    return pl.pallas_call(
        _proj_rope_kernel,
        grid=(t // tm,),
        in_specs=[pl.BlockSpec((tm, d), row), _const_spec(w_in.shape)]
                 + [pl.BlockSpec((tm, LANES), tab)] * 4,
        out_specs=[pl.BlockSpec((tm, w), row) for w, _ in widths],
        out_shape=[jax.ShapeDtypeStruct((t, w), dt) for w, dt in widths],
        compiler_params=_cparams(1),
        name="proj_rope",
    )(x1, w_in, *tables)


def _kv_lane_variants(x):
    lane = lax.broadcasted_iota(jnp.int32, x.shape, 1)
    lo = lane < HEAD_DIM
    swapped = pltpu.roll(x, HEAD_DIM, 1)
    zero = jnp.zeros_like(x)
    kv0 = (jnp.where(lo, x, zero).astype(BF16), jnp.where(lo, zero, swapped).astype(BF16))
    kv1 = (jnp.where(lo, swapped, zero).astype(BF16), jnp.where(lo, zero, x).astype(BF16))
    return kv0, kv1


def _attn_prompt_kernel(sinks_ref, q_ref, kc_ref, kp_ref, vc_ref, vp_ref, o_ref):
    n = pl.program_id(1)
    L = WINDOW
    k_full = jnp.concatenate([kp_ref[...], kc_ref[...]], axis=0)
    v_full = jnp.concatenate([vp_ref[...], vc_ref[...]], axis=0)
    k_var = _kv_lane_variants(k_full)
    v_var = _kv_lane_variants(v_full)
    own = (lax.broadcasted_iota(jnp.int32, (L, L), 1)
           <= lax.broadcasted_iota(jnp.int32, (L, L), 0))
    scale = HEAD_DIM ** -0.5
    units = [(i, j, par) for i in range(q_ref.shape[0] // L) for j in range(Q_W // LANES) for par in range(2)]

    def scores(unit):
        i, j, par = unit
        q2 = q_ref[i * L:(i + 1) * L, j * LANES:(j + 1) * LANES] * scale
        return _dot_nt(q2, k_var[(2 * j) // GQA_GROUP][par][i * L:(i + 2) * L])

    s_next = scores(units[0])
    out = None
    for idx, (i, j, par) in enumerate(units):
        s = s_next
        if idx + 1 < len(units):
            s_next = scores(units[idx + 1])
        s_prev = s[:, :L]
        if i == 0:
            s_prev = jnp.where(n > 0, s_prev, NEG_INF)
        sw = jnp.where(own, s[:, L:], s_prev)
        sink = sinks_ref[2 * j + par]
        m = jnp.maximum(jnp.max(sw, axis=-1, keepdims=True), sink)
        e = jnp.exp(sw - m)
        denom = jnp.sum(e, axis=-1, keepdims=True) + jnp.exp(sink - m)
        e_band = jnp.concatenate([jnp.where(own, 0.0, e), jnp.where(own, e, 0.0)], axis=1)
        r = _dot(e_band.astype(BF16), v_var[(2 * j) // GQA_GROUP][par][i * L:(i + 2) * L]) / denom
        if par == 0:
            out = r
        else:
            o_ref[i * L:(i + 1) * L, j * LANES:(j + 1) * LANES] = (out + r).astype(o_ref.dtype)


def _attn_prompt(qa, ka, va, sinks, batch, seq):
    tq = ATTN_Q_TILE
    nq = seq // tq
    per = tq // WINDOW
    cur = lambda b, n: (b * nq + n, 0)
    prev = lambda b, n: ((b * nq + n) * per - jnp.minimum(n, 1), 0)
    return pl.pallas_call(
        _attn_prompt_kernel,
        grid=(batch, nq),
        in_specs=[pl.BlockSpec(memory_space=pltpu.SMEM),
                  pl.BlockSpec((tq, Q_W), cur),
                  pl.BlockSpec((tq, KV_W), cur), pl.BlockSpec((WINDOW, KV_W), prev),
                  pl.BlockSpec((tq, KV_W), cur), pl.BlockSpec((WINDOW, KV_W), prev)],
        out_specs=pl.BlockSpec((tq, Q_W), cur),
        out_shape=jax.ShapeDtypeStruct(qa.shape, BF16),
        compiler_params=_cparams(2),
        name="attn_prompt",
    )(sinks, qa, ka, ka, va, va)


def _attn_sample_kernel(sink_rows_ref, q_ref, kn_ref, vn_ref, ck_ref, cv_ref,
                        o_ref, nk_ref, nv_ref):
    bt, t_new, _ = q_ref.shape
    w = ck_ref.shape[1]
    n_rows = N_Q_HEADS * t_new
    lane_lo = lax.broadcasted_iota(jnp.int32, (bt * t_new, LANES), 1) < HEAD_DIM
    r_i = lax.broadcasted_iota(jnp.int32, (n_rows, w + t_new), 0) % t_new
    m_i = lax.broadcasted_iota(jnp.int32, (n_rows, w + t_new), 1)
    mask = ((m_i < w) & (m_i > r_i + (w - WINDOW))) | ((m_i >= w) & (m_i - w <= r_i))
    sink = sink_rows_ref[...]
    scale = HEAD_DIM ** -0.5
    kc, kn = ck_ref[...], kn_ref[...]
    vc, vn = cv_ref[...], vn_ref[...]
    k_all = jnp.concatenate([kc, kn], axis=1).astype(BF16)
    v_all = jnp.concatenate([vc, vn], axis=1).astype(BF16)
    q = q_ref[...].reshape(bt * t_new, Q_W)
    pieces = []
    for h in range(N_Q_HEADS):
        q2 = q[:, (h // 2) * LANES:(h // 2 + 1) * LANES]
        src_lo = h % 2 == 0
        dst_lo = h // GQA_GROUP == 0
        if src_lo != dst_lo:
            q2 = pltpu.roll(q2, HEAD_DIM, 1)
        pieces.append(jnp.where(lane_lo == dst_lo, q2, 0.0).reshape(bt, t_new, LANES))
    q_rows = (jnp.concatenate(pieces, axis=1) * scale).astype(BF16)
    s = jnp.einsum("bqd,bkd->bqk", q_rows, k_all, preferred_element_type=F32)
    s = jnp.where(mask, s, NEG_INF)
    m = jnp.maximum(jnp.max(s, axis=-1, keepdims=True), sink)
    e = jnp.exp(s - m)
    denom = jnp.sum(e, axis=-1, keepdims=True) + jnp.exp(sink - m)
    pv = jnp.einsum("bqk,bkd->bqd", e.astype(BF16), v_all, preferred_element_type=F32) / denom
    for j in range(Q_W // LANES):
        out = None
        for par in range(2):
            h = 2 * j + par
            dst_lo = h // GQA_GROUP == 0
            piece = pv[:, h * t_new:(h + 1) * t_new, :].reshape(bt * t_new, LANES)
            piece = jnp.where(lane_lo == dst_lo, piece, 0.0)
            if dst_lo != (par == 0):
                piece = pltpu.roll(piece, HEAD_DIM, 1)
            out = piece if out is None else out + piece
        o_ref[:, :, j * LANES:(j + 1) * LANES] = out.reshape(bt, t_new, LANES).astype(o_ref.dtype)
    nk_ref[:, :w - t_new, :] = kc[:, t_new:, :]
    nk_ref[:, w - t_new:, :] = kn
    nv_ref[:, :w - t_new, :] = vc[:, t_new:, :]
    nv_ref[:, w - t_new:, :] = vn


def _attn_sample(qa, ka, va, sinks, cache_k, cache_v):
    db, t_new, _ = qa.shape
    w = cache_k.shape[1]
    bt = SAMPLE_BT
    sink_rows = jnp.repeat(sinks.astype(F32), t_new).reshape(N_Q_HEADS * t_new, 1)
    blk = lambda *s: pl.BlockSpec((bt,) + s, lambda i: (i, 0, 0))
    return pl.pallas_call(
        _attn_sample_kernel,
        grid=(db // bt,),
        in_specs=[_const_spec(sink_rows.shape), blk(t_new, Q_W), blk(t_new, KV_W), blk(t_new, KV_W),
                  blk(w, KV_W), blk(w, KV_W)],
        out_specs=[blk(t_new, Q_W), blk(w, KV_W), blk(w, KV_W)],
        out_shape=[jax.ShapeDtypeStruct((db, t_new, Q_W), F32),
                   jax.ShapeDtypeStruct((db, w, KV_W), F32),
                   jax.ShapeDtypeStruct((db, w, KV_W), F32)],
        compiler_params=_cparams(1),
        name="attn_sample",
    )(sink_rows, qa, ka, va, cache_k, cache_v)


def _ret_log_gamma(h):
    return math.log(1.0 - 2.0 ** (-5.0 - h))


def _ret_decays(chunk, h):
    lg = _ret_log_gamma(h)
    i = lax.broadcasted_iota(jnp.int32, (chunk, chunk), 0)
    j = lax.broadcasted_iota(jnp.int32, (chunk, chunk), 1)
    rel_mask = jnp.where(i >= j, jnp.exp((j + 1).astype(F32) * -lg), 0.0)
    col = lax.broadcasted_iota(jnp.int32, (chunk, 1), 0).astype(F32)
    q_decay = jnp.exp((col + 1.0) * lg)
    k_decay = jnp.exp((chunk - 1.0 - col) * lg)
    return rel_mask, q_decay, k_decay, math.exp(chunk * lg)


def _ret_chunk_all_heads(q, k, v, g, states, decays):
    heads = range(N_RET_HEADS)
    batched = q.ndim == 3
    mm = (lambda a, b: jnp.einsum("bmk,bkn->bmn", a, b, preferred_element_type=F32)) if batched else _dot
    mm_nt = (lambda a, b: jnp.einsum("bmk,bnk->bmn", a, b, preferred_element_type=F32)) if batched else _dot_nt
    qh = [q[..., h * RET_KEY_DIM:(h + 1) * RET_KEY_DIM] for h in heads]
    kh = [k[..., h * RET_KEY_DIM:(h + 1) * RET_KEY_DIM] for h in heads]
    vh = [v[..., h * RET_VAL_DIM:(h + 1) * RET_VAL_DIM] for h in heads]
    qb = [x.astype(BF16) for x in qh]
    vb = [x.astype(BF16) for x in vh]
    inner = [mm_nt(qb[h], kh[h].astype(BF16)) * decays[h][0] for h in heads]
    if batched:
        lhs = [jnp.concatenate([qh[h], inner[h]], axis=-1).astype(BF16) for h in heads]
        rhs = [jnp.concatenate([states[h], vh[h]], axis=-2).astype(BF16) for h in heads]
    else:
        lhs = [jnp.concatenate([qb[h], inner[h].astype(BF16)], axis=-1) for h in heads]
        rhs = [jnp.concatenate([states[h].astype(BF16), vb[h]], axis=-2) for h in heads]
    o = [mm(lhs[h], rhs[h]) * decays[h][1] for h in heads]
    kd_t = [jnp.swapaxes(kh[h].astype(F32) * decays[h][2], -1, -2).astype(BF16) for h in heads]
    new_states = [states[h] * decays[h][3] + mm(kd_t[h], vb[h]) for h in heads]
    outs = []
    for h in heads:
        mu = jnp.mean(o[h], axis=-1, keepdims=True)
        oc = o[h] - mu
        var = jnp.mean(oc * oc, axis=-1, keepdims=True)
        gate = g[..., h * RET_VAL_DIM:(h + 1) * RET_VAL_DIM].astype(F32)
        outs.append(oc * lax.rsqrt(var + GN_EPS) * jax.nn.silu(gate))
    return jnp.concatenate(outs, axis=-1), new_states


def _ret_prompt_kernel(q_ref, k_ref, v_ref, g_ref, o_ref, st_ref, state_ref):
    n = pl.program_id(1)
    C = RET_CHUNK

    @pl.when(n == 0)
    def _():
        state_ref[...] = jnp.zeros_like(state_ref)

    decays = [_ret_decays(C, h) for h in range(N_RET_HEADS)]
    states = [state_ref[h] for h in range(N_RET_HEADS)]
    for c in range(q_ref.shape[0] // C):
        rows = slice(c * C, (c + 1) * C)
        out, states = _ret_chunk_all_heads(q_ref[rows, :], k_ref[rows, :], v_ref[rows, :], g_ref[rows, :],
                                           states, decays)
        o_ref[rows, :] = out.astype(o_ref.dtype)
    for h in range(N_RET_HEADS):
        state_ref[h] = states[h]

    @pl.when(n == pl.num_programs(1) - 1)
    def _():
        st_ref[0] = state_ref[...]


def _ret_prompt(qr, kr, vr, gr, batch, seq):
    ts = RET_TILE
    ns = seq // ts
    row = lambda b, n: (b * ns + n, 0)
    return pl.pallas_call(
        _ret_prompt_kernel,
        grid=(batch, ns),
        in_specs=[pl.BlockSpec((ts, RQ_W), row), pl.BlockSpec((ts, RQ_W), row),
                  pl.BlockSpec((ts, RV_W), row), pl.BlockSpec((ts, RV_W), row)],
        out_specs=[pl.BlockSpec((ts, RV_W), row),
                   pl.BlockSpec((1, N_RET_HEADS, RET_KEY_DIM, RET_VAL_DIM), lambda b, n: (b, 0, 0, 0))],
        out_shape=[jax.ShapeDtypeStruct((batch * seq, RV_W), BF16),
                   jax.ShapeDtypeStruct((batch, N_RET_HEADS, RET_KEY_DIM, RET_VAL_DIM), F32)],
        scratch_shapes=[pltpu.VMEM((N_RET_HEADS, RET_KEY_DIM, RET_VAL_DIM), F32)],
        compiler_params=_cparams(2),
        name="ret_prompt",
    )(qr, kr, vr, gr)


def _ret_sample_kernel(q_ref, k_ref, v_ref, g_ref, st_ref, o_ref, nst_ref):
    t_new = q_ref.shape[1]
    decays = [_ret_decays(t_new, h) for h in range(N_RET_HEADS)]
    states = [st_ref[:, h] for h in range(N_RET_HEADS)]
    out, states = _ret_chunk_all_heads(q_ref[...], k_ref[...], v_ref[...], g_ref[...], states, decays)
    o_ref[...] = out.astype(o_ref.dtype)
    for h in range(N_RET_HEADS):
        nst_ref[:, h] = states[h]


def _ret_sample(qr, kr, vr, gr, state):
    db, t_new, _ = qr.shape
    bt = SAMPLE_BT
    blk = lambda *s: pl.BlockSpec((bt,) + s, lambda i: (i,) + (0,) * len(s))
    st = blk(N_RET_HEADS, RET_KEY_DIM, RET_VAL_DIM)
    return pl.pallas_call(
        _ret_sample_kernel,
        grid=(db // bt,),
        in_specs=[blk(t_new, RQ_W), blk(t_new, RQ_W), blk(t_new, RV_W), blk(t_new, RV_W), st],
        out_specs=[blk(t_new, RV_W), st],
        out_shape=[jax.ShapeDtypeStruct((db, t_new, RV_W), F32),
                   jax.ShapeDtypeStruct(state.shape, state.dtype)],
        compiler_params=_cparams(1),
        name="ret_sample",
    )(qr, kr, vr, gr, state)


def _merge_ln_kernel(alpha, ln_row, x_ref, oa_ref, orr_ref, ga_ref, gb_ref, wba_ref, wbr_ref, wo_ref,
                     g_ref, b_ref, o_ref):
    ya = _dot(oa_ref[...].astype(BF16), wba_ref[...])
    yr = _dot(orr_ref[...].astype(BF16), wbr_ref[...])
    merged = (jax.nn.sigmoid(ga_ref[...].astype(F32)) * ya
              + jax.nn.sigmoid(gb_ref[...].astype(F32)) * yr)
    z = _dot(merged.astype(BF16), wo_ref[...])
    o_ref[...] = _layer_norm(alpha * x_ref[...] + z,
                             g_ref[ln_row:ln_row + 1, :], b_ref[ln_row:ln_row + 1, :])


def _merge_ln(x1, oa, orr, ga, gb, wba, wbr, wo, ln_g, ln_b, ln_row, alpha):
    t, d = x1.shape
    tm = TOKEN_TILE
    row = lambda i: (i, 0)
    return pl.pallas_call(
        functools.partial(_merge_ln_kernel, alpha, ln_row),
        grid=(t // tm,),
        in_specs=[pl.BlockSpec((tm, d), row), pl.BlockSpec((tm, Q_W), row), pl.BlockSpec((tm, RV_W), row),
                  pl.BlockSpec((tm, d), row), pl.BlockSpec((tm, d), row),
                  _const_spec(wba.shape), _const_spec(wbr.shape), _const_spec(wo.shape),
                  _const_spec(ln_g.shape), _const_spec(ln_b.shape)],
        out_specs=pl.BlockSpec((tm, d), row),
        out_shape=jax.ShapeDtypeStruct((t, d), F32),
        compiler_params=_cparams(1),
        name="merge_ln",
    )(x1, oa, orr, ga, gb, wba, wbr, wo, ln_g, ln_b)


def _rope_tables_for(pos):
    tables = _rope_tables(pos, HEAD_DIM, _attn_inv_freq) + _rope_tables(pos, RET_KEY_DIM, _ret_inv_freq)
    period = pos.shape[0]
    if period < TOKEN_TILE:
        tables = tuple(jnp.tile(tb, (TOKEN_TILE // period, 1)) for tb in tables)
        period = TOKEN_TILE
    return tables, period // TOKEN_TILE


def kernel(x_prompt, x_sample, cache_k, cache_v, state_ret, ln_g, ln_b, ffn_wi, ffn_wo, w_in,
           attn_sinks, w_br_attn, w_br_ret, w_o):
    B, S, D = x_prompt.shape
    DB, T, _ = x_sample.shape
    depth = ln_g.shape[0]
    alpha = (2.0 * depth) ** 0.25
    W = cache_k.shape[2]
    tables_p, tab_blocks_p = _rope_tables_for(jnp.arange(S, dtype=jnp.int32))
    tables_s, tab_blocks_s = _rope_tables_for(PAST_LEN + jnp.arange(T, dtype=jnp.int32))

    y_p = x_prompt.reshape(B * S, D)
    y_s = x_sample.reshape(DB * T, D)
    outs = [[] for _ in range(6)]
    for l in range(depth):
        wi, wo = ffn_wi[l], ffn_wo[l]
        w_in_l = w_in[l].astype(BF16)
        wba, wbr, w_o_l = w_br_attn[l].astype(BF16), w_br_ret[l].astype(BF16), w_o[l].astype(BF16)
        sinks = attn_sinks[l].astype(F32)
        g_l, b_l = ln_g[l], ln_b[l]

        x1_p, x1_s = _ffn_ln(y_p, y_s, wi, wo, 0, g_l, b_l, 0, alpha)
        qa_p, ka_p, va_p, qr_p, kr_p, vr_p, gr_p, ga_p, gb_p = _proj_rope(
            x1_p, w_in_l, tables_p, tab_blocks_p, BF16)
        qa_s, ka_s, va_s, qr_s, kr_s, vr_s, gr_s, ga_s, gb_s = _proj_rope(
            x1_s, w_in_l, tables_s, tab_blocks_s, F32)

        oa_p = _attn_prompt(qa_p, ka_p, va_p, sinks, B, S)
        or_p, st_p = _ret_prompt(qr_p, kr_p, vr_p, gr_p, B, S)
        w = min(WINDOW, S)
        nk_p = ka_p.reshape(B, S, N_KV_HEADS, HEAD_DIM)[:, S - w:]
        nv_p = va_p.reshape(B, S, N_KV_HEADS, HEAD_DIM)[:, S - w:]

        r3 = lambda a: a.reshape(DB, T, a.shape[-1])
        oa_s, nk_s, nv_s = _attn_sample(r3(qa_s), r3(ka_s), r3(va_s), sinks,
                                        cache_k[l].reshape(DB, W, KV_W), cache_v[l].reshape(DB, W, KV_W))
        or_s, st_s = _ret_sample(r3(qr_s), r3(kr_s), r3(vr_s), r3(gr_s), state_ret[l])
        oa_s, or_s = oa_s.reshape(DB * T, Q_W), or_s.reshape(DB * T, RV_W)
        shape5 = (DB, W, N_KV_HEADS, HEAD_DIM)

        x2_p = _merge_ln(x1_p, oa_p, or_p, ga_p, gb_p, wba, wbr, w_o_l, g_l, b_l, 1, alpha)
        x2_s = _merge_ln(x1_s, oa_s, or_s, ga_s, gb_s, wba, wbr, w_o_l, g_l, b_l, 1, alpha)
        y_p, y_s = _ffn_ln(x2_p, x2_s, wi, wo, 1, g_l, b_l, 2, alpha)
        for lst, val in zip(outs, (nk_p, nv_p, st_p, nk_s.reshape(shape5), nv_s.reshape(shape5), st_s)):
            lst.append(val)
    return (y_p.reshape(B, S, D), y_s.reshape(DB, T, D)) + tuple(jnp.stack(o) for o in outs)
```

```python
import functools
import math

import jax
import jax.numpy as jnp
from jax import lax
from jax.experimental import pallas as pl
from jax.experimental.pallas import tpu as pltpu

F32 = jnp.float32
BF16 = jnp.bfloat16

PAST_LEN = 8192
N_Q_HEADS = 8
N_KV_HEADS = 2
HEAD_DIM = 64
GQA_GROUP = N_Q_HEADS // N_KV_HEADS
WINDOW = 128
ROPE_THETA = 10000.0
N_RET_HEADS = 4
RET_KEY_DIM = 128
RET_VAL_DIM = 256
RET_CHUNK = 128
RET_THETA = 10000.0
LN_EPS = 1e-5
GN_EPS = 1e-6
NEG_INF = -1e30

Q_W = N_Q_HEADS * HEAD_DIM
KV_W = N_KV_HEADS * HEAD_DIM
RQ_W = N_RET_HEADS * RET_KEY_DIM
RV_W = N_RET_HEADS * RET_VAL_DIM

LANES = 128
VMEM_LIMIT_BYTES = 56 * 1024 * 1024

TOKEN_TILE = 512
FF_CHUNK = 256
MIXER_TILE = 512
SAMPLE_BT = 8


def _dot(a, b):
    return jnp.dot(a, b, preferred_element_type=F32)


def _dot_nt(a, b):
    return lax.dot_general(a, b, (((1,), (1,)), ((), ())), preferred_element_type=F32)


def _layer_norm(y, g, b):
    mu = jnp.mean(y, axis=-1, keepdims=True)
    yc = y - mu
    var = jnp.mean(yc * yc, axis=-1, keepdims=True)
    return yc * lax.rsqrt(var + LN_EPS) * g + b


def _cparams(n_axes, semantics=None):
    return pltpu.CompilerParams(
        dimension_semantics=semantics or ("arbitrary",) * n_axes,
        vmem_limit_bytes=VMEM_LIMIT_BYTES)


def _const_spec(shape):
    nd = len(shape)
    return pl.BlockSpec(shape, lambda *_: (0,) * nd, pipeline_mode=pl.Buffered(1))


def _segment_specs(n_prompt_tiles, tm, width, tile_of):
    prompt = pl.BlockSpec((tm, width), lambda i: (jnp.minimum(tile_of(i), n_prompt_tiles - 1), 0))
    sample = pl.BlockSpec((tm, width), lambda i: (jnp.maximum(tile_of(i) - n_prompt_tiles, 0), 0))
    return prompt, sample


def _store_segment(is_prompt, prompt_ref, sample_ref, value):
    @pl.when(is_prompt)
    def _():
        prompt_ref[...] = value.astype(prompt_ref.dtype)

    @pl.when(jnp.logical_not(is_prompt))
    def _():
        sample_ref[...] = value.astype(sample_ref.dtype)


def _ffn_chunk(xb, w_gate, w_up, w_out):
    act = (jax.nn.silu(_dot(xb, w_gate)) * _dot(xb, w_up)).astype(BF16)
    return _dot(act, w_out)


def _ffn_ln_kernel(alpha, n_chunks, ln_row, n_prompt_tiles,
                   xp_ref, xs_ref, wg_ref, wu_ref, wo_ref, g_ref, b_ref, op_ref, os_ref,
                   acc_ref, wi_bf_ref, wo_bf_ref):
    step = pl.program_id(0)
    tile = jnp.maximum(step - (n_chunks - 1), 0)
    is_prompt = tile < n_prompt_tiles
    fc = FF_CHUNK

    def finish(x):
        y = alpha * x + 0.5 * acc_ref[...]
        y = _layer_norm(y, g_ref[ln_row:ln_row + 1, :], b_ref[ln_row:ln_row + 1, :])
        _store_segment(is_prompt, op_ref, os_ref, y)

    @pl.when(step < n_chunks)
    def _():
        w_gate, w_up, w_out = wg_ref[...].astype(BF16), wu_ref[...].astype(BF16), wo_ref[...].astype(BF16)
        wi_bf_ref[step, :, :fc] = w_gate
        wi_bf_ref[step, :, fc:] = w_up
        wo_bf_ref[step] = w_out
        x = xp_ref[...]
        part = _ffn_chunk(x.astype(BF16), w_gate, w_up, w_out)

        @pl.when(step == 0)
        def _():
            acc_ref[...] = part

        @pl.when(step > 0)
        def _():
            acc_ref[...] += part

        @pl.when(step == n_chunks - 1)
        def _():
            finish(x)

    @pl.when(step >= n_chunks)
    def _():
        x = jnp.where(is_prompt, xp_ref[...], xs_ref[...])
        xb = x.astype(BF16)
        for c in range(n_chunks):
            part = _ffn_chunk(xb, wi_bf_ref[c, :, :fc], wi_bf_ref[c, :, fc:], wo_bf_ref[c])
            if c == 0:
                acc_ref[...] = part
            else:
                acc_ref[...] += part
        finish(x)


def _ffn_ln(xp, xs, wi, wo, half, ln_g, ln_b, ln_row, alpha):
    d = xp.shape[1]
    d_ff = wo.shape[1]
    tm, fc = TOKEN_TILE, FF_CHUNK
    assert xp.shape[0] % tm == 0 and xs.shape[0] % tm == 0 and d_ff % fc == 0
    n_p, n_s = xp.shape[0] // tm, xs.shape[0] // tm
    n_chunks = d_ff // fc
    tile_of = lambda i: jnp.maximum(i - (n_chunks - 1), 0)
    chunk_of = lambda i: jnp.minimum(i, n_chunks - 1)
    seg_p, seg_s = _segment_specs(n_p, tm, d, tile_of)
    return pl.pallas_call(
        functools.partial(_ffn_ln_kernel, alpha, n_chunks, ln_row, n_p),
        grid=(n_chunks - 1 + n_p + n_s,),
        in_specs=[seg_p, seg_s,
                  pl.BlockSpec((None, d, fc), lambda i: (half, 0, chunk_of(i))),
                  pl.BlockSpec((None, d, fc), lambda i: (half, 0, n_chunks + chunk_of(i))),
                  pl.BlockSpec((None, fc, d), lambda i: (half, chunk_of(i), 0)),
                  _const_spec(ln_g.shape), _const_spec(ln_b.shape)],
        out_specs=[seg_p, seg_s],
        out_shape=[jax.ShapeDtypeStruct(xp.shape, F32), jax.ShapeDtypeStruct(xs.shape, F32)],
        scratch_shapes=[pltpu.VMEM((tm, d), F32),
                        pltpu.VMEM((n_chunks, d, 2 * fc), BF16),
                        pltpu.VMEM((n_chunks, fc, d), BF16)],
        compiler_params=_cparams(1),
        name="ffn_ln",
    )(xp, xs, wi, wi, wo, ln_g, ln_b)


def _rope_tables(pos, dim, theta_fn):
    half = dim // 2
    inv_freq = theta_fn(half)
    ang = pos.astype(F32)[:, None] * inv_freq[None, :]
    reps = LANES // half
    cos = jnp.tile(jnp.cos(ang), (1, reps))
    sign = jnp.tile(jnp.concatenate([-jnp.ones((half,), F32), jnp.ones((half,), F32)]), LANES // dim)
    sin = jnp.tile(jnp.sin(ang), (1, reps)) * sign[None, :]
    return cos, sin


def _attn_inv_freq(half):
    return 1.0 / (ROPE_THETA ** (jnp.arange(0, 2 * half, 2, dtype=F32) / (2 * half)))


def _ret_inv_freq(half):
    return 1.0 / (RET_THETA ** jnp.linspace(0.0, 1.0, half, dtype=F32))


def _rope_block(x, cos, sin, dim):
    half = dim // 2
    if dim == LANES:
        rot = pltpu.roll(x, half, 1)
    else:
        lane = lax.broadcasted_iota(jnp.int32, x.shape, 1)
        first_half = (lane % dim) < half
        rot = jnp.where(first_half, pltpu.roll(x, LANES - half, 1), pltpu.roll(x, half, 1))
    return x * cos + rot * sin


def _proj_rope_kernel(x_ref, w_ref, cosa_ref, sina_ref, cosr_ref, sinr_ref,
                      qa_ref, ka_ref, va_ref, qr_ref, kr_ref, vr_ref, gr_ref, ga_ref, gb_ref):
    xb = x_ref[...].astype(BF16)
    cosa, sina = cosa_ref[...], sina_ref[...]
    cosr, sinr = cosr_ref[...], sinr_ref[...]
    off = 0

    y = _dot(xb, w_ref[:, off:off + Q_W])
    for j in range(Q_W // LANES):
        qa_ref[:, j * LANES:(j + 1) * LANES] = _rope_block(
            y[:, j * LANES:(j + 1) * LANES], cosa, sina, HEAD_DIM).astype(qa_ref.dtype)
    off += Q_W
    y = _dot(xb, w_ref[:, off:off + 2 * KV_W])
    ka_ref[...] = _rope_block(y[:, :KV_W], cosa, sina, HEAD_DIM)
    va_ref[...] = y[:, KV_W:]
    off += 2 * KV_W
    y = _dot(xb, w_ref[:, off:off + RQ_W])
    for j in range(RQ_W // LANES):
        qr_ref[:, j * LANES:(j + 1) * LANES] = _rope_block(
            y[:, j * LANES:(j + 1) * LANES], cosr, sinr, RET_KEY_DIM).astype(qr_ref.dtype)
    off += RQ_W
    y = _dot(xb, w_ref[:, off:off + RQ_W])
    for j in range(RQ_W // LANES):
        kr = _rope_block(y[:, j * LANES:(j + 1) * LANES], cosr, sinr, RET_KEY_DIM)
        kr_ref[:, j * LANES:(j + 1) * LANES] = (kr * (RET_KEY_DIM ** -0.5)).astype(kr_ref.dtype)
    off += RQ_W
    for ref in (vr_ref, gr_ref, ga_ref, gb_ref):
        width = ref.shape[1]
        for c in range(width // 512):
            ref[:, c * 512:(c + 1) * 512] = _dot(
                xb, w_ref[:, off + c * 512:off + (c + 1) * 512]).astype(ref.dtype)
        off += width


def _proj_rope(x1, w_in, tables, table_blocks, mixer_dtype):
    t, d = x1.shape
    n_in = w_in.shape[1]
    d_model = (n_in - Q_W - 2 * KV_W - 2 * RQ_W - 2 * RV_W) // 2
    tm = TOKEN_TILE
    row = lambda i: (i, 0)
    tab = lambda i: (i % table_blocks, 0)
    widths = [(Q_W, mixer_dtype), (KV_W, F32), (KV_W, F32), (RQ_W, mixer_dtype), (RQ_W, mixer_dtype),
              (RV_W, mixer_dtype), (RV_W, mixer_dtype), (d_model, BF16), (d_model, BF16)]
    return pl.pallas_call(
        _proj_rope_kernel,
        grid=(t // tm,),
        in_specs=[pl.BlockSpec((tm, d), row), _const_spec(w_in.shape)]
                 + [pl.BlockSpec((tm, LANES), tab)] * 4,
        out_specs=[pl.BlockSpec((tm, w), row) for w, _ in widths],
        out_shape=[jax.ShapeDtypeStruct((t, w), dt) for w, dt in widths],
        compiler_params=_cparams(1),
        name="proj_rope",
    )(x1, w_in, *tables)


def _kv_lane_variants(x):
    lane = lax.broadcasted_iota(jnp.int32, x.shape, 1)
    lo = lane < HEAD_DIM
    swapped = pltpu.roll(x, HEAD_DIM, 1)
    zero = jnp.zeros_like(x)
    kv0 = (jnp.where(lo, x, zero).astype(BF16), jnp.where(lo, zero, swapped).astype(BF16))
    kv1 = (jnp.where(lo, swapped, zero).astype(BF16), jnp.where(lo, zero, x).astype(BF16))
    return kv0, kv1


def _attn_prompt_kernel(sinks_ref, q_ref, kc_ref, kp_ref, vc_ref, vp_ref, o_ref):
    n = pl.program_id(1)
    L = WINDOW
    k_full = jnp.concatenate([kp_ref[...], kc_ref[...]], axis=0)
    v_full = jnp.concatenate([vp_ref[...], vc_ref[...]], axis=0)
    k_var = _kv_lane_variants(k_full)
    v_var = _kv_lane_variants(v_full)
    own = (lax.broadcasted_iota(jnp.int32, (L, L), 1)
           <= lax.broadcasted_iota(jnp.int32, (L, L), 0))
    scale = HEAD_DIM ** -0.5

    units = [(i, j, par) for i in range(q_ref.shape[0] // L) for j in range(Q_W // LANES) for par in range(2)]

    def scores(unit):
        i, j, par = unit
        q2 = q_ref[i * L:(i + 1) * L, j * LANES:(j + 1) * LANES] * scale
        return _dot_nt(q2, k_var[(2 * j) // GQA_GROUP][par][i * L:(i + 2) * L])

    s_next = scores(units[0])
    out = None
    for idx, (i, j, par) in enumerate(units):
        s = s_next
        if idx + 1 < len(units):
            s_next = scores(units[idx + 1])
        s_prev = s[:, :L]
        if i == 0:
            s_prev = jnp.where(n > 0, s_prev, NEG_INF)
        sw = jnp.where(own, s[:, L:], s_prev)
        sink = sinks_ref[2 * j + par]
        m = jnp.maximum(jnp.max(sw, axis=-1, keepdims=True), sink)
        e = jnp.exp(sw - m)
        denom = jnp.sum(e, axis=-1, keepdims=True) + jnp.exp(sink - m)
        e_band = jnp.concatenate([jnp.where(own, 0.0, e), jnp.where(own, e, 0.0)], axis=1)
        r = _dot(e_band.astype(BF16), v_var[(2 * j) // GQA_GROUP][par][i * L:(i + 2) * L]) / denom
        if par == 0:
            out = r
        else:
            o_ref[i * L:(i + 1) * L, j * LANES:(j + 1) * LANES] = (out + r).astype(o_ref.dtype)


def _attn_prompt(qa, ka, va, sinks, batch, seq):
    tq = MIXER_TILE
    nq = seq // tq
    per = tq // WINDOW
    cur = lambda b, n: (b * nq + n, 0)
    prev = lambda b, n: ((b * nq + n) * per - jnp.minimum(n, 1), 0)
    return pl.pallas_call(
        _attn_prompt_kernel,
        grid=(batch, nq),
        in_specs=[pl.BlockSpec(memory_space=pltpu.SMEM),
                  pl.BlockSpec((tq, Q_W), cur),
                  pl.BlockSpec((tq, KV_W), cur), pl.BlockSpec((WINDOW, KV_W), prev),
                  pl.BlockSpec((tq, KV_W), cur), pl.BlockSpec((WINDOW, KV_W), prev)],
        out_specs=pl.BlockSpec((tq, Q_W), cur),
        out_shape=jax.ShapeDtypeStruct(qa.shape, BF16),
        compiler_params=_cparams(2),
        name="attn_prompt",
    )(sinks, qa, ka, ka, va, va)


def _attn_sample_kernel(sink_rows_ref, q_ref, kn_ref, vn_ref, ck_ref, cv_ref,
                        o_ref, nk_ref, nv_ref):
    bt, t_new, _ = q_ref.shape
    w = ck_ref.shape[1]
    n_rows = N_Q_HEADS * t_new
    lane_lo = lax.broadcasted_iota(jnp.int32, (bt * t_new, LANES), 1) < HEAD_DIM
    r_i = lax.broadcasted_iota(jnp.int32, (n_rows, w + t_new), 0) % t_new
    m_i = lax.broadcasted_iota(jnp.int32, (n_rows, w + t_new), 1)
    mask = ((m_i < w) & (m_i > r_i + (w - WINDOW))) | ((m_i >= w) & (m_i - w <= r_i))
    sink = sink_rows_ref[...]
    scale = HEAD_DIM ** -0.5
    kc, kn = ck_ref[...], kn_ref[...]
    vc, vn = cv_ref[...], vn_ref[...]
    k_all = jnp.concatenate([kc, kn], axis=1).astype(BF16)
    v_all = jnp.concatenate([vc, vn], axis=1).astype(BF16)
    q = q_ref[...].reshape(bt * t_new, Q_W)
    pieces = []
    for h in range(N_Q_HEADS):
        q2 = q[:, (h // 2) * LANES:(h // 2 + 1) * LANES]
        src_lo = h % 2 == 0
        dst_lo = h // GQA_GROUP == 0
        if src_lo != dst_lo:
            q2 = pltpu.roll(q2, HEAD_DIM, 1)
        pieces.append(jnp.where(lane_lo == dst_lo, q2, 0.0).reshape(bt, t_new, LANES))
    q_rows = (jnp.concatenate(pieces, axis=1) * scale).astype(BF16)
    s = jnp.einsum("bqd,bkd->bqk", q_rows, k_all, preferred_element_type=F32)
    s = jnp.where(mask, s, NEG_INF)
    m = jnp.maximum(jnp.max(s, axis=-1, keepdims=True), sink)
    e = jnp.exp(s - m)
    denom = jnp.sum(e, axis=-1, keepdims=True) + jnp.exp(sink - m)
    pv = jnp.einsum("bqk,bkd->bqd", e.astype(BF16), v_all, preferred_element_type=F32) / denom
    for j in range(Q_W // LANES):
        out = None
        for par in range(2):
            h = 2 * j + par
            dst_lo = h // GQA_GROUP == 0
            piece = pv[:, h * t_new:(h + 1) * t_new, :].reshape(bt * t_new, LANES)
            piece = jnp.where(lane_lo == dst_lo, piece, 0.0)
            if dst_lo != (par == 0):
                piece = pltpu.roll(piece, HEAD_DIM, 1)
            out = piece if out is None else out + piece
        o_ref[:, :, j * LANES:(j + 1) * LANES] = out.reshape(bt, t_new, LANES).astype(o_ref.dtype)
    nk_ref[:, :w - t_new, :] = kc[:, t_new:, :]
    nk_ref[:, w - t_new:, :] = kn
    nv_ref[:, :w - t_new, :] = vc[:, t_new:, :]
    nv_ref[:, w - t_new:, :] = vn


def _ret_log_gamma(h):
    return math.log(1.0 - 2.0 ** (-5.0 - h))


def _ret_decays(chunk, h):
    lg = _ret_log_gamma(h)
    i = lax.broadcasted_iota(jnp.int32, (chunk, chunk), 0)
    j = lax.broadcasted_iota(jnp.int32, (chunk, chunk), 1)
    rel_mask = jnp.where(i >= j, jnp.exp((j + 1).astype(F32) * -lg), 0.0)
    col = lax.broadcasted_iota(jnp.int32, (chunk, 1), 0).astype(F32)
    q_decay = jnp.exp((col + 1.0) * lg)
    k_decay = jnp.exp((chunk - 1.0 - col) * lg)
    return rel_mask, q_decay, k_decay, math.exp(chunk * lg)


def _ret_chunk_all_heads(q, k, v, g, states, decays):
    heads = range(N_RET_HEADS)
    batched = q.ndim == 3
    mm = (lambda a, b: jnp.einsum("bmk,bkn->bmn", a, b, preferred_element_type=F32)) if batched else _dot
    mm_nt = (lambda a, b: jnp.einsum("bmk,bnk->bmn", a, b, preferred_element_type=F32)) if batched else _dot_nt
    qh = [q[..., h * RET_KEY_DIM:(h + 1) * RET_KEY_DIM] for h in heads]
    kh = [k[..., h * RET_KEY_DIM:(h + 1) * RET_KEY_DIM] for h in heads]
    vh = [v[..., h * RET_VAL_DIM:(h + 1) * RET_VAL_DIM] for h in heads]
    qb = [x.astype(BF16) for x in qh]
    vb = [x.astype(BF16) for x in vh]
    inner = [mm_nt(qb[h], kh[h].astype(BF16)) * decays[h][0] for h in heads]
    if batched:
        lhs = [jnp.concatenate([qh[h], inner[h]], axis=-1).astype(BF16) for h in heads]
        rhs = [jnp.concatenate([states[h], vh[h]], axis=-2).astype(BF16) for h in heads]
    else:
        lhs = [jnp.concatenate([qb[h], inner[h].astype(BF16)], axis=-1) for h in heads]
        rhs = [jnp.concatenate([states[h].astype(BF16), vb[h]], axis=-2) for h in heads]
    o = [mm(lhs[h], rhs[h]) * decays[h][1] for h in heads]
    kd_t = [jnp.swapaxes(kh[h].astype(F32) * decays[h][2], -1, -2).astype(BF16) for h in heads]
    new_states = [states[h] * decays[h][3] + mm(kd_t[h], vb[h]) for h in heads]
    outs = []
    for h in heads:
        mu = jnp.mean(o[h], axis=-1, keepdims=True)
        oc = o[h] - mu
        var = jnp.mean(oc * oc, axis=-1, keepdims=True)
        gate = g[..., h * RET_VAL_DIM:(h + 1) * RET_VAL_DIM].astype(F32)
        outs.append(oc * lax.rsqrt(var + GN_EPS) * jax.nn.silu(gate))
    return jnp.concatenate(outs, axis=-1), new_states


def _ret_prompt_kernel(q_ref, k_ref, v_ref, g_ref, o_ref, st_ref, state_ref):
    n = pl.program_id(1)
    C = RET_CHUNK

    @pl.when(n == 0)
    def _():
        state_ref[...] = jnp.zeros_like(state_ref)

    decays = [_ret_decays(C, h) for h in range(N_RET_HEADS)]
    states = [state_ref[h] for h in range(N_RET_HEADS)]
    for c in range(q_ref.shape[0] // C):
        rows = slice(c * C, (c + 1) * C)
        out, states = _ret_chunk_all_heads(q_ref[rows, :], k_ref[rows, :], v_ref[rows, :], g_ref[rows, :],
                                           states, decays)
        o_ref[rows, :] = out.astype(o_ref.dtype)
    for h in range(N_RET_HEADS):
        state_ref[h] = states[h]

    @pl.when(n == pl.num_programs(1) - 1)
    def _():
        st_ref[0] = state_ref[...]


def _ret_prompt(qr, kr, vr, gr, batch, seq):
    ts = MIXER_TILE
    ns = seq // ts
    row = lambda b, n: (b * ns + n, 0)
    return pl.pallas_call(
        _ret_prompt_kernel,
        grid=(batch, ns),
        in_specs=[pl.BlockSpec((ts, RQ_W), row), pl.BlockSpec((ts, RQ_W), row),
                  pl.BlockSpec((ts, RV_W), row), pl.BlockSpec((ts, RV_W), row)],
        out_specs=[pl.BlockSpec((ts, RV_W), row),
                   pl.BlockSpec((1, N_RET_HEADS, RET_KEY_DIM, RET_VAL_DIM), lambda b, n: (b, 0, 0, 0))],
        out_shape=[jax.ShapeDtypeStruct((batch * seq, RV_W), BF16),
                   jax.ShapeDtypeStruct((batch, N_RET_HEADS, RET_KEY_DIM, RET_VAL_DIM), F32)],
        scratch_shapes=[pltpu.VMEM((N_RET_HEADS, RET_KEY_DIM, RET_VAL_DIM), F32)],
        compiler_params=_cparams(2),
        name="ret_prompt",
    )(qr, kr, vr, gr)


def _ret_sample_kernel(q_ref, k_ref, v_ref, g_ref, st_ref, o_ref, nst_ref):
    t_new = q_ref.shape[1]
    decays = [_ret_decays(t_new, h) for h in range(N_RET_HEADS)]
    states = [st_ref[:, h] for h in range(N_RET_HEADS)]
    out, states = _ret_chunk_all_heads(q_ref[...], k_ref[...], v_ref[...], g_ref[...], states, decays)
    o_ref[...] = out.astype(o_ref.dtype)
    for h in range(N_RET_HEADS):
        nst_ref[:, h] = states[h]


def _sample_mixers_kernel(sink_rows_ref, qa_ref, kn_ref, vn_ref, ck_ref, cv_ref,
                          qr_ref, kr_ref, vr_ref, gr_ref, st_ref,
                          oa_ref, nk_ref, nv_ref, or_ref, nst_ref):
    _attn_sample_kernel(sink_rows_ref, qa_ref, kn_ref, vn_ref, ck_ref, cv_ref, oa_ref, nk_ref, nv_ref)
    _ret_sample_kernel(qr_ref, kr_ref, vr_ref, gr_ref, st_ref, or_ref, nst_ref)


def _sample_mixers(qa, ka, va, sinks, cache_k, cache_v, qr, kr, vr, gr, state):
    db, t_new, _ = qa.shape
    w = cache_k.shape[1]
    bt = SAMPLE_BT
    sink_rows = jnp.repeat(sinks.astype(F32), t_new).reshape(N_Q_HEADS * t_new, 1)
    blk = lambda *s: pl.BlockSpec((bt,) + s, lambda i: (i,) + (0,) * len(s))
    st = blk(N_RET_HEADS, RET_KEY_DIM, RET_VAL_DIM)
    return pl.pallas_call(
        _sample_mixers_kernel,
        grid=(db // bt,),
        in_specs=[_const_spec(sink_rows.shape), blk(t_new, Q_W), blk(t_new, KV_W), blk(t_new, KV_W),
                  blk(w, KV_W), blk(w, KV_W),
                  blk(t_new, RQ_W), blk(t_new, RQ_W), blk(t_new, RV_W), blk(t_new, RV_W), st],
        out_specs=[blk(t_new, Q_W), blk(w, KV_W), blk(w, KV_W), blk(t_new, RV_W), st],
        out_shape=[jax.ShapeDtypeStruct((db, t_new, Q_W), F32),
                   jax.ShapeDtypeStruct((db, w, KV_W), F32),
                   jax.ShapeDtypeStruct((db, w, KV_W), F32),
                   jax.ShapeDtypeStruct((db, t_new, RV_W), F32),
                   jax.ShapeDtypeStruct(state.shape, state.dtype)],
        compiler_params=_cparams(1),
        name="sample_mixers",
    )(sink_rows, qa, ka, va, cache_k, cache_v, qr, kr, vr, gr, state)


def _merge_ln_kernel(alpha, ln_row, x_ref, oa_ref, orr_ref, ga_ref, gb_ref, wba_ref, wbr_ref, wo_ref,
                     g_ref, b_ref, o_ref):
    ya = _dot(oa_ref[...].astype(BF16), wba_ref[...])
    yr = _dot(orr_ref[...].astype(BF16), wbr_ref[...])
    merged = (jax.nn.sigmoid(ga_ref[...].astype(F32)) * ya
              + jax.nn.sigmoid(gb_ref[...].astype(F32)) * yr)
    z = _dot(merged.astype(BF16), wo_ref[...])
    o_ref[...] = _layer_norm(alpha * x_ref[...] + z,
                             g_ref[ln_row:ln_row + 1, :], b_ref[ln_row:ln_row + 1, :])


def _merge_ln(x1, oa, orr, ga, gb, wba, wbr, wo, ln_g, ln_b, ln_row, alpha):
    t, d = x1.shape
    tm = TOKEN_TILE
    row = lambda i: (i, 0)
    return pl.pallas_call(
        functools.partial(_merge_ln_kernel, alpha, ln_row),
        grid=(t // tm,),
        in_specs=[pl.BlockSpec((tm, d), row), pl.BlockSpec((tm, Q_W), row), pl.BlockSpec((tm, RV_W), row),
                  pl.BlockSpec((tm, d), row), pl.BlockSpec((tm, d), row),
                  _const_spec(wba.shape), _const_spec(wbr.shape), _const_spec(wo.shape),
                  _const_spec(ln_g.shape), _const_spec(ln_b.shape)],
        out_specs=pl.BlockSpec((tm, d), row),
        out_shape=jax.ShapeDtypeStruct((t, d), F32),
        compiler_params=_cparams(1),
        name="merge_ln",
    )(x1, oa, orr, ga, gb, wba, wbr, wo, ln_g, ln_b)


def _rope_tables_for(pos):
    tables = _rope_tables(pos, HEAD_DIM, _attn_inv_freq) + _rope_tables(pos, RET_KEY_DIM, _ret_inv_freq)
    period = pos.shape[0]
    if period < TOKEN_TILE:
        tables = tuple(jnp.tile(tb, (TOKEN_TILE // period, 1)) for tb in tables)
        period = TOKEN_TILE
    return tables, period // TOKEN_TILE


def kernel(x_prompt, x_sample, cache_k, cache_v, state_ret, ln_g, ln_b, ffn_wi, ffn_wo, w_in,
           attn_sinks, w_br_attn, w_br_ret, w_o):
    B, S, D = x_prompt.shape
    DB, T, _ = x_sample.shape
    depth = ln_g.shape[0]
    alpha = (2.0 * depth) ** 0.25
    W = cache_k.shape[2]
    tables_p, tab_blocks_p = _rope_tables_for(jnp.arange(S, dtype=jnp.int32))
    tables_s, tab_blocks_s = _rope_tables_for(PAST_LEN + jnp.arange(T, dtype=jnp.int32))

    y_p = x_prompt.reshape(B * S, D)
    y_s = x_sample.reshape(DB * T, D)
    outs = [[] for _ in range(6)]
    for l in range(depth):
        wi, wo = ffn_wi[l], ffn_wo[l]
        w_in_l = w_in[l].astype(BF16)
        wba, wbr, w_o_l = w_br_attn[l].astype(BF16), w_br_ret[l].astype(BF16), w_o[l].astype(BF16)
        sinks = attn_sinks[l].astype(F32)
        g_l, b_l = ln_g[l], ln_b[l]

        x1_p, x1_s = _ffn_ln(y_p, y_s, wi, wo, 0, g_l, b_l, 0, alpha)
        qa_p, ka_p, va_p, qr_p, kr_p, vr_p, gr_p, ga_p, gb_p = _proj_rope(
            x1_p, w_in_l, tables_p, tab_blocks_p, BF16)
        qa_s, ka_s, va_s, qr_s, kr_s, vr_s, gr_s, ga_s, gb_s = _proj_rope(
            x1_s, w_in_l, tables_s, tab_blocks_s, F32)

        oa_p = _attn_prompt(qa_p, ka_p, va_p, sinks, B, S)
        or_p, st_p = _ret_prompt(qr_p, kr_p, vr_p, gr_p, B, S)
        w = min(WINDOW, S)
        nk_p = ka_p.reshape(B, S, N_KV_HEADS, HEAD_DIM)[:, S - w:]
        nv_p = va_p.reshape(B, S, N_KV_HEADS, HEAD_DIM)[:, S - w:]

        r3 = lambda a: a.reshape(DB, T, a.shape[-1])
        oa_s, nk_s, nv_s, or_s, st_s = _sample_mixers(
            r3(qa_s), r3(ka_s), r3(va_s), sinks,
            cache_k[l].reshape(DB, W, KV_W), cache_v[l].reshape(DB, W, KV_W),
            r3(qr_s), r3(kr_s), r3(vr_s), r3(gr_s), state_ret[l])
        oa_s, or_s = oa_s.reshape(DB * T, Q_W), or_s.reshape(DB * T, RV_W)
        shape5 = (DB, W, N_KV_HEADS, HEAD_DIM)

        x2_p = _merge_ln(x1_p, oa_p, or_p, ga_p, gb_p, wba, wbr, w_o_l, g_l, b_l, 1, alpha)
        x2_s = _merge_ln(x1_s, oa_s, or_s, ga_s, gb_s, wba, wbr, w_o_l, g_l, b_l, 1, alpha)
        y_p, y_s = _ffn_ln(x2_p, x2_s, wi, wo, 1, g_l, b_l, 2, alpha)
        for lst, val in zip(outs, (nk_p, nv_p, st_p, nk_s.reshape(shape5), nv_s.reshape(shape5), st_s)):
            lst.append(val)
    return (y_p.reshape(B, S, D), y_s.reshape(DB, T, D)) + tuple(jnp.stack(o) for o in outs)
```

```python
import functools
import math

import jax
import jax.numpy as jnp
from jax import lax
from jax.experimental import pallas as pl
from jax.experimental.pallas import tpu as pltpu

F32 = jnp.float32
BF16 = jnp.bfloat16

PAST_LEN = 8192
N_Q_HEADS = 8
N_KV_HEADS = 2
HEAD_DIM = 64
GQA_GROUP = N_Q_HEADS // N_KV_HEADS
WINDOW = 128
ROPE_THETA = 10000.0
N_RET_HEADS = 4
RET_KEY_DIM = 128
RET_VAL_DIM = 256
RET_CHUNK = 128
RET_THETA = 10000.0
LN_EPS = 1e-5
GN_EPS = 1e-6
NEG_INF = -1e30

Q_W = N_Q_HEADS * HEAD_DIM
KV_W = N_KV_HEADS * HEAD_DIM
RQ_W = N_RET_HEADS * RET_KEY_DIM
RV_W = N_RET_HEADS * RET_VAL_DIM

LANES = 128
VMEM_LIMIT_BYTES = 56 * 1024 * 1024

TOKEN_TILE = 512
DENSE_TILE = 1024
FF_CHUNK = 256
MIXER_TILE = 2048
SAMPLE_BT = 8


def _dot(a, b):
    return jnp.dot(a, b, preferred_element_type=F32)


def _dot_nt(a, b):
    return lax.dot_general(a, b, (((1,), (1,)), ((), ())), preferred_element_type=F32)


def _layer_norm(y, g, b):
    mu = jnp.mean(y, axis=-1, keepdims=True)
    yc = y - mu
    var = jnp.mean(yc * yc, axis=-1, keepdims=True)
    return yc * lax.rsqrt(var + LN_EPS) * g + b


def _cparams(n_axes, semantics=None):
    return pltpu.CompilerParams(
        dimension_semantics=semantics or ("arbitrary",) * n_axes,
        vmem_limit_bytes=VMEM_LIMIT_BYTES)


def _const_spec(shape):
    nd = len(shape)
    return pl.BlockSpec(shape, lambda *_: (0,) * nd, pipeline_mode=pl.Buffered(1))


def _segment_specs(n_prompt_tiles, tm, width, tile_of):
    prompt = pl.BlockSpec((tm, width), lambda i: (jnp.minimum(tile_of(i), n_prompt_tiles - 1), 0))
    sample = pl.BlockSpec((tm, width), lambda i: (jnp.maximum(tile_of(i) - n_prompt_tiles, 0), 0))
    return prompt, sample


def _store_segment(is_prompt, prompt_ref, sample_ref, value):
    @pl.when(is_prompt)
    def _():
        prompt_ref[...] = value.astype(prompt_ref.dtype)

    @pl.when(jnp.logical_not(is_prompt))
    def _():
        sample_ref[...] = value.astype(sample_ref.dtype)


def _ffn_chunk(xb, w_gate, w_up, w_out):
    act = (jax.nn.silu(_dot(xb, w_gate)) * _dot(xb, w_up)).astype(BF16)
    return _dot(act, w_out)


def _ffn_tile_maps(n_chunks, n_tiles):
    matmul_tile = lambda i: jnp.clip(i - (n_chunks - 1), 0, n_tiles - 1)
    norm_tile = lambda i: jnp.clip(i - n_chunks, 0, n_tiles - 1)
    return matmul_tile, norm_tile


def _ffn_ln_kernel(alpha, n_chunks, n_tiles, ln_row, n_prompt_tiles,
                   xp_ref, xs_ref, wg_ref, wu_ref, wo_ref, g_ref, b_ref, op_ref, os_ref,
                   acc_ref, y_ref, wi_bf_ref, wo_bf_ref):
    step = pl.program_id(0)
    matmul_tile, norm_tile = _ffn_tile_maps(n_chunks, n_tiles)
    last_step = n_chunks + n_tiles - 1
    fc = FF_CHUNK

    norm_is_prompt = norm_tile(step) < n_prompt_tiles

    def norm_previous():
        return _layer_norm(y_ref[...], g_ref[ln_row:ln_row + 1, :], b_ref[ln_row:ln_row + 1, :])

    @pl.when(step < n_chunks)
    def _():
        w_gate, w_up, w_out = wg_ref[...].astype(BF16), wu_ref[...].astype(BF16), wo_ref[...].astype(BF16)
        wi_bf_ref[step, :, :fc] = w_gate
        wi_bf_ref[step, :, fc:] = w_up
        wo_bf_ref[step] = w_out
        x = xp_ref[...]
        part = _ffn_chunk(x.astype(BF16), w_gate, w_up, w_out)

        @pl.when(step == 0)
        def _():
            acc_ref[...] = part

        @pl.when(step > 0)
        def _():
            acc_ref[...] += part

        @pl.when(step == n_chunks - 1)
        def _():
            y_ref[...] = alpha * x + 0.5 * acc_ref[...]

    def full_tile_step(out_ref):
        x = jnp.where(matmul_tile(step) < n_prompt_tiles, xp_ref[...], xs_ref[...])
        xb = x.astype(BF16)
        for c in range(n_chunks):
            part = _ffn_chunk(xb, wi_bf_ref[c, :, :fc], wi_bf_ref[c, :, fc:], wo_bf_ref[c])
            if c == 0:
                acc_ref[...] = part
            else:
                acc_ref[...] += part
        out_ref[...] = norm_previous()
        y_ref[...] = alpha * x + 0.5 * acc_ref[...]

    in_full_steps = (step >= n_chunks) & (step < last_step)
    pl.when(in_full_steps & norm_is_prompt)(functools.partial(full_tile_step, op_ref))
    pl.when(in_full_steps & jnp.logical_not(norm_is_prompt))(functools.partial(full_tile_step, os_ref))

    @pl.when(step == last_step)
    def _():
        _store_segment(norm_is_prompt, op_ref, os_ref, norm_previous())


def _ffn_ln(xp, xs, wi, wo, half, ln_g, ln_b, ln_row, alpha):
    d = xp.shape[1]
    d_ff = wo.shape[1]
    tm, fc = TOKEN_TILE, FF_CHUNK
    assert xp.shape[0] % tm == 0 and xs.shape[0] % tm == 0 and d_ff % fc == 0
    n_p, n_s = xp.shape[0] // tm, xs.shape[0] // tm
    n_chunks = d_ff // fc
    n_tiles = n_p + n_s
    matmul_tile, norm_tile = _ffn_tile_maps(n_chunks, n_tiles)
    chunk_of = lambda i: jnp.minimum(i, n_chunks - 1)
    return pl.pallas_call(
        functools.partial(_ffn_ln_kernel, alpha, n_chunks, n_tiles, ln_row, n_p),
        grid=(n_chunks + n_tiles,),
        in_specs=[*_segment_specs(n_p, tm, d, matmul_tile),
                  pl.BlockSpec((None, d, fc), lambda i: (half, 0, chunk_of(i))),
                  pl.BlockSpec((None, d, fc), lambda i: (half, 0, n_chunks + chunk_of(i))),
                  pl.BlockSpec((None, fc, d), lambda i: (half, chunk_of(i), 0)),
                  _const_spec(ln_g.shape), _const_spec(ln_b.shape)],
        out_specs=list(_segment_specs(n_p, tm, d, norm_tile)),
        out_shape=[jax.ShapeDtypeStruct(xp.shape, F32), jax.ShapeDtypeStruct(xs.shape, F32)],
        scratch_shapes=[pltpu.VMEM((tm, d), F32), pltpu.VMEM((tm, d), F32),
                        pltpu.VMEM((n_chunks, d, 2 * fc), BF16),
                        pltpu.VMEM((n_chunks, fc, d), BF16)],
        compiler_params=_cparams(1),
        name="ffn_ln",
    )(xp, xs, wi, wi, wo, ln_g, ln_b)


def _rope_tables(pos, dim, theta_fn):
    half = dim // 2
    inv_freq = theta_fn(half)
    ang = pos.astype(F32)[:, None] * inv_freq[None, :]
    reps = LANES // half
    cos = jnp.tile(jnp.cos(ang), (1, reps))
    sign = jnp.tile(jnp.concatenate([-jnp.ones((half,), F32), jnp.ones((half,), F32)]), LANES // dim)
    sin = jnp.tile(jnp.sin(ang), (1, reps)) * sign[None, :]
    return cos, sin


def _attn_inv_freq(half):
    return 1.0 / (ROPE_THETA ** (jnp.arange(0, 2 * half, 2, dtype=F32) / (2 * half)))


def _ret_inv_freq(half):
    return 1.0 / (RET_THETA ** jnp.linspace(0.0, 1.0, half, dtype=F32))


def _rope_block(x, cos, sin, dim):
    half = dim // 2
    if dim == LANES:
        rot = pltpu.roll(x, half, 1)
    else:
        lane = lax.broadcasted_iota(jnp.int32, x.shape, 1)
        first_half = (lane % dim) < half
        rot = jnp.where(first_half, pltpu.roll(x, LANES - half, 1), pltpu.roll(x, half, 1))
    return x * cos + rot * sin


def _proj_rope_kernel(x_ref, w_ref, cosa_ref, sina_ref, cosr_ref, sinr_ref,
                      qa_ref, ka_ref, va_ref, qr_ref, kr_ref, vr_ref, gr_ref, ga_ref, gb_ref):
    xb = x_ref[...].astype(BF16)
    cosa, sina = cosa_ref[...], sina_ref[...]
    cosr, sinr = cosr_ref[...], sinr_ref[...]
    off = 0

    y = _dot(xb, w_ref[:, off:off + Q_W])
    for j in range(Q_W // LANES):
        qa_ref[:, j * LANES:(j + 1) * LANES] = _rope_block(
            y[:, j * LANES:(j + 1) * LANES], cosa, sina, HEAD_DIM).astype(qa_ref.dtype)
    off += Q_W
    y = _dot(xb, w_ref[:, off:off + 2 * KV_W])
    ka_ref[...] = _rope_block(y[:, :KV_W], cosa, sina, HEAD_DIM)
    va_ref[...] = y[:, KV_W:]
    off += 2 * KV_W
    y = _dot(xb, w_ref[:, off:off + RQ_W])
    for j in range(RQ_W // LANES):
        qr_ref[:, j * LANES:(j + 1) * LANES] = _rope_block(
            y[:, j * LANES:(j + 1) * LANES], cosr, sinr, RET_KEY_DIM).astype(qr_ref.dtype)
    off += RQ_W
    y = _dot(xb, w_ref[:, off:off + RQ_W])
    for j in range(RQ_W // LANES):
        kr = _rope_block(y[:, j * LANES:(j + 1) * LANES], cosr, sinr, RET_KEY_DIM)
        kr_ref[:, j * LANES:(j + 1) * LANES] = (kr * (RET_KEY_DIM ** -0.5)).astype(kr_ref.dtype)
    off += RQ_W
    for ref in (vr_ref, gr_ref, ga_ref, gb_ref):
        width = ref.shape[1]
        for c in range(width // 512):
            ref[:, c * 512:(c + 1) * 512] = _dot(
                xb, w_ref[:, off + c * 512:off + (c + 1) * 512]).astype(ref.dtype)
        off += width


def _proj_rope(x1, w_in, tables, table_blocks, mixer_dtype):
    t, d = x1.shape
    n_in = w_in.shape[1]
    d_model = (n_in - Q_W - 2 * KV_W - 2 * RQ_W - 2 * RV_W) // 2
    tm = DENSE_TILE
    row = lambda i: (i, 0)
    tab = lambda i: (i % table_blocks, 0)
    widths = [(Q_W, mixer_dtype), (KV_W, F32), (KV_W, F32), (RQ_W, mixer_dtype), (RQ_W, mixer_dtype),
              (RV_W, mixer_dtype), (RV_W, mixer_dtype), (d_model, BF16), (d_model, BF16)]
    return pl.pallas_call(
        _proj_rope_kernel,
        grid=(t // tm,),
        in_specs=[pl.BlockSpec((tm, d), row), _const_spec(w_in.shape)]
                 + [pl.BlockSpec((tm, LANES), tab)] * 4,
        out_specs=[pl.BlockSpec((tm, w), row) for w, _ in widths],
        out_shape=[jax.ShapeDtypeStruct((t, w), dt) for w, dt in widths],
        compiler_params=_cparams(1),
        name="proj_rope",
    )(x1, w_in, *tables)


def _kv_lane_variants(x):
    lane = lax.broadcasted_iota(jnp.int32, x.shape, 1)
    lo = lane < HEAD_DIM
    swapped = pltpu.roll(x, HEAD_DIM, 1)
    zero = jnp.zeros_like(x)
    kv0 = (jnp.where(lo, x, zero).astype(BF16), jnp.where(lo, zero, swapped).astype(BF16))
    kv1 = (jnp.where(lo, swapped, zero).astype(BF16), jnp.where(lo, zero, x).astype(BF16))
    return kv0, kv1


def _attn_prompt_kernel(sinks_ref, q_ref, kc_ref, kp_ref, vc_ref, vp_ref, o_ref):
    n = pl.program_id(1)
    L = WINDOW
    k_full = jnp.concatenate([kp_ref[...], kc_ref[...]], axis=0)
    v_full = jnp.concatenate([vp_ref[...], vc_ref[...]], axis=0)
    k_var = _kv_lane_variants(k_full)
    v_var = _kv_lane_variants(v_full)
    own = (lax.broadcasted_iota(jnp.int32, (L, L), 1)
           <= lax.broadcasted_iota(jnp.int32, (L, L), 0))
    scale = HEAD_DIM ** -0.5

    units = [(i, j, par) for i in range(q_ref.shape[0] // L) for j in range(Q_W // LANES) for par in range(2)]

    def scores(unit):
        i, j, par = unit
        q2 = q_ref[i * L:(i + 1) * L, j * LANES:(j + 1) * LANES] * scale
        return _dot_nt(q2, k_var[(2 * j) // GQA_GROUP][par][i * L:(i + 2) * L])

    s_next = scores(units[0])
    out = None
    for idx, (i, j, par) in enumerate(units):
        s = s_next
        if idx + 1 < len(units):
            s_next = scores(units[idx + 1])
        s_prev = s[:, :L]
        if i == 0:
            s_prev = jnp.where(n > 0, s_prev, NEG_INF)
        sw = jnp.where(own, s[:, L:], s_prev)
        sink = sinks_ref[2 * j + par]
        m = jnp.maximum(jnp.max(sw, axis=-1, keepdims=True), sink)
        e = jnp.exp(sw - m)
        denom = jnp.sum(e, axis=-1, keepdims=True) + jnp.exp(sink - m)
        e_band = jnp.concatenate([jnp.where(own, 0.0, e), jnp.where(own, e, 0.0)], axis=1)
        r = _dot(e_band.astype(BF16), v_var[(2 * j) // GQA_GROUP][par][i * L:(i + 2) * L]) / denom
        if par == 0:
            out = r
        else:
            o_ref[i * L:(i + 1) * L, j * LANES:(j + 1) * LANES] = (out + r).astype(o_ref.dtype)


def _attn_prompt(qa, ka, va, sinks, batch, seq):
    tq = MIXER_TILE
    nq = seq // tq
    per = tq // WINDOW
    cur = lambda b, n: (b * nq + n, 0)
    prev = lambda b, n: ((b * nq + n) * per - jnp.minimum(n, 1), 0)
    return pl.pallas_call(
        _attn_prompt_kernel,
        grid=(batch, nq),
        in_specs=[pl.BlockSpec(memory_space=pltpu.SMEM),
                  pl.BlockSpec((tq, Q_W), cur),
                  pl.BlockSpec((tq, KV_W), cur), pl.BlockSpec((WINDOW, KV_W), prev),
                  pl.BlockSpec((tq, KV_W), cur), pl.BlockSpec((WINDOW, KV_W), prev)],
        out_specs=pl.BlockSpec((tq, Q_W), cur),
        out_shape=jax.ShapeDtypeStruct(qa.shape, BF16),
        compiler_params=_cparams(2),
        name="attn_prompt",
    )(sinks, qa, ka, ka, va, va)


def _attn_sample_kernel(sink_rows_ref, q_ref, kn_ref, vn_ref, ck_ref, cv_ref,
                        o_ref, nk_ref, nv_ref):
    bt, t_new, _ = q_ref.shape
    w = ck_ref.shape[1]
    n_rows = N_Q_HEADS * t_new
    lane_lo = lax.broadcasted_iota(jnp.int32, (bt * t_new, LANES), 1) < HEAD_DIM
    r_i = lax.broadcasted_iota(jnp.int32, (n_rows, w + t_new), 0) % t_new
    m_i = lax.broadcasted_iota(jnp.int32, (n_rows, w + t_new), 1)
    mask = ((m_i < w) & (m_i > r_i + (w - WINDOW))) | ((m_i >= w) & (m_i - w <= r_i))
    sink = sink_rows_ref[...]
    scale = HEAD_DIM ** -0.5
    kc, kn = ck_ref[...], kn_ref[...]
    vc, vn = cv_ref[...], vn_ref[...]
    k_all = jnp.concatenate([kc, kn], axis=1).astype(BF16)
    v_all = jnp.concatenate([vc, vn], axis=1).astype(BF16)
    q = q_ref[...].reshape(bt * t_new, Q_W)
    pieces = []
    for h in range(N_Q_HEADS):
        q2 = q[:, (h // 2) * LANES:(h // 2 + 1) * LANES]
        src_lo = h % 2 == 0
        dst_lo = h // GQA_GROUP == 0
        if src_lo != dst_lo:
            q2 = pltpu.roll(q2, HEAD_DIM, 1)
        pieces.append(jnp.where(lane_lo == dst_lo, q2, 0.0).reshape(bt, t_new, LANES))
    q_rows = (jnp.concatenate(pieces, axis=1) * scale).astype(BF16)
    s = jnp.einsum("bqd,bkd->bqk", q_rows, k_all, preferred_element_type=F32)
    s = jnp.where(mask, s, NEG_INF)
    m = jnp.maximum(jnp.max(s, axis=-1, keepdims=True), sink)
    e = jnp.exp(s - m)
    denom = jnp.sum(e, axis=-1, keepdims=True) + jnp.exp(sink - m)
    pv = jnp.einsum("bqk,bkd->bqd", e.astype(BF16), v_all, preferred_element_type=F32) / denom
    for j in range(Q_W // LANES):
        out = None
        for par in range(2):
            h = 2 * j + par
            dst_lo = h // GQA_GROUP == 0
            piece = pv[:, h * t_new:(h + 1) * t_new, :].reshape(bt * t_new, LANES)
            piece = jnp.where(lane_lo == dst_lo, piece, 0.0)
            if dst_lo != (par == 0):
                piece = pltpu.roll(piece, HEAD_DIM, 1)
            out = piece if out is None else out + piece
        o_ref[:, :, j * LANES:(j + 1) * LANES] = out.reshape(bt, t_new, LANES).astype(o_ref.dtype)
    nk_ref[:, :w - t_new, :] = kc[:, t_new:, :]
    nk_ref[:, w - t_new:, :] = kn
    nv_ref[:, :w - t_new, :] = vc[:, t_new:, :]
    nv_ref[:, w - t_new:, :] = vn


def _ret_log_gamma(h):
    return math.log(1.0 - 2.0 ** (-5.0 - h))


def _ret_decays(chunk, h):
    lg = _ret_log_gamma(h)
    i = lax.broadcasted_iota(jnp.int32, (chunk, chunk), 0)
    j = lax.broadcasted_iota(jnp.int32, (chunk, chunk), 1)
    rel_mask = jnp.where(i >= j, jnp.exp((j + 1).astype(F32) * -lg), 0.0)
    col = lax.broadcasted_iota(jnp.int32, (chunk, 1), 0).astype(F32)
    q_decay = jnp.exp((col + 1.0) * lg)
    k_decay = jnp.exp((chunk - 1.0 - col) * lg)
    return rel_mask, q_decay, k_decay, math.exp(chunk * lg)


def _ret_chunk_all_heads(q, k, v, g, states, decays):
    heads = range(N_RET_HEADS)
    batched = q.ndim == 3
    mm = (lambda a, b: jnp.einsum("bmk,bkn->bmn", a, b, preferred_element_type=F32)) if batched else _dot
    mm_nt = (lambda a, b: jnp.einsum("bmk,bnk->bmn", a, b, preferred_element_type=F32)) if batched else _dot_nt
    qh = [q[..., h * RET_KEY_DIM:(h + 1) * RET_KEY_DIM] for h in heads]
    kh = [k[..., h * RET_KEY_DIM:(h + 1) * RET_KEY_DIM] for h in heads]
    vh = [v[..., h * RET_VAL_DIM:(h + 1) * RET_VAL_DIM] for h in heads]
    qb = [x.astype(BF16) for x in qh]
    vb = [x.astype(BF16) for x in vh]
    inner = [mm_nt(qb[h], kh[h].astype(BF16)) * decays[h][0] for h in heads]
    if batched:
        lhs = [jnp.concatenate([qh[h], inner[h]], axis=-1).astype(BF16) for h in heads]
        rhs = [jnp.concatenate([states[h], vh[h]], axis=-2).astype(BF16) for h in heads]
    else:
        lhs = [jnp.concatenate([qb[h], inner[h].astype(BF16)], axis=-1) for h in heads]
        rhs = [jnp.concatenate([states[h].astype(BF16), vb[h]], axis=-2) for h in heads]
    o = [mm(lhs[h], rhs[h]) * decays[h][1] for h in heads]
    kd_t = [jnp.swapaxes(kh[h].astype(F32) * decays[h][2], -1, -2).astype(BF16) for h in heads]
    new_states = [states[h] * decays[h][3] + mm(kd_t[h], vb[h]) for h in heads]
    outs = []
    for h in heads:
        mu = jnp.mean(o[h], axis=-1, keepdims=True)
        oc = o[h] - mu
        var = jnp.mean(oc * oc, axis=-1, keepdims=True)
        gate = g[..., h * RET_VAL_DIM:(h + 1) * RET_VAL_DIM].astype(F32)
        outs.append(oc * lax.rsqrt(var + GN_EPS) * jax.nn.silu(gate))
    return jnp.concatenate(outs, axis=-1), new_states


def _ret_prompt_kernel(q_ref, k_ref, v_ref, g_ref, o_ref, st_ref, state_ref):
    n = pl.program_id(1)
    C = RET_CHUNK

    @pl.when(n == 0)
    def _():
        state_ref[...] = jnp.zeros_like(state_ref)

    decays = [_ret_decays(C, h) for h in range(N_RET_HEADS)]
    states = [state_ref[h] for h in range(N_RET_HEADS)]
    for c in range(q_ref.shape[0] // C):
        rows = slice(c * C, (c + 1) * C)
        out, states = _ret_chunk_all_heads(q_ref[rows, :], k_ref[rows, :], v_ref[rows, :], g_ref[rows, :],
                                           states, decays)
        o_ref[rows, :] = out.astype(o_ref.dtype)
    for h in range(N_RET_HEADS):
        state_ref[h] = states[h]

    @pl.when(n == pl.num_programs(1) - 1)
    def _():
        st_ref[0] = state_ref[...]


def _ret_prompt(qr, kr, vr, gr, batch, seq):
    ts = MIXER_TILE
    ns = seq // ts
    row = lambda b, n: (b * ns + n, 0)
    return pl.pallas_call(
        _ret_prompt_kernel,
        grid=(batch, ns),
        in_specs=[pl.BlockSpec((ts, RQ_W), row), pl.BlockSpec((ts, RQ_W), row),
                  pl.BlockSpec((ts, RV_W), row), pl.BlockSpec((ts, RV_W), row)],
        out_specs=[pl.BlockSpec((ts, RV_W), row),
                   pl.BlockSpec((1, N_RET_HEADS, RET_KEY_DIM, RET_VAL_DIM), lambda b, n: (b, 0, 0, 0))],
        out_shape=[jax.ShapeDtypeStruct((batch * seq, RV_W), BF16),
                   jax.ShapeDtypeStruct((batch, N_RET_HEADS, RET_KEY_DIM, RET_VAL_DIM), F32)],
        scratch_shapes=[pltpu.VMEM((N_RET_HEADS, RET_KEY_DIM, RET_VAL_DIM), F32)],
        compiler_params=_cparams(2),
        name="ret_prompt",
    )(qr, kr, vr, gr)


def _ret_sample_kernel(q_ref, k_ref, v_ref, g_ref, st_ref, o_ref, nst_ref):
    t_new = q_ref.shape[1]
    decays = [_ret_decays(t_new, h) for h in range(N_RET_HEADS)]
    states = [st_ref[:, h] for h in range(N_RET_HEADS)]
    out, states = _ret_chunk_all_heads(q_ref[...], k_ref[...], v_ref[...], g_ref[...], states, decays)
    o_ref[...] = out.astype(o_ref.dtype)
    for h in range(N_RET_HEADS):
        nst_ref[:, h] = states[h]


def _sample_mixers_kernel(sink_rows_ref, qa_ref, kn_ref, vn_ref, ck_ref, cv_ref,
                          qr_ref, kr_ref, vr_ref, gr_ref, st_ref,
                          oa_ref, nk_ref, nv_ref, or_ref, nst_ref):
    _attn_sample_kernel(sink_rows_ref, qa_ref, kn_ref, vn_ref, ck_ref, cv_ref, oa_ref, nk_ref, nv_ref)
    _ret_sample_kernel(qr_ref, kr_ref, vr_ref, gr_ref, st_ref, or_ref, nst_ref)


def _sample_mixers(qa, ka, va, sinks, cache_k, cache_v, qr, kr, vr, gr, state):
    db, t_new, _ = qa.shape
    w = cache_k.shape[1]
    bt = SAMPLE_BT
    sink_rows = jnp.repeat(sinks.astype(F32), t_new).reshape(N_Q_HEADS * t_new, 1)
    blk = lambda *s: pl.BlockSpec((bt,) + s, lambda i: (i,) + (0,) * len(s))
    st = blk(N_RET_HEADS, RET_KEY_DIM, RET_VAL_DIM)
    return pl.pallas_call(
        _sample_mixers_kernel,
        grid=(db // bt,),
        in_specs=[_const_spec(sink_rows.shape), blk(t_new, Q_W), blk(t_new, KV_W), blk(t_new, KV_W),
                  blk(w, KV_W), blk(w, KV_W),
                  blk(t_new, RQ_W), blk(t_new, RQ_W), blk(t_new, RV_W), blk(t_new, RV_W), st],
        out_specs=[blk(t_new, Q_W), blk(w, KV_W), blk(w, KV_W), blk(t_new, RV_W), st],
        out_shape=[jax.ShapeDtypeStruct((db, t_new, Q_W), F32),
                   jax.ShapeDtypeStruct((db, w, KV_W), F32),
                   jax.ShapeDtypeStruct((db, w, KV_W), F32),
                   jax.ShapeDtypeStruct((db, t_new, RV_W), F32),
                   jax.ShapeDtypeStruct(state.shape, state.dtype)],
        compiler_params=_cparams(1),
        name="sample_mixers",
    )(sink_rows, qa, ka, va, cache_k, cache_v, qr, kr, vr, gr, state)


def _merge_ln_kernel(alpha, ln_row, x_ref, oa_ref, orr_ref, ga_ref, gb_ref, wba_ref, wbr_ref, wo_ref,
                     g_ref, b_ref, o_ref):
    ya = _dot(oa_ref[...].astype(BF16), wba_ref[...])
    yr = _dot(orr_ref[...].astype(BF16), wbr_ref[...])
    merged = (jax.nn.sigmoid(ga_ref[...].astype(F32)) * ya
              + jax.nn.sigmoid(gb_ref[...].astype(F32)) * yr)
    z = _dot(merged.astype(BF16), wo_ref[...])
    o_ref[...] = _layer_norm(alpha * x_ref[...] + z,
                             g_ref[ln_row:ln_row + 1, :], b_ref[ln_row:ln_row + 1, :])


def _merge_ln(x1, oa, orr, ga, gb, wba, wbr, wo, ln_g, ln_b, ln_row, alpha):
    t, d = x1.shape
    tm = DENSE_TILE
    row = lambda i: (i, 0)
    return pl.pallas_call(
        functools.partial(_merge_ln_kernel, alpha, ln_row),
        grid=(t // tm,),
        in_specs=[pl.BlockSpec((tm, d), row), pl.BlockSpec((tm, Q_W), row), pl.BlockSpec((tm, RV_W), row),
                  pl.BlockSpec((tm, d), row), pl.BlockSpec((tm, d), row),
                  _const_spec(wba.shape), _const_spec(wbr.shape), _const_spec(wo.shape),
                  _const_spec(ln_g.shape), _const_spec(ln_b.shape)],
        out_specs=pl.BlockSpec((tm, d), row),
        out_shape=jax.ShapeDtypeStruct((t, d), F32),
        compiler_params=_cparams(1),
        name="merge_ln",
    )(x1, oa, orr, ga, gb, wba, wbr, wo, ln_g, ln_b)


def _rope_tables_for(pos):
    tables = _rope_tables(pos, HEAD_DIM, _attn_inv_freq) + _rope_tables(pos, RET_KEY_DIM, _ret_inv_freq)
    period = pos.shape[0]
    if period < DENSE_TILE:
        tables = tuple(jnp.tile(tb, (DENSE_TILE // period, 1)) for tb in tables)
        period = DENSE_TILE
    return tables, period // DENSE_TILE


def kernel(x_prompt, x_sample, cache_k, cache_v, state_ret, ln_g, ln_b, ffn_wi, ffn_wo, w_in,
           attn_sinks, w_br_attn, w_br_ret, w_o):
    B, S, D = x_prompt.shape
    DB, T, _ = x_sample.shape
    depth = ln_g.shape[0]
    alpha = (2.0 * depth) ** 0.25
    W = cache_k.shape[2]
    tables_p, tab_blocks_p = _rope_tables_for(jnp.arange(S, dtype=jnp.int32))
    tables_s, tab_blocks_s = _rope_tables_for(PAST_LEN + jnp.arange(T, dtype=jnp.int32))

    y_p = x_prompt.reshape(B * S, D)
    y_s = x_sample.reshape(DB * T, D)
    outs = [[] for _ in range(6)]
    for l in range(depth):
        wi, wo = ffn_wi[l], ffn_wo[l]
        w_in_l = w_in[l].astype(BF16)
        wba, wbr, w_o_l = w_br_attn[l].astype(BF16), w_br_ret[l].astype(BF16), w_o[l].astype(BF16)
        sinks = attn_sinks[l].astype(F32)
        g_l, b_l = ln_g[l], ln_b[l]

        x1_p, x1_s = _ffn_ln(y_p, y_s, wi, wo, 0, g_l, b_l, 0, alpha)
        qa_p, ka_p, va_p, qr_p, kr_p, vr_p, gr_p, ga_p, gb_p = _proj_rope(
            x1_p, w_in_l, tables_p, tab_blocks_p, BF16)
        qa_s, ka_s, va_s, qr_s, kr_s, vr_s, gr_s, ga_s, gb_s = _proj_rope(
            x1_s, w_in_l, tables_s, tab_blocks_s, F32)

        oa_p = _attn_prompt(qa_p, ka_p, va_p, sinks, B, S)
        or_p, st_p = _ret_prompt(qr_p, kr_p, vr_p, gr_p, B, S)
        w = min(WINDOW, S)
        nk_p = ka_p.reshape(B, S, N_KV_HEADS, HEAD_DIM)[:, S - w:]
        nv_p = va_p.reshape(B, S, N_KV_HEADS, HEAD_DIM)[:, S - w:]

        r3 = lambda a: a.reshape(DB, T, a.shape[-1])
        oa_s, nk_s, nv_s, or_s, st_s = _sample_mixers(
            r3(qa_s), r3(ka_s), r3(va_s), sinks,
            cache_k[l].reshape(DB, W, KV_W), cache_v[l].reshape(DB, W, KV_W),
            r3(qr_s), r3(kr_s), r3(vr_s), r3(gr_s), state_ret[l])
        oa_s, or_s = oa_s.reshape(DB * T, Q_W), or_s.reshape(DB * T, RV_W)
        shape5 = (DB, W, N_KV_HEADS, HEAD_DIM)

        x2_p = _merge_ln(x1_p, oa_p, or_p, ga_p, gb_p, wba, wbr, w_o_l, g_l, b_l, 1, alpha)
        x2_s = _merge_ln(x1_s, oa_s, or_s, ga_s, gb_s, wba, wbr, w_o_l, g_l, b_l, 1, alpha)
        y_p, y_s = _ffn_ln(x2_p, x2_s, wi, wo, 1, g_l, b_l, 2, alpha)
        for lst, val in zip(outs, (nk_p, nv_p, st_p, nk_s.reshape(shape5), nv_s.reshape(shape5), st_s)):
            lst.append(val)
    return (y_p.reshape(B, S, D), y_s.reshape(DB, T, D)) + tuple(jnp.stack(o) for o in outs)
```

```python
import functools
import math

import jax
import jax.numpy as jnp
from jax import lax
from jax.experimental import pallas as pl
from jax.experimental.pallas import tpu as pltpu

F32 = jnp.float32
BF16 = jnp.bfloat16

PAST_LEN = 8192
N_Q_HEADS = 8
N_KV_HEADS = 2
HEAD_DIM = 64
GQA_GROUP = N_Q_HEADS // N_KV_HEADS
WINDOW = 128
ROPE_THETA = 10000.0
N_RET_HEADS = 4
RET_KEY_DIM = 128
RET_VAL_DIM = 256
RET_CHUNK = 128
RET_THETA = 10000.0
LN_EPS = 1e-5
GN_EPS = 1e-6
NEG_INF = -1e30

Q_W = N_Q_HEADS * HEAD_DIM
KV_W = N_KV_HEADS * HEAD_DIM
RQ_W = N_RET_HEADS * RET_KEY_DIM
RV_W = N_RET_HEADS * RET_VAL_DIM

LANES = 128
VMEM_LIMIT_BYTES = 56 * 1024 * 1024

TOKEN_TILE = 512
DENSE_TILE = 1024
DENSE_MIN_STEPS = 4
FF_CHUNK = 256
ATTN_TILE = 2048
RET_TILE = 1024
SAMPLE_BT = 8
CAST_STEPS = 16


def _dot(a, b):
    return jnp.dot(a, b, preferred_element_type=F32)


def _dot_nt(a, b):
    return lax.dot_general(a, b, (((1,), (1,)), ((), ())), preferred_element_type=F32)


def _layer_norm(y, g, b):
    mu = jnp.mean(y, axis=-1, keepdims=True)
    yc = y - mu
    var = jnp.mean(yc * yc, axis=-1, keepdims=True)
    return yc * lax.rsqrt(var + LN_EPS) * g + b


def _cparams(n_axes, semantics=None):
    return pltpu.CompilerParams(
        dimension_semantics=semantics or ("arbitrary",) * n_axes,
        vmem_limit_bytes=VMEM_LIMIT_BYTES)


def _const_spec(shape):
    nd = len(shape)
    return pl.BlockSpec(shape, lambda *_: (0,) * nd, pipeline_mode=pl.Buffered(1))


def _segment_specs(n_prompt_tiles, tm, width, tile_of):
    prompt = pl.BlockSpec((tm, width), lambda i: (jnp.minimum(tile_of(i), n_prompt_tiles - 1), 0))
    sample = pl.BlockSpec((tm, width), lambda i: (jnp.maximum(tile_of(i) - n_prompt_tiles, 0), 0))
    return prompt, sample


def _store_segment(is_prompt, prompt_ref, sample_ref, value):
    @pl.when(is_prompt)
    def _():
        prompt_ref[...] = value.astype(prompt_ref.dtype)

    @pl.when(jnp.logical_not(is_prompt))
    def _():
        sample_ref[...] = value.astype(sample_ref.dtype)


def _ffn_chunk(xb, w_gate, w_up, w_out):
    act = (jax.nn.silu(_dot(xb, w_gate)) * _dot(xb, w_up)).astype(BF16)
    return _dot(act, w_out)


def _ffn_tile_maps(n_chunks, n_tiles):
    matmul_tile = lambda i: jnp.clip(i - (n_chunks - 1), 0, n_tiles - 1)
    norm_tile = lambda i: jnp.clip(i - n_chunks, 0, n_tiles - 1)
    return matmul_tile, norm_tile


def _ffn_ln_kernel(alpha, n_chunks, n_tiles, ln_row, n_prompt_tiles, n_cast, *refs):
    xp_ref, xs_ref, wg_ref, wu_ref, wo_ref, g_ref, b_ref = refs[:7]
    cast_in = refs[7:7 + n_cast]
    op_ref, os_ref = refs[7 + n_cast:9 + n_cast]
    cast_out = refs[9 + n_cast:9 + 2 * n_cast]
    acc_ref, y_ref, wi_bf_ref, wo_bf_ref = refs[9 + 2 * n_cast:]
    step = pl.program_id(0)
    matmul_tile, norm_tile = _ffn_tile_maps(n_chunks, n_tiles)
    last_step = n_chunks + n_tiles - 1
    fc = FF_CHUNK

    norm_is_prompt = norm_tile(step) < n_prompt_tiles

    def norm_previous():
        return _layer_norm(y_ref[...], g_ref[ln_row:ln_row + 1, :], b_ref[ln_row:ln_row + 1, :])

    @pl.when(step < n_chunks)
    def _():
        w_gate, w_up, w_out = wg_ref[...].astype(BF16), wu_ref[...].astype(BF16), wo_ref[...].astype(BF16)
        wi_bf_ref[step, :, :fc] = w_gate
        wi_bf_ref[step, :, fc:] = w_up
        wo_bf_ref[step] = w_out
        x = xp_ref[...]
        part = _ffn_chunk(x.astype(BF16), w_gate, w_up, w_out)

        @pl.when(step == 0)
        def _():
            acc_ref[...] = part

        @pl.when(step > 0)
        def _():
            acc_ref[...] += part

        @pl.when(step == n_chunks - 1)
        def _():
            y_ref[...] = alpha * x + 0.5 * acc_ref[...]

    def full_tile_step(out_ref):
        x = jnp.where(matmul_tile(step) < n_prompt_tiles, xp_ref[...], xs_ref[...])
        xb = x.astype(BF16)
        for c in range(n_chunks):
            part = _ffn_chunk(xb, wi_bf_ref[c, :, :fc], wi_bf_ref[c, :, fc:], wo_bf_ref[c])
            if c == 0:
                acc_ref[...] = part
            else:
                acc_ref[...] += part
        out_ref[...] = norm_previous()
        y_ref[...] = alpha * x + 0.5 * acc_ref[...]

    in_full_steps = (step >= n_chunks) & (step < last_step)

    if n_cast:
        @pl.when((step >= n_chunks) & (step < n_chunks + CAST_STEPS))
        def _():
            for src, dst in zip(cast_in, cast_out):
                dst[...] = src[...].astype(dst.dtype)

    pl.when(in_full_steps & norm_is_prompt)(functools.partial(full_tile_step, op_ref))
    pl.when(in_full_steps & jnp.logical_not(norm_is_prompt))(functools.partial(full_tile_step, os_ref))

    @pl.when(step == last_step)
    def _():
        _store_segment(norm_is_prompt, op_ref, os_ref, norm_previous())


def _ffn_ln(xp, xs, wi, wo, half, ln_g, ln_b, ln_row, alpha, cast_weights=()):
    d = xp.shape[1]
    d_ff = wo.shape[1]
    tm, fc = TOKEN_TILE, FF_CHUNK
    assert xp.shape[0] % tm == 0 and xs.shape[0] % tm == 0 and d_ff % fc == 0
    n_p, n_s = xp.shape[0] // tm, xs.shape[0] // tm
    n_chunks = d_ff // fc
    n_tiles = n_p + n_s
    assert n_tiles - 1 >= CAST_STEPS
    matmul_tile, norm_tile = _ffn_tile_maps(n_chunks, n_tiles)
    chunk_of = lambda i: jnp.minimum(i, n_chunks - 1)
    cast_specs = []
    for w in cast_weights:
        rows = w.shape[0] // CAST_STEPS
        assert w.shape[0] % CAST_STEPS == 0 and rows % 16 == 0
        cast_specs.append(pl.BlockSpec((rows, w.shape[1]),
                                       lambda i: (jnp.clip(i - n_chunks, 0, CAST_STEPS - 1), 0)))
    return pl.pallas_call(
        functools.partial(_ffn_ln_kernel, alpha, n_chunks, n_tiles, ln_row, n_p, len(cast_weights)),
        grid=(n_chunks + n_tiles,),
        in_specs=[*_segment_specs(n_p, tm, d, matmul_tile),
                  pl.BlockSpec((None, d, fc), lambda i: (half, 0, chunk_of(i))),
                  pl.BlockSpec((None, d, fc), lambda i: (half, 0, n_chunks + chunk_of(i))),
                  pl.BlockSpec((None, fc, d), lambda i: (half, chunk_of(i), 0)),
                  _const_spec(ln_g.shape), _const_spec(ln_b.shape), *cast_specs],
        out_specs=[*_segment_specs(n_p, tm, d, norm_tile), *cast_specs],
        out_shape=[jax.ShapeDtypeStruct(xp.shape, F32), jax.ShapeDtypeStruct(xs.shape, F32)]
                  + [jax.ShapeDtypeStruct(w.shape, BF16) for w in cast_weights],
        scratch_shapes=[pltpu.VMEM((tm, d), F32), pltpu.VMEM((tm, d), F32),
                        pltpu.VMEM((n_chunks, d, 2 * fc), BF16),
                        pltpu.VMEM((n_chunks, fc, d), BF16)],
        compiler_params=_cparams(1),
        name="ffn_ln",
    )(xp, xs, wi, wi, wo, ln_g, ln_b, *cast_weights)


def _rope_tables(pos, dim, theta_fn):
    half = dim // 2
    inv_freq = theta_fn(half)
    ang = pos.astype(F32)[:, None] * inv_freq[None, :]
    reps = LANES // half
    cos = jnp.tile(jnp.cos(ang), (1, reps))
    sign = jnp.tile(jnp.concatenate([-jnp.ones((half,), F32), jnp.ones((half,), F32)]), LANES // dim)
    sin = jnp.tile(jnp.sin(ang), (1, reps)) * sign[None, :]
    return cos, sin


def _attn_inv_freq(half):
    return 1.0 / (ROPE_THETA ** (jnp.arange(0, 2 * half, 2, dtype=F32) / (2 * half)))


def _ret_inv_freq(half):
    return 1.0 / (RET_THETA ** jnp.linspace(0.0, 1.0, half, dtype=F32))


def _rope_block(x, cos, sin, dim):
    half = dim // 2
    if dim == LANES:
        rot = pltpu.roll(x, half, 1)
    else:
        lane = lax.broadcasted_iota(jnp.int32, x.shape, 1)
        first_half = (lane % dim) < half
        rot = jnp.where(first_half, pltpu.roll(x, LANES - half, 1), pltpu.roll(x, half, 1))
    return x * cos + rot * sin


def _proj_rope_kernel(x_ref, w_ref, cosa_ref, sina_ref, cosr_ref, sinr_ref,
                      qa_ref, ka_ref, va_ref, qr_ref, kr_ref, vr_ref, gr_ref, ga_ref, gb_ref):
    xb = x_ref[...].astype(BF16)
    cosa, sina = cosa_ref[...], sina_ref[...]
    cosr, sinr = cosr_ref[...], sinr_ref[...]
    off = 0

    y = _dot(xb, w_ref[:, off:off + Q_W])
    for j in range(Q_W // LANES):
        qa_ref[:, j * LANES:(j + 1) * LANES] = _rope_block(
            y[:, j * LANES:(j + 1) * LANES], cosa, sina, HEAD_DIM).astype(qa_ref.dtype)
    off += Q_W
    y = _dot(xb, w_ref[:, off:off + 2 * KV_W])
    ka_ref[...] = _rope_block(y[:, :KV_W], cosa, sina, HEAD_DIM)
    va_ref[...] = y[:, KV_W:]
    off += 2 * KV_W
    y = _dot(xb, w_ref[:, off:off + RQ_W])
    for j in range(RQ_W // LANES):
        qr_ref[:, j * LANES:(j + 1) * LANES] = _rope_block(
            y[:, j * LANES:(j + 1) * LANES], cosr, sinr, RET_KEY_DIM).astype(qr_ref.dtype)
    off += RQ_W
    y = _dot(xb, w_ref[:, off:off + RQ_W])
    for j in range(RQ_W // LANES):
        kr = _rope_block(y[:, j * LANES:(j + 1) * LANES], cosr, sinr, RET_KEY_DIM)
        kr_ref[:, j * LANES:(j + 1) * LANES] = (kr * (RET_KEY_DIM ** -0.5)).astype(kr_ref.dtype)
    off += RQ_W
    for ref in (vr_ref, gr_ref, ga_ref, gb_ref):
        width = ref.shape[1]
        for c in range(width // 512):
            ref[:, c * 512:(c + 1) * 512] = _dot(
                xb, w_ref[:, off + c * 512:off + (c + 1) * 512]).astype(ref.dtype)
        off += width


def _dense_tile(tokens):
    return min(DENSE_TILE, tokens // DENSE_MIN_STEPS)


def _proj_rope(x1, w_in, tables, mixer_dtype):
    t, d = x1.shape
    n_in = w_in.shape[1]
    d_model = (n_in - Q_W - 2 * KV_W - 2 * RQ_W - 2 * RV_W) // 2
    tm = _dense_tile(t)
    row = lambda i: (i, 0)
    table_blocks = tables[0].shape[0] // tm
    tab = lambda i: (i % table_blocks, 0)
    widths = [(Q_W, mixer_dtype), (KV_W, F32), (KV_W, F32), (RQ_W, mixer_dtype), (RQ_W, mixer_dtype),
              (RV_W, mixer_dtype), (RV_W, mixer_dtype), (d_model, BF16), (d_model, BF16)]
    return pl.pallas_call(
        _proj_rope_kernel,
        grid=(t // tm,),
        in_specs=[pl.BlockSpec((tm, d), row), _const_spec(w_in.shape)]
                 + [pl.BlockSpec((tm, LANES), tab)] * 4,
        out_specs=[pl.BlockSpec((tm, w), row) for w, _ in widths],
        out_shape=[jax.ShapeDtypeStruct((t, w), dt) for w, dt in widths],
        compiler_params=_cparams(1),
        name="proj_rope",
    )(x1, w_in, *tables)


def _kv_lane_variants(x):
    lane = lax.broadcasted_iota(jnp.int32, x.shape, 1)
    lo = lane < HEAD_DIM
    swapped = pltpu.roll(x, HEAD_DIM, 1)
    zero = jnp.zeros_like(x)
    kv0 = (jnp.where(lo, x, zero).astype(BF16), jnp.where(lo, zero, swapped).astype(BF16))
    kv1 = (jnp.where(lo, swapped, zero).astype(BF16), jnp.where(lo, zero, x).astype(BF16))
    return kv0, kv1


def _attn_prompt_kernel(sinks_ref, q_ref, kc_ref, kp_ref, vc_ref, vp_ref, o_ref):
    n = pl.program_id(1)
    L = WINDOW
    k_full = jnp.concatenate([kp_ref[...], kc_ref[...]], axis=0)
    v_full = jnp.concatenate([vp_ref[...], vc_ref[...]], axis=0)
    k_var = _kv_lane_variants(k_full)
    v_var = _kv_lane_variants(v_full)
    own = (lax.broadcasted_iota(jnp.int32, (L, L), 1)
           <= lax.broadcasted_iota(jnp.int32, (L, L), 0))
    scale = HEAD_DIM ** -0.5

    units = [(i, j, par) for i in range(q_ref.shape[0] // L) for j in range(Q_W // LANES) for par in range(2)]

    def scores(unit):
        i, j, par = unit
        q2 = q_ref[i * L:(i + 1) * L, j * LANES:(j + 1) * LANES] * scale
        return _dot_nt(q2, k_var[(2 * j) // GQA_GROUP][par][i * L:(i + 2) * L])

    s_next = scores(units[0])
    out = None
    for idx, (i, j, par) in enumerate(units):
        s = s_next
        if idx + 1 < len(units):
            s_next = scores(units[idx + 1])
        s_prev = s[:, :L]
        if i == 0:
            s_prev = jnp.where(n > 0, s_prev, NEG_INF)
        sw = jnp.where(own, s[:, L:], s_prev)
        sink = sinks_ref[2 * j + par]
        m = jnp.maximum(jnp.max(sw, axis=-1, keepdims=True), sink)
        e = jnp.exp(sw - m)
        denom = jnp.sum(e, axis=-1, keepdims=True) + jnp.exp(sink - m)
        e_band = jnp.concatenate([jnp.where(own, 0.0, e), jnp.where(own, e, 0.0)], axis=1)
        r = _dot(e_band.astype(BF16), v_var[(2 * j) // GQA_GROUP][par][i * L:(i + 2) * L]) / denom
        if par == 0:
            out = r
        else:
            o_ref[i * L:(i + 1) * L, j * LANES:(j + 1) * LANES] = (out + r).astype(o_ref.dtype)


def _attn_prompt(qa, ka, va, sinks, batch, seq):
    tq = ATTN_TILE
    nq = seq // tq
    per = tq // WINDOW
    cur = lambda b, n: (b * nq + n, 0)
    prev = lambda b, n: ((b * nq + n) * per - jnp.minimum(n, 1), 0)
    return pl.pallas_call(
        _attn_prompt_kernel,
        grid=(batch, nq),
        in_specs=[pl.BlockSpec(memory_space=pltpu.SMEM),
                  pl.BlockSpec((tq, Q_W), cur),
                  pl.BlockSpec((tq, KV_W), cur), pl.BlockSpec((WINDOW, KV_W), prev),
                  pl.BlockSpec((tq, KV_W), cur), pl.BlockSpec((WINDOW, KV_W), prev)],
        out_specs=pl.BlockSpec((tq, Q_W), cur),
        out_shape=jax.ShapeDtypeStruct(qa.shape, BF16),
        compiler_params=_cparams(2),
        name="attn_prompt",
    )(sinks, qa, ka, ka, va, va)


def _attn_sample_kernel(sink_rows_ref, q_ref, kn_ref, vn_ref, ck_ref, cv_ref,
                        o_ref, nk_ref, nv_ref):
    bt, t_new, _ = q_ref.shape
    w = ck_ref.shape[1]
    n_rows = N_Q_HEADS * t_new
    lane_lo = lax.broadcasted_iota(jnp.int32, (bt * t_new, LANES), 1) < HEAD_DIM
    r_i = lax.broadcasted_iota(jnp.int32, (n_rows, w + t_new), 0) % t_new
    m_i = lax.broadcasted_iota(jnp.int32, (n_rows, w + t_new), 1)
    mask = ((m_i < w) & (m_i > r_i + (w - WINDOW))) | ((m_i >= w) & (m_i - w <= r_i))
    sink = sink_rows_ref[...]
    scale = HEAD_DIM ** -0.5
    kc, kn = ck_ref[...], kn_ref[...]
    vc, vn = cv_ref[...], vn_ref[...]
    k_all = jnp.concatenate([kc, kn], axis=1).astype(BF16)
    v_all = jnp.concatenate([vc, vn], axis=1).astype(BF16)
    q = q_ref[...].reshape(bt * t_new, Q_W)
    pieces = []
    for h in range(N_Q_HEADS):
        q2 = q[:, (h // 2) * LANES:(h // 2 + 1) * LANES]
        src_lo = h % 2 == 0
        dst_lo = h // GQA_GROUP == 0
        if src_lo != dst_lo:
            q2 = pltpu.roll(q2, HEAD_DIM, 1)
        pieces.append(jnp.where(lane_lo == dst_lo, q2, 0.0).reshape(bt, t_new, LANES))
    q_rows = (jnp.concatenate(pieces, axis=1) * scale).astype(BF16)
    s = jnp.einsum("bqd,bkd->bqk", q_rows, k_all, preferred_element_type=F32)
    s = jnp.where(mask, s, NEG_INF)
    m = jnp.maximum(jnp.max(s, axis=-1, keepdims=True), sink)
    e = jnp.exp(s - m)
    denom = jnp.sum(e, axis=-1, keepdims=True) + jnp.exp(sink - m)
    pv = jnp.einsum("bqk,bkd->bqd", e.astype(BF16), v_all, preferred_element_type=F32) / denom
    for j in range(Q_W // LANES):
        out = None
        for par in range(2):
            h = 2 * j + par
            dst_lo = h // GQA_GROUP == 0
            piece = pv[:, h * t_new:(h + 1) * t_new, :].reshape(bt * t_new, LANES)
            piece = jnp.where(lane_lo == dst_lo, piece, 0.0)
            if dst_lo != (par == 0):
                piece = pltpu.roll(piece, HEAD_DIM, 1)
            out = piece if out is None else out + piece
        o_ref[:, :, j * LANES:(j + 1) * LANES] = out.reshape(bt, t_new, LANES).astype(o_ref.dtype)
    nk_ref[:, :w - t_new, :] = kc[:, t_new:, :]
    nk_ref[:, w - t_new:, :] = kn
    nv_ref[:, :w - t_new, :] = vc[:, t_new:, :]
    nv_ref[:, w - t_new:, :] = vn


def _ret_log_gamma(h):
    return math.log(1.0 - 2.0 ** (-5.0 - h))


def _ret_decays(chunk, h):
    lg = _ret_log_gamma(h)
    i = lax.broadcasted_iota(jnp.int32, (chunk, chunk), 0)
    j = lax.broadcasted_iota(jnp.int32, (chunk, chunk), 1)
    rel_mask = jnp.where(i >= j, jnp.exp((j + 1).astype(F32) * -lg), 0.0)
    col = lax.broadcasted_iota(jnp.int32, (chunk, 1), 0).astype(F32)
    q_decay = jnp.exp((col + 1.0) * lg)
    k_decay = jnp.exp((chunk - 1.0 - col) * lg)
    return rel_mask, q_decay, k_decay, math.exp(chunk * lg)


def _ret_chunk_all_heads(q, k, v, g, states, decays):
    heads = range(N_RET_HEADS)
    batched = q.ndim == 3
    mm = (lambda a, b: jnp.einsum("bmk,bkn->bmn", a, b, preferred_element_type=F32)) if batched else _dot
    mm_nt = (lambda a, b: jnp.einsum("bmk,bnk->bmn", a, b, preferred_element_type=F32)) if batched else _dot_nt
    qh = [q[..., h * RET_KEY_DIM:(h + 1) * RET_KEY_DIM] for h in heads]
    kh = [k[..., h * RET_KEY_DIM:(h + 1) * RET_KEY_DIM] for h in heads]
    vh = [v[..., h * RET_VAL_DIM:(h + 1) * RET_VAL_DIM] for h in heads]
    qb = [x.astype(BF16) for x in qh]
    vb = [x.astype(BF16) for x in vh]
    inner = [mm_nt(qb[h], kh[h].astype(BF16)) * decays[h][0] for h in heads]
    if batched:
        lhs = [jnp.concatenate([qh[h], inner[h]], axis=-1).astype(BF16) for h in heads]
        rhs = [jnp.concatenate([states[h], vh[h]], axis=-2).astype(BF16) for h in heads]
    else:
        lhs = [jnp.concatenate([qb[h], inner[h].astype(BF16)], axis=-1) for h in heads]
        rhs = [jnp.concatenate([states[h].astype(BF16), vb[h]], axis=-2) for h in heads]
    o = [mm(lhs[h], rhs[h]) * decays[h][1] for h in heads]
    kd_t = [jnp.swapaxes(kh[h].astype(F32) * decays[h][2], -1, -2).astype(BF16) for h in heads]
    new_states = [states[h] * decays[h][3] + mm(kd_t[h], vb[h]) for h in heads]
    outs = []
    for h in heads:
        mu = jnp.mean(o[h], axis=-1, keepdims=True)
        oc = o[h] - mu
        var = jnp.mean(oc * oc, axis=-1, keepdims=True)
        gate = g[..., h * RET_VAL_DIM:(h + 1) * RET_VAL_DIM].astype(F32)
        outs.append(oc * lax.rsqrt(var + GN_EPS) * jax.nn.silu(gate))
    return jnp.concatenate(outs, axis=-1), new_states


def _ret_prompt_kernel(q_ref, k_ref, v_ref, g_ref, o_ref, st_ref, state_ref):
    n = pl.program_id(1)
    C = RET_CHUNK

    @pl.when(n == 0)
    def _():
        state_ref[...] = jnp.zeros_like(state_ref)

    decays = [_ret_decays(C, h) for h in range(N_RET_HEADS)]
    states = [state_ref[h] for h in range(N_RET_HEADS)]
    for c in range(q_ref.shape[0] // C):
        rows = slice(c * C, (c + 1) * C)
        out, states = _ret_chunk_all_heads(q_ref[rows, :], k_ref[rows, :], v_ref[rows, :], g_ref[rows, :],
                                           states, decays)
        o_ref[rows, :] = out.astype(o_ref.dtype)
    for h in range(N_RET_HEADS):
        state_ref[h] = states[h]

    @pl.when(n == pl.num_programs(1) - 1)
    def _():
        st_ref[0] = state_ref[...]


def _ret_prompt(qr, kr, vr, gr, batch, seq):
    ts = RET_TILE
    ns = seq // ts
    row = lambda b, n: (b * ns + n, 0)
    return pl.pallas_call(
        _ret_prompt_kernel,
        grid=(batch, ns),
        in_specs=[pl.BlockSpec((ts, RQ_W), row), pl.BlockSpec((ts, RQ_W), row),
                  pl.BlockSpec((ts, RV_W), row), pl.BlockSpec((ts, RV_W), row)],
        out_specs=[pl.BlockSpec((ts, RV_W), row),
                   pl.BlockSpec((1, N_RET_HEADS, RET_KEY_DIM, RET_VAL_DIM), lambda b, n: (b, 0, 0, 0))],
        out_shape=[jax.ShapeDtypeStruct((batch * seq, RV_W), BF16),
                   jax.ShapeDtypeStruct((batch, N_RET_HEADS, RET_KEY_DIM, RET_VAL_DIM), F32)],
        scratch_shapes=[pltpu.VMEM((N_RET_HEADS, RET_KEY_DIM, RET_VAL_DIM), F32)],
        compiler_params=_cparams(2),
        name="ret_prompt",
    )(qr, kr, vr, gr)


def _ret_sample_kernel(q_ref, k_ref, v_ref, g_ref, st_ref, o_ref, nst_ref):
    t_new = q_ref.shape[1]
    decays = [_ret_decays(t_new, h) for h in range(N_RET_HEADS)]
    states = [st_ref[:, h] for h in range(N_RET_HEADS)]
    out, states = _ret_chunk_all_heads(q_ref[...], k_ref[...], v_ref[...], g_ref[...], states, decays)
    o_ref[...] = out.astype(o_ref.dtype)
    for h in range(N_RET_HEADS):
        nst_ref[:, h] = states[h]


def _sample_mixers_kernel(sink_rows_ref, qa_ref, kn_ref, vn_ref, ck_ref, cv_ref,
                          qr_ref, kr_ref, vr_ref, gr_ref, st_ref,
                          oa_ref, nk_ref, nv_ref, or_ref, nst_ref):
    _attn_sample_kernel(sink_rows_ref, qa_ref, kn_ref, vn_ref, ck_ref, cv_ref, oa_ref, nk_ref, nv_ref)
    _ret_sample_kernel(qr_ref, kr_ref, vr_ref, gr_ref, st_ref, or_ref, nst_ref)


def _sample_mixers(qa, ka, va, sinks, cache_k, cache_v, qr, kr, vr, gr, state):
    db, t_new, _ = qa.shape
    w = cache_k.shape[1]
    bt = SAMPLE_BT
    sink_rows = jnp.repeat(sinks.astype(F32), t_new).reshape(N_Q_HEADS * t_new, 1)
    blk = lambda *s: pl.BlockSpec((bt,) + s, lambda i: (i,) + (0,) * len(s))
    st = blk(N_RET_HEADS, RET_KEY_DIM, RET_VAL_DIM)
    return pl.pallas_call(
        _sample_mixers_kernel,
        grid=(db // bt,),
        in_specs=[_const_spec(sink_rows.shape), blk(t_new, Q_W), blk(t_new, KV_W), blk(t_new, KV_W),
                  blk(w, KV_W), blk(w, KV_W),
                  blk(t_new, RQ_W), blk(t_new, RQ_W), blk(t_new, RV_W), blk(t_new, RV_W), st],
        out_specs=[blk(t_new, Q_W), blk(w, KV_W), blk(w, KV_W), blk(t_new, RV_W), st],
        out_shape=[jax.ShapeDtypeStruct((db, t_new, Q_W), F32),
                   jax.ShapeDtypeStruct((db, w, KV_W), F32),
                   jax.ShapeDtypeStruct((db, w, KV_W), F32),
                   jax.ShapeDtypeStruct((db, t_new, RV_W), F32),
                   jax.ShapeDtypeStruct(state.shape, state.dtype)],
        compiler_params=_cparams(1),
        name="sample_mixers",
    )(sink_rows, qa, ka, va, cache_k, cache_v, qr, kr, vr, gr, state)


def _merge_ln_kernel(alpha, ln_row, x_ref, oa_ref, orr_ref, ga_ref, gb_ref, wba_ref, wbr_ref, wo_ref,
                     g_ref, b_ref, o_ref):
    ya = _dot(oa_ref[...].astype(BF16), wba_ref[...])
    yr = _dot(orr_ref[...].astype(BF16), wbr_ref[...])
    merged = (jax.nn.sigmoid(ga_ref[...].astype(F32)) * ya
              + jax.nn.sigmoid(gb_ref[...].astype(F32)) * yr)
    z = _dot(merged.astype(BF16), wo_ref[...])
    o_ref[...] = _layer_norm(alpha * x_ref[...] + z,
                             g_ref[ln_row:ln_row + 1, :], b_ref[ln_row:ln_row + 1, :])


def _merge_ln(x1, oa, orr, ga, gb, wba, wbr, wo, ln_g, ln_b, ln_row, alpha):
    t, d = x1.shape
    tm = _dense_tile(t)
    row = lambda i: (i, 0)
    return pl.pallas_call(
        functools.partial(_merge_ln_kernel, alpha, ln_row),
        grid=(t // tm,),
        in_specs=[pl.BlockSpec((tm, d), row), pl.BlockSpec((tm, Q_W), row), pl.BlockSpec((tm, RV_W), row),
                  pl.BlockSpec((tm, d), row), pl.BlockSpec((tm, d), row),
                  _const_spec(wba.shape), _const_spec(wbr.shape), _const_spec(wo.shape),
                  _const_spec(ln_g.shape), _const_spec(ln_b.shape)],
        out_specs=pl.BlockSpec((tm, d), row),
        out_shape=jax.ShapeDtypeStruct((t, d), F32),
        compiler_params=_cparams(1),
        name="merge_ln",
    )(x1, oa, orr, ga, gb, wba, wbr, wo, ln_g, ln_b)


def _rope_tables_for(pos):
    tables = _rope_tables(pos, HEAD_DIM, _attn_inv_freq) + _rope_tables(pos, RET_KEY_DIM, _ret_inv_freq)
    period = pos.shape[0]
    if period < DENSE_TILE:
        tables = tuple(jnp.tile(tb, (DENSE_TILE // period, 1)) for tb in tables)
    return tables


def kernel(x_prompt, x_sample, cache_k, cache_v, state_ret, ln_g, ln_b, ffn_wi, ffn_wo, w_in,
           attn_sinks, w_br_attn, w_br_ret, w_o):
    B, S, D = x_prompt.shape
    DB, T, _ = x_sample.shape
    depth = ln_g.shape[0]
    alpha = (2.0 * depth) ** 0.25
    W = cache_k.shape[2]
    tables_p = _rope_tables_for(jnp.arange(S, dtype=jnp.int32))
    tables_s = _rope_tables_for(PAST_LEN + jnp.arange(T, dtype=jnp.int32))

    y_p = x_prompt.reshape(B * S, D)
    y_s = x_sample.reshape(DB * T, D)
    outs = [[] for _ in range(6)]
    for l in range(depth):
        wi, wo = ffn_wi[l], ffn_wo[l]
        sinks = attn_sinks[l].astype(F32)
        g_l, b_l = ln_g[l], ln_b[l]

        x1_p, x1_s, w_in_l, wba, wbr, w_o_l = _ffn_ln(
            y_p, y_s, wi, wo, 0, g_l, b_l, 0, alpha,
            cast_weights=(w_in[l], w_br_attn[l], w_br_ret[l], w_o[l]))
        qa_p, ka_p, va_p, qr_p, kr_p, vr_p, gr_p, ga_p, gb_p = _proj_rope(
            x1_p, w_in_l, tables_p, BF16)
        qa_s, ka_s, va_s, qr_s, kr_s, vr_s, gr_s, ga_s, gb_s = _proj_rope(
            x1_s, w_in_l, tables_s, F32)

        oa_p = _attn_prompt(qa_p, ka_p, va_p, sinks, B, S)
        or_p, st_p = _ret_prompt(qr_p, kr_p, vr_p, gr_p, B, S)
        w = min(WINDOW, S)
        nk_p = ka_p.reshape(B, S, N_KV_HEADS, HEAD_DIM)[:, S - w:]
        nv_p = va_p.reshape(B, S, N_KV_HEADS, HEAD_DIM)[:, S - w:]

        r3 = lambda a: a.reshape(DB, T, a.shape[-1])
        oa_s, nk_s, nv_s, or_s, st_s = _sample_mixers(
            r3(qa_s), r3(ka_s), r3(va_s), sinks,
            cache_k[l].reshape(DB, W, KV_W), cache_v[l].reshape(DB, W, KV_W),
            r3(qr_s), r3(kr_s), r3(vr_s), r3(gr_s), state_ret[l])
        oa_s, or_s = oa_s.reshape(DB * T, Q_W), or_s.reshape(DB * T, RV_W)
        shape5 = (DB, W, N_KV_HEADS, HEAD_DIM)

        x2_p = _merge_ln(x1_p, oa_p, or_p, ga_p, gb_p, wba, wbr, w_o_l, g_l, b_l, 1, alpha)
        x2_s = _merge_ln(x1_s, oa_s, or_s, ga_s, gb_s, wba, wbr, w_o_l, g_l, b_l, 1, alpha)
        y_p, y_s = _ffn_ln(x2_p, x2_s, wi, wo, 1, g_l, b_l, 2, alpha)
        for lst, val in zip(outs, (nk_p, nv_p, st_p, nk_s.reshape(shape5), nv_s.reshape(shape5), st_s)):
            lst.append(val)
    return (y_p.reshape(B, S, D), y_s.reshape(DB, T, D)) + tuple(jnp.stack(o) for o in outs)
```

```python
import functools
import math

import jax
import jax.numpy as jnp
from jax import lax
from jax.experimental import pallas as pl
from jax.experimental.pallas import tpu as pltpu

F32 = jnp.float32
BF16 = jnp.bfloat16

PAST_LEN = 8192
N_Q_HEADS = 8
N_KV_HEADS = 2
HEAD_DIM = 64
GQA_GROUP = N_Q_HEADS // N_KV_HEADS
WINDOW = 128
ROPE_THETA = 10000.0
N_RET_HEADS = 4
RET_KEY_DIM = 128
RET_VAL_DIM = 256
RET_CHUNK = 128
RET_THETA = 10000.0
LN_EPS = 1e-5
GN_EPS = 1e-6
NEG_INF = -1e30

Q_W = N_Q_HEADS * HEAD_DIM
KV_W = N_KV_HEADS * HEAD_DIM
RQ_W = N_RET_HEADS * RET_KEY_DIM
RV_W = N_RET_HEADS * RET_VAL_DIM

LANES = 128
SUBLANES = 8
VMEM_LIMIT_BYTES = 56 * 1024 * 1024

TOKEN_TILE = 512
DENSE_TILE = 1024
DENSE_MIN_STEPS = 4
FF_CHUNK = 256
ATTN_TILE = 2048
RET_TILE = 1024
LN_ANCHOR_CHUNK = 8
MERGE_ROWS = 256
SAMPLE_BT = 8
CAST_STEPS = 16


def _dot(a, b):
    return jnp.dot(a, b, preferred_element_type=F32)


def _dot_nt(a, b):
    return lax.dot_general(a, b, (((1,), (1,)), ((), ())), preferred_element_type=F32)


def _layer_norm(y, g, b):
    mu = jnp.mean(y, axis=-1, keepdims=True)
    yc = y - mu
    var = jnp.mean(yc * yc, axis=-1, keepdims=True)
    return yc * lax.rsqrt(var + LN_EPS) * g + b


def _cparams(n_axes, semantics=None):
    return pltpu.CompilerParams(
        dimension_semantics=semantics or ("arbitrary",) * n_axes,
        vmem_limit_bytes=VMEM_LIMIT_BYTES)


def _const_spec(shape):
    nd = len(shape)
    return pl.BlockSpec(shape, lambda *_: (0,) * nd, pipeline_mode=pl.Buffered(1))


def _segment_specs(n_prompt_tiles, tm, width, tile_of):
    prompt = pl.BlockSpec((tm, width), lambda i: (jnp.minimum(tile_of(i), n_prompt_tiles - 1), 0))
    sample = pl.BlockSpec((tm, width), lambda i: (jnp.maximum(tile_of(i) - n_prompt_tiles, 0), 0))
    return prompt, sample


def _store_segment(is_prompt, prompt_ref, sample_ref, value):
    @pl.when(is_prompt)
    def _():
        prompt_ref[...] = value.astype(prompt_ref.dtype)

    @pl.when(jnp.logical_not(is_prompt))
    def _():
        sample_ref[...] = value.astype(sample_ref.dtype)


def _ffn_chunk(xb, w_gate, w_up, w_out):
    act = (jax.nn.silu(_dot(xb, w_gate)) * _dot(xb, w_up)).astype(BF16)
    return _dot(act, w_out)


def _ffn_tile_maps(n_chunks, n_tiles):
    matmul_tile = lambda i: jnp.clip(i - (n_chunks - 1), 0, n_tiles - 1)
    norm_tile = lambda i: jnp.clip(i - n_chunks, 0, n_tiles - 1)
    return matmul_tile, norm_tile


def _ffn_ln_kernel(alpha, n_chunks, n_tiles, ln_row, n_prompt_tiles, n_cast, *refs):
    xp_ref, xs_ref, wg_ref, wu_ref, wo_ref, g_ref, b_ref = refs[:7]
    cast_in = refs[7:7 + n_cast]
    op_ref, os_ref = refs[7 + n_cast:9 + n_cast]
    cast_out = refs[9 + n_cast:9 + 2 * n_cast]
    acc_ref, y_ref, wi_bf_ref, wo_bf_ref = refs[9 + 2 * n_cast:]
    step = pl.program_id(0)
    matmul_tile, norm_tile = _ffn_tile_maps(n_chunks, n_tiles)
    last_step = n_chunks + n_tiles - 1
    fc = FF_CHUNK

    norm_is_prompt = norm_tile(step) < n_prompt_tiles

    def norm_previous():
        return _layer_norm(y_ref[...], g_ref[ln_row:ln_row + 1, :], b_ref[ln_row:ln_row + 1, :])

    @pl.when(step < n_chunks)
    def _():
        w_gate, w_up, w_out = wg_ref[...].astype(BF16), wu_ref[...].astype(BF16), wo_ref[...].astype(BF16)
        wi_bf_ref[step, :, :fc] = w_gate
        wi_bf_ref[step, :, fc:] = w_up
        wo_bf_ref[step] = w_out
        x = xp_ref[...]
        part = _ffn_chunk(x.astype(BF16), w_gate, w_up, w_out)

        @pl.when(step == 0)
        def _():
            acc_ref[...] = part

        @pl.when(step > 0)
        def _():
            acc_ref[...] += part

        @pl.when(step == n_chunks - 1)
        def _():
            y_ref[...] = alpha * x + 0.5 * acc_ref[...]

    def full_tile_step(out_ref):
        normed = norm_previous()
        out_ref[...] = normed
        rows, width = normed.shape
        anchor = jnp.max(normed.reshape(rows // SUBLANES, SUBLANES, width), axis=0)
        anchor = functools.reduce(jnp.maximum, [anchor[:, j * LANES:(j + 1) * LANES]
                                                for j in range(width // LANES)])
        zero = jnp.minimum(jnp.abs(anchor), 0.0)
        x = jnp.where(matmul_tile(step) < n_prompt_tiles, xp_ref[...], xs_ref[...])
        xb = x.astype(BF16)
        for c in range(n_chunks):
            part = _ffn_chunk(xb, wi_bf_ref[c, :, :fc], wi_bf_ref[c, :, fc:], wo_bf_ref[c])
            if c == 0:
                acc_ref[...] = part
            else:
                acc_ref[...] += part
            if c == min(LN_ANCHOR_CHUNK, n_chunks - 1):
                acc_ref[0:SUBLANES, 0:LANES] += zero
        y_ref[...] = alpha * x + 0.5 * acc_ref[...]

    in_full_steps = (step >= n_chunks) & (step < last_step)

    if n_cast:
        @pl.when((step >= n_chunks) & (step < n_chunks + CAST_STEPS))
        def _():
            for src, dst in zip(cast_in, cast_out):
                dst[...] = src[...].astype(dst.dtype)

    pl.when(in_full_steps & norm_is_prompt)(functools.partial(full_tile_step, op_ref))
    pl.when(in_full_steps & jnp.logical_not(norm_is_prompt))(functools.partial(full_tile_step, os_ref))

    @pl.when(step == last_step)
    def _():
        _store_segment(norm_is_prompt, op_ref, os_ref, norm_previous())


def _ffn_ln(xp, xs, wi, wo, half, ln_g, ln_b, ln_row, alpha, cast_weights=()):
    d = xp.shape[1]
    d_ff = wo.shape[1]
    tm, fc = TOKEN_TILE, FF_CHUNK
    assert xp.shape[0] % tm == 0 and xs.shape[0] % tm == 0 and d_ff % fc == 0
    n_p, n_s = xp.shape[0] // tm, xs.shape[0] // tm
    n_chunks = d_ff // fc
    n_tiles = n_p + n_s
    assert n_tiles - 1 >= CAST_STEPS
    matmul_tile, norm_tile = _ffn_tile_maps(n_chunks, n_tiles)
    chunk_of = lambda i: jnp.minimum(i, n_chunks - 1)
    cast_specs = []
    for w in cast_weights:
        rows = w.shape[0] // CAST_STEPS
        assert w.shape[0] % CAST_STEPS == 0 and rows % 16 == 0
        cast_specs.append(pl.BlockSpec((rows, w.shape[1]),
                                       lambda i: (jnp.clip(i - n_chunks, 0, CAST_STEPS - 1), 0)))
    return pl.pallas_call(
        functools.partial(_ffn_ln_kernel, alpha, n_chunks, n_tiles, ln_row, n_p, len(cast_weights)),
        grid=(n_chunks + n_tiles,),
        in_specs=[*_segment_specs(n_p, tm, d, matmul_tile),
                  pl.BlockSpec((None, d, fc), lambda i: (half, 0, chunk_of(i))),
                  pl.BlockSpec((None, d, fc), lambda i: (half, 0, n_chunks + chunk_of(i))),
                  pl.BlockSpec((None, fc, d), lambda i: (half, chunk_of(i), 0)),
                  _const_spec(ln_g.shape), _const_spec(ln_b.shape), *cast_specs],
        out_specs=[*_segment_specs(n_p, tm, d, norm_tile), *cast_specs],
        out_shape=[jax.ShapeDtypeStruct(xp.shape, F32), jax.ShapeDtypeStruct(xs.shape, F32)]
                  + [jax.ShapeDtypeStruct(w.shape, BF16) for w in cast_weights],
        scratch_shapes=[pltpu.VMEM((tm, d), F32), pltpu.VMEM((tm, d), F32),
                        pltpu.VMEM((n_chunks, d, 2 * fc), BF16),
                        pltpu.VMEM((n_chunks, fc, d), BF16)],
        compiler_params=_cparams(1),
        name="ffn_ln",
    )(xp, xs, wi, wi, wo, ln_g, ln_b, *cast_weights)


def _rope_tables(pos, dim, theta_fn):
    half = dim // 2
    inv_freq = theta_fn(half)
    ang = pos.astype(F32)[:, None] * inv_freq[None, :]
    reps = LANES // half
    cos = jnp.tile(jnp.cos(ang), (1, reps))
    sign = jnp.tile(jnp.concatenate([-jnp.ones((half,), F32), jnp.ones((half,), F32)]), LANES // dim)
    sin = jnp.tile(jnp.sin(ang), (1, reps)) * sign[None, :]
    return cos, sin


def _attn_inv_freq(half):
    return 1.0 / (ROPE_THETA ** (jnp.arange(0, 2 * half, 2, dtype=F32) / (2 * half)))


def _ret_inv_freq(half):
    return 1.0 / (RET_THETA ** jnp.linspace(0.0, 1.0, half, dtype=F32))


def _rope_block(x, cos, sin, dim):
    half = dim // 2
    if dim == LANES:
        rot = pltpu.roll(x, half, 1)
    else:
        lane = lax.broadcasted_iota(jnp.int32, x.shape, 1)
        first_half = (lane % dim) < half
        rot = jnp.where(first_half, pltpu.roll(x, LANES - half, 1), pltpu.roll(x, half, 1))
    return x * cos + rot * sin


def _proj_rope_kernel(x_ref, w_ref, cosa_ref, sina_ref, cosr_ref, sinr_ref,
                      qa_ref, ka_ref, va_ref, qr_ref, kr_ref, vr_ref, gr_ref, ga_ref, gb_ref):
    xb = x_ref[...].astype(BF16)
    cosa, sina = cosa_ref[...], sina_ref[...]
    cosr, sinr = cosr_ref[...], sinr_ref[...]
    off = 0

    y = _dot(xb, w_ref[:, off:off + Q_W])
    for j in range(Q_W // LANES):
        qa_ref[:, j * LANES:(j + 1) * LANES] = _rope_block(
            y[:, j * LANES:(j + 1) * LANES], cosa, sina, HEAD_DIM).astype(qa_ref.dtype)
    off += Q_W
    y = _dot(xb, w_ref[:, off:off + 2 * KV_W])
    ka_ref[...] = _rope_block(y[:, :KV_W], cosa, sina, HEAD_DIM)
    va_ref[...] = y[:, KV_W:]
    off += 2 * KV_W
    y = _dot(xb, w_ref[:, off:off + RQ_W])
    for j in range(RQ_W // LANES):
        qr_ref[:, j * LANES:(j + 1) * LANES] = _rope_block(
            y[:, j * LANES:(j + 1) * LANES], cosr, sinr, RET_KEY_DIM).astype(qr_ref.dtype)
    off += RQ_W
    y = _dot(xb, w_ref[:, off:off + RQ_W])
    for j in range(RQ_W // LANES):
        kr = _rope_block(y[:, j * LANES:(j + 1) * LANES], cosr, sinr, RET_KEY_DIM)
        kr_ref[:, j * LANES:(j + 1) * LANES] = (kr * (RET_KEY_DIM ** -0.5)).astype(kr_ref.dtype)
    off += RQ_W
    for ref in (vr_ref, gr_ref, ga_ref, gb_ref):
        width = ref.shape[1]
        for c in range(width // 512):
            ref[:, c * 512:(c + 1) * 512] = _dot(
                xb, w_ref[:, off + c * 512:off + (c + 1) * 512]).astype(ref.dtype)
        off += width


def _dense_tile(tokens):
    return min(DENSE_TILE, tokens // DENSE_MIN_STEPS)


def _proj_rope(x1, w_in, tables, mixer_dtype):
    t, d = x1.shape
    n_in = w_in.shape[1]
    d_model = (n_in - Q_W - 2 * KV_W - 2 * RQ_W - 2 * RV_W) // 2
    tm = _dense_tile(t)
    row = lambda i: (i, 0)
    table_blocks = tables[0].shape[0] // tm
    tab = lambda i: (i % table_blocks, 0)
    widths = [(Q_W, mixer_dtype), (KV_W, F32), (KV_W, F32), (RQ_W, mixer_dtype), (RQ_W, mixer_dtype),
              (RV_W, mixer_dtype), (RV_W, mixer_dtype), (d_model, BF16), (d_model, BF16)]
    return pl.pallas_call(
        _proj_rope_kernel,
        grid=(t // tm,),
        in_specs=[pl.BlockSpec((tm, d), row), _const_spec(w_in.shape)]
                 + [pl.BlockSpec((tm, LANES), tab)] * 4,
        out_specs=[pl.BlockSpec((tm, w), row) for w, _ in widths],
        out_shape=[jax.ShapeDtypeStruct((t, w), dt) for w, dt in widths],
        compiler_params=_cparams(1),
        name="proj_rope",
    )(x1, w_in, *tables)


def _kv_lane_variants(x):
    lane = lax.broadcasted_iota(jnp.int32, x.shape, 1)
    lo = lane < HEAD_DIM
    swapped = pltpu.roll(x, HEAD_DIM, 1)
    zero = jnp.zeros_like(x)
    kv0 = (jnp.where(lo, x, zero).astype(BF16), jnp.where(lo, zero, swapped).astype(BF16))
    kv1 = (jnp.where(lo, swapped, zero).astype(BF16), jnp.where(lo, zero, x).astype(BF16))
    return kv0, kv1


def _attn_prompt_kernel(sinks_ref, q_ref, kc_ref, kp_ref, vc_ref, vp_ref, o_ref):
    n = pl.program_id(1)
    L = WINDOW
    k_full = jnp.concatenate([kp_ref[...], kc_ref[...]], axis=0)
    v_full = jnp.concatenate([vp_ref[...], vc_ref[...]], axis=0)
    k_var = _kv_lane_variants(k_full)
    v_var = _kv_lane_variants(v_full)
    own = (lax.broadcasted_iota(jnp.int32, (L, L), 1)
           <= lax.broadcasted_iota(jnp.int32, (L, L), 0))
    scale = HEAD_DIM ** -0.5

    units = [(i, j, par) for i in range(q_ref.shape[0] // L) for j in range(Q_W // LANES) for par in range(2)]

    def scores(unit):
        i, j, par = unit
        q2 = q_ref[i * L:(i + 1) * L, j * LANES:(j + 1) * LANES] * scale
        return _dot_nt(q2, k_var[(2 * j) // GQA_GROUP][par][i * L:(i + 2) * L])

    s_next = scores(units[0])
    out = None
    for idx, (i, j, par) in enumerate(units):
        s = s_next
        if idx + 1 < len(units):
            s_next = scores(units[idx + 1])
        s_prev = s[:, :L]
        if i == 0:
            s_prev = jnp.where(n > 0, s_prev, NEG_INF)
        sw = jnp.where(own, s[:, L:], s_prev)
        sink = sinks_ref[2 * j + par]
        m = jnp.maximum(jnp.max(sw, axis=-1, keepdims=True), sink)
        e = jnp.exp(sw - m)
        denom = jnp.sum(e, axis=-1, keepdims=True) + jnp.exp(sink - m)
        e_band = jnp.concatenate([jnp.where(own, 0.0, e), jnp.where(own, e, 0.0)], axis=1)
        r = _dot(e_band.astype(BF16), v_var[(2 * j) // GQA_GROUP][par][i * L:(i + 2) * L]) / denom
        if par == 0:
            out = r
        else:
            o_ref[i * L:(i + 1) * L, j * LANES:(j + 1) * LANES] = (out + r).astype(o_ref.dtype)


def _attn_prompt(qa, ka, va, sinks, batch, seq):
    tq = ATTN_TILE
    nq = seq // tq
    per = tq // WINDOW
    cur = lambda b, n: (b * nq + n, 0)
    prev = lambda b, n: ((b * nq + n) * per - jnp.minimum(n, 1), 0)
    return pl.pallas_call(
        _attn_prompt_kernel,
        grid=(batch, nq),
        in_specs=[pl.BlockSpec(memory_space=pltpu.SMEM),
                  pl.BlockSpec((tq, Q_W), cur),
                  pl.BlockSpec((tq, KV_W), cur), pl.BlockSpec((WINDOW, KV_W), prev),
                  pl.BlockSpec((tq, KV_W), cur), pl.BlockSpec((WINDOW, KV_W), prev)],
        out_specs=pl.BlockSpec((tq, Q_W), cur),
        out_shape=jax.ShapeDtypeStruct(qa.shape, BF16),
        compiler_params=_cparams(2),
        name="attn_prompt",
    )(sinks, qa, ka, ka, va, va)


def _attn_sample_kernel(sink_rows_ref, q_ref, kn_ref, vn_ref, ck_ref, cv_ref,
                        o_ref, nk_ref, nv_ref):
    bt, t_new, _ = q_ref.shape
    w = ck_ref.shape[1]
    n_rows = N_Q_HEADS * t_new
    lane_lo = lax.broadcasted_iota(jnp.int32, (bt * t_new, LANES), 1) < HEAD_DIM
    r_i = lax.broadcasted_iota(jnp.int32, (n_rows, w + t_new), 0) % t_new
    m_i = lax.broadcasted_iota(jnp.int32, (n_rows, w + t_new), 1)
    mask = ((m_i < w) & (m_i > r_i + (w - WINDOW))) | ((m_i >= w) & (m_i - w <= r_i))
    sink = sink_rows_ref[...]
    scale = HEAD_DIM ** -0.5
    kc, kn = ck_ref[...], kn_ref[...]
    vc, vn = cv_ref[...], vn_ref[...]
    k_all = jnp.concatenate([kc, kn], axis=1).astype(BF16)
    v_all = jnp.concatenate([vc, vn], axis=1).astype(BF16)
    q = q_ref[...].reshape(bt * t_new, Q_W)
    pieces = []
    for h in range(N_Q_HEADS):
        q2 = q[:, (h // 2) * LANES:(h // 2 + 1) * LANES]
        src_lo = h % 2 == 0
        dst_lo = h // GQA_GROUP == 0
        if src_lo != dst_lo:
            q2 = pltpu.roll(q2, HEAD_DIM, 1)
        pieces.append(jnp.where(lane_lo == dst_lo, q2, 0.0).reshape(bt, t_new, LANES))
    q_rows = (jnp.concatenate(pieces, axis=1) * scale).astype(BF16)
    s = jnp.einsum("bqd,bkd->bqk", q_rows, k_all, preferred_element_type=F32)
    s = jnp.where(mask, s, NEG_INF)
    m = jnp.maximum(jnp.max(s, axis=-1, keepdims=True), sink)
    e = jnp.exp(s - m)
    denom = jnp.sum(e, axis=-1, keepdims=True) + jnp.exp(sink - m)
    pv = jnp.einsum("bqk,bkd->bqd", e.astype(BF16), v_all, preferred_element_type=F32) / denom
    for j in range(Q_W // LANES):
        out = None
        for par in range(2):
            h = 2 * j + par
            dst_lo = h // GQA_GROUP == 0
            piece = pv[:, h * t_new:(h + 1) * t_new, :].reshape(bt * t_new, LANES)
            piece = jnp.where(lane_lo == dst_lo, piece, 0.0)
            if dst_lo != (par == 0):
                piece = pltpu.roll(piece, HEAD_DIM, 1)
            out = piece if out is None else out + piece
        o_ref[:, :, j * LANES:(j + 1) * LANES] = out.reshape(bt, t_new, LANES).astype(o_ref.dtype)
    nk_ref[:, :w - t_new, :] = kc[:, t_new:, :]
    nk_ref[:, w - t_new:, :] = kn
    nv_ref[:, :w - t_new, :] = vc[:, t_new:, :]
    nv_ref[:, w - t_new:, :] = vn


def _ret_log_gamma(h):
    return math.log(1.0 - 2.0 ** (-5.0 - h))


def _ret_decays(chunk, h):
    lg = _ret_log_gamma(h)
    i = lax.broadcasted_iota(jnp.int32, (chunk, chunk), 0)
    j = lax.broadcasted_iota(jnp.int32, (chunk, chunk), 1)
    rel_mask = jnp.where(i >= j, jnp.exp((j + 1).astype(F32) * -lg), 0.0)
    col = lax.broadcasted_iota(jnp.int32, (chunk, 1), 0).astype(F32)
    q_decay = jnp.exp((col + 1.0) * lg)
    k_decay = jnp.exp((chunk - 1.0 - col) * lg)
    return rel_mask, q_decay, k_decay, math.exp(chunk * lg)


def _ret_chunk_all_heads(q, k, v, g, states, decays):
    heads = range(N_RET_HEADS)
    batched = q.ndim == 3
    mm = (lambda a, b: jnp.einsum("bmk,bkn->bmn", a, b, preferred_element_type=F32)) if batched else _dot
    mm_nt = (lambda a, b: jnp.einsum("bmk,bnk->bmn", a, b, preferred_element_type=F32)) if batched else _dot_nt
    qh = [q[..., h * RET_KEY_DIM:(h + 1) * RET_KEY_DIM] for h in heads]
    kh = [k[..., h * RET_KEY_DIM:(h + 1) * RET_KEY_DIM] for h in heads]
    vh = [v[..., h * RET_VAL_DIM:(h + 1) * RET_VAL_DIM] for h in heads]
    qb = [x.astype(BF16) for x in qh]
    vb = [x.astype(BF16) for x in vh]
    inner = [mm_nt(qb[h], kh[h].astype(BF16)) * decays[h][0] for h in heads]
    if batched:
        lhs = [jnp.concatenate([qh[h], inner[h]], axis=-1).astype(BF16) for h in heads]
        rhs = [jnp.concatenate([states[h], vh[h]], axis=-2).astype(BF16) for h in heads]
    else:
        lhs = [jnp.concatenate([qb[h], inner[h].astype(BF16)], axis=-1) for h in heads]
        rhs = [jnp.concatenate([states[h].astype(BF16), vb[h]], axis=-2) for h in heads]
    o = [mm(lhs[h], rhs[h]) * decays[h][1] for h in heads]
    kd_t = [jnp.swapaxes(kh[h].astype(F32) * decays[h][2], -1, -2).astype(BF16) for h in heads]
    new_states = [states[h] * decays[h][3] + mm(kd_t[h], vb[h]) for h in heads]
    outs = []
    for h in heads:
        mu = jnp.mean(o[h], axis=-1, keepdims=True)
        oc = o[h] - mu
        var = jnp.mean(oc * oc, axis=-1, keepdims=True)
        gate = g[..., h * RET_VAL_DIM:(h + 1) * RET_VAL_DIM].astype(F32)
        outs.append(oc * lax.rsqrt(var + GN_EPS) * jax.nn.silu(gate))
    return jnp.concatenate(outs, axis=-1), new_states


def _ret_prompt_kernel(q_ref, k_ref, v_ref, g_ref, o_ref, st_ref, state_ref):
    n = pl.program_id(1)
    C = RET_CHUNK

    @pl.when(n == 0)
    def _():
        state_ref[...] = jnp.zeros_like(state_ref)

    decays = [_ret_decays(C, h) for h in range(N_RET_HEADS)]
    states = [state_ref[h] for h in range(N_RET_HEADS)]
    for c in range(q_ref.shape[0] // C):
        rows = slice(c * C, (c + 1) * C)
        out, states = _ret_chunk_all_heads(q_ref[rows, :], k_ref[rows, :], v_ref[rows, :], g_ref[rows, :],
                                           states, decays)
        o_ref[rows, :] = out.astype(o_ref.dtype)
    for h in range(N_RET_HEADS):
        state_ref[h] = states[h]

    @pl.when(n == pl.num_programs(1) - 1)
    def _():
        st_ref[0] = state_ref[...]


def _ret_prompt(qr, kr, vr, gr, batch, seq):
    ts = RET_TILE
    ns = seq // ts
    row = lambda b, n: (b * ns + n, 0)
    return pl.pallas_call(
        _ret_prompt_kernel,
        grid=(batch, ns),
        in_specs=[pl.BlockSpec((ts, RQ_W), row), pl.BlockSpec((ts, RQ_W), row),
                  pl.BlockSpec((ts, RV_W), row), pl.BlockSpec((ts, RV_W), row)],
        out_specs=[pl.BlockSpec((ts, RV_W), row),
                   pl.BlockSpec((1, N_RET_HEADS, RET_KEY_DIM, RET_VAL_DIM), lambda b, n: (b, 0, 0, 0))],
        out_shape=[jax.ShapeDtypeStruct((batch * seq, RV_W), BF16),
                   jax.ShapeDtypeStruct((batch, N_RET_HEADS, RET_KEY_DIM, RET_VAL_DIM), F32)],
        scratch_shapes=[pltpu.VMEM((N_RET_HEADS, RET_KEY_DIM, RET_VAL_DIM), F32)],
        compiler_params=_cparams(2),
        name="ret_prompt",
    )(qr, kr, vr, gr)


def _ret_sample_kernel(q_ref, k_ref, v_ref, g_ref, st_ref, o_ref, nst_ref):
    t_new = q_ref.shape[1]
    decays = [_ret_decays(t_new, h) for h in range(N_RET_HEADS)]
    states = [st_ref[:, h] for h in range(N_RET_HEADS)]
    out, states = _ret_chunk_all_heads(q_ref[...], k_ref[...], v_ref[...], g_ref[...], states, decays)
    o_ref[...] = out.astype(o_ref.dtype)
    for h in range(N_RET_HEADS):
        nst_ref[:, h] = states[h]


def _sample_mixers_kernel(sink_rows_ref, qa_ref, kn_ref, vn_ref, ck_ref, cv_ref,
                          qr_ref, kr_ref, vr_ref, gr_ref, st_ref,
                          oa_ref, nk_ref, nv_ref, or_ref, nst_ref):
    _attn_sample_kernel(sink_rows_ref, qa_ref, kn_ref, vn_ref, ck_ref, cv_ref, oa_ref, nk_ref, nv_ref)
    _ret_sample_kernel(qr_ref, kr_ref, vr_ref, gr_ref, st_ref, or_ref, nst_ref)


def _sample_mixers(qa, ka, va, sinks, cache_k, cache_v, qr, kr, vr, gr, state):
    db, t_new, _ = qa.shape
    w = cache_k.shape[1]
    bt = SAMPLE_BT
    sink_rows = jnp.repeat(sinks.astype(F32), t_new).reshape(N_Q_HEADS * t_new, 1)
    blk = lambda *s: pl.BlockSpec((bt,) + s, lambda i: (i,) + (0,) * len(s))
    st = blk(N_RET_HEADS, RET_KEY_DIM, RET_VAL_DIM)
    return pl.pallas_call(
        _sample_mixers_kernel,
        grid=(db // bt,),
        in_specs=[_const_spec(sink_rows.shape), blk(t_new, Q_W), blk(t_new, KV_W), blk(t_new, KV_W),
                  blk(w, KV_W), blk(w, KV_W),
                  blk(t_new, RQ_W), blk(t_new, RQ_W), blk(t_new, RV_W), blk(t_new, RV_W), st],
        out_specs=[blk(t_new, Q_W), blk(w, KV_W), blk(w, KV_W), blk(t_new, RV_W), st],
        out_shape=[jax.ShapeDtypeStruct((db, t_new, Q_W), F32),
                   jax.ShapeDtypeStruct((db, w, KV_W), F32),
                   jax.ShapeDtypeStruct((db, w, KV_W), F32),
                   jax.ShapeDtypeStruct((db, t_new, RV_W), F32),
                   jax.ShapeDtypeStruct(state.shape, state.dtype)],
        compiler_params=_cparams(1),
        name="sample_mixers",
    )(sink_rows, qa, ka, va, cache_k, cache_v, qr, kr, vr, gr, state)


def _merge_ln_kernel(alpha, ln_row, x_ref, oa_ref, orr_ref, ga_ref, gb_ref, wba_ref, wbr_ref, wo_ref,
                     g_ref, b_ref, o_ref):
    n_blocks = max(1, x_ref.shape[0] // MERGE_ROWS)
    rb = x_ref.shape[0] // n_blocks
    blocks = [slice(r * rb, (r + 1) * rb) for r in range(n_blocks)]
    ya = [_dot(oa_ref[rows, :].astype(BF16), wba_ref[...]) for rows in blocks]
    yr = [_dot(orr_ref[rows, :].astype(BF16), wbr_ref[...]) for rows in blocks]
    merged = [(jax.nn.sigmoid(ga_ref[rows, :].astype(F32)) * ya[r]
               + jax.nn.sigmoid(gb_ref[rows, :].astype(F32)) * yr[r]).astype(BF16)
              for r, rows in enumerate(blocks)]
    z = [_dot(merged[r], wo_ref[...]) for r in range(n_blocks)]
    for r, rows in enumerate(blocks):
        o_ref[rows, :] = _layer_norm(alpha * x_ref[rows, :] + z[r],
                                     g_ref[ln_row:ln_row + 1, :], b_ref[ln_row:ln_row + 1, :])


def _merge_ln(x1, oa, orr, ga, gb, wba, wbr, wo, ln_g, ln_b, ln_row, alpha):
    t, d = x1.shape
    tm = _dense_tile(t)
    row = lambda i: (i, 0)
    return pl.pallas_call(
        functools.partial(_merge_ln_kernel, alpha, ln_row),
        grid=(t // tm,),
        in_specs=[pl.BlockSpec((tm, d), row), pl.BlockSpec((tm, Q_W), row), pl.BlockSpec((tm, RV_W), row),
                  pl.BlockSpec((tm, d), row), pl.BlockSpec((tm, d), row),
                  _const_spec(wba.shape), _const_spec(wbr.shape), _const_spec(wo.shape),
                  _const_spec(ln_g.shape), _const_spec(ln_b.shape)],
        out_specs=pl.BlockSpec((tm, d), row),
        out_shape=jax.ShapeDtypeStruct((t, d), F32),
        compiler_params=_cparams(1),
        name="merge_ln",
    )(x1, oa, orr, ga, gb, wba, wbr, wo, ln_g, ln_b)


def _rope_tables_for(pos):
    tables = _rope_tables(pos, HEAD_DIM, _attn_inv_freq) + _rope_tables(pos, RET_KEY_DIM, _ret_inv_freq)
    period = pos.shape[0]
    if period < DENSE_TILE:
        tables = tuple(jnp.tile(tb, (DENSE_TILE // period, 1)) for tb in tables)
    return tables


def kernel(x_prompt, x_sample, cache_k, cache_v, state_ret, ln_g, ln_b, ffn_wi, ffn_wo, w_in,
           attn_sinks, w_br_attn, w_br_ret, w_o):
    B, S, D = x_prompt.shape
    DB, T, _ = x_sample.shape
    depth = ln_g.shape[0]
    alpha = (2.0 * depth) ** 0.25
    W = cache_k.shape[2]
    tables_p = _rope_tables_for(jnp.arange(S, dtype=jnp.int32))
    tables_s = _rope_tables_for(PAST_LEN + jnp.arange(T, dtype=jnp.int32))

    y_p = x_prompt.reshape(B * S, D)
    y_s = x_sample.reshape(DB * T, D)
    outs = [[] for _ in range(6)]
    for l in range(depth):
        wi, wo = ffn_wi[l], ffn_wo[l]
        sinks = attn_sinks[l].astype(F32)
        g_l, b_l = ln_g[l], ln_b[l]

        x1_p, x1_s, w_in_l, wba, wbr, w_o_l = _ffn_ln(
            y_p, y_s, wi, wo, 0, g_l, b_l, 0, alpha,
            cast_weights=(w_in[l], w_br_attn[l], w_br_ret[l], w_o[l]))
        qa_p, ka_p, va_p, qr_p, kr_p, vr_p, gr_p, ga_p, gb_p = _proj_rope(
            x1_p, w_in_l, tables_p, BF16)
        qa_s, ka_s, va_s, qr_s, kr_s, vr_s, gr_s, ga_s, gb_s = _proj_rope(
            x1_s, w_in_l, tables_s, F32)

        oa_p = _attn_prompt(qa_p, ka_p, va_p, sinks, B, S)
        or_p, st_p = _ret_prompt(qr_p, kr_p, vr_p, gr_p, B, S)
        w = min(WINDOW, S)
        nk_p = ka_p.reshape(B, S, N_KV_HEADS, HEAD_DIM)[:, S - w:]
        nv_p = va_p.reshape(B, S, N_KV_HEADS, HEAD_DIM)[:, S - w:]

        r3 = lambda a: a.reshape(DB, T, a.shape[-1])
        oa_s, nk_s, nv_s, or_s, st_s = _sample_mixers(
            r3(qa_s), r3(ka_s), r3(va_s), sinks,
            cache_k[l].reshape(DB, W, KV_W), cache_v[l].reshape(DB, W, KV_W),
            r3(qr_s), r3(kr_s), r3(vr_s), r3(gr_s), state_ret[l])
        oa_s, or_s = oa_s.reshape(DB * T, Q_W), or_s.reshape(DB * T, RV_W)
        shape5 = (DB, W, N_KV_HEADS, HEAD_DIM)

        x2_p = _merge_ln(x1_p, oa_p, or_p, ga_p, gb_p, wba, wbr, w_o_l, g_l, b_l, 1, alpha)
        x2_s = _merge_ln(x1_s, oa_s, or_s, ga_s, gb_s, wba, wbr, w_o_l, g_l, b_l, 1, alpha)
        y_p, y_s = _ffn_ln(x2_p, x2_s, wi, wo, 1, g_l, b_l, 2, alpha)
        for lst, val in zip(outs, (nk_p, nv_p, st_p, nk_s.reshape(shape5), nv_s.reshape(shape5), st_s)):
            lst.append(val)
    return (y_p.reshape(B, S, D), y_s.reshape(DB, T, D)) + tuple(jnp.stack(o) for o in outs)
```

```python
import functools
import math

import jax
import jax.numpy as jnp
from jax import lax
from jax.experimental import pallas as pl
from jax.experimental.pallas import tpu as pltpu

F32 = jnp.float32
BF16 = jnp.bfloat16

PAST_LEN = 8192
N_Q_HEADS = 8
N_KV_HEADS = 2
HEAD_DIM = 64
GQA_GROUP = N_Q_HEADS // N_KV_HEADS
WINDOW = 128
ROPE_THETA = 10000.0
N_RET_HEADS = 4
RET_KEY_DIM = 128
RET_VAL_DIM = 256
RET_CHUNK = 128
RET_THETA = 10000.0
LN_EPS = 1e-5
GN_EPS = 1e-6
NEG_INF = -1e30

Q_W = N_Q_HEADS * HEAD_DIM
KV_W = N_KV_HEADS * HEAD_DIM
RQ_W = N_RET_HEADS * RET_KEY_DIM
RV_W = N_RET_HEADS * RET_VAL_DIM

LANES = 128
SUBLANES = 8
VMEM_LIMIT_BYTES = 56 * 1024 * 1024

TOKEN_TILE = 512
DENSE_TILE = 1024
DENSE_MIN_STEPS = 4
FF_CHUNK = 256
ATTN_TILE = 2048
RET_TILE = 1024
LN_ANCHOR_CHUNK = 8
MERGE_ROWS = 256
SAMPLE_BT = 8
CAST_STEPS = 16


def _dot(a, b):
    return jnp.dot(a, b, preferred_element_type=F32)


def _dot_nt(a, b):
    return lax.dot_general(a, b, (((1,), (1,)), ((), ())), preferred_element_type=F32)


def _layer_norm(y, g, b):
    mu = jnp.mean(y, axis=-1, keepdims=True)
    yc = y - mu
    var = jnp.mean(yc * yc, axis=-1, keepdims=True)
    return yc * lax.rsqrt(var + LN_EPS) * g + b


def _cparams(n_axes, semantics=None):
    return pltpu.CompilerParams(
        dimension_semantics=semantics or ("arbitrary",) * n_axes,
        vmem_limit_bytes=VMEM_LIMIT_BYTES)


def _const_spec(shape):
    nd = len(shape)
    return pl.BlockSpec(shape, lambda *_: (0,) * nd, pipeline_mode=pl.Buffered(1))


def _segment_specs(n_prompt_tiles, tm, width, tile_of):
    prompt = pl.BlockSpec((tm, width), lambda i: (jnp.minimum(tile_of(i), n_prompt_tiles - 1), 0))
    sample = pl.BlockSpec((tm, width), lambda i: (jnp.maximum(tile_of(i) - n_prompt_tiles, 0), 0))
    return prompt, sample


def _store_segment(is_prompt, prompt_ref, sample_ref, value):
    @pl.when(is_prompt)
    def _():
        prompt_ref[...] = value.astype(prompt_ref.dtype)

    @pl.when(jnp.logical_not(is_prompt))
    def _():
        sample_ref[...] = value.astype(sample_ref.dtype)


def _ffn_chunk(xb, w_gate, w_up, w_out):
    act = (jax.nn.silu(_dot(xb, w_gate)) * _dot(xb, w_up)).astype(BF16)
    return _dot(act, w_out)


def _ffn_tile_maps(n_chunks, n_tiles):
    matmul_tile = lambda i: jnp.clip(i - (n_chunks - 1), 0, n_tiles - 1)
    norm_tile = lambda i: jnp.clip(i - n_chunks, 0, n_tiles - 1)
    return matmul_tile, norm_tile


def _ffn_ln_kernel(alpha, n_chunks, n_tiles, ln_row, n_prompt_tiles, n_cast, *refs):
    xp_ref, xs_ref, wg_ref, wu_ref, wo_ref, g_ref, b_ref = refs[:7]
    cast_in = refs[7:7 + n_cast]
    op_ref, os_ref = refs[7 + n_cast:9 + n_cast]
    cast_out = refs[9 + n_cast:9 + 2 * n_cast]
    acc_ref, y_ref, wi_bf_ref, wo_bf_ref = refs[9 + 2 * n_cast:]
    step = pl.program_id(0)
    matmul_tile, norm_tile = _ffn_tile_maps(n_chunks, n_tiles)
    last_step = n_chunks + n_tiles - 1
    fc = FF_CHUNK

    norm_is_prompt = norm_tile(step) < n_prompt_tiles

    def norm_previous():
        return _layer_norm(y_ref[...], g_ref[ln_row:ln_row + 1, :], b_ref[ln_row:ln_row + 1, :])

    @pl.when(step < n_chunks)
    def _():
        w_gate, w_up, w_out = wg_ref[...].astype(BF16), wu_ref[...].astype(BF16), wo_ref[...].astype(BF16)
        wi_bf_ref[step, :, :fc] = w_gate
        wi_bf_ref[step, :, fc:] = w_up
        wo_bf_ref[step] = w_out
        x = xp_ref[...]
        part = _ffn_chunk(x.astype(BF16), w_gate, w_up, w_out)

        @pl.when(step == 0)
        def _():
            acc_ref[...] = part

        @pl.when(step > 0)
        def _():
            acc_ref[...] += part

        @pl.when(step == n_chunks - 1)
        def _():
            y_ref[...] = alpha * x + 0.5 * acc_ref[...]

    def full_tile_step(out_ref):
        normed = norm_previous()
        out_ref[...] = normed
        rows, width = normed.shape
        anchor = jnp.max(normed.reshape(rows // SUBLANES, SUBLANES, width), axis=0)
        anchor = functools.reduce(jnp.maximum, [anchor[:, j * LANES:(j + 1) * LANES]
                                                for j in range(width // LANES)])
        zero = jnp.minimum(jnp.abs(anchor), 0.0)
        x = jnp.where(matmul_tile(step) < n_prompt_tiles, xp_ref[...], xs_ref[...])
        xb = x.astype(BF16)
        for c in range(n_chunks):
            part = _ffn_chunk(xb, wi_bf_ref[c, :, :fc], wi_bf_ref[c, :, fc:], wo_bf_ref[c])
            if c == 0:
                acc_ref[...] = part
            else:
                acc_ref[...] += part
            if c == min(LN_ANCHOR_CHUNK, n_chunks - 1):
                acc_ref[0:SUBLANES, 0:LANES] += zero
        y_ref[...] = alpha * x + 0.5 * acc_ref[...]

    in_full_steps = (step >= n_chunks) & (step < last_step)

    if n_cast:
        @pl.when((step >= n_chunks) & (step < n_chunks + CAST_STEPS))
        def _():
            for src, dst in zip(cast_in, cast_out):
                dst[...] = src[...].astype(dst.dtype)

    pl.when(in_full_steps & norm_is_prompt)(functools.partial(full_tile_step, op_ref))
    pl.when(in_full_steps & jnp.logical_not(norm_is_prompt))(functools.partial(full_tile_step, os_ref))

    @pl.when(step == last_step)
    def _():
        _store_segment(norm_is_prompt, op_ref, os_ref, norm_previous())


def _ffn_ln(xp, xs, wi, wo, half, ln_g, ln_b, ln_row, alpha, cast_weights=()):
    d = xp.shape[1]
    d_ff = wo.shape[1]
    tm, fc = TOKEN_TILE, FF_CHUNK
    assert xp.shape[0] % tm == 0 and xs.shape[0] % tm == 0 and d_ff % fc == 0
    n_p, n_s = xp.shape[0] // tm, xs.shape[0] // tm
    n_chunks = d_ff // fc
    n_tiles = n_p + n_s
    assert n_tiles - 1 >= CAST_STEPS
    matmul_tile, norm_tile = _ffn_tile_maps(n_chunks, n_tiles)
    chunk_of = lambda i: jnp.minimum(i, n_chunks - 1)
    cast_specs = []
    for w in cast_weights:
        rows = w.shape[0] // CAST_STEPS
        assert w.shape[0] % CAST_STEPS == 0 and rows % 16 == 0
        cast_specs.append(pl.BlockSpec((rows, w.shape[1]),
                                       lambda i: (jnp.clip(i - n_chunks, 0, CAST_STEPS - 1), 0)))
    return pl.pallas_call(
        functools.partial(_ffn_ln_kernel, alpha, n_chunks, n_tiles, ln_row, n_p, len(cast_weights)),
        grid=(n_chunks + n_tiles,),
        in_specs=[*_segment_specs(n_p, tm, d, matmul_tile),
                  pl.BlockSpec((None, d, fc), lambda i: (half, 0, chunk_of(i))),
                  pl.BlockSpec((None, d, fc), lambda i: (half, 0, n_chunks + chunk_of(i))),
                  pl.BlockSpec((None, fc, d), lambda i: (half, chunk_of(i), 0)),
                  _const_spec(ln_g.shape), _const_spec(ln_b.shape), *cast_specs],
        out_specs=[*_segment_specs(n_p, tm, d, norm_tile), *cast_specs],
        out_shape=[jax.ShapeDtypeStruct(xp.shape, F32), jax.ShapeDtypeStruct(xs.shape, F32)]
                  + [jax.ShapeDtypeStruct(w.shape, BF16) for w in cast_weights],
        scratch_shapes=[pltpu.VMEM((tm, d), F32), pltpu.VMEM((tm, d), F32),
                        pltpu.VMEM((n_chunks, d, 2 * fc), BF16),
                        pltpu.VMEM((n_chunks, fc, d), BF16)],
        compiler_params=_cparams(1),
        name="ffn_ln",
    )(xp, xs, wi, wi, wo, ln_g, ln_b, *cast_weights)


def _rope_tables(pos, dim, theta_fn):
    half = dim // 2
    inv_freq = theta_fn(half)
    ang = pos.astype(F32)[:, None] * inv_freq[None, :]
    reps = LANES // half
    cos = jnp.tile(jnp.cos(ang), (1, reps))
    sign = jnp.tile(jnp.concatenate([-jnp.ones((half,), F32), jnp.ones((half,), F32)]), LANES // dim)
    sin = jnp.tile(jnp.sin(ang), (1, reps)) * sign[None, :]
    return cos, sin


def _attn_inv_freq(half):
    return 1.0 / (ROPE_THETA ** (jnp.arange(0, 2 * half, 2, dtype=F32) / (2 * half)))


def _ret_inv_freq(half):
    return 1.0 / (RET_THETA ** jnp.linspace(0.0, 1.0, half, dtype=F32))


def _rope_block(x, cos, sin, dim):
    half = dim // 2
    if dim == LANES:
        rot = pltpu.roll(x, half, 1)
    else:
        lane = lax.broadcasted_iota(jnp.int32, x.shape, 1)
        first_half = (lane % dim) < half
        rot = jnp.where(first_half, pltpu.roll(x, LANES - half, 1), pltpu.roll(x, half, 1))
    return x * cos + rot * sin


def _proj_rope_kernel(x_ref, w_ref, cosa_ref, sina_ref, cosr_ref, sinr_ref,
                      qa_ref, ka_ref, va_ref, qr_ref, kr_ref, vr_ref, gr_ref, ga_ref, gb_ref):
    xb = x_ref[...].astype(BF16)
    cosa, sina = cosa_ref[...], sina_ref[...]
    cosr, sinr = cosr_ref[...], sinr_ref[...]
    off = 0

    y = _dot(xb, w_ref[:, off:off + Q_W])
    for j in range(Q_W // LANES):
        qa_ref[:, j * LANES:(j + 1) * LANES] = _rope_block(
            y[:, j * LANES:(j + 1) * LANES], cosa, sina, HEAD_DIM).astype(qa_ref.dtype)
    off += Q_W
    y = _dot(xb, w_ref[:, off:off + 2 * KV_W])
    ka_ref[...] = _rope_block(y[:, :KV_W], cosa, sina, HEAD_DIM)
    va_ref[...] = y[:, KV_W:]
    off += 2 * KV_W
    y = _dot(xb, w_ref[:, off:off + RQ_W])
    for j in range(RQ_W // LANES):
        qr_ref[:, j * LANES:(j + 1) * LANES] = _rope_block(
            y[:, j * LANES:(j + 1) * LANES], cosr, sinr, RET_KEY_DIM).astype(qr_ref.dtype)
    off += RQ_W
    y = _dot(xb, w_ref[:, off:off + RQ_W])
    for j in range(RQ_W // LANES):
        kr = _rope_block(y[:, j * LANES:(j + 1) * LANES], cosr, sinr, RET_KEY_DIM)
        kr_ref[:, j * LANES:(j + 1) * LANES] = (kr * (RET_KEY_DIM ** -0.5)).astype(kr_ref.dtype)
    off += RQ_W
    for ref in (vr_ref, gr_ref, ga_ref, gb_ref):
        width = ref.shape[1]
        for c in range(width // 512):
            ref[:, c * 512:(c + 1) * 512] = _dot(
                xb, w_ref[:, off + c * 512:off + (c + 1) * 512]).astype(ref.dtype)
        off += width


def _dense_tile(tokens):
    return min(DENSE_TILE, tokens // DENSE_MIN_STEPS)


def _proj_rope(x1, w_in, tables, mixer_dtype):
    t, d = x1.shape
    n_in = w_in.shape[1]
    d_model = (n_in - Q_W - 2 * KV_W - 2 * RQ_W - 2 * RV_W) // 2
    tm = _dense_tile(t)
    row = lambda i: (i, 0)
    table_blocks = tables[0].shape[0] // tm
    tab = lambda i: (i % table_blocks, 0)
    widths = [(Q_W, mixer_dtype), (KV_W, F32), (KV_W, F32), (RQ_W, mixer_dtype), (RQ_W, mixer_dtype),
              (RV_W, mixer_dtype), (RV_W, mixer_dtype), (d_model, BF16), (d_model, BF16)]
    return pl.pallas_call(
        _proj_rope_kernel,
        grid=(t // tm,),
        in_specs=[pl.BlockSpec((tm, d), row), _const_spec(w_in.shape)]
                 + [pl.BlockSpec((tm, LANES), tab)] * 4,
        out_specs=[pl.BlockSpec((tm, w), row) for w, _ in widths],
        out_shape=[jax.ShapeDtypeStruct((t, w), dt) for w, dt in widths],
        compiler_params=_cparams(1),
        name="proj_rope",
    )(x1, w_in, *tables)


def _kv_lane_variants(x):
    lane = lax.broadcasted_iota(jnp.int32, x.shape, 1)
    lo = lane < HEAD_DIM
    swapped = pltpu.roll(x, HEAD_DIM, 1)
    zero = jnp.zeros_like(x)
    kv0 = (jnp.where(lo, x, zero).astype(BF16), jnp.where(lo, zero, swapped).astype(BF16))
    kv1 = (jnp.where(lo, swapped, zero).astype(BF16), jnp.where(lo, zero, x).astype(BF16))
    return kv0, kv1


def _attn_prompt_kernel(sinks_ref, q_ref, kc_ref, kp_ref, vc_ref, vp_ref, o_ref):
    n = pl.program_id(1)
    L = WINDOW
    k_full = jnp.concatenate([kp_ref[...], kc_ref[...]], axis=0)
    v_full = jnp.concatenate([vp_ref[...], vc_ref[...]], axis=0)
    k_var = _kv_lane_variants(k_full)
    v_var = _kv_lane_variants(v_full)
    own = (lax.broadcasted_iota(jnp.int32, (L, L), 1)
           <= lax.broadcasted_iota(jnp.int32, (L, L), 0))
    scale = HEAD_DIM ** -0.5

    units = [(i, j, par) for i in range(q_ref.shape[0] // L) for j in range(Q_W // LANES) for par in range(2)]

    def scores(unit):
        i, j, par = unit
        q2 = q_ref[i * L:(i + 1) * L, j * LANES:(j + 1) * LANES] * scale
        return _dot_nt(q2, k_var[(2 * j) // GQA_GROUP][par][i * L:(i + 2) * L])

    s_next = scores(units[0])
    out = None
    for idx, (i, j, par) in enumerate(units):
        s = s_next
        if idx + 1 < len(units):
            s_next = scores(units[idx + 1])
        s_prev = s[:, :L]
        if i == 0:
            s_prev = jnp.where(n > 0, s_prev, NEG_INF)
        sw = jnp.where(own, s[:, L:], s_prev)
        sink = sinks_ref[2 * j + par]
        m = jnp.maximum(jnp.max(sw, axis=-1, keepdims=True), sink)
        e = jnp.exp(sw - m)
        denom = jnp.sum(e, axis=-1, keepdims=True) + jnp.exp(sink - m)
        e_band = jnp.concatenate([jnp.where(own, 0.0, e), jnp.where(own, e, 0.0)], axis=1)
        r = _dot(e_band.astype(BF16), v_var[(2 * j) // GQA_GROUP][par][i * L:(i + 2) * L]) / denom
        if par == 0:
            out = r
        else:
            o_ref[i * L:(i + 1) * L, j * LANES:(j + 1) * LANES] = (out + r).astype(o_ref.dtype)


def _attn_prompt(qa, ka, va, sinks, batch, seq):
    tq = ATTN_TILE
    nq = seq // tq
    per = tq // WINDOW
    cur = lambda b, n: (b * nq + n, 0)
    prev = lambda b, n: ((b * nq + n) * per - jnp.minimum(n, 1), 0)
    return pl.pallas_call(
        _attn_prompt_kernel,
        grid=(batch, nq),
        in_specs=[pl.BlockSpec(memory_space=pltpu.SMEM),
                  pl.BlockSpec((tq, Q_W), cur),
                  pl.BlockSpec((tq, KV_W), cur), pl.BlockSpec((WINDOW, KV_W), prev),
                  pl.BlockSpec((tq, KV_W), cur), pl.BlockSpec((WINDOW, KV_W), prev)],
        out_specs=pl.BlockSpec((tq, Q_W), cur),
        out_shape=jax.ShapeDtypeStruct(qa.shape, BF16),
        compiler_params=_cparams(2),
        name="attn_prompt",
    )(sinks, qa, ka, ka, va, va)


def _attn_sample_kernel(sink_rows_ref, q_ref, kn_ref, vn_ref, ck_ref, cv_ref,
                        o_ref, nk_ref, nv_ref):
    bt, t_new, _ = q_ref.shape
    w = ck_ref.shape[1]
    n_rows = N_Q_HEADS * t_new
    lane_lo = lax.broadcasted_iota(jnp.int32, (bt * t_new, LANES), 1) < HEAD_DIM
    r_i = lax.broadcasted_iota(jnp.int32, (n_rows, w + t_new), 0) % t_new
    m_i = lax.broadcasted_iota(jnp.int32, (n_rows, w + t_new), 1)
    mask = ((m_i < w) & (m_i > r_i + (w - WINDOW))) | ((m_i >= w) & (m_i - w <= r_i))
    sink = sink_rows_ref[...]
    scale = HEAD_DIM ** -0.5
    kc, kn = ck_ref[...], kn_ref[...]
    vc, vn = cv_ref[...], vn_ref[...]
    k_all = jnp.concatenate([kc, kn], axis=1).astype(BF16)
    v_all = jnp.concatenate([vc, vn], axis=1).astype(BF16)
    q = q_ref[...].reshape(bt * t_new, Q_W)
    pieces = []
    for h in range(N_Q_HEADS):
        q2 = q[:, (h // 2) * LANES:(h // 2 + 1) * LANES]
        src_lo = h % 2 == 0
        dst_lo = h // GQA_GROUP == 0
        if src_lo != dst_lo:
            q2 = pltpu.roll(q2, HEAD_DIM, 1)
        pieces.append(jnp.where(lane_lo == dst_lo, q2, 0.0).reshape(bt, t_new, LANES))
    q_rows = (jnp.concatenate(pieces, axis=1) * scale).astype(BF16)
    s = jnp.einsum("bqd,bkd->bqk", q_rows, k_all, preferred_element_type=F32)
    s = jnp.where(mask, s, NEG_INF)
    m = jnp.maximum(jnp.max(s, axis=-1, keepdims=True), sink)
    e = jnp.exp(s - m)
    denom = jnp.sum(e, axis=-1, keepdims=True) + jnp.exp(sink - m)
    pv = jnp.einsum("bqk,bkd->bqd", e.astype(BF16), v_all, preferred_element_type=F32) / denom
    for j in range(Q_W // LANES):
        out = None
        for par in range(2):
            h = 2 * j + par
            dst_lo = h // GQA_GROUP == 0
            piece = pv[:, h * t_new:(h + 1) * t_new, :].reshape(bt * t_new, LANES)
            piece = jnp.where(lane_lo == dst_lo, piece, 0.0)
            if dst_lo != (par == 0):
                piece = pltpu.roll(piece, HEAD_DIM, 1)
            out = piece if out is None else out + piece
        o_ref[:, :, j * LANES:(j + 1) * LANES] = out.reshape(bt, t_new, LANES).astype(o_ref.dtype)
    nk_ref[:, :w - t_new, :] = kc[:, t_new:, :]
    nk_ref[:, w - t_new:, :] = kn
    nv_ref[:, :w - t_new, :] = vc[:, t_new:, :]
    nv_ref[:, w - t_new:, :] = vn


def _ret_log_gamma(h):
    return math.log(1.0 - 2.0 ** (-5.0 - h))


def _ret_decays(chunk, h):
    lg = _ret_log_gamma(h)
    i = lax.broadcasted_iota(jnp.int32, (chunk, chunk), 0)
    j = lax.broadcasted_iota(jnp.int32, (chunk, chunk), 1)
    rel_mask = jnp.where(i >= j, jnp.exp((j + 1).astype(F32) * -lg), 0.0)
    col = lax.broadcasted_iota(jnp.int32, (chunk, 1), 0).astype(F32)
    gn_eps = GN_EPS * jnp.exp((col + 1.0) * (-2.0 * lg))
    k_decay = jnp.exp((chunk - 1.0 - col) * lg)
    return rel_mask, gn_eps, k_decay, math.exp(chunk * lg)


def _ret_chunk_all_heads(q, k, v, g, states, decays):
    heads = range(N_RET_HEADS)
    batched = q.ndim == 3
    mm = (lambda a, b: jnp.einsum("bmk,bkn->bmn", a, b, preferred_element_type=F32)) if batched else _dot
    mm_nt = (lambda a, b: jnp.einsum("bmk,bnk->bmn", a, b, preferred_element_type=F32)) if batched else _dot_nt
    qh = [q[..., h * RET_KEY_DIM:(h + 1) * RET_KEY_DIM] for h in heads]
    kh = [k[..., h * RET_KEY_DIM:(h + 1) * RET_KEY_DIM] for h in heads]
    vh = [v[..., h * RET_VAL_DIM:(h + 1) * RET_VAL_DIM] for h in heads]
    qb = [x.astype(BF16) for x in qh]
    vb = [x.astype(BF16) for x in vh]
    inner = [mm_nt(qb[h], kh[h].astype(BF16)) * decays[h][0] for h in heads]
    if batched:
        lhs = [jnp.concatenate([qh[h], inner[h]], axis=-1).astype(BF16) for h in heads]
        rhs = [jnp.concatenate([states[h], vh[h]], axis=-2).astype(BF16) for h in heads]
    else:
        lhs = [jnp.concatenate([qb[h], inner[h].astype(BF16)], axis=-1) for h in heads]
        rhs = [jnp.concatenate([states[h].astype(BF16), vb[h]], axis=-2) for h in heads]
    p = [mm(lhs[h], rhs[h]) for h in heads]
    kd_t = [jnp.swapaxes(kh[h].astype(F32) * decays[h][2], -1, -2).astype(BF16) for h in heads]
    new_states = [states[h] * decays[h][3] + mm(kd_t[h], vb[h]) for h in heads]
    outs = []
    for h in heads:
        mu = jnp.mean(p[h], axis=-1, keepdims=True)
        pc = p[h] - mu
        var = jnp.mean(pc * pc, axis=-1, keepdims=True)
        gate = g[..., h * RET_VAL_DIM:(h + 1) * RET_VAL_DIM].astype(F32)
        outs.append(pc * lax.rsqrt(var + decays[h][1]) * jax.nn.silu(gate))
    return jnp.concatenate(outs, axis=-1), new_states


def _ret_prompt_kernel(q_ref, k_ref, v_ref, g_ref, o_ref, st_ref, state_ref):
    n = pl.program_id(1)
    C = RET_CHUNK

    @pl.when(n == 0)
    def _():
        state_ref[...] = jnp.zeros_like(state_ref)

    decays = [_ret_decays(C, h) for h in range(N_RET_HEADS)]
    states = [state_ref[h] for h in range(N_RET_HEADS)]
    for c in range(q_ref.shape[0] // C):
        rows = slice(c * C, (c + 1) * C)
        out, states = _ret_chunk_all_heads(q_ref[rows, :], k_ref[rows, :], v_ref[rows, :], g_ref[rows, :],
                                           states, decays)
        o_ref[rows, :] = out.astype(o_ref.dtype)
    for h in range(N_RET_HEADS):
        state_ref[h] = states[h]

    @pl.when(n == pl.num_programs(1) - 1)
    def _():
        st_ref[0] = state_ref[...]


def _ret_prompt(qr, kr, vr, gr, batch, seq):
    ts = RET_TILE
    ns = seq // ts
    row = lambda b, n: (b * ns + n, 0)
    return pl.pallas_call(
        _ret_prompt_kernel,
        grid=(batch, ns),
        in_specs=[pl.BlockSpec((ts, RQ_W), row), pl.BlockSpec((ts, RQ_W), row),
                  pl.BlockSpec((ts, RV_W), row), pl.BlockSpec((ts, RV_W), row)],
        out_specs=[pl.BlockSpec((ts, RV_W), row),
                   pl.BlockSpec((1, N_RET_HEADS, RET_KEY_DIM, RET_VAL_DIM), lambda b, n: (b, 0, 0, 0))],
        out_shape=[jax.ShapeDtypeStruct((batch * seq, RV_W), BF16),
                   jax.ShapeDtypeStruct((batch, N_RET_HEADS, RET_KEY_DIM, RET_VAL_DIM), F32)],
        scratch_shapes=[pltpu.VMEM((N_RET_HEADS, RET_KEY_DIM, RET_VAL_DIM), F32)],
        compiler_params=_cparams(2),
        name="ret_prompt",
    )(qr, kr, vr, gr)


def _ret_sample_kernel(q_ref, k_ref, v_ref, g_ref, st_ref, o_ref, nst_ref):
    t_new = q_ref.shape[1]
    decays = [_ret_decays(t_new, h) for h in range(N_RET_HEADS)]
    states = [st_ref[:, h] for h in range(N_RET_HEADS)]
    out, states = _ret_chunk_all_heads(q_ref[...], k_ref[...], v_ref[...], g_ref[...], states, decays)
    o_ref[...] = out.astype(o_ref.dtype)
    for h in range(N_RET_HEADS):
        nst_ref[:, h] = states[h]


def _sample_mixers_kernel(sink_rows_ref, qa_ref, kn_ref, vn_ref, ck_ref, cv_ref,
                          qr_ref, kr_ref, vr_ref, gr_ref, st_ref,
                          oa_ref, nk_ref, nv_ref, or_ref, nst_ref):
    _attn_sample_kernel(sink_rows_ref, qa_ref, kn_ref, vn_ref, ck_ref, cv_ref, oa_ref, nk_ref, nv_ref)
    _ret_sample_kernel(qr_ref, kr_ref, vr_ref, gr_ref, st_ref, or_ref, nst_ref)


def _sample_mixers(qa, ka, va, sinks, cache_k, cache_v, qr, kr, vr, gr, state):
    db, t_new, _ = qa.shape
    w = cache_k.shape[1]
    bt = SAMPLE_BT
    sink_rows = jnp.repeat(sinks.astype(F32), t_new).reshape(N_Q_HEADS * t_new, 1)
    blk = lambda *s: pl.BlockSpec((bt,) + s, lambda i: (i,) + (0,) * len(s))
    st = blk(N_RET_HEADS, RET_KEY_DIM, RET_VAL_DIM)
    return pl.pallas_call(
        _sample_mixers_kernel,
        grid=(db // bt,),
        in_specs=[_const_spec(sink_rows.shape), blk(t_new, Q_W), blk(t_new, KV_W), blk(t_new, KV_W),
                  blk(w, KV_W), blk(w, KV_W),
                  blk(t_new, RQ_W), blk(t_new, RQ_W), blk(t_new, RV_W), blk(t_new, RV_W), st],
        out_specs=[blk(t_new, Q_W), blk(w, KV_W), blk(w, KV_W), blk(t_new, RV_W), st],
        out_shape=[jax.ShapeDtypeStruct((db, t_new, Q_W), F32),
                   jax.ShapeDtypeStruct((db, w, KV_W), F32),
                   jax.ShapeDtypeStruct((db, w, KV_W), F32),
                   jax.ShapeDtypeStruct((db, t_new, RV_W), F32),
                   jax.ShapeDtypeStruct(state.shape, state.dtype)],
        compiler_params=_cparams(1),
        name="sample_mixers",
    )(sink_rows, qa, ka, va, cache_k, cache_v, qr, kr, vr, gr, state)


def _merge_ln_kernel(alpha, ln_row, x_ref, oa_ref, orr_ref, ga_ref, gb_ref, wba_ref, wbr_ref, wo_ref,
                     g_ref, b_ref, o_ref):
    n_blocks = max(1, x_ref.shape[0] // MERGE_ROWS)
    rb = x_ref.shape[0] // n_blocks
    blocks = [slice(r * rb, (r + 1) * rb) for r in range(n_blocks)]
    ya = [_dot(oa_ref[rows, :].astype(BF16), wba_ref[...]) for rows in blocks]
    yr = [_dot(orr_ref[rows, :].astype(BF16), wbr_ref[...]) for rows in blocks]
    merged = [(jax.nn.sigmoid(ga_ref[rows, :].astype(F32)) * ya[r]
               + jax.nn.sigmoid(gb_ref[rows, :].astype(F32)) * yr[r]).astype(BF16)
              for r, rows in enumerate(blocks)]
    z = [_dot(merged[r], wo_ref[...]) for r in range(n_blocks)]
    for r, rows in enumerate(blocks):
        o_ref[rows, :] = _layer_norm(alpha * x_ref[rows, :] + z[r],
                                     g_ref[ln_row:ln_row + 1, :], b_ref[ln_row:ln_row + 1, :])


def _merge_ln(x1, oa, orr, ga, gb, wba, wbr, wo, ln_g, ln_b, ln_row, alpha):
    t, d = x1.shape
    tm = _dense_tile(t)
    row = lambda i: (i, 0)
    return pl.pallas_call(
        functools.partial(_merge_ln_kernel, alpha, ln_row),
        grid=(t // tm,),
        in_specs=[pl.BlockSpec((tm, d), row), pl.BlockSpec((tm, Q_W), row), pl.BlockSpec((tm, RV_W), row),
                  pl.BlockSpec((tm, d), row), pl.BlockSpec((tm, d), row),
                  _const_spec(wba.shape), _const_spec(wbr.shape), _const_spec(wo.shape),
                  _const_spec(ln_g.shape), _const_spec(ln_b.shape)],
        out_specs=pl.BlockSpec((tm, d), row),
        out_shape=jax.ShapeDtypeStruct((t, d), F32),
        compiler_params=_cparams(1),
        name="merge_ln",
    )(x1, oa, orr, ga, gb, wba, wbr, wo, ln_g, ln_b)


def _rope_tables_for(pos):
    tables = _rope_tables(pos, HEAD_DIM, _attn_inv_freq) + _rope_tables(pos, RET_KEY_DIM, _ret_inv_freq)
    period = pos.shape[0]
    if period < DENSE_TILE:
        tables = tuple(jnp.tile(tb, (DENSE_TILE // period, 1)) for tb in tables)
    return tables


def kernel(x_prompt, x_sample, cache_k, cache_v, state_ret, ln_g, ln_b, ffn_wi, ffn_wo, w_in,
           attn_sinks, w_br_attn, w_br_ret, w_o):
    B, S, D = x_prompt.shape
    DB, T, _ = x_sample.shape
    depth = ln_g.shape[0]
    alpha = (2.0 * depth) ** 0.25
    W = cache_k.shape[2]
    tables_p = _rope_tables_for(jnp.arange(S, dtype=jnp.int32))
    tables_s = _rope_tables_for(PAST_LEN + jnp.arange(T, dtype=jnp.int32))

    y_p = x_prompt.reshape(B * S, D)
    y_s = x_sample.reshape(DB * T, D)
    outs = [[] for _ in range(6)]
    for l in range(depth):
        wi, wo = ffn_wi[l], ffn_wo[l]
        sinks = attn_sinks[l].astype(F32)
        g_l, b_l = ln_g[l], ln_b[l]

        x1_p, x1_s, w_in_l, wba, wbr, w_o_l = _ffn_ln(
            y_p, y_s, wi, wo, 0, g_l, b_l, 0, alpha,
            cast_weights=(w_in[l], w_br_attn[l], w_br_ret[l], w_o[l]))
        qa_p, ka_p, va_p, qr_p, kr_p, vr_p, gr_p, ga_p, gb_p = _proj_rope(
            x1_p, w_in_l, tables_p, BF16)
        qa_s, ka_s, va_s, qr_s, kr_s, vr_s, gr_s, ga_s, gb_s = _proj_rope(
            x1_s, w_in_l, tables_s, F32)

        oa_p = _attn_prompt(qa_p, ka_p, va_p, sinks, B, S)
        or_p, st_p = _ret_prompt(qr_p, kr_p, vr_p, gr_p, B, S)
        w = min(WINDOW, S)
        nk_p = ka_p.reshape(B, S, N_KV_HEADS, HEAD_DIM)[:, S - w:]
        nv_p = va_p.reshape(B, S, N_KV_HEADS, HEAD_DIM)[:, S - w:]

        r3 = lambda a: a.reshape(DB, T, a.shape[-1])
        oa_s, nk_s, nv_s, or_s, st_s = _sample_mixers(
            r3(qa_s), r3(ka_s), r3(va_s), sinks,
            cache_k[l].reshape(DB, W, KV_W), cache_v[l].reshape(DB, W, KV_W),
            r3(qr_s), r3(kr_s), r3(vr_s), r3(gr_s), state_ret[l])
        oa_s, or_s = oa_s.reshape(DB * T, Q_W), or_s.reshape(DB * T, RV_W)
        shape5 = (DB, W, N_KV_HEADS, HEAD_DIM)

        x2_p = _merge_ln(x1_p, oa_p, or_p, ga_p, gb_p, wba, wbr, w_o_l, g_l, b_l, 1, alpha)
        x2_s = _merge_ln(x1_s, oa_s, or_s, ga_s, gb_s, wba, wbr, w_o_l, g_l, b_l, 1, alpha)
        y_p, y_s = _ffn_ln(x2_p, x2_s, wi, wo, 1, g_l, b_l, 2, alpha)
        for lst, val in zip(outs, (nk_p, nv_p, st_p, nk_s.reshape(shape5), nv_s.reshape(shape5), st_s)):
            lst.append(val)
    return (y_p.reshape(B, S, D), y_s.reshape(DB, T, D)) + tuple(jnp.stack(o) for o in outs)
```

```python
import functools
import math

import jax
import jax.numpy as jnp
from jax import lax
from jax.experimental import pallas as pl
from jax.experimental.pallas import tpu as pltpu

F32 = jnp.float32
BF16 = jnp.bfloat16

PAST_LEN = 8192
N_Q_HEADS = 8
N_KV_HEADS = 2
HEAD_DIM = 64
GQA_GROUP = N_Q_HEADS // N_KV_HEADS
WINDOW = 128
ROPE_THETA = 10000.0
N_RET_HEADS = 4
RET_KEY_DIM = 128
RET_VAL_DIM = 256
RET_CHUNK = 128
RET_THETA = 10000.0
LN_EPS = 1e-5
GN_EPS = 1e-6
NEG_INF = -1e30

Q_W = N_Q_HEADS * HEAD_DIM
KV_W = N_KV_HEADS * HEAD_DIM
RQ_W = N_RET_HEADS * RET_KEY_DIM
RV_W = N_RET_HEADS * RET_VAL_DIM

LANES = 128
SUBLANES = 8
VMEM_LIMIT_BYTES = 56 * 1024 * 1024

TOKEN_TILE = 512
DENSE_TILE = 1024
DENSE_MIN_STEPS = 4
FF_CHUNK = 256
ATTN_TILE = 2048
RET_TILE = 1024
LN_ANCHOR_CHUNK = 8
MERGE_ROWS = 256
RIDER_PROJ_TILE = 512
CAST_STEPS = 16


def _dot(a, b):
    return jnp.dot(a, b, preferred_element_type=F32)


def _dot_nt(a, b):
    return lax.dot_general(a, b, (((1,), (1,)), ((), ())), preferred_element_type=F32)


def _layer_norm(y, g, b):
    mu = jnp.mean(y, axis=-1, keepdims=True)
    yc = y - mu
    var = jnp.mean(yc * yc, axis=-1, keepdims=True)
    return yc * lax.rsqrt(var + LN_EPS) * g + b


def _cparams(n_axes, semantics=None):
    return pltpu.CompilerParams(
        dimension_semantics=semantics or ("arbitrary",) * n_axes,
        vmem_limit_bytes=VMEM_LIMIT_BYTES)


def _const_spec(shape):
    nd = len(shape)
    return pl.BlockSpec(shape, lambda *_: (0,) * nd, pipeline_mode=pl.Buffered(1))


def _segment_specs(n_prompt_tiles, tm, width, tile_of):
    prompt = pl.BlockSpec((tm, width), lambda i: (jnp.minimum(tile_of(i), n_prompt_tiles - 1), 0))
    sample = pl.BlockSpec((tm, width), lambda i: (jnp.maximum(tile_of(i) - n_prompt_tiles, 0), 0))
    return prompt, sample


def _store_segment(is_prompt, prompt_ref, sample_ref, value):
    @pl.when(is_prompt)
    def _():
        prompt_ref[...] = value.astype(prompt_ref.dtype)

    @pl.when(jnp.logical_not(is_prompt))
    def _():
        sample_ref[...] = value.astype(sample_ref.dtype)


def _ffn_chunk(xb, w_gate, w_up, w_out):
    act = (jax.nn.silu(_dot(xb, w_gate)) * _dot(xb, w_up)).astype(BF16)
    return _dot(act, w_out)


def _ffn_tile_maps(n_chunks, n_tiles):
    matmul_tile = lambda i: jnp.clip(i - (n_chunks - 1), 0, n_tiles - 1)
    norm_tile = lambda i: jnp.clip(i - n_chunks, 0, n_tiles - 1)
    return matmul_tile, norm_tile


def _ffn_ln_kernel(alpha, n_chunks, n_tiles, ln_row, n_prompt_tiles, n_cast, *refs):
    xp_ref, xs_ref, wg_ref, wu_ref, wo_ref, g_ref, b_ref = refs[:7]
    cast_in = refs[7:7 + n_cast]
    op_ref, os_ref = refs[7 + n_cast:9 + n_cast]
    cast_out = refs[9 + n_cast:9 + 2 * n_cast]
    acc_ref, y_ref, wi_bf_ref, wo_bf_ref = refs[9 + 2 * n_cast:]
    step = pl.program_id(0)
    matmul_tile, norm_tile = _ffn_tile_maps(n_chunks, n_tiles)
    last_step = n_chunks + n_tiles - 1
    fc = FF_CHUNK

    norm_is_prompt = norm_tile(step) < n_prompt_tiles

    def norm_previous():
        return _layer_norm(y_ref[...], g_ref[ln_row:ln_row + 1, :], b_ref[ln_row:ln_row + 1, :])

    @pl.when(step < n_chunks)
    def _():
        w_gate, w_up, w_out = wg_ref[...].astype(BF16), wu_ref[...].astype(BF16), wo_ref[...].astype(BF16)
        wi_bf_ref[step, :, :fc] = w_gate
        wi_bf_ref[step, :, fc:] = w_up
        wo_bf_ref[step] = w_out
        x = xp_ref[...]
        part = _ffn_chunk(x.astype(BF16), w_gate, w_up, w_out)

        @pl.when(step == 0)
        def _():
            acc_ref[...] = part

        @pl.when(step > 0)
        def _():
            acc_ref[...] += part

        @pl.when(step == n_chunks - 1)
        def _():
            y_ref[...] = alpha * x + 0.5 * acc_ref[...]

    def full_tile_step(out_ref):
        normed = norm_previous()
        out_ref[...] = normed
        rows, width = normed.shape
        anchor = jnp.max(normed.reshape(rows // SUBLANES, SUBLANES, width), axis=0)
        anchor = functools.reduce(jnp.maximum, [anchor[:, j * LANES:(j + 1) * LANES]
                                                for j in range(width // LANES)])
        zero = jnp.minimum(jnp.abs(anchor), 0.0)
        x = jnp.where(matmul_tile(step) < n_prompt_tiles, xp_ref[...], xs_ref[...])
        xb = x.astype(BF16)
        for c in range(n_chunks):
            part = _ffn_chunk(xb, wi_bf_ref[c, :, :fc], wi_bf_ref[c, :, fc:], wo_bf_ref[c])
            if c == 0:
                acc_ref[...] = part
            else:
                acc_ref[...] += part
            if c == min(LN_ANCHOR_CHUNK, n_chunks - 1):
                acc_ref[0:SUBLANES, 0:LANES] += zero
        y_ref[...] = alpha * x + 0.5 * acc_ref[...]

    in_full_steps = (step >= n_chunks) & (step < last_step)

    if n_cast:
        @pl.when((step >= n_chunks) & (step < n_chunks + CAST_STEPS))
        def _():
            for src, dst in zip(cast_in, cast_out):
                dst[...] = src[...].astype(dst.dtype)

    pl.when(in_full_steps & norm_is_prompt)(functools.partial(full_tile_step, op_ref))
    pl.when(in_full_steps & jnp.logical_not(norm_is_prompt))(functools.partial(full_tile_step, os_ref))

    @pl.when(step == last_step)
    def _():
        _store_segment(norm_is_prompt, op_ref, os_ref, norm_previous())


def _ffn_ln(xp, xs, wi, wo, half, ln_g, ln_b, ln_row, alpha, cast_weights=()):
    d = xp.shape[1]
    d_ff = wo.shape[1]
    tm, fc = TOKEN_TILE, FF_CHUNK
    assert xp.shape[0] % tm == 0 and xs.shape[0] % tm == 0 and d_ff % fc == 0
    n_p, n_s = xp.shape[0] // tm, xs.shape[0] // tm
    n_chunks = d_ff // fc
    n_tiles = n_p + n_s
    assert n_tiles - 1 >= CAST_STEPS
    matmul_tile, norm_tile = _ffn_tile_maps(n_chunks, n_tiles)
    chunk_of = lambda i: jnp.minimum(i, n_chunks - 1)
    cast_specs = []
    for w in cast_weights:
        rows = w.shape[0] // CAST_STEPS
        assert w.shape[0] % CAST_STEPS == 0 and rows % 16 == 0
        cast_specs.append(pl.BlockSpec((rows, w.shape[1]),
                                       lambda i: (jnp.clip(i - n_chunks, 0, CAST_STEPS - 1), 0)))
    return pl.pallas_call(
        functools.partial(_ffn_ln_kernel, alpha, n_chunks, n_tiles, ln_row, n_p, len(cast_weights)),
        grid=(n_chunks + n_tiles,),
        in_specs=[*_segment_specs(n_p, tm, d, matmul_tile),
                  pl.BlockSpec((None, d, fc), lambda i: (half, 0, chunk_of(i))),
                  pl.BlockSpec((None, d, fc), lambda i: (half, 0, n_chunks + chunk_of(i))),
                  pl.BlockSpec((None, fc, d), lambda i: (half, chunk_of(i), 0)),
                  _const_spec(ln_g.shape), _const_spec(ln_b.shape), *cast_specs],
        out_specs=[*_segment_specs(n_p, tm, d, norm_tile), *cast_specs],
        out_shape=[jax.ShapeDtypeStruct(xp.shape, F32), jax.ShapeDtypeStruct(xs.shape, F32)]
                  + [jax.ShapeDtypeStruct(w.shape, BF16) for w in cast_weights],
        scratch_shapes=[pltpu.VMEM((tm, d), F32), pltpu.VMEM((tm, d), F32),
                        pltpu.VMEM((n_chunks, d, 2 * fc), BF16),
                        pltpu.VMEM((n_chunks, fc, d), BF16)],
        compiler_params=_cparams(1),
        name="ffn_ln",
    )(xp, xs, wi, wi, wo, ln_g, ln_b, *cast_weights)


def _rope_tables(pos, dim, theta_fn):
    half = dim // 2
    inv_freq = theta_fn(half)
    ang = pos.astype(F32)[:, None] * inv_freq[None, :]
    reps = LANES // half
    cos = jnp.tile(jnp.cos(ang), (1, reps))
    sign = jnp.tile(jnp.concatenate([-jnp.ones((half,), F32), jnp.ones((half,), F32)]), LANES // dim)
    sin = jnp.tile(jnp.sin(ang), (1, reps)) * sign[None, :]
    return cos, sin


def _attn_inv_freq(half):
    return 1.0 / (ROPE_THETA ** (jnp.arange(0, 2 * half, 2, dtype=F32) / (2 * half)))


def _ret_inv_freq(half):
    return 1.0 / (RET_THETA ** jnp.linspace(0.0, 1.0, half, dtype=F32))


def _rope_block(x, cos, sin, dim):
    half = dim // 2
    if dim == LANES:
        rot = pltpu.roll(x, half, 1)
    else:
        lane = lax.broadcasted_iota(jnp.int32, x.shape, 1)
        first_half = (lane % dim) < half
        rot = jnp.where(first_half, pltpu.roll(x, LANES - half, 1), pltpu.roll(x, half, 1))
    return x * cos + rot * sin


def _proj_rope_kernel(x_ref, w_ref, cosa_ref, sina_ref, cosr_ref, sinr_ref,
                      qa_ref, ka_ref, va_ref, qr_ref, kr_ref, vr_ref, gr_ref, ga_ref, gb_ref):
    xb = x_ref[...].astype(BF16)
    cosa, sina = cosa_ref[...], sina_ref[...]
    cosr, sinr = cosr_ref[...], sinr_ref[...]
    off = 0

    y = _dot(xb, w_ref[:, off:off + Q_W])
    for j in range(Q_W // LANES):
        qa_ref[:, j * LANES:(j + 1) * LANES] = _rope_block(
            y[:, j * LANES:(j + 1) * LANES], cosa, sina, HEAD_DIM).astype(qa_ref.dtype)
    off += Q_W
    y = _dot(xb, w_ref[:, off:off + 2 * KV_W])
    ka_ref[...] = _rope_block(y[:, :KV_W], cosa, sina, HEAD_DIM)
    va_ref[...] = y[:, KV_W:]
    off += 2 * KV_W
    y = _dot(xb, w_ref[:, off:off + RQ_W])
    for j in range(RQ_W // LANES):
        qr_ref[:, j * LANES:(j + 1) * LANES] = _rope_block(
            y[:, j * LANES:(j + 1) * LANES], cosr, sinr, RET_KEY_DIM).astype(qr_ref.dtype)
    off += RQ_W
    y = _dot(xb, w_ref[:, off:off + RQ_W])
    for j in range(RQ_W // LANES):
        kr = _rope_block(y[:, j * LANES:(j + 1) * LANES], cosr, sinr, RET_KEY_DIM)
        kr_ref[:, j * LANES:(j + 1) * LANES] = (kr * (RET_KEY_DIM ** -0.5)).astype(kr_ref.dtype)
    off += RQ_W
    for ref in (vr_ref, gr_ref, ga_ref, gb_ref):
        width = ref.shape[1]
        for c in range(width // 512):
            ref[:, c * 512:(c + 1) * 512] = _dot(
                xb, w_ref[:, off + c * 512:off + (c + 1) * 512]).astype(ref.dtype)
        off += width


def _dense_tile(tokens):
    return min(DENSE_TILE, tokens // DENSE_MIN_STEPS)


N_PROJ_IN, N_PROJ_OUT = 6, 9


def _proj_with_rider_kernel(rider_kernel, n_rider_in, *refs):
    proj_in, rest = refs[:N_PROJ_IN], refs[N_PROJ_IN:]
    rider_in, rest = rest[:n_rider_in], rest[n_rider_in:]
    proj_out, rider_out = rest[:N_PROJ_OUT], rest[N_PROJ_OUT:]
    rider_kernel(*rider_in, *rider_out)
    _proj_rope_kernel(*proj_in, *proj_out)


def _proj_rope(x1, w_in, tables, mixer_dtype, rider=None):
    t, d = x1.shape
    n_in = w_in.shape[1]
    d_model = (n_in - Q_W - 2 * KV_W - 2 * RQ_W - 2 * RV_W) // 2
    tm = _dense_tile(t) if rider is None else rider[0]
    row = lambda i: (i, 0)
    table_blocks = tables[0].shape[0] // tm
    tab = lambda i: (i % table_blocks, 0)
    widths = [(Q_W, mixer_dtype), (KV_W, F32), (KV_W, F32), (RQ_W, mixer_dtype), (RQ_W, mixer_dtype),
              (RV_W, mixer_dtype), (RV_W, mixer_dtype), (d_model, BF16), (d_model, BF16)]
    in_specs = [pl.BlockSpec((tm, d), row), _const_spec(w_in.shape)] + [pl.BlockSpec((tm, LANES), tab)] * 4
    out_specs = [pl.BlockSpec((tm, w), row) for w, _ in widths]
    out_shape = [jax.ShapeDtypeStruct((t, w), dt) for w, dt in widths]
    args = [x1, w_in, *tables]
    body = _proj_rope_kernel
    if rider is not None:
        r_kernel, r_in_specs, r_out_specs, r_out_shape, r_args = rider[1](t // tm)
        body = functools.partial(_proj_with_rider_kernel, r_kernel, len(r_args))
        in_specs, out_specs = in_specs + r_in_specs, out_specs + r_out_specs
        out_shape, args = out_shape + r_out_shape, args + r_args
    outs = pl.pallas_call(
        body,
        grid=(t // tm,),
        in_specs=in_specs,
        out_specs=out_specs,
        out_shape=out_shape,
        compiler_params=_cparams(1),
        name="proj_rope",
    )(*args)
    return outs if rider is None else (outs[:N_PROJ_OUT], outs[N_PROJ_OUT:])


def _kv_lane_variants(x):
    lane = lax.broadcasted_iota(jnp.int32, x.shape, 1)
    lo = lane < HEAD_DIM
    swapped = pltpu.roll(x, HEAD_DIM, 1)
    zero = jnp.zeros_like(x)
    kv0 = (jnp.where(lo, x, zero).astype(BF16), jnp.where(lo, zero, swapped).astype(BF16))
    kv1 = (jnp.where(lo, swapped, zero).astype(BF16), jnp.where(lo, zero, x).astype(BF16))
    return kv0, kv1


def _attn_prompt_kernel(sinks_ref, q_ref, kc_ref, kp_ref, vc_ref, vp_ref, o_ref):
    n = pl.program_id(1)
    L = WINDOW
    k_full = jnp.concatenate([kp_ref[...], kc_ref[...]], axis=0)
    v_full = jnp.concatenate([vp_ref[...], vc_ref[...]], axis=0)
    k_var = _kv_lane_variants(k_full)
    v_var = _kv_lane_variants(v_full)
    own = (lax.broadcasted_iota(jnp.int32, (L, L), 1)
           <= lax.broadcasted_iota(jnp.int32, (L, L), 0))
    scale = HEAD_DIM ** -0.5

    units = [(i, j, par) for i in range(q_ref.shape[0] // L) for j in range(Q_W // LANES) for par in range(2)]

    def scores(unit):
        i, j, par = unit
        q2 = q_ref[i * L:(i + 1) * L, j * LANES:(j + 1) * LANES] * scale
        return _dot_nt(q2, k_var[(2 * j) // GQA_GROUP][par][i * L:(i + 2) * L])

    s_next = scores(units[0])
    out = None
    for idx, (i, j, par) in enumerate(units):
        s = s_next
        if idx + 1 < len(units):
            s_next = scores(units[idx + 1])
        s_prev = s[:, :L]
        if i == 0:
            s_prev = jnp.where(n > 0, s_prev, NEG_INF)
        sw = jnp.where(own, s[:, L:], s_prev)
        sink = sinks_ref[2 * j + par]
        m = jnp.maximum(jnp.max(sw, axis=-1, keepdims=True), sink)
        e = jnp.exp(sw - m)
        denom = jnp.sum(e, axis=-1, keepdims=True) + jnp.exp(sink - m)
        e_band = jnp.concatenate([jnp.where(own, 0.0, e), jnp.where(own, e, 0.0)], axis=1)
        r = _dot(e_band.astype(BF16), v_var[(2 * j) // GQA_GROUP][par][i * L:(i + 2) * L]) / denom
        if par == 0:
            out = r
        else:
            o_ref[i * L:(i + 1) * L, j * LANES:(j + 1) * LANES] = (out + r).astype(o_ref.dtype)


def _attn_prompt(qa, ka, va, sinks, batch, seq):
    tq = ATTN_TILE
    nq = seq // tq
    per = tq // WINDOW
    cur = lambda b, n: (b * nq + n, 0)
    prev = lambda b, n: ((b * nq + n) * per - jnp.minimum(n, 1), 0)
    return pl.pallas_call(
        _attn_prompt_kernel,
        grid=(batch, nq),
        in_specs=[pl.BlockSpec(memory_space=pltpu.SMEM),
                  pl.BlockSpec((tq, Q_W), cur),
                  pl.BlockSpec((tq, KV_W), cur), pl.BlockSpec((WINDOW, KV_W), prev),
                  pl.BlockSpec((tq, KV_W), cur), pl.BlockSpec((WINDOW, KV_W), prev)],
        out_specs=pl.BlockSpec((tq, Q_W), cur),
        out_shape=jax.ShapeDtypeStruct(qa.shape, BF16),
        compiler_params=_cparams(2),
        name="attn_prompt",
    )(sinks, qa, ka, ka, va, va)


def _attn_sample_kernel(sink_rows_ref, q_ref, kn_ref, vn_ref, ck_ref, cv_ref,
                        o_ref, nk_ref, nv_ref):
    bt, t_new, _ = q_ref.shape
    w = ck_ref.shape[1]
    n_rows = N_Q_HEADS * t_new
    lane_lo = lax.broadcasted_iota(jnp.int32, (bt * t_new, LANES), 1) < HEAD_DIM
    r_i = lax.broadcasted_iota(jnp.int32, (n_rows, w + t_new), 0) % t_new
    m_i = lax.broadcasted_iota(jnp.int32, (n_rows, w + t_new), 1)
    mask = ((m_i < w) & (m_i > r_i + (w - WINDOW))) | ((m_i >= w) & (m_i - w <= r_i))
    sink = sink_rows_ref[...]
    scale = HEAD_DIM ** -0.5
    kc, kn = ck_ref[...], kn_ref[...]
    vc, vn = cv_ref[...], vn_ref[...]
    k_all = jnp.concatenate([kc, kn], axis=1).astype(BF16)
    v_all = jnp.concatenate([vc, vn], axis=1).astype(BF16)
    q = q_ref[...].reshape(bt * t_new, Q_W)
    pieces = []
    for h in range(N_Q_HEADS):
        q2 = q[:, (h // 2) * LANES:(h // 2 + 1) * LANES]
        src_lo = h % 2 == 0
        dst_lo = h // GQA_GROUP == 0
        if src_lo != dst_lo:
            q2 = pltpu.roll(q2, HEAD_DIM, 1)
        pieces.append(jnp.where(lane_lo == dst_lo, q2, 0.0).reshape(bt, t_new, LANES))
    q_rows = (jnp.concatenate(pieces, axis=1) * scale).astype(BF16)
    s = jnp.einsum("bqd,bkd->bqk", q_rows, k_all, preferred_element_type=F32)
    s = jnp.where(mask, s, NEG_INF)
    m = jnp.maximum(jnp.max(s, axis=-1, keepdims=True), sink)
    e = jnp.exp(s - m)
    denom = jnp.sum(e, axis=-1, keepdims=True) + jnp.exp(sink - m)
    pv = jnp.einsum("bqk,bkd->bqd", e.astype(BF16), v_all, preferred_element_type=F32) / denom
    for j in range(Q_W // LANES):
        out = None
        for par in range(2):
            h = 2 * j + par
            dst_lo = h // GQA_GROUP == 0
            piece = pv[:, h * t_new:(h + 1) * t_new, :].reshape(bt * t_new, LANES)
            piece = jnp.where(lane_lo == dst_lo, piece, 0.0)
            if dst_lo != (par == 0):
                piece = pltpu.roll(piece, HEAD_DIM, 1)
            out = piece if out is None else out + piece
        o_ref[:, :, j * LANES:(j + 1) * LANES] = out.reshape(bt, t_new, LANES).astype(o_ref.dtype)
    nk_ref[:, :w - t_new, :] = kc[:, t_new:, :]
    nk_ref[:, w - t_new:, :] = kn
    nv_ref[:, :w - t_new, :] = vc[:, t_new:, :]
    nv_ref[:, w - t_new:, :] = vn


def _ret_log_gamma(h):
    return math.log(1.0 - 2.0 ** (-5.0 - h))


def _ret_decays(chunk, h):
    lg = _ret_log_gamma(h)
    i = lax.broadcasted_iota(jnp.int32, (chunk, chunk), 0)
    j = lax.broadcasted_iota(jnp.int32, (chunk, chunk), 1)
    rel_mask = jnp.where(i >= j, jnp.exp((j + 1).astype(F32) * -lg), 0.0)
    col = lax.broadcasted_iota(jnp.int32, (chunk, 1), 0).astype(F32)
    gn_eps = GN_EPS * jnp.exp((col + 1.0) * (-2.0 * lg))
    k_decay = jnp.exp((chunk - 1.0 - col) * lg)
    return rel_mask, gn_eps, k_decay, math.exp(chunk * lg)


def _ret_chunk_all_heads(q, k, v, g, states, decays):
    heads = range(N_RET_HEADS)
    batched = q.ndim == 3
    mm = (lambda a, b: jnp.einsum("bmk,bkn->bmn", a, b, preferred_element_type=F32)) if batched else _dot
    mm_nt = (lambda a, b: jnp.einsum("bmk,bnk->bmn", a, b, preferred_element_type=F32)) if batched else _dot_nt
    qh = [q[..., h * RET_KEY_DIM:(h + 1) * RET_KEY_DIM] for h in heads]
    kh = [k[..., h * RET_KEY_DIM:(h + 1) * RET_KEY_DIM] for h in heads]
    vh = [v[..., h * RET_VAL_DIM:(h + 1) * RET_VAL_DIM] for h in heads]
    qb = [x.astype(BF16) for x in qh]
    vb = [x.astype(BF16) for x in vh]
    inner = [mm_nt(qb[h], kh[h].astype(BF16)) * decays[h][0] for h in heads]
    if batched:
        lhs = [jnp.concatenate([qh[h], inner[h]], axis=-1).astype(BF16) for h in heads]
        rhs = [jnp.concatenate([states[h], vh[h]], axis=-2).astype(BF16) for h in heads]
    else:
        lhs = [jnp.concatenate([qb[h], inner[h].astype(BF16)], axis=-1) for h in heads]
        rhs = [jnp.concatenate([states[h].astype(BF16), vb[h]], axis=-2) for h in heads]
    p = [mm(lhs[h], rhs[h]) for h in heads]
    kd_t = [jnp.swapaxes(kh[h].astype(F32) * decays[h][2], -1, -2).astype(BF16) for h in heads]
    new_states = [states[h] * decays[h][3] + mm(kd_t[h], vb[h]) for h in heads]
    outs = []
    for h in heads:
        mu = jnp.mean(p[h], axis=-1, keepdims=True)
        pc = p[h] - mu
        var = jnp.mean(pc * pc, axis=-1, keepdims=True)
        gate = g[..., h * RET_VAL_DIM:(h + 1) * RET_VAL_DIM].astype(F32)
        outs.append(pc * lax.rsqrt(var + decays[h][1]) * jax.nn.silu(gate))
    return jnp.concatenate(outs, axis=-1), new_states


def _ret_prompt_kernel(q_ref, k_ref, v_ref, g_ref, o_ref, st_ref, state_ref):
    n = pl.program_id(1)
    C = RET_CHUNK

    @pl.when(n == 0)
    def _():
        state_ref[...] = jnp.zeros_like(state_ref)

    decays = [_ret_decays(C, h) for h in range(N_RET_HEADS)]
    states = [state_ref[h] for h in range(N_RET_HEADS)]
    for c in range(q_ref.shape[0] // C):
        rows = slice(c * C, (c + 1) * C)
        out, states = _ret_chunk_all_heads(q_ref[rows, :], k_ref[rows, :], v_ref[rows, :], g_ref[rows, :],
                                           states, decays)
        o_ref[rows, :] = out.astype(o_ref.dtype)
    for h in range(N_RET_HEADS):
        state_ref[h] = states[h]

    @pl.when(n == pl.num_programs(1) - 1)
    def _():
        st_ref[0] = state_ref[...]


def _ret_prompt(qr, kr, vr, gr, batch, seq):
    ts = RET_TILE
    ns = seq // ts
    row = lambda b, n: (b * ns + n, 0)
    return pl.pallas_call(
        _ret_prompt_kernel,
        grid=(batch, ns),
        in_specs=[pl.BlockSpec((ts, RQ_W), row), pl.BlockSpec((ts, RQ_W), row),
                  pl.BlockSpec((ts, RV_W), row), pl.BlockSpec((ts, RV_W), row)],
        out_specs=[pl.BlockSpec((ts, RV_W), row),
                   pl.BlockSpec((1, N_RET_HEADS, RET_KEY_DIM, RET_VAL_DIM), lambda b, n: (b, 0, 0, 0))],
        out_shape=[jax.ShapeDtypeStruct((batch * seq, RV_W), BF16),
                   jax.ShapeDtypeStruct((batch, N_RET_HEADS, RET_KEY_DIM, RET_VAL_DIM), F32)],
        scratch_shapes=[pltpu.VMEM((N_RET_HEADS, RET_KEY_DIM, RET_VAL_DIM), F32)],
        compiler_params=_cparams(2),
        name="ret_prompt",
    )(qr, kr, vr, gr)


def _ret_sample_kernel(q_ref, k_ref, v_ref, g_ref, st_ref, o_ref, nst_ref):
    t_new = q_ref.shape[1]
    decays = [_ret_decays(t_new, h) for h in range(N_RET_HEADS)]
    states = [st_ref[:, h] for h in range(N_RET_HEADS)]
    out, states = _ret_chunk_all_heads(q_ref[...], k_ref[...], v_ref[...], g_ref[...], states, decays)
    o_ref[...] = out.astype(o_ref.dtype)
    for h in range(N_RET_HEADS):
        nst_ref[:, h] = states[h]


def _sample_mixers_kernel(sink_rows_ref, qa_ref, kn_ref, vn_ref, ck_ref, cv_ref,
                          qr_ref, kr_ref, vr_ref, gr_ref, st_ref,
                          oa_ref, nk_ref, nv_ref, or_ref, nst_ref):
    _attn_sample_kernel(sink_rows_ref, qa_ref, kn_ref, vn_ref, ck_ref, cv_ref, oa_ref, nk_ref, nv_ref)
    _ret_sample_kernel(qr_ref, kr_ref, vr_ref, gr_ref, st_ref, or_ref, nst_ref)


def _sample_mixers_parts(qa, ka, va, sinks, cache_k, cache_v, qr, kr, vr, gr, state):
    db, t_new, _ = qa.shape
    w = cache_k.shape[1]
    sink_rows = jnp.repeat(sinks.astype(F32), t_new).reshape(N_Q_HEADS * t_new, 1)

    def parts(n_steps):
        assert db % n_steps == 0
        bt = db // n_steps
        blk = lambda *s: pl.BlockSpec((bt,) + s, lambda i: (i,) + (0,) * len(s))
        st = blk(N_RET_HEADS, RET_KEY_DIM, RET_VAL_DIM)
        in_specs = [_const_spec(sink_rows.shape), blk(t_new, Q_W), blk(t_new, KV_W), blk(t_new, KV_W),
                    blk(w, KV_W), blk(w, KV_W),
                    blk(t_new, RQ_W), blk(t_new, RQ_W), blk(t_new, RV_W), blk(t_new, RV_W), st]
        out_specs = [blk(t_new, Q_W), blk(w, KV_W), blk(w, KV_W), blk(t_new, RV_W), st]
        out_shape = [jax.ShapeDtypeStruct((db, t_new, Q_W), F32),
                     jax.ShapeDtypeStruct((db, w, KV_W), F32),
                     jax.ShapeDtypeStruct((db, w, KV_W), F32),
                     jax.ShapeDtypeStruct((db, t_new, RV_W), F32),
                     jax.ShapeDtypeStruct(state.shape, state.dtype)]
        args = [sink_rows, qa, ka, va, cache_k, cache_v, qr, kr, vr, gr, state]
        return _sample_mixers_kernel, in_specs, out_specs, out_shape, args

    return parts


def _merge_ln_kernel(alpha, ln_row, x_ref, oa_ref, orr_ref, ga_ref, gb_ref, wba_ref, wbr_ref, wo_ref,
                     g_ref, b_ref, o_ref):
    n_blocks = max(1, x_ref.shape[0] // MERGE_ROWS)
    rb = x_ref.shape[0] // n_blocks
    blocks = [slice(r * rb, (r + 1) * rb) for r in range(n_blocks)]
    ya = [_dot(oa_ref[rows, :].astype(BF16), wba_ref[...]) for rows in blocks]
    yr = [_dot(orr_ref[rows, :].astype(BF16), wbr_ref[...]) for rows in blocks]
    merged = [(jax.nn.sigmoid(ga_ref[rows, :].astype(F32)) * ya[r]
               + jax.nn.sigmoid(gb_ref[rows, :].astype(F32)) * yr[r]).astype(BF16)
              for r, rows in enumerate(blocks)]
    z = [_dot(merged[r], wo_ref[...]) for r in range(n_blocks)]
    for r, rows in enumerate(blocks):
        o_ref[rows, :] = _layer_norm(alpha * x_ref[rows, :] + z[r],
                                     g_ref[ln_row:ln_row + 1, :], b_ref[ln_row:ln_row + 1, :])


def _merge_ln(x1, oa, orr, ga, gb, wba, wbr, wo, ln_g, ln_b, ln_row, alpha):
    t, d = x1.shape
    tm = _dense_tile(t)
    row = lambda i: (i, 0)
    return pl.pallas_call(
        functools.partial(_merge_ln_kernel, alpha, ln_row),
        grid=(t // tm,),
        in_specs=[pl.BlockSpec((tm, d), row), pl.BlockSpec((tm, Q_W), row), pl.BlockSpec((tm, RV_W), row),
                  pl.BlockSpec((tm, d), row), pl.BlockSpec((tm, d), row),
                  _const_spec(wba.shape), _const_spec(wbr.shape), _const_spec(wo.shape),
                  _const_spec(ln_g.shape), _const_spec(ln_b.shape)],
        out_specs=pl.BlockSpec((tm, d), row),
        out_shape=jax.ShapeDtypeStruct((t, d), F32),
        compiler_params=_cparams(1),
        name="merge_ln",
    )(x1, oa, orr, ga, gb, wba, wbr, wo, ln_g, ln_b)


def _rope_tables_for(pos):
    tables = _rope_tables(pos, HEAD_DIM, _attn_inv_freq) + _rope_tables(pos, RET_KEY_DIM, _ret_inv_freq)
    period = pos.shape[0]
    if period < DENSE_TILE:
        tables = tuple(jnp.tile(tb, (DENSE_TILE // period, 1)) for tb in tables)
    return tables


def kernel(x_prompt, x_sample, cache_k, cache_v, state_ret, ln_g, ln_b, ffn_wi, ffn_wo, w_in,
           attn_sinks, w_br_attn, w_br_ret, w_o):
    B, S, D = x_prompt.shape
    DB, T, _ = x_sample.shape
    depth = ln_g.shape[0]
    alpha = (2.0 * depth) ** 0.25
    W = cache_k.shape[2]
    tables_p = _rope_tables_for(jnp.arange(S, dtype=jnp.int32))
    tables_s = _rope_tables_for(PAST_LEN + jnp.arange(T, dtype=jnp.int32))

    y_p = x_prompt.reshape(B * S, D)
    y_s = x_sample.reshape(DB * T, D)
    outs = [[] for _ in range(6)]
    for l in range(depth):
        wi, wo = ffn_wi[l], ffn_wo[l]
        sinks = attn_sinks[l].astype(F32)
        g_l, b_l = ln_g[l], ln_b[l]

        x1_p, x1_s, w_in_l, wba, wbr, w_o_l = _ffn_ln(
            y_p, y_s, wi, wo, 0, g_l, b_l, 0, alpha,
            cast_weights=(w_in[l], w_br_attn[l], w_br_ret[l], w_o[l]))
        qa_s, ka_s, va_s, qr_s, kr_s, vr_s, gr_s, ga_s, gb_s = _proj_rope(
            x1_s, w_in_l, tables_s, F32)

        r3 = lambda a: a.reshape(DB, T, a.shape[-1])
        sample_mixers = _sample_mixers_parts(
            r3(qa_s), r3(ka_s), r3(va_s), sinks,
            cache_k[l].reshape(DB, W, KV_W), cache_v[l].reshape(DB, W, KV_W),
            r3(qr_s), r3(kr_s), r3(vr_s), r3(gr_s), state_ret[l])
        (qa_p, ka_p, va_p, qr_p, kr_p, vr_p, gr_p, ga_p, gb_p), (oa_s, nk_s, nv_s, or_s, st_s) = _proj_rope(
            x1_p, w_in_l, tables_p, BF16, rider=(RIDER_PROJ_TILE, sample_mixers))
        oa_s, or_s = oa_s.reshape(DB * T, Q_W), or_s.reshape(DB * T, RV_W)

        oa_p = _attn_prompt(qa_p, ka_p, va_p, sinks, B, S)
        or_p, st_p = _ret_prompt(qr_p, kr_p, vr_p, gr_p, B, S)
        w = min(WINDOW, S)
        nk_p = ka_p.reshape(B, S, N_KV_HEADS, HEAD_DIM)[:, S - w:]
        nv_p = va_p.reshape(B, S, N_KV_HEADS, HEAD_DIM)[:, S - w:]
        shape5 = (DB, W, N_KV_HEADS, HEAD_DIM)

        x2_p = _merge_ln(x1_p, oa_p, or_p, ga_p, gb_p, wba, wbr, w_o_l, g_l, b_l, 1, alpha)
        x2_s = _merge_ln(x1_s, oa_s, or_s, ga_s, gb_s, wba, wbr, w_o_l, g_l, b_l, 1, alpha)
        y_p, y_s = _ffn_ln(x2_p, x2_s, wi, wo, 1, g_l, b_l, 2, alpha)
        for lst, val in zip(outs, (nk_p, nv_p, st_p, nk_s.reshape(shape5), nv_s.reshape(shape5), st_s)):
            lst.append(val)
    return (y_p.reshape(B, S, D), y_s.reshape(DB, T, D)) + tuple(jnp.stack(o) for o in outs)
```

```python
import functools
import math

import jax
import jax.numpy as jnp
from jax import lax
from jax.experimental import pallas as pl
from jax.experimental.pallas import tpu as pltpu

F32 = jnp.float32
BF16 = jnp.bfloat16

PAST_LEN = 8192
N_Q_HEADS = 8
N_KV_HEADS = 2
HEAD_DIM = 64
GQA_GROUP = N_Q_HEADS // N_KV_HEADS
WINDOW = 128
ROPE_THETA = 10000.0
N_RET_HEADS = 4
RET_KEY_DIM = 128
RET_VAL_DIM = 256
RET_CHUNK = 128
RET_THETA = 10000.0
LN_EPS = 1e-5
GN_EPS = 1e-6
NEG_INF = -1e30

Q_W = N_Q_HEADS * HEAD_DIM
KV_W = N_KV_HEADS * HEAD_DIM
RQ_W = N_RET_HEADS * RET_KEY_DIM
RV_W = N_RET_HEADS * RET_VAL_DIM

LANES = 128
SUBLANES = 8
VMEM_LIMIT_BYTES = 56 * 1024 * 1024

TOKEN_TILE = 512
DENSE_TILE = 1024
DENSE_MIN_STEPS = 4
FF_CHUNK = 256
ATTN_TILE = 2048
RET_TILE = 1024
LN_ANCHOR_CHUNK = 8
MERGE_ROWS = 256
RIDER_PROJ_TILE = 512
RIDER_LEAD_GROUPS = 2
CAST_STEPS = 16


def _dot(a, b):
    return jnp.dot(a, b, preferred_element_type=F32)


def _dot_nt(a, b):
    return lax.dot_general(a, b, (((1,), (1,)), ((), ())), preferred_element_type=F32)


def _layer_norm(y, g, b):
    mu = jnp.mean(y, axis=-1, keepdims=True)
    yc = y - mu
    var = jnp.mean(yc * yc, axis=-1, keepdims=True)
    return yc * lax.rsqrt(var + LN_EPS) * g + b


def _cparams(n_axes, semantics=None):
    return pltpu.CompilerParams(
        dimension_semantics=semantics or ("arbitrary",) * n_axes,
        vmem_limit_bytes=VMEM_LIMIT_BYTES)


def _const_spec(shape):
    nd = len(shape)
    return pl.BlockSpec(shape, lambda *_: (0,) * nd, pipeline_mode=pl.Buffered(1))


def _segment_specs(n_prompt_tiles, tm, width, tile_of):
    prompt = pl.BlockSpec((tm, width), lambda i: (jnp.minimum(tile_of(i), n_prompt_tiles - 1), 0))
    sample = pl.BlockSpec((tm, width), lambda i: (jnp.maximum(tile_of(i) - n_prompt_tiles, 0), 0))
    return prompt, sample


def _store_segment(is_prompt, prompt_ref, sample_ref, value):
    @pl.when(is_prompt)
    def _():
        prompt_ref[...] = value.astype(prompt_ref.dtype)

    @pl.when(jnp.logical_not(is_prompt))
    def _():
        sample_ref[...] = value.astype(sample_ref.dtype)


def _ffn_chunk(xb, w_gate, w_up, w_out):
    act = (jax.nn.silu(_dot(xb, w_gate)) * _dot(xb, w_up)).astype(BF16)
    return _dot(act, w_out)


def _ffn_tile_maps(n_chunks, n_tiles):
    matmul_tile = lambda i: jnp.clip(i - (n_chunks - 1), 0, n_tiles - 1)
    norm_tile = lambda i: jnp.clip(i - n_chunks, 0, n_tiles - 1)
    return matmul_tile, norm_tile


def _ffn_ln_kernel(alpha, n_chunks, n_tiles, ln_row, n_prompt_tiles, n_cast, *refs):
    xp_ref, xs_ref, wg_ref, wu_ref, wo_ref, g_ref, b_ref = refs[:7]
    cast_in = refs[7:7 + n_cast]
    op_ref, os_ref = refs[7 + n_cast:9 + n_cast]
    cast_out = refs[9 + n_cast:9 + 2 * n_cast]
    acc_ref, y_ref, wi_bf_ref, wo_bf_ref = refs[9 + 2 * n_cast:]
    step = pl.program_id(0)
    matmul_tile, norm_tile = _ffn_tile_maps(n_chunks, n_tiles)
    last_step = n_chunks + n_tiles - 1
    fc = FF_CHUNK

    norm_is_prompt = norm_tile(step) < n_prompt_tiles

    def norm_previous():
        return _layer_norm(y_ref[...], g_ref[ln_row:ln_row + 1, :], b_ref[ln_row:ln_row + 1, :])

    @pl.when(step < n_chunks)
    def _():
        w_gate, w_up, w_out = wg_ref[...].astype(BF16), wu_ref[...].astype(BF16), wo_ref[...].astype(BF16)
        wi_bf_ref[step, :, :fc] = w_gate
        wi_bf_ref[step, :, fc:] = w_up
        wo_bf_ref[step] = w_out
        x = xp_ref[...]
        part = _ffn_chunk(x.astype(BF16), w_gate, w_up, w_out)

        @pl.when(step == 0)
        def _():
            acc_ref[...] = part

        @pl.when(step > 0)
        def _():
            acc_ref[...] += part

        @pl.when(step == n_chunks - 1)
        def _():
            y_ref[...] = alpha * x + 0.5 * acc_ref[...]

    def full_tile_step(out_ref):
        normed = norm_previous()
        out_ref[...] = normed
        rows, width = normed.shape
        anchor = jnp.max(normed.reshape(rows // SUBLANES, SUBLANES, width), axis=0)
        anchor = functools.reduce(jnp.maximum, [anchor[:, j * LANES:(j + 1) * LANES]
                                                for j in range(width // LANES)])
        zero = jnp.minimum(jnp.abs(anchor), 0.0)
        x = jnp.where(matmul_tile(step) < n_prompt_tiles, xp_ref[...], xs_ref[...])
        xb = x.astype(BF16)
        for c in range(n_chunks):
            part = _ffn_chunk(xb, wi_bf_ref[c, :, :fc], wi_bf_ref[c, :, fc:], wo_bf_ref[c])
            if c == 0:
                acc_ref[...] = part
            else:
                acc_ref[...] += part
            if c == min(LN_ANCHOR_CHUNK, n_chunks - 1):
                acc_ref[0:SUBLANES, 0:LANES] += zero
        y_ref[...] = alpha * x + 0.5 * acc_ref[...]

    in_full_steps = (step >= n_chunks) & (step < last_step)

    if n_cast:
        @pl.when((step >= n_chunks) & (step < n_chunks + CAST_STEPS))
        def _():
            for src, dst in zip(cast_in, cast_out):
                dst[...] = src[...].astype(dst.dtype)

    pl.when(in_full_steps & norm_is_prompt)(functools.partial(full_tile_step, op_ref))
    pl.when(in_full_steps & jnp.logical_not(norm_is_prompt))(functools.partial(full_tile_step, os_ref))

    @pl.when(step == last_step)
    def _():
        _store_segment(norm_is_prompt, op_ref, os_ref, norm_previous())


def _ffn_ln(xp, xs, wi, wo, half, ln_g, ln_b, ln_row, alpha, cast_weights=()):
    d = xp.shape[1]
    d_ff = wo.shape[1]
    tm, fc = TOKEN_TILE, FF_CHUNK
    assert xp.shape[0] % tm == 0 and xs.shape[0] % tm == 0 and d_ff % fc == 0
    n_p, n_s = xp.shape[0] // tm, xs.shape[0] // tm
    n_chunks = d_ff // fc
    n_tiles = n_p + n_s
    assert n_tiles - 1 >= CAST_STEPS
    matmul_tile, norm_tile = _ffn_tile_maps(n_chunks, n_tiles)
    chunk_of = lambda i: jnp.minimum(i, n_chunks - 1)
    cast_specs = []
    for w in cast_weights:
        rows = w.shape[0] // CAST_STEPS
        assert w.shape[0] % CAST_STEPS == 0 and rows % 16 == 0
        cast_specs.append(pl.BlockSpec((rows, w.shape[1]),
                                       lambda i: (jnp.clip(i - n_chunks, 0, CAST_STEPS - 1), 0)))
    return pl.pallas_call(
        functools.partial(_ffn_ln_kernel, alpha, n_chunks, n_tiles, ln_row, n_p, len(cast_weights)),
        grid=(n_chunks + n_tiles,),
        in_specs=[*_segment_specs(n_p, tm, d, matmul_tile),
                  pl.BlockSpec((None, d, fc), lambda i: (half, 0, chunk_of(i))),
                  pl.BlockSpec((None, d, fc), lambda i: (half, 0, n_chunks + chunk_of(i))),
                  pl.BlockSpec((None, fc, d), lambda i: (half, chunk_of(i), 0)),
                  _const_spec(ln_g.shape), _const_spec(ln_b.shape), *cast_specs],
        out_specs=[*_segment_specs(n_p, tm, d, norm_tile), *cast_specs],
        out_shape=[jax.ShapeDtypeStruct(xp.shape, F32), jax.ShapeDtypeStruct(xs.shape, F32)]
                  + [jax.ShapeDtypeStruct(w.shape, BF16) for w in cast_weights],
        scratch_shapes=[pltpu.VMEM((tm, d), F32), pltpu.VMEM((tm, d), F32),
                        pltpu.VMEM((n_chunks, d, 2 * fc), BF16),
                        pltpu.VMEM((n_chunks, fc, d), BF16)],
        compiler_params=_cparams(1),
        name="ffn_ln",
    )(xp, xs, wi, wi, wo, ln_g, ln_b, *cast_weights)


def _rope_tables(pos, dim, theta_fn):
    half = dim // 2
    inv_freq = theta_fn(half)
    ang = pos.astype(F32)[:, None] * inv_freq[None, :]
    reps = LANES // half
    cos = jnp.tile(jnp.cos(ang), (1, reps))
    sign = jnp.tile(jnp.concatenate([-jnp.ones((half,), F32), jnp.ones((half,), F32)]), LANES // dim)
    sin = jnp.tile(jnp.sin(ang), (1, reps)) * sign[None, :]
    return cos, sin


def _attn_inv_freq(half):
    return 1.0 / (ROPE_THETA ** (jnp.arange(0, 2 * half, 2, dtype=F32) / (2 * half)))


def _ret_inv_freq(half):
    return 1.0 / (RET_THETA ** jnp.linspace(0.0, 1.0, half, dtype=F32))


def _rope_block(x, cos, sin, dim):
    half = dim // 2
    if dim == LANES:
        rot = pltpu.roll(x, half, 1)
    else:
        lane = lax.broadcasted_iota(jnp.int32, x.shape, 1)
        first_half = (lane % dim) < half
        rot = jnp.where(first_half, pltpu.roll(x, LANES - half, 1), pltpu.roll(x, half, 1))
    return x * cos + rot * sin


def _run(stages):
    for _ in stages:
        pass


def _proj_rope_stages(x_ref, w_ref, cosa_ref, sina_ref, cosr_ref, sinr_ref,
                      qa_ref, ka_ref, va_ref, qr_ref, kr_ref, vr_ref, gr_ref, ga_ref, gb_ref):
    xb = x_ref[...].astype(BF16)
    cosa, sina = cosa_ref[...], sina_ref[...]
    cosr, sinr = cosr_ref[...], sinr_ref[...]
    off = 0

    y = _dot(xb, w_ref[:, off:off + Q_W])
    for j in range(Q_W // LANES):
        qa_ref[:, j * LANES:(j + 1) * LANES] = _rope_block(
            y[:, j * LANES:(j + 1) * LANES], cosa, sina, HEAD_DIM).astype(qa_ref.dtype)
    off += Q_W
    yield
    y = _dot(xb, w_ref[:, off:off + 2 * KV_W])
    ka_ref[...] = _rope_block(y[:, :KV_W], cosa, sina, HEAD_DIM)
    va_ref[...] = y[:, KV_W:]
    off += 2 * KV_W
    yield
    y = _dot(xb, w_ref[:, off:off + RQ_W])
    for j in range(RQ_W // LANES):
        qr_ref[:, j * LANES:(j + 1) * LANES] = _rope_block(
            y[:, j * LANES:(j + 1) * LANES], cosr, sinr, RET_KEY_DIM).astype(qr_ref.dtype)
    off += RQ_W
    yield
    y = _dot(xb, w_ref[:, off:off + RQ_W])
    for j in range(RQ_W // LANES):
        kr = _rope_block(y[:, j * LANES:(j + 1) * LANES], cosr, sinr, RET_KEY_DIM)
        kr_ref[:, j * LANES:(j + 1) * LANES] = (kr * (RET_KEY_DIM ** -0.5)).astype(kr_ref.dtype)
    off += RQ_W
    yield
    for ref in (vr_ref, gr_ref, ga_ref, gb_ref):
        width = ref.shape[1]
        for c in range(width // 512):
            ref[:, c * 512:(c + 1) * 512] = _dot(
                xb, w_ref[:, off + c * 512:off + (c + 1) * 512]).astype(ref.dtype)
            yield
        off += width


def _proj_rope_kernel(*refs):
    _run(_proj_rope_stages(*refs))


def _dense_tile(tokens):
    return min(DENSE_TILE, tokens // DENSE_MIN_STEPS)


N_PROJ_IN, N_PROJ_OUT = 6, 9


def _proj_with_rider_kernel(rider_stages, n_rider_in, *refs):
    proj_in, rest = refs[:N_PROJ_IN], refs[N_PROJ_IN:]
    rider_in, rest = rest[:n_rider_in], rest[n_rider_in:]
    proj_out, rider_out = rest[:N_PROJ_OUT], rest[N_PROJ_OUT:]
    rider = rider_stages(*rider_in, *rider_out)
    proj = _proj_rope_stages(*proj_in, *proj_out)
    next(rider)
    for _ in range(RIDER_LEAD_GROUPS):
        next(proj)
    _run(rider)
    _run(proj)


def _proj_rope(x1, w_in, tables, mixer_dtype, rider=None):
    t, d = x1.shape
    n_in = w_in.shape[1]
    d_model = (n_in - Q_W - 2 * KV_W - 2 * RQ_W - 2 * RV_W) // 2
    tm = _dense_tile(t) if rider is None else rider[0]
    row = lambda i: (i, 0)
    table_blocks = tables[0].shape[0] // tm
    tab = lambda i: (i % table_blocks, 0)
    widths = [(Q_W, mixer_dtype), (KV_W, F32), (KV_W, F32), (RQ_W, mixer_dtype), (RQ_W, mixer_dtype),
              (RV_W, mixer_dtype), (RV_W, mixer_dtype), (d_model, BF16), (d_model, BF16)]
    in_specs = [pl.BlockSpec((tm, d), row), _const_spec(w_in.shape)] + [pl.BlockSpec((tm, LANES), tab)] * 4
    out_specs = [pl.BlockSpec((tm, w), row) for w, _ in widths]
    out_shape = [jax.ShapeDtypeStruct((t, w), dt) for w, dt in widths]
    args = [x1, w_in, *tables]
    body = _proj_rope_kernel
    if rider is not None:
        r_stages, r_in_specs, r_out_specs, r_out_shape, r_args = rider[1](t // tm)
        body = functools.partial(_proj_with_rider_kernel, r_stages, len(r_args))
        in_specs, out_specs = in_specs + r_in_specs, out_specs + r_out_specs
        out_shape, args = out_shape + r_out_shape, args + r_args
    outs = pl.pallas_call(
        body,
        grid=(t // tm,),
        in_specs=in_specs,
        out_specs=out_specs,
        out_shape=out_shape,
        compiler_params=_cparams(1),
        name="proj_rope",
    )(*args)
    return outs if rider is None else (outs[:N_PROJ_OUT], outs[N_PROJ_OUT:])


def _kv_lane_variants(x):
    lane = lax.broadcasted_iota(jnp.int32, x.shape, 1)
    lo = lane < HEAD_DIM
    swapped = pltpu.roll(x, HEAD_DIM, 1)
    zero = jnp.zeros_like(x)
    kv0 = (jnp.where(lo, x, zero).astype(BF16), jnp.where(lo, zero, swapped).astype(BF16))
    kv1 = (jnp.where(lo, swapped, zero).astype(BF16), jnp.where(lo, zero, x).astype(BF16))
    return kv0, kv1


def _attn_prompt_kernel(sinks_ref, q_ref, kc_ref, kp_ref, vc_ref, vp_ref, o_ref):
    n = pl.program_id(1)
    L = WINDOW
    k_full = jnp.concatenate([kp_ref[...], kc_ref[...]], axis=0)
    v_full = jnp.concatenate([vp_ref[...], vc_ref[...]], axis=0)
    k_var = _kv_lane_variants(k_full)
    v_var = _kv_lane_variants(v_full)
    own = (lax.broadcasted_iota(jnp.int32, (L, L), 1)
           <= lax.broadcasted_iota(jnp.int32, (L, L), 0))
    scale = HEAD_DIM ** -0.5

    units = [(i, j, par) for i in range(q_ref.shape[0] // L) for j in range(Q_W // LANES) for par in range(2)]

    def scores(unit):
        i, j, par = unit
        q2 = q_ref[i * L:(i + 1) * L, j * LANES:(j + 1) * LANES] * scale
        return _dot_nt(q2, k_var[(2 * j) // GQA_GROUP][par][i * L:(i + 2) * L])

    s_next = scores(units[0])
    out = None
    for idx, (i, j, par) in enumerate(units):
        s = s_next
        if idx + 1 < len(units):
            s_next = scores(units[idx + 1])
        s_prev = s[:, :L]
        if i == 0:
            s_prev = jnp.where(n > 0, s_prev, NEG_INF)
        sw = jnp.where(own, s[:, L:], s_prev)
        sink = sinks_ref[2 * j + par]
        m = jnp.maximum(jnp.max(sw, axis=-1, keepdims=True), sink)
        e = jnp.exp(sw - m)
        denom = jnp.sum(e, axis=-1, keepdims=True) + jnp.exp(sink - m)
        e_band = jnp.concatenate([jnp.where(own, 0.0, e), jnp.where(own, e, 0.0)], axis=1)
        r = _dot(e_band.astype(BF16), v_var[(2 * j) // GQA_GROUP][par][i * L:(i + 2) * L]) / denom
        if par == 0:
            out = r
        else:
            o_ref[i * L:(i + 1) * L, j * LANES:(j + 1) * LANES] = (out + r).astype(o_ref.dtype)


def _attn_prompt(qa, ka, va, sinks, batch, seq):
    tq = ATTN_TILE
    nq = seq // tq
    per = tq // WINDOW
    cur = lambda b, n: (b * nq + n, 0)
    prev = lambda b, n: ((b * nq + n) * per - jnp.minimum(n, 1), 0)
    return pl.pallas_call(
        _attn_prompt_kernel,
        grid=(batch, nq),
        in_specs=[pl.BlockSpec(memory_space=pltpu.SMEM),
                  pl.BlockSpec((tq, Q_W), cur),
                  pl.BlockSpec((tq, KV_W), cur), pl.BlockSpec((WINDOW, KV_W), prev),
                  pl.BlockSpec((tq, KV_W), cur), pl.BlockSpec((WINDOW, KV_W), prev)],
        out_specs=pl.BlockSpec((tq, Q_W), cur),
        out_shape=jax.ShapeDtypeStruct(qa.shape, BF16),
        compiler_params=_cparams(2),
        name="attn_prompt",
    )(sinks, qa, ka, ka, va, va)


def _attn_sample_stages(sink_rows_ref, q_ref, kn_ref, vn_ref, ck_ref, cv_ref,
                        o_ref, nk_ref, nv_ref):
    bt, t_new, _ = q_ref.shape
    assert t_new == SUBLANES
    w = ck_ref.shape[1]
    n_rows = N_Q_HEADS * t_new
    lane_lo = lax.broadcasted_iota(jnp.int32, (bt * t_new, LANES), 1) < HEAD_DIM
    r_i = lax.broadcasted_iota(jnp.int32, (n_rows, w + t_new), 0) % t_new
    m_i = lax.broadcasted_iota(jnp.int32, (n_rows, w + t_new), 1)
    mask = ((m_i < w) & (m_i > r_i + (w - WINDOW))) | ((m_i >= w) & (m_i - w <= r_i))
    sink = sink_rows_ref[...]
    scale = HEAD_DIM ** -0.5
    kc, kn = ck_ref[...], kn_ref[...]
    vc, vn = cv_ref[...], vn_ref[...]
    k_all = jnp.concatenate([kc, kn], axis=1).astype(BF16)
    v_all = jnp.concatenate([vc, vn], axis=1).astype(BF16)
    q = q_ref[...].reshape(bt * t_new, Q_W)
    pieces = []
    for h in range(N_Q_HEADS):
        q2 = q[:, (h // 2) * LANES:(h // 2 + 1) * LANES]
        src_lo = h % 2 == 0
        dst_lo = h // GQA_GROUP == 0
        if src_lo != dst_lo:
            q2 = pltpu.roll(q2, HEAD_DIM, 1)
        pieces.append(jnp.where(lane_lo == dst_lo, q2, 0.0).reshape(bt, t_new, LANES))
    q_rows = (jnp.concatenate(pieces, axis=1) * scale).astype(BF16)
    s = jnp.einsum("bqd,bkd->bqk", q_rows, k_all, preferred_element_type=F32)
    yield
    s = jnp.where(mask, s, NEG_INF)
    m = jnp.maximum(jnp.max(s, axis=-1, keepdims=True), sink)
    e = jnp.exp(s - m)
    denom = jnp.sum(e, axis=-1, keepdims=True) + jnp.exp(sink - m)
    pv = jnp.einsum("bqk,bkd->bqd", e.astype(BF16), v_all, preferred_element_type=F32) / denom
    for j in range(Q_W // LANES):
        out = None
        for par in range(2):
            h = 2 * j + par
            dst_lo = h // GQA_GROUP == 0
            piece = pv[:, h * t_new:(h + 1) * t_new, :].reshape(bt * t_new, LANES)
            piece = jnp.where(lane_lo == dst_lo, piece, 0.0)
            if dst_lo != (par == 0):
                piece = pltpu.roll(piece, HEAD_DIM, 1)
            out = piece if out is None else out + piece
        o_ref[:, :, j * LANES:(j + 1) * LANES] = out.reshape(bt, t_new, LANES).astype(o_ref.dtype)
    nk_ref[:, :w - t_new, :] = kc[:, t_new:, :]
    nk_ref[:, w - t_new:, :] = kn
    nv_ref[:, :w - t_new, :] = vc[:, t_new:, :]
    nv_ref[:, w - t_new:, :] = vn


def _ret_log_gamma(h):
    return math.log(1.0 - 2.0 ** (-5.0 - h))


def _ret_decays(chunk, h):
    lg = _ret_log_gamma(h)
    i = lax.broadcasted_iota(jnp.int32, (chunk, chunk), 0)
    j = lax.broadcasted_iota(jnp.int32, (chunk, chunk), 1)
    rel_mask = jnp.where(i >= j, jnp.exp((j + 1).astype(F32) * -lg), 0.0)
    col = lax.broadcasted_iota(jnp.int32, (chunk, 1), 0).astype(F32)
    gn_eps = GN_EPS * jnp.exp((col + 1.0) * (-2.0 * lg))
    k_decay = jnp.exp((chunk - 1.0 - col) * lg)
    return rel_mask, gn_eps, k_decay, math.exp(chunk * lg)


def _ret_chunk_scores(q, k, v, decays):
    heads = range(N_RET_HEADS)
    batched = q.ndim == 3
    mm = (lambda a, b: jnp.einsum("bmk,bkn->bmn", a, b, preferred_element_type=F32)) if batched else _dot
    mm_nt = (lambda a, b: jnp.einsum("bmk,bnk->bmn", a, b, preferred_element_type=F32)) if batched else _dot_nt
    qh = [q[..., h * RET_KEY_DIM:(h + 1) * RET_KEY_DIM] for h in heads]
    kh = [k[..., h * RET_KEY_DIM:(h + 1) * RET_KEY_DIM] for h in heads]
    vh = [v[..., h * RET_VAL_DIM:(h + 1) * RET_VAL_DIM] for h in heads]
    qb = [x.astype(BF16) for x in qh]
    vb = [x.astype(BF16) for x in vh]
    inner = [mm_nt(qb[h], kh[h].astype(BF16)) * decays[h][0] for h in heads]
    return batched, mm, qh, kh, vh, qb, vb, inner


def _ret_chunk_finish(scores, g, states, decays):
    batched, mm, qh, kh, vh, qb, vb, inner = scores
    heads = range(N_RET_HEADS)
    if batched:
        lhs = [jnp.concatenate([qh[h], inner[h]], axis=-1).astype(BF16) for h in heads]
        rhs = [jnp.concatenate([states[h], vh[h]], axis=-2).astype(BF16) for h in heads]
    else:
        lhs = [jnp.concatenate([qb[h], inner[h].astype(BF16)], axis=-1) for h in heads]
        rhs = [jnp.concatenate([states[h].astype(BF16), vb[h]], axis=-2) for h in heads]
    p = [mm(lhs[h], rhs[h]) for h in heads]
    kd_t = [jnp.swapaxes(kh[h].astype(F32) * decays[h][2], -1, -2).astype(BF16) for h in heads]
    new_states = [states[h] * decays[h][3] + mm(kd_t[h], vb[h]) for h in heads]
    outs = []
    for h in heads:
        mu = jnp.mean(p[h], axis=-1, keepdims=True)
        pc = p[h] - mu
        var = jnp.mean(pc * pc, axis=-1, keepdims=True)
        gate = g[..., h * RET_VAL_DIM:(h + 1) * RET_VAL_DIM].astype(F32)
        outs.append(pc * lax.rsqrt(var + decays[h][1]) * jax.nn.silu(gate))
    return jnp.concatenate(outs, axis=-1), new_states


def _ret_chunk_all_heads(q, k, v, g, states, decays):
    return _ret_chunk_finish(_ret_chunk_scores(q, k, v, decays), g, states, decays)


def _ret_prompt_kernel(q_ref, k_ref, v_ref, g_ref, o_ref, st_ref, state_ref):
    n = pl.program_id(1)
    C = RET_CHUNK

    @pl.when(n == 0)
    def _():
        state_ref[...] = jnp.zeros_like(state_ref)

    decays = [_ret_decays(C, h) for h in range(N_RET_HEADS)]
    states = [state_ref[h] for h in range(N_RET_HEADS)]
    for c in range(q_ref.shape[0] // C):
        rows = slice(c * C, (c + 1) * C)
        out, states = _ret_chunk_all_heads(q_ref[rows, :], k_ref[rows, :], v_ref[rows, :], g_ref[rows, :],
                                           states, decays)
        o_ref[rows, :] = out.astype(o_ref.dtype)
    for h in range(N_RET_HEADS):
        state_ref[h] = states[h]

    @pl.when(n == pl.num_programs(1) - 1)
    def _():
        st_ref[0] = state_ref[...]


def _ret_prompt(qr, kr, vr, gr, batch, seq):
    ts = RET_TILE
    ns = seq // ts
    row = lambda b, n: (b * ns + n, 0)
    return pl.pallas_call(
        _ret_prompt_kernel,
        grid=(batch, ns),
        in_specs=[pl.BlockSpec((ts, RQ_W), row), pl.BlockSpec((ts, RQ_W), row),
                  pl.BlockSpec((ts, RV_W), row), pl.BlockSpec((ts, RV_W), row)],
        out_specs=[pl.BlockSpec((ts, RV_W), row),
                   pl.BlockSpec((1, N_RET_HEADS, RET_KEY_DIM, RET_VAL_DIM), lambda b, n: (b, 0, 0, 0))],
        out_shape=[jax.ShapeDtypeStruct((batch * seq, RV_W), BF16),
                   jax.ShapeDtypeStruct((batch, N_RET_HEADS, RET_KEY_DIM, RET_VAL_DIM), F32)],
        scratch_shapes=[pltpu.VMEM((N_RET_HEADS, RET_KEY_DIM, RET_VAL_DIM), F32)],
        compiler_params=_cparams(2),
        name="ret_prompt",
    )(qr, kr, vr, gr)


def _ret_sample_stages(q_ref, k_ref, v_ref, g_ref, st_ref, o_ref, nst_ref):
    t_new = q_ref.shape[1]
    decays = [_ret_decays(t_new, h) for h in range(N_RET_HEADS)]
    scores = _ret_chunk_scores(q_ref[...], k_ref[...], v_ref[...], decays)
    yield
    states = [st_ref[:, h] for h in range(N_RET_HEADS)]
    out, states = _ret_chunk_finish(scores, g_ref[...], states, decays)
    o_ref[...] = out.astype(o_ref.dtype)
    for h in range(N_RET_HEADS):
        nst_ref[:, h] = states[h]


def _sample_mixers_stages(sink_rows_ref, qa_ref, kn_ref, vn_ref, ck_ref, cv_ref,
                          qr_ref, kr_ref, vr_ref, gr_ref, st_ref,
                          oa_ref, nk_ref, nv_ref, or_ref, nst_ref):
    attn = _attn_sample_stages(sink_rows_ref, qa_ref, kn_ref, vn_ref, ck_ref, cv_ref, oa_ref, nk_ref, nv_ref)
    ret = _ret_sample_stages(qr_ref, kr_ref, vr_ref, gr_ref, st_ref, or_ref, nst_ref)
    next(attn)
    next(ret)
    yield
    _run(attn)
    _run(ret)


def _sample_mixers_parts(qa, ka, va, sinks, cache_k, cache_v, qr, kr, vr, gr, state):
    db, t_new, _ = qa.shape
    w = cache_k.shape[1]
    sink_rows = jnp.repeat(sinks.astype(F32), t_new).reshape(N_Q_HEADS * t_new, 1)

    def parts(n_steps):
        assert db % n_steps == 0
        bt = db // n_steps
        blk = lambda *s: pl.BlockSpec((bt,) + s, lambda i: (i,) + (0,) * len(s))
        st = blk(N_RET_HEADS, RET_KEY_DIM, RET_VAL_DIM)
        in_specs = [_const_spec(sink_rows.shape), blk(t_new, Q_W), blk(t_new, KV_W), blk(t_new, KV_W),
                    blk(w, KV_W), blk(w, KV_W),
                    blk(t_new, RQ_W), blk(t_new, RQ_W), blk(t_new, RV_W), blk(t_new, RV_W), st]
        out_specs = [blk(t_new, Q_W), blk(w, KV_W), blk(w, KV_W), blk(t_new, RV_W), st]
        out_shape = [jax.ShapeDtypeStruct((db, t_new, Q_W), F32),
                     jax.ShapeDtypeStruct((db, w, KV_W), F32),
                     jax.ShapeDtypeStruct((db, w, KV_W), F32),
                     jax.ShapeDtypeStruct((db, t_new, RV_W), F32),
                     jax.ShapeDtypeStruct(state.shape, state.dtype)]
        args = [sink_rows, qa, ka, va, cache_k, cache_v, qr, kr, vr, gr, state]
        return _sample_mixers_stages, in_specs, out_specs, out_shape, args

    return parts


def _merge_ln_kernel(alpha, ln_row, x_ref, oa_ref, orr_ref, ga_ref, gb_ref, wba_ref, wbr_ref, wo_ref,
                     g_ref, b_ref, o_ref):
    n_blocks = max(1, x_ref.shape[0] // MERGE_ROWS)
    rb = x_ref.shape[0] // n_blocks
    blocks = [slice(r * rb, (r + 1) * rb) for r in range(n_blocks)]
    ya = [_dot(oa_ref[rows, :].astype(BF16), wba_ref[...]) for rows in blocks]
    yr = [_dot(orr_ref[rows, :].astype(BF16), wbr_ref[...]) for rows in blocks]
    merged = [(jax.nn.sigmoid(ga_ref[rows, :].astype(F32)) * ya[r]
               + jax.nn.sigmoid(gb_ref[rows, :].astype(F32)) * yr[r]).astype(BF16)
              for r, rows in enumerate(blocks)]
    z = [_dot(merged[r], wo_ref[...]) for r in range(n_blocks)]
    for r, rows in enumerate(blocks):
        o_ref[rows, :] = _layer_norm(alpha * x_ref[rows, :] + z[r],
                                     g_ref[ln_row:ln_row + 1, :], b_ref[ln_row:ln_row + 1, :])


def _merge_ln(x1, oa, orr, ga, gb, wba, wbr, wo, ln_g, ln_b, ln_row, alpha):
    t, d = x1.shape
    tm = _dense_tile(t)
    row = lambda i: (i, 0)
    return pl.pallas_call(
        functools.partial(_merge_ln_kernel, alpha, ln_row),
        grid=(t // tm,),
        in_specs=[pl.BlockSpec((tm, d), row), pl.BlockSpec((tm, Q_W), row), pl.BlockSpec((tm, RV_W), row),
                  pl.BlockSpec((tm, d), row), pl.BlockSpec((tm, d), row),
                  _const_spec(wba.shape), _const_spec(wbr.shape), _const_spec(wo.shape),
                  _const_spec(ln_g.shape), _const_spec(ln_b.shape)],
        out_specs=pl.BlockSpec((tm, d), row),
        out_shape=jax.ShapeDtypeStruct((t, d), F32),
        compiler_params=_cparams(1),
        name="merge_ln",
    )(x1, oa, orr, ga, gb, wba, wbr, wo, ln_g, ln_b)


def _rope_tables_for(pos):
    tables = _rope_tables(pos, HEAD_DIM, _attn_inv_freq) + _rope_tables(pos, RET_KEY_DIM, _ret_inv_freq)
    period = pos.shape[0]
    if period < DENSE_TILE:
        tables = tuple(jnp.tile(tb, (DENSE_TILE // period, 1)) for tb in tables)
    return tables


def kernel(x_prompt, x_sample, cache_k, cache_v, state_ret, ln_g, ln_b, ffn_wi, ffn_wo, w_in,
           attn_sinks, w_br_attn, w_br_ret, w_o):
    B, S, D = x_prompt.shape
    DB, T, _ = x_sample.shape
    depth = ln_g.shape[0]
    alpha = (2.0 * depth) ** 0.25
    W = cache_k.shape[2]
    tables_p = _rope_tables_for(jnp.arange(S, dtype=jnp.int32))
    tables_s = _rope_tables_for(PAST_LEN + jnp.arange(T, dtype=jnp.int32))

    y_p = x_prompt.reshape(B * S, D)
    y_s = x_sample.reshape(DB * T, D)
    outs = [[] for _ in range(6)]
    for l in range(depth):
        wi, wo = ffn_wi[l], ffn_wo[l]
        sinks = attn_sinks[l].astype(F32)
        g_l, b_l = ln_g[l], ln_b[l]

        x1_p, x1_s, w_in_l, wba, wbr, w_o_l = _ffn_ln(
            y_p, y_s, wi, wo, 0, g_l, b_l, 0, alpha,
            cast_weights=(w_in[l], w_br_attn[l], w_br_ret[l], w_o[l]))
        qa_s, ka_s, va_s, qr_s, kr_s, vr_s, gr_s, ga_s, gb_s = _proj_rope(
            x1_s, w_in_l, tables_s, F32)

        r3 = lambda a: a.reshape(DB, T, a.shape[-1])
        sample_mixers = _sample_mixers_parts(
            r3(qa_s), r3(ka_s), r3(va_s), sinks,
            cache_k[l].reshape(DB, W, KV_W), cache_v[l].reshape(DB, W, KV_W),
            r3(qr_s), r3(kr_s), r3(vr_s), r3(gr_s), state_ret[l])
        (qa_p, ka_p, va_p, qr_p, kr_p, vr_p, gr_p, ga_p, gb_p), (oa_s, nk_s, nv_s, or_s, st_s) = _proj_rope(
            x1_p, w_in_l, tables_p, BF16, rider=(RIDER_PROJ_TILE, sample_mixers))
        oa_s, or_s = oa_s.reshape(DB * T, Q_W), or_s.reshape(DB * T, RV_W)

        oa_p = _attn_prompt(qa_p, ka_p, va_p, sinks, B, S)
        or_p, st_p = _ret_prompt(qr_p, kr_p, vr_p, gr_p, B, S)
        w = min(WINDOW, S)
        nk_p = ka_p.reshape(B, S, N_KV_HEADS, HEAD_DIM)[:, S - w:]
        nv_p = va_p.reshape(B, S, N_KV_HEADS, HEAD_DIM)[:, S - w:]
        shape5 = (DB, W, N_KV_HEADS, HEAD_DIM)

        x2_p = _merge_ln(x1_p, oa_p, or_p, ga_p, gb_p, wba, wbr, w_o_l, g_l, b_l, 1, alpha)
        x2_s = _merge_ln(x1_s, oa_s, or_s, ga_s, gb_s, wba, wbr, w_o_l, g_l, b_l, 1, alpha)
        y_p, y_s = _ffn_ln(x2_p, x2_s, wi, wo, 1, g_l, b_l, 2, alpha)
        for lst, val in zip(outs, (nk_p, nv_p, st_p, nk_s.reshape(shape5), nv_s.reshape(shape5), st_s)):
            lst.append(val)
    return (y_p.reshape(B, S, D), y_s.reshape(DB, T, D)) + tuple(jnp.stack(o) for o in outs)
```

```python
import functools
import math

import jax
import jax.numpy as jnp
from jax import lax
from jax.experimental import pallas as pl
from jax.experimental.pallas import tpu as pltpu

F32 = jnp.float32
BF16 = jnp.bfloat16

PAST_LEN = 8192
N_Q_HEADS = 8
N_KV_HEADS = 2
HEAD_DIM = 64
GQA_GROUP = N_Q_HEADS // N_KV_HEADS
WINDOW = 128
ROPE_THETA = 10000.0
N_RET_HEADS = 4
RET_KEY_DIM = 128
RET_VAL_DIM = 256
RET_CHUNK = 128
RET_THETA = 10000.0
LN_EPS = 1e-5
GN_EPS = 1e-6
NEG_INF = -1e30

Q_W = N_Q_HEADS * HEAD_DIM
KV_W = N_KV_HEADS * HEAD_DIM
RQ_W = N_RET_HEADS * RET_KEY_DIM
RV_W = N_RET_HEADS * RET_VAL_DIM

LANES = 128
SUBLANES = 8
VMEM_LIMIT_BYTES = 56 * 1024 * 1024

TOKEN_TILE = 512
DENSE_TILE = 1024
DENSE_MIN_STEPS = 4
FF_CHUNK = 256
ATTN_TILE = 2048
RET_TILE = 1024
LN_ANCHOR_CHUNK = 8
MERGE_ROWS = 256
RIDER_PROJ_TILE = 512
RIDER_LEAD_GROUPS = 2
CAST_STEPS = 16


def _dot(a, b):
    return jnp.dot(a, b, preferred_element_type=F32)


def _dot_nt(a, b):
    return lax.dot_general(a, b, (((1,), (1,)), ((), ())), preferred_element_type=F32)


def _layer_norm(y, g, b):
    mu = jnp.mean(y, axis=-1, keepdims=True)
    yc = y - mu
    var = jnp.mean(yc * yc, axis=-1, keepdims=True)
    return yc * lax.rsqrt(var + LN_EPS) * g + b


def _cparams(n_axes, semantics=None):
    return pltpu.CompilerParams(
        dimension_semantics=semantics or ("arbitrary",) * n_axes,
        vmem_limit_bytes=VMEM_LIMIT_BYTES)


def _const_spec(shape):
    nd = len(shape)
    return pl.BlockSpec(shape, lambda *_: (0,) * nd, pipeline_mode=pl.Buffered(1))


def _segment_specs(n_prompt_tiles, tm, width, tile_of):
    prompt = pl.BlockSpec((tm, width), lambda i: (jnp.minimum(tile_of(i), n_prompt_tiles - 1), 0))
    sample = pl.BlockSpec((tm, width), lambda i: (jnp.maximum(tile_of(i) - n_prompt_tiles, 0), 0))
    return prompt, sample


def _store_segment(is_prompt, prompt_ref, sample_ref, value):
    @pl.when(is_prompt)
    def _():
        prompt_ref[...] = value.astype(prompt_ref.dtype)

    @pl.when(jnp.logical_not(is_prompt))
    def _():
        sample_ref[...] = value.astype(sample_ref.dtype)


def _ffn_chunk(xb, w_gate, w_up, w_out):
    act = (jax.nn.silu(_dot(xb, w_gate)) * _dot(xb, w_up)).astype(BF16)
    return _dot(act, w_out)


def _ffn_tile_maps(n_chunks, n_tiles):
    matmul_tile = lambda i: jnp.clip(i - (n_chunks - 1), 0, n_tiles - 1)
    norm_tile = lambda i: jnp.clip(i - n_chunks, 0, n_tiles - 1)
    return matmul_tile, norm_tile


def _ffn_ln_kernel(alpha, n_chunks, n_tiles, ln_row, n_prompt_tiles, n_cast, *refs):
    xp_ref, xs_ref, wg_ref, wu_ref, wo_ref, g_ref, b_ref = refs[:7]
    cast_in = refs[7:7 + n_cast]
    op_ref, os_ref = refs[7 + n_cast:9 + n_cast]
    cast_out = refs[9 + n_cast:9 + 2 * n_cast]
    acc_ref, y_ref, wi_bf_ref, wo_bf_ref = refs[9 + 2 * n_cast:]
    step = pl.program_id(0)
    matmul_tile, norm_tile = _ffn_tile_maps(n_chunks, n_tiles)
    last_step = n_chunks + n_tiles - 1
    fc = FF_CHUNK

    norm_is_prompt = norm_tile(step) < n_prompt_tiles

    def norm_previous():
        return _layer_norm(y_ref[...], g_ref[ln_row:ln_row + 1, :], b_ref[ln_row:ln_row + 1, :])

    @pl.when(step < n_chunks)
    def _():
        w_gate, w_up, w_out = wg_ref[...].astype(BF16), wu_ref[...].astype(BF16), wo_ref[...].astype(BF16)
        wi_bf_ref[step, :, :fc] = w_gate
        wi_bf_ref[step, :, fc:] = w_up
        wo_bf_ref[step] = w_out
        x = xp_ref[...]
        part = _ffn_chunk(x.astype(BF16), w_gate, w_up, w_out)

        @pl.when(step == 0)
        def _():
            acc_ref[...] = part

        @pl.when(step > 0)
        def _():
            acc_ref[...] += part

        @pl.when(step == n_chunks - 1)
        def _():
            y_ref[...] = alpha * x + 0.5 * acc_ref[...]

    def full_tile_step(out_ref):
        normed = norm_previous()
        out_ref[...] = normed
        rows, width = normed.shape
        anchor = jnp.max(normed.reshape(rows // SUBLANES, SUBLANES, width), axis=0)
        anchor = functools.reduce(jnp.maximum, [anchor[:, j * LANES:(j + 1) * LANES]
                                                for j in range(width // LANES)])
        zero = jnp.minimum(jnp.abs(anchor), 0.0)
        x = jnp.where(matmul_tile(step) < n_prompt_tiles, xp_ref[...], xs_ref[...])
        xb = x.astype(BF16)
        for c in range(n_chunks):
            part = _ffn_chunk(xb, wi_bf_ref[c, :, :fc], wi_bf_ref[c, :, fc:], wo_bf_ref[c])
            if c == 0:
                acc_ref[...] = part
            else:
                acc_ref[...] += part
            if c == min(LN_ANCHOR_CHUNK, n_chunks - 1):
                acc_ref[0:SUBLANES, 0:LANES] += zero
        y_ref[...] = alpha * x + 0.5 * acc_ref[...]

    in_full_steps = (step >= n_chunks) & (step < last_step)

    if n_cast:
        @pl.when((step >= n_chunks) & (step < n_chunks + CAST_STEPS))
        def _():
            for src, dst in zip(cast_in, cast_out):
                dst[...] = src[...].astype(dst.dtype)

    pl.when(in_full_steps & norm_is_prompt)(functools.partial(full_tile_step, op_ref))
    pl.when(in_full_steps & jnp.logical_not(norm_is_prompt))(functools.partial(full_tile_step, os_ref))

    @pl.when(step == last_step)
    def _():
        _store_segment(norm_is_prompt, op_ref, os_ref, norm_previous())


def _ffn_ln(xp, xs, wi, wo, half, ln_g, ln_b, ln_row, alpha, cast_weights=()):
    d = xp.shape[1]
    d_ff = wo.shape[1]
    tm, fc = TOKEN_TILE, FF_CHUNK
    assert xp.shape[0] % tm == 0 and xs.shape[0] % tm == 0 and d_ff % fc == 0
    n_p, n_s = xp.shape[0] // tm, xs.shape[0] // tm
    n_chunks = d_ff // fc
    n_tiles = n_p + n_s
    assert n_tiles - 1 >= CAST_STEPS
    matmul_tile, norm_tile = _ffn_tile_maps(n_chunks, n_tiles)
    chunk_of = lambda i: jnp.minimum(i, n_chunks - 1)
    cast_specs = []
    for w in cast_weights:
        rows = w.shape[0] // CAST_STEPS
        assert w.shape[0] % CAST_STEPS == 0 and rows % 16 == 0
        cast_specs.append(pl.BlockSpec((rows, w.shape[1]),
                                       lambda i: (jnp.clip(i - n_chunks, 0, CAST_STEPS - 1), 0)))
    return pl.pallas_call(
        functools.partial(_ffn_ln_kernel, alpha, n_chunks, n_tiles, ln_row, n_p, len(cast_weights)),
        grid=(n_chunks + n_tiles,),
        in_specs=[*_segment_specs(n_p, tm, d, matmul_tile),
                  pl.BlockSpec((None, d, fc), lambda i: (half, 0, chunk_of(i))),
                  pl.BlockSpec((None, d, fc), lambda i: (half, 0, n_chunks + chunk_of(i))),
                  pl.BlockSpec((None, fc, d), lambda i: (half, chunk_of(i), 0)),
                  _const_spec(ln_g.shape), _const_spec(ln_b.shape), *cast_specs],
        out_specs=[*_segment_specs(n_p, tm, d, norm_tile), *cast_specs],
        out_shape=[jax.ShapeDtypeStruct(xp.shape, F32), jax.ShapeDtypeStruct(xs.shape, F32)]
                  + [jax.ShapeDtypeStruct(w.shape, BF16) for w in cast_weights],
        scratch_shapes=[pltpu.VMEM((tm, d), F32), pltpu.VMEM((tm, d), F32),
                        pltpu.VMEM((n_chunks, d, 2 * fc), BF16),
                        pltpu.VMEM((n_chunks, fc, d), BF16)],
        compiler_params=_cparams(1),
        name="ffn_ln",
    )(xp, xs, wi, wi, wo, ln_g, ln_b, *cast_weights)


def _rope_tables(pos, dim, theta_fn):
    half = dim // 2
    inv_freq = theta_fn(half)
    ang = pos.astype(F32)[:, None] * inv_freq[None, :]
    reps = LANES // half
    cos = jnp.tile(jnp.cos(ang), (1, reps))
    sign = jnp.tile(jnp.concatenate([-jnp.ones((half,), F32), jnp.ones((half,), F32)]), LANES // dim)
    sin = jnp.tile(jnp.sin(ang), (1, reps)) * sign[None, :]
    return cos, sin


def _attn_inv_freq(half):
    return 1.0 / (ROPE_THETA ** (jnp.arange(0, 2 * half, 2, dtype=F32) / (2 * half)))


def _ret_inv_freq(half):
    return 1.0 / (RET_THETA ** jnp.linspace(0.0, 1.0, half, dtype=F32))


def _rope_block(x, cos, sin, dim):
    half = dim // 2
    if dim == LANES:
        rot = pltpu.roll(x, half, 1)
    else:
        lane = lax.broadcasted_iota(jnp.int32, x.shape, 1)
        first_half = (lane % dim) < half
        rot = jnp.where(first_half, pltpu.roll(x, LANES - half, 1), pltpu.roll(x, half, 1))
    return x * cos + rot * sin


def _run(stages):
    for _ in stages:
        pass


def _proj_rope_stages(x_ref, w_ref, cosa_ref, sina_ref, cosr_ref, sinr_ref,
                      qa_ref, ka_ref, va_ref, qr_ref, kr_ref, vr_ref, gr_ref, ga_ref, gb_ref,
                      k_tail_ref=None, v_tail_ref=None):
    xb = x_ref[...].astype(BF16)
    cosa, sina = cosa_ref[...], sina_ref[...]
    cosr, sinr = cosr_ref[...], sinr_ref[...]
    off = 0

    y = _dot(xb, w_ref[:, off:off + Q_W])
    for j in range(Q_W // LANES):
        qa_ref[:, j * LANES:(j + 1) * LANES] = _rope_block(
            y[:, j * LANES:(j + 1) * LANES], cosa, sina, HEAD_DIM).astype(qa_ref.dtype)
    off += Q_W
    yield
    y = _dot(xb, w_ref[:, off:off + 2 * KV_W])
    k_rot = _rope_block(y[:, :KV_W], cosa, sina, HEAD_DIM)
    ka_ref[...] = k_rot
    va_ref[...] = y[:, KV_W:]
    if k_tail_ref is not None:
        tail = k_tail_ref.shape[0]
        k_tail_ref[...] = k_rot[-tail:, :]
        v_tail_ref[...] = y[-tail:, KV_W:]
    off += 2 * KV_W
    yield
    y = _dot(xb, w_ref[:, off:off + RQ_W])
    for j in range(RQ_W // LANES):
        qr_ref[:, j * LANES:(j + 1) * LANES] = _rope_block(
            y[:, j * LANES:(j + 1) * LANES], cosr, sinr, RET_KEY_DIM).astype(qr_ref.dtype)
    off += RQ_W
    yield
    y = _dot(xb, w_ref[:, off:off + RQ_W])
    for j in range(RQ_W // LANES):
        kr = _rope_block(y[:, j * LANES:(j + 1) * LANES], cosr, sinr, RET_KEY_DIM)
        kr_ref[:, j * LANES:(j + 1) * LANES] = (kr * (RET_KEY_DIM ** -0.5)).astype(kr_ref.dtype)
    off += RQ_W
    yield
    for ref in (vr_ref, gr_ref, ga_ref, gb_ref):
        width = ref.shape[1]
        for c in range(width // 512):
            ref[:, c * 512:(c + 1) * 512] = _dot(
                xb, w_ref[:, off + c * 512:off + (c + 1) * 512]).astype(ref.dtype)
            yield
        off += width


def _proj_rope_kernel(*refs):
    _run(_proj_rope_stages(*refs))


def _dense_tile(tokens):
    return min(DENSE_TILE, tokens // DENSE_MIN_STEPS)


N_PROJ_IN = 6


def _proj_with_rider_kernel(rider_stages, n_rider_in, n_proj_out, *refs):
    proj_in, rest = refs[:N_PROJ_IN], refs[N_PROJ_IN:]
    rider_in, rest = rest[:n_rider_in], rest[n_rider_in:]
    proj_out, rider_out = rest[:n_proj_out], rest[n_proj_out:]
    rider = rider_stages(*rider_in, *rider_out)
    proj = _proj_rope_stages(*proj_in, *proj_out)
    next(rider)
    for _ in range(RIDER_LEAD_GROUPS):
        next(proj)
    _run(rider)
    _run(proj)


def _proj_rope(x1, w_in, tables, mixer_dtype, rider=None, tail=None):
    t, d = x1.shape
    n_in = w_in.shape[1]
    d_model = (n_in - Q_W - 2 * KV_W - 2 * RQ_W - 2 * RV_W) // 2
    tm = _dense_tile(t) if rider is None else rider[0]
    row = lambda i: (i, 0)
    table_blocks = tables[0].shape[0] // tm
    tab = lambda i: (i % table_blocks, 0)
    widths = [(Q_W, mixer_dtype), (KV_W, F32), (KV_W, F32), (RQ_W, mixer_dtype), (RQ_W, mixer_dtype),
              (RV_W, mixer_dtype), (RV_W, mixer_dtype), (d_model, BF16), (d_model, BF16)]
    in_specs = [pl.BlockSpec((tm, d), row), _const_spec(w_in.shape)] + [pl.BlockSpec((tm, LANES), tab)] * 4
    out_specs = [pl.BlockSpec((tm, w), row) for w, _ in widths]
    out_shape = [jax.ShapeDtypeStruct((t, w), dt) for w, dt in widths]
    if tail is not None:
        seq_len, tail_rows = tail
        assert seq_len % tm == 0 and tail_rows <= tm
        tiles_per_seq = seq_len // tm
        out_specs += [pl.BlockSpec((tail_rows, KV_W), lambda i: (i // tiles_per_seq, 0))] * 2
        out_shape += [jax.ShapeDtypeStruct((t // seq_len * tail_rows, KV_W), F32)] * 2
    n_proj_out = len(out_specs)
    args = [x1, w_in, *tables]
    body = _proj_rope_kernel
    if rider is not None:
        r_stages, r_in_specs, r_out_specs, r_out_shape, r_args = rider[1](t // tm)
        body = functools.partial(_proj_with_rider_kernel, r_stages, len(r_args), n_proj_out)
        in_specs, out_specs = in_specs + r_in_specs, out_specs + r_out_specs
        out_shape, args = out_shape + r_out_shape, args + r_args
    outs = pl.pallas_call(
        body,
        grid=(t // tm,),
        in_specs=in_specs,
        out_specs=out_specs,
        out_shape=out_shape,
        compiler_params=_cparams(1),
        name="proj_rope",
    )(*args)
    return outs if rider is None else (outs[:n_proj_out], outs[n_proj_out:])


def _kv_lane_variants(x):
    lane = lax.broadcasted_iota(jnp.int32, x.shape, 1)
    lo = lane < HEAD_DIM
    swapped = pltpu.roll(x, HEAD_DIM, 1)
    zero = jnp.zeros_like(x)
    kv0 = (jnp.where(lo, x, zero).astype(BF16), jnp.where(lo, zero, swapped).astype(BF16))
    kv1 = (jnp.where(lo, swapped, zero).astype(BF16), jnp.where(lo, zero, x).astype(BF16))
    return kv0, kv1


def _attn_prompt_kernel(sinks_ref, q_ref, kc_ref, kp_ref, vc_ref, vp_ref, o_ref):
    n = pl.program_id(1)
    L = WINDOW
    k_full = jnp.concatenate([kp_ref[...], kc_ref[...]], axis=0)
    v_full = jnp.concatenate([vp_ref[...], vc_ref[...]], axis=0)
    k_var = _kv_lane_variants(k_full)
    v_var = _kv_lane_variants(v_full)
    own = (lax.broadcasted_iota(jnp.int32, (L, L), 1)
           <= lax.broadcasted_iota(jnp.int32, (L, L), 0))
    scale = HEAD_DIM ** -0.5

    units = [(i, j, par) for i in range(q_ref.shape[0] // L) for j in range(Q_W // LANES) for par in range(2)]

    def scores(unit):
        i, j, par = unit
        q2 = q_ref[i * L:(i + 1) * L, j * LANES:(j + 1) * LANES] * scale
        return _dot_nt(q2, k_var[(2 * j) // GQA_GROUP][par][i * L:(i + 2) * L])

    s_next = scores(units[0])
    out = None
    for idx, (i, j, par) in enumerate(units):
        s = s_next
        if idx + 1 < len(units):
            s_next = scores(units[idx + 1])
        s_prev = s[:, :L]
        if i == 0:
            s_prev = jnp.where(n > 0, s_prev, NEG_INF)
        sw = jnp.where(own, s[:, L:], s_prev)
        sink = sinks_ref[2 * j + par]
        m = jnp.maximum(jnp.max(sw, axis=-1, keepdims=True), sink)
        e = jnp.exp(sw - m)
        denom = jnp.sum(e, axis=-1, keepdims=True) + jnp.exp(sink - m)
        e_band = jnp.concatenate([jnp.where(own, 0.0, e), jnp.where(own, e, 0.0)], axis=1)
        r = _dot(e_band.astype(BF16), v_var[(2 * j) // GQA_GROUP][par][i * L:(i + 2) * L]) / denom
        if par == 0:
            out = r
        else:
            o_ref[i * L:(i + 1) * L, j * LANES:(j + 1) * LANES] = (out + r).astype(o_ref.dtype)


def _attn_prompt(qa, ka, va, sinks, batch, seq):
    tq = ATTN_TILE
    nq = seq // tq
    per = tq // WINDOW
    cur = lambda b, n: (b * nq + n, 0)
    prev = lambda b, n: ((b * nq + n) * per - jnp.minimum(n, 1), 0)
    return pl.pallas_call(
        _attn_prompt_kernel,
        grid=(batch, nq),
        in_specs=[pl.BlockSpec(memory_space=pltpu.SMEM),
                  pl.BlockSpec((tq, Q_W), cur),
                  pl.BlockSpec((tq, KV_W), cur), pl.BlockSpec((WINDOW, KV_W), prev),
                  pl.BlockSpec((tq, KV_W), cur), pl.BlockSpec((WINDOW, KV_W), prev)],
        out_specs=pl.BlockSpec((tq, Q_W), cur),
        out_shape=jax.ShapeDtypeStruct(qa.shape, BF16),
        compiler_params=_cparams(2),
        name="attn_prompt",
    )(sinks, qa, ka, ka, va, va)


def _attn_sample_stages(sink_rows_ref, q_ref, kn_ref, vn_ref, ck_ref, cv_ref,
                        o_ref, nk_ref, nv_ref):
    bt, t_new, _ = q_ref.shape
    assert t_new == SUBLANES
    w = ck_ref.shape[1]
    n_rows = N_Q_HEADS * t_new
    lane_lo = lax.broadcasted_iota(jnp.int32, (bt * t_new, LANES), 1) < HEAD_DIM
    r_i = lax.broadcasted_iota(jnp.int32, (n_rows, w + t_new), 0) % t_new
    m_i = lax.broadcasted_iota(jnp.int32, (n_rows, w + t_new), 1)
    mask = ((m_i < w) & (m_i > r_i + (w - WINDOW))) | ((m_i >= w) & (m_i - w <= r_i))
    sink = sink_rows_ref[...]
    scale = HEAD_DIM ** -0.5
    kc, kn = ck_ref[...], kn_ref[...]
    vc, vn = cv_ref[...], vn_ref[...]
    k_all = jnp.concatenate([kc, kn], axis=1).astype(BF16)
    v_all = jnp.concatenate([vc, vn], axis=1).astype(BF16)
    q = q_ref[...].reshape(bt * t_new, Q_W)
    pieces = []
    for h in range(N_Q_HEADS):
        q2 = q[:, (h // 2) * LANES:(h // 2 + 1) * LANES]
        src_lo = h % 2 == 0
        dst_lo = h // GQA_GROUP == 0
        if src_lo != dst_lo:
            q2 = pltpu.roll(q2, HEAD_DIM, 1)
        pieces.append(jnp.where(lane_lo == dst_lo, q2, 0.0).reshape(bt, t_new, LANES))
    q_rows = (jnp.concatenate(pieces, axis=1) * scale).astype(BF16)
    s = jnp.einsum("bqd,bkd->bqk", q_rows, k_all, preferred_element_type=F32)
    yield
    s = jnp.where(mask, s, NEG_INF)
    m = jnp.maximum(jnp.max(s, axis=-1, keepdims=True), sink)
    e = jnp.exp(s - m)
    denom = jnp.sum(e, axis=-1, keepdims=True) + jnp.exp(sink - m)
    pv = jnp.einsum("bqk,bkd->bqd", e.astype(BF16), v_all, preferred_element_type=F32) / denom
    for j in range(Q_W // LANES):
        out = None
        for par in range(2):
            h = 2 * j + par
            dst_lo = h // GQA_GROUP == 0
            piece = pv[:, h * t_new:(h + 1) * t_new, :].reshape(bt * t_new, LANES)
            piece = jnp.where(lane_lo == dst_lo, piece, 0.0)
            if dst_lo != (par == 0):
                piece = pltpu.roll(piece, HEAD_DIM, 1)
            out = piece if out is None else out + piece
        o_ref[:, :, j * LANES:(j + 1) * LANES] = out.reshape(bt, t_new, LANES).astype(o_ref.dtype)
    nk_ref[:, :w - t_new, :] = kc[:, t_new:, :]
    nk_ref[:, w - t_new:, :] = kn
    nv_ref[:, :w - t_new, :] = vc[:, t_new:, :]
    nv_ref[:, w - t_new:, :] = vn


def _ret_log_gamma(h):
    return math.log(1.0 - 2.0 ** (-5.0 - h))


def _ret_decays(chunk, h):
    lg = _ret_log_gamma(h)
    i = lax.broadcasted_iota(jnp.int32, (chunk, chunk), 0)
    j = lax.broadcasted_iota(jnp.int32, (chunk, chunk), 1)
    rel_mask = jnp.where(i >= j, jnp.exp((j + 1).astype(F32) * -lg), 0.0)
    col = lax.broadcasted_iota(jnp.int32, (chunk, 1), 0).astype(F32)
    gn_eps = GN_EPS * jnp.exp((col + 1.0) * (-2.0 * lg))
    k_decay = jnp.exp((chunk - 1.0 - col) * lg)
    return rel_mask, gn_eps, k_decay, math.exp(chunk * lg)


def _ret_chunk_scores(q, k, v, decays):
    heads = range(N_RET_HEADS)
    batched = q.ndim == 3
    mm = (lambda a, b: jnp.einsum("bmk,bkn->bmn", a, b, preferred_element_type=F32)) if batched else _dot
    mm_nt = (lambda a, b: jnp.einsum("bmk,bnk->bmn", a, b, preferred_element_type=F32)) if batched else _dot_nt
    qh = [q[..., h * RET_KEY_DIM:(h + 1) * RET_KEY_DIM] for h in heads]
    kh = [k[..., h * RET_KEY_DIM:(h + 1) * RET_KEY_DIM] for h in heads]
    vh = [v[..., h * RET_VAL_DIM:(h + 1) * RET_VAL_DIM] for h in heads]
    qb = [x.astype(BF16) for x in qh]
    vb = [x.astype(BF16) for x in vh]
    inner = [mm_nt(qb[h], kh[h].astype(BF16)) * decays[h][0] for h in heads]
    return batched, mm, qh, kh, vh, qb, vb, inner


def _ret_chunk_finish(scores, g, states, decays):
    batched, mm, qh, kh, vh, qb, vb, inner = scores
    heads = range(N_RET_HEADS)
    if batched:
        lhs = [jnp.concatenate([qh[h], inner[h]], axis=-1).astype(BF16) for h in heads]
        rhs = [jnp.concatenate([states[h], vh[h]], axis=-2).astype(BF16) for h in heads]
    else:
        lhs = [jnp.concatenate([qb[h], inner[h].astype(BF16)], axis=-1) for h in heads]
        rhs = [jnp.concatenate([states[h].astype(BF16), vb[h]], axis=-2) for h in heads]
    p = [mm(lhs[h], rhs[h]) for h in heads]
    kd_t = [jnp.swapaxes(kh[h].astype(F32) * decays[h][2], -1, -2).astype(BF16) for h in heads]
    new_states = [states[h] * decays[h][3] + mm(kd_t[h], vb[h]) for h in heads]
    outs = []
    for h in heads:
        mu = jnp.mean(p[h], axis=-1, keepdims=True)
        pc = p[h] - mu
        var = jnp.mean(pc * pc, axis=-1, keepdims=True)
        gate = g[..., h * RET_VAL_DIM:(h + 1) * RET_VAL_DIM].astype(F32)
        outs.append(pc * lax.rsqrt(var + decays[h][1]) * jax.nn.silu(gate))
    return jnp.concatenate(outs, axis=-1), new_states


def _ret_chunk_all_heads(q, k, v, g, states, decays):
    return _ret_chunk_finish(_ret_chunk_scores(q, k, v, decays), g, states, decays)


def _ret_prompt_kernel(q_ref, k_ref, v_ref, g_ref, o_ref, st_ref, state_ref):
    n = pl.program_id(1)
    C = RET_CHUNK

    @pl.when(n == 0)
    def _():
        state_ref[...] = jnp.zeros_like(state_ref)

    decays = [_ret_decays(C, h) for h in range(N_RET_HEADS)]
    states = [state_ref[h] for h in range(N_RET_HEADS)]
    for c in range(q_ref.shape[0] // C):
        rows = slice(c * C, (c + 1) * C)
        out, states = _ret_chunk_all_heads(q_ref[rows, :], k_ref[rows, :], v_ref[rows, :], g_ref[rows, :],
                                           states, decays)
        o_ref[rows, :] = out.astype(o_ref.dtype)
    for h in range(N_RET_HEADS):
        state_ref[h] = states[h]

    @pl.when(n == pl.num_programs(1) - 1)
    def _():
        st_ref[0] = state_ref[...]


def _ret_prompt(qr, kr, vr, gr, batch, seq):
    ts = RET_TILE
    ns = seq // ts
    row = lambda b, n: (b * ns + n, 0)
    return pl.pallas_call(
        _ret_prompt_kernel,
        grid=(batch, ns),
        in_specs=[pl.BlockSpec((ts, RQ_W), row), pl.BlockSpec((ts, RQ_W), row),
                  pl.BlockSpec((ts, RV_W), row), pl.BlockSpec((ts, RV_W), row)],
        out_specs=[pl.BlockSpec((ts, RV_W), row),
                   pl.BlockSpec((1, N_RET_HEADS, RET_KEY_DIM, RET_VAL_DIM), lambda b, n: (b, 0, 0, 0))],
        out_shape=[jax.ShapeDtypeStruct((batch * seq, RV_W), BF16),
                   jax.ShapeDtypeStruct((batch, N_RET_HEADS, RET_KEY_DIM, RET_VAL_DIM), F32)],
        scratch_shapes=[pltpu.VMEM((N_RET_HEADS, RET_KEY_DIM, RET_VAL_DIM), F32)],
        compiler_params=_cparams(2),
        name="ret_prompt",
    )(qr, kr, vr, gr)


def _ret_sample_stages(q_ref, k_ref, v_ref, g_ref, st_ref, o_ref, nst_ref):
    t_new = q_ref.shape[1]
    decays = [_ret_decays(t_new, h) for h in range(N_RET_HEADS)]
    scores = _ret_chunk_scores(q_ref[...], k_ref[...], v_ref[...], decays)
    yield
    states = [st_ref[:, h] for h in range(N_RET_HEADS)]
    out, states = _ret_chunk_finish(scores, g_ref[...], states, decays)
    o_ref[...] = out.astype(o_ref.dtype)
    for h in range(N_RET_HEADS):
        nst_ref[:, h] = states[h]


def _sample_mixers_stages(sink_rows_ref, qa_ref, kn_ref, vn_ref, ck_ref, cv_ref,
                          qr_ref, kr_ref, vr_ref, gr_ref, st_ref,
                          oa_ref, nk_ref, nv_ref, or_ref, nst_ref):
    attn = _attn_sample_stages(sink_rows_ref, qa_ref, kn_ref, vn_ref, ck_ref, cv_ref, oa_ref, nk_ref, nv_ref)
    ret = _ret_sample_stages(qr_ref, kr_ref, vr_ref, gr_ref, st_ref, or_ref, nst_ref)
    next(attn)
    next(ret)
    yield
    _run(attn)
    _run(ret)


def _sample_mixers_parts(qa, ka, va, sinks, cache_k, cache_v, qr, kr, vr, gr, state):
    db, t_new, _ = qa.shape
    w = cache_k.shape[1]
    sink_rows = jnp.repeat(sinks.astype(F32), t_new).reshape(N_Q_HEADS * t_new, 1)

    def parts(n_steps):
        assert db % n_steps == 0
        bt = db // n_steps
        blk = lambda *s: pl.BlockSpec((bt,) + s, lambda i: (i,) + (0,) * len(s))
        st = blk(N_RET_HEADS, RET_KEY_DIM, RET_VAL_DIM)
        in_specs = [_const_spec(sink_rows.shape), blk(t_new, Q_W), blk(t_new, KV_W), blk(t_new, KV_W),
                    blk(w, KV_W), blk(w, KV_W),
                    blk(t_new, RQ_W), blk(t_new, RQ_W), blk(t_new, RV_W), blk(t_new, RV_W), st]
        out_specs = [blk(t_new, Q_W), blk(w, KV_W), blk(w, KV_W), blk(t_new, RV_W), st]
        out_shape = [jax.ShapeDtypeStruct((db, t_new, Q_W), F32),
                     jax.ShapeDtypeStruct((db, w, KV_W), F32),
                     jax.ShapeDtypeStruct((db, w, KV_W), F32),
                     jax.ShapeDtypeStruct((db, t_new, RV_W), F32),
                     jax.ShapeDtypeStruct(state.shape, state.dtype)]
        args = [sink_rows, qa, ka, va, cache_k, cache_v, qr, kr, vr, gr, state]
        return _sample_mixers_stages, in_specs, out_specs, out_shape, args

    return parts


def _merge_ln_kernel(alpha, ln_row, x_ref, oa_ref, orr_ref, ga_ref, gb_ref, wba_ref, wbr_ref, wo_ref,
                     g_ref, b_ref, o_ref):
    n_blocks = max(1, x_ref.shape[0] // MERGE_ROWS)
    rb = x_ref.shape[0] // n_blocks
    blocks = [slice(r * rb, (r + 1) * rb) for r in range(n_blocks)]
    ya = [_dot(oa_ref[rows, :].astype(BF16), wba_ref[...]) for rows in blocks]
    yr = [_dot(orr_ref[rows, :].astype(BF16), wbr_ref[...]) for rows in blocks]
    merged = [(jax.nn.sigmoid(ga_ref[rows, :].astype(F32)) * ya[r]
               + jax.nn.sigmoid(gb_ref[rows, :].astype(F32)) * yr[r]).astype(BF16)
              for r, rows in enumerate(blocks)]
    z = [_dot(merged[r], wo_ref[...]) for r in range(n_blocks)]
    for r, rows in enumerate(blocks):
        o_ref[rows, :] = _layer_norm(alpha * x_ref[rows, :] + z[r],
                                     g_ref[ln_row:ln_row + 1, :], b_ref[ln_row:ln_row + 1, :])


def _merge_ln(x1, oa, orr, ga, gb, wba, wbr, wo, ln_g, ln_b, ln_row, alpha):
    t, d = x1.shape
    tm = _dense_tile(t)
    row = lambda i: (i, 0)
    return pl.pallas_call(
        functools.partial(_merge_ln_kernel, alpha, ln_row),
        grid=(t // tm,),
        in_specs=[pl.BlockSpec((tm, d), row), pl.BlockSpec((tm, Q_W), row), pl.BlockSpec((tm, RV_W), row),
                  pl.BlockSpec((tm, d), row), pl.BlockSpec((tm, d), row),
                  _const_spec(wba.shape), _const_spec(wbr.shape), _const_spec(wo.shape),
                  _const_spec(ln_g.shape), _const_spec(ln_b.shape)],
        out_specs=pl.BlockSpec((tm, d), row),
        out_shape=jax.ShapeDtypeStruct((t, d), F32),
        compiler_params=_cparams(1),
        name="merge_ln",
    )(x1, oa, orr, ga, gb, wba, wbr, wo, ln_g, ln_b)


def _rope_tables_for(pos):
    tables = _rope_tables(pos, HEAD_DIM, _attn_inv_freq) + _rope_tables(pos, RET_KEY_DIM, _ret_inv_freq)
    period = pos.shape[0]
    if period < DENSE_TILE:
        tables = tuple(jnp.tile(tb, (DENSE_TILE // period, 1)) for tb in tables)
    return tables


def kernel(x_prompt, x_sample, cache_k, cache_v, state_ret, ln_g, ln_b, ffn_wi, ffn_wo, w_in,
           attn_sinks, w_br_attn, w_br_ret, w_o):
    B, S, D = x_prompt.shape
    DB, T, _ = x_sample.shape
    depth = ln_g.shape[0]
    alpha = (2.0 * depth) ** 0.25
    W = cache_k.shape[2]
    tables_p = _rope_tables_for(jnp.arange(S, dtype=jnp.int32))
    tables_s = _rope_tables_for(PAST_LEN + jnp.arange(T, dtype=jnp.int32))

    y_p = x_prompt.reshape(B * S, D)
    y_s = x_sample.reshape(DB * T, D)
    outs = [[] for _ in range(6)]
    for l in range(depth):
        wi, wo = ffn_wi[l], ffn_wo[l]
        sinks = attn_sinks[l].astype(F32)
        g_l, b_l = ln_g[l], ln_b[l]

        x1_p, x1_s, w_in_l, wba, wbr, w_o_l = _ffn_ln(
            y_p, y_s, wi, wo, 0, g_l, b_l, 0, alpha,
            cast_weights=(w_in[l], w_br_attn[l], w_br_ret[l], w_o[l]))
        qa_s, ka_s, va_s, qr_s, kr_s, vr_s, gr_s, ga_s, gb_s = _proj_rope(
            x1_s, w_in_l, tables_s, F32)

        r3 = lambda a: a.reshape(DB, T, a.shape[-1])
        sample_mixers = _sample_mixers_parts(
            r3(qa_s), r3(ka_s), r3(va_s), sinks,
            cache_k[l].reshape(DB, W, KV_W), cache_v[l].reshape(DB, W, KV_W),
            r3(qr_s), r3(kr_s), r3(vr_s), r3(gr_s), state_ret[l])
        w = min(WINDOW, S)
        (qa_p, ka_p, va_p, qr_p, kr_p, vr_p, gr_p, ga_p, gb_p, nk_p, nv_p), (oa_s, nk_s, nv_s, or_s, st_s) = \
            _proj_rope(x1_p, w_in_l, tables_p, BF16, rider=(RIDER_PROJ_TILE, sample_mixers), tail=(S, w))
        oa_s, or_s = oa_s.reshape(DB * T, Q_W), or_s.reshape(DB * T, RV_W)
        nk_p = nk_p.reshape(B, w, N_KV_HEADS, HEAD_DIM)
        nv_p = nv_p.reshape(B, w, N_KV_HEADS, HEAD_DIM)

        oa_p = _attn_prompt(qa_p, ka_p, va_p, sinks, B, S)
        or_p, st_p = _ret_prompt(qr_p, kr_p, vr_p, gr_p, B, S)
        shape5 = (DB, W, N_KV_HEADS, HEAD_DIM)

        x2_p = _merge_ln(x1_p, oa_p, or_p, ga_p, gb_p, wba, wbr, w_o_l, g_l, b_l, 1, alpha)
        x2_s = _merge_ln(x1_s, oa_s, or_s, ga_s, gb_s, wba, wbr, w_o_l, g_l, b_l, 1, alpha)
        y_p, y_s = _ffn_ln(x2_p, x2_s, wi, wo, 1, g_l, b_l, 2, alpha)
        for lst, val in zip(outs, (nk_p, nv_p, st_p, nk_s.reshape(shape5), nv_s.reshape(shape5), st_s)):
            lst.append(val)
    return (y_p.reshape(B, S, D), y_s.reshape(DB, T, D)) + tuple(jnp.stack(o) for o in outs)
```

```python
import functools
import math

import jax
import jax.numpy as jnp
from jax import lax
from jax.experimental import pallas as pl
from jax.experimental.pallas import tpu as pltpu

F32 = jnp.float32
BF16 = jnp.bfloat16

PAST_LEN = 8192
N_Q_HEADS = 8
N_KV_HEADS = 2
HEAD_DIM = 64
GQA_GROUP = N_Q_HEADS // N_KV_HEADS
WINDOW = 128
ROPE_THETA = 10000.0
N_RET_HEADS = 4
RET_KEY_DIM = 128
RET_VAL_DIM = 256
RET_CHUNK = 128
RET_THETA = 10000.0
LN_EPS = 1e-5
GN_EPS = 1e-6
NEG_INF = -1e30

Q_W = N_Q_HEADS * HEAD_DIM
KV_W = N_KV_HEADS * HEAD_DIM
RQ_W = N_RET_HEADS * RET_KEY_DIM
RV_W = N_RET_HEADS * RET_VAL_DIM

LANES = 128
SUBLANES = 8
VMEM_LIMIT_BYTES = 56 * 1024 * 1024

TOKEN_TILE = 512
DENSE_TILE = 1024
DENSE_MIN_STEPS = 4
FF_CHUNK = 256
ATTN_TILE = 2048
RET_TILE = 1024
LN_ANCHOR_CHUNK = 8
MERGE_ROWS = 256
RIDER_PROJ_TILE = 512
RIDER_LEAD_GROUPS = 2
CAST_STEPS = 16


def _dot(a, b):
    return jnp.dot(a, b, preferred_element_type=F32)


def _dot_nt(a, b):
    return lax.dot_general(a, b, (((1,), (1,)), ((), ())), preferred_element_type=F32)


def _layer_norm(y, g, b):
    mu = jnp.mean(y, axis=-1, keepdims=True)
    yc = y - mu
    var = jnp.mean(yc * yc, axis=-1, keepdims=True)
    return yc * lax.rsqrt(var + LN_EPS) * g + b


def _cparams(n_axes, semantics=None):
    return pltpu.CompilerParams(
        dimension_semantics=semantics or ("arbitrary",) * n_axes,
        vmem_limit_bytes=VMEM_LIMIT_BYTES)


def _const_spec(shape):
    nd = len(shape)
    return pl.BlockSpec(shape, lambda *_: (0,) * nd, pipeline_mode=pl.Buffered(1))


def _segment_specs(n_prompt_tiles, tm, width, tile_of):
    prompt = pl.BlockSpec((tm, width), lambda i: (jnp.minimum(tile_of(i), n_prompt_tiles - 1), 0))
    sample = pl.BlockSpec((tm, width), lambda i: (jnp.maximum(tile_of(i) - n_prompt_tiles, 0), 0))
    return prompt, sample


def _store_segment(is_prompt, prompt_ref, sample_ref, value):
    @pl.when(is_prompt)
    def _():
        prompt_ref[...] = value.astype(prompt_ref.dtype)

    @pl.when(jnp.logical_not(is_prompt))
    def _():
        sample_ref[...] = value.astype(sample_ref.dtype)


def _ffn_chunk(xb, w_gate, w_up, w_out):
    act = (jax.nn.silu(_dot(xb, w_gate)) * _dot(xb, w_up)).astype(BF16)
    return _dot(act, w_out)


def _ffn_tile_maps(n_chunks, n_tiles):
    matmul_tile = lambda i: jnp.clip(i - (n_chunks - 1), 0, n_tiles - 1)
    norm_tile = lambda i: jnp.clip(i - n_chunks, 0, n_tiles - 1)
    return matmul_tile, norm_tile


def _ffn_ln_kernel(alpha, n_chunks, n_tiles, ln_row, n_prompt_tiles, n_cast, *refs):
    xp_ref, xs_ref, wg_ref, wu_ref, wo_ref, g_ref, b_ref = refs[:7]
    cast_in = refs[7:7 + n_cast]
    op_ref, os_ref = refs[7 + n_cast:9 + n_cast]
    cast_out = refs[9 + n_cast:9 + 2 * n_cast]
    acc_ref, y_ref, wi_bf_ref, wo_bf_ref = refs[9 + 2 * n_cast:]
    step = pl.program_id(0)
    matmul_tile, norm_tile = _ffn_tile_maps(n_chunks, n_tiles)
    last_step = n_chunks + n_tiles - 1
    fc = FF_CHUNK

    norm_is_prompt = norm_tile(step) < n_prompt_tiles

    def norm_previous():
        return _layer_norm(y_ref[...], g_ref[ln_row:ln_row + 1, :], b_ref[ln_row:ln_row + 1, :])

    @pl.when(step < n_chunks)
    def _():
        w_gate, w_up, w_out = wg_ref[...].astype(BF16), wu_ref[...].astype(BF16), wo_ref[...].astype(BF16)
        wi_bf_ref[step, :, :fc] = w_gate
        wi_bf_ref[step, :, fc:] = w_up
        wo_bf_ref[step] = w_out
        x = xp_ref[...]
        part = _ffn_chunk(x.astype(BF16), w_gate, w_up, w_out)

        @pl.when(step == 0)
        def _():
            acc_ref[...] = part

        @pl.when(step > 0)
        def _():
            acc_ref[...] += part

        @pl.when(step == n_chunks - 1)
        def _():
            y_ref[...] = alpha * x + 0.5 * acc_ref[...]

    def full_tile_step(out_ref):
        normed = norm_previous()
        out_ref[...] = normed
        rows, width = normed.shape
        anchor = jnp.max(normed.reshape(rows // SUBLANES, SUBLANES, width), axis=0)
        anchor = functools.reduce(jnp.maximum, [anchor[:, j * LANES:(j + 1) * LANES]
                                                for j in range(width // LANES)])
        zero = jnp.minimum(jnp.abs(anchor), 0.0)
        x = jnp.where(matmul_tile(step) < n_prompt_tiles, xp_ref[...], xs_ref[...])
        xb = x.astype(BF16)
        for c in range(n_chunks):
            part = _ffn_chunk(xb, wi_bf_ref[c, :, :fc], wi_bf_ref[c, :, fc:], wo_bf_ref[c])
            if c == 0:
                acc_ref[...] = part
            else:
                acc_ref[...] += part
            if c == min(LN_ANCHOR_CHUNK, n_chunks - 1):
                acc_ref[0:SUBLANES, 0:LANES] += zero
        y_ref[...] = alpha * x + 0.5 * acc_ref[...]

    in_full_steps = (step >= n_chunks) & (step < last_step)

    if n_cast:
        @pl.when((step >= n_chunks) & (step < n_chunks + CAST_STEPS))
        def _():
            for src, dst in zip(cast_in, cast_out):
                dst[...] = src[...].astype(dst.dtype)

    pl.when(in_full_steps & norm_is_prompt)(functools.partial(full_tile_step, op_ref))
    pl.when(in_full_steps & jnp.logical_not(norm_is_prompt))(functools.partial(full_tile_step, os_ref))

    @pl.when(step == last_step)
    def _():
        _store_segment(norm_is_prompt, op_ref, os_ref, norm_previous())


def _ffn_ln(xp, xs, wi, wo, half, ln_g, ln_b, ln_row, alpha, cast_weights=()):
    d = xp.shape[1]
    d_ff = wo.shape[1]
    tm, fc = TOKEN_TILE, FF_CHUNK
    assert xp.shape[0] % tm == 0 and xs.shape[0] % tm == 0 and d_ff % fc == 0
    n_p, n_s = xp.shape[0] // tm, xs.shape[0] // tm
    n_chunks = d_ff // fc
    n_tiles = n_p + n_s
    assert n_tiles - 1 >= CAST_STEPS
    matmul_tile, norm_tile = _ffn_tile_maps(n_chunks, n_tiles)
    chunk_of = lambda i: jnp.minimum(i, n_chunks - 1)
    cast_specs = []
    for w in cast_weights:
        rows = w.shape[0] // CAST_STEPS
        assert w.shape[0] % CAST_STEPS == 0 and rows % 16 == 0
        cast_specs.append(pl.BlockSpec((rows, w.shape[1]),
                                       lambda i: (jnp.clip(i - n_chunks, 0, CAST_STEPS - 1), 0)))
    return pl.pallas_call(
        functools.partial(_ffn_ln_kernel, alpha, n_chunks, n_tiles, ln_row, n_p, len(cast_weights)),
        grid=(n_chunks + n_tiles,),
        in_specs=[*_segment_specs(n_p, tm, d, matmul_tile),
                  pl.BlockSpec((None, d, fc), lambda i: (half, 0, chunk_of(i))),
                  pl.BlockSpec((None, d, fc), lambda i: (half, 0, n_chunks + chunk_of(i))),
                  pl.BlockSpec((None, fc, d), lambda i: (half, chunk_of(i), 0)),
                  _const_spec(ln_g.shape), _const_spec(ln_b.shape), *cast_specs],
        out_specs=[*_segment_specs(n_p, tm, d, norm_tile), *cast_specs],
        out_shape=[jax.ShapeDtypeStruct(xp.shape, F32), jax.ShapeDtypeStruct(xs.shape, F32)]
                  + [jax.ShapeDtypeStruct(w.shape, BF16) for w in cast_weights],
        scratch_shapes=[pltpu.VMEM((tm, d), F32), pltpu.VMEM((tm, d), F32),
                        pltpu.VMEM((n_chunks, d, 2 * fc), BF16),
                        pltpu.VMEM((n_chunks, fc, d), BF16)],
        compiler_params=_cparams(1),
        name="ffn_ln",
    )(xp, xs, wi, wi, wo, ln_g, ln_b, *cast_weights)


def _rope_tables(pos, dim, theta_fn):
    half = dim // 2
    inv_freq = theta_fn(half)
    ang = pos.astype(F32)[:, None] * inv_freq[None, :]
    reps = LANES // half
    cos = jnp.tile(jnp.cos(ang), (1, reps))
    sign = jnp.tile(jnp.concatenate([-jnp.ones((half,), F32), jnp.ones((half,), F32)]), LANES // dim)
    sin = jnp.tile(jnp.sin(ang), (1, reps)) * sign[None, :]
    return cos, sin


def _attn_inv_freq(half):
    return 1.0 / (ROPE_THETA ** (jnp.arange(0, 2 * half, 2, dtype=F32) / (2 * half)))


def _ret_inv_freq(half):
    return 1.0 / (RET_THETA ** jnp.linspace(0.0, 1.0, half, dtype=F32))


def _rope_block(x, cos, sin, dim):
    half = dim // 2
    if dim == LANES:
        rot = pltpu.roll(x, half, 1)
    else:
        lane = lax.broadcasted_iota(jnp.int32, x.shape, 1)
        first_half = (lane % dim) < half
        rot = jnp.where(first_half, pltpu.roll(x, LANES - half, 1), pltpu.roll(x, half, 1))
    return x * cos + rot * sin


def _run(stages):
    for _ in stages:
        pass


def _proj_rope_stages(x_ref, w_ref, cosa_ref, sina_ref, cosr_ref, sinr_ref,
                      qa_ref, ka_ref, va_ref, qr_ref, kr_ref, vr_ref, gr_ref, ga_ref, gb_ref,
                      k_tail_ref=None, v_tail_ref=None):
    xb = x_ref[...].astype(BF16)
    cosa, sina = cosa_ref[...], sina_ref[...]
    cosr, sinr = cosr_ref[...], sinr_ref[...]
    off = 0

    y = _dot(xb, w_ref[:, off:off + Q_W])
    for j in range(Q_W // LANES):
        qa_ref[:, j * LANES:(j + 1) * LANES] = _rope_block(
            y[:, j * LANES:(j + 1) * LANES], cosa, sina, HEAD_DIM).astype(qa_ref.dtype)
    off += Q_W
    yield
    y = _dot(xb, w_ref[:, off:off + 2 * KV_W])
    k_rot = _rope_block(y[:, :KV_W], cosa, sina, HEAD_DIM)
    ka_ref[...] = k_rot
    va_ref[...] = y[:, KV_W:]
    if k_tail_ref is not None:
        tail = k_tail_ref.shape[1]
        k_tail_ref[...] = k_rot[-tail:, :].T
        v_tail_ref[...] = y[-tail:, KV_W:].T
    off += 2 * KV_W
    yield
    y = _dot(xb, w_ref[:, off:off + RQ_W])
    for j in range(RQ_W // LANES):
        qr_ref[:, j * LANES:(j + 1) * LANES] = _rope_block(
            y[:, j * LANES:(j + 1) * LANES], cosr, sinr, RET_KEY_DIM).astype(qr_ref.dtype)
    off += RQ_W
    yield
    y = _dot(xb, w_ref[:, off:off + RQ_W])
    for j in range(RQ_W // LANES):
        kr = _rope_block(y[:, j * LANES:(j + 1) * LANES], cosr, sinr, RET_KEY_DIM)
        kr_ref[:, j * LANES:(j + 1) * LANES] = (kr * (RET_KEY_DIM ** -0.5)).astype(kr_ref.dtype)
    off += RQ_W
    yield
    for ref in (vr_ref, gr_ref, ga_ref, gb_ref):
        width = ref.shape[1]
        for c in range(width // 512):
            ref[:, c * 512:(c + 1) * 512] = _dot(
                xb, w_ref[:, off + c * 512:off + (c + 1) * 512]).astype(ref.dtype)
            yield
        off += width


def _proj_rope_kernel(*refs):
    _run(_proj_rope_stages(*refs))


def _dense_tile(tokens):
    return min(DENSE_TILE, tokens // DENSE_MIN_STEPS)


N_PROJ_IN = 6


def _proj_with_rider_kernel(rider_stages, n_rider_in, n_proj_out, *refs):
    proj_in, rest = refs[:N_PROJ_IN], refs[N_PROJ_IN:]
    rider_in, rest = rest[:n_rider_in], rest[n_rider_in:]
    proj_out, rider_out = rest[:n_proj_out], rest[n_proj_out:]
    rider = rider_stages(*rider_in, *rider_out)
    proj = _proj_rope_stages(*proj_in, *proj_out)
    next(rider)
    for _ in range(RIDER_LEAD_GROUPS):
        next(proj)
    _run(rider)
    _run(proj)


def _proj_rope(x1, w_in, tables, mixer_dtype, rider=None, tail=None):
    t, d = x1.shape
    n_in = w_in.shape[1]
    d_model = (n_in - Q_W - 2 * KV_W - 2 * RQ_W - 2 * RV_W) // 2
    tm = _dense_tile(t) if rider is None else rider[0]
    row = lambda i: (i, 0)
    table_blocks = tables[0].shape[0] // tm
    tab = lambda i: (i % table_blocks, 0)
    widths = [(Q_W, mixer_dtype), (KV_W, F32), (KV_W, F32), (RQ_W, mixer_dtype), (RQ_W, mixer_dtype),
              (RV_W, mixer_dtype), (RV_W, mixer_dtype), (d_model, BF16), (d_model, BF16)]
    in_specs = [pl.BlockSpec((tm, d), row), _const_spec(w_in.shape)] + [pl.BlockSpec((tm, LANES), tab)] * 4
    out_specs = [pl.BlockSpec((tm, w), row) for w, _ in widths]
    out_shape = [jax.ShapeDtypeStruct((t, w), dt) for w, dt in widths]
    if tail is not None:
        seq_len, tail_rows = tail
        assert seq_len % tm == 0 and tail_rows <= tm
        tiles_per_seq = seq_len // tm
        out_specs += [pl.BlockSpec((KV_W, tail_rows), lambda i: (i // tiles_per_seq, 0))] * 2
        out_shape += [jax.ShapeDtypeStruct((t // seq_len * KV_W, tail_rows), F32)] * 2
    n_proj_out = len(out_specs)
    args = [x1, w_in, *tables]
    body = _proj_rope_kernel
    if rider is not None:
        r_stages, r_in_specs, r_out_specs, r_out_shape, r_args = rider[1](t // tm)
        body = functools.partial(_proj_with_rider_kernel, r_stages, len(r_args), n_proj_out)
        in_specs, out_specs = in_specs + r_in_specs, out_specs + r_out_specs
        out_shape, args = out_shape + r_out_shape, args + r_args
    outs = pl.pallas_call(
        body,
        grid=(t // tm,),
        in_specs=in_specs,
        out_specs=out_specs,
        out_shape=out_shape,
        compiler_params=_cparams(1),
        name="proj_rope",
    )(*args)
    return outs if rider is None else (outs[:n_proj_out], outs[n_proj_out:])


def _kv_lane_variants(x):
    lane = lax.broadcasted_iota(jnp.int32, x.shape, 1)
    lo = lane < HEAD_DIM
    swapped = pltpu.roll(x, HEAD_DIM, 1)
    zero = jnp.zeros_like(x)
    kv0 = (jnp.where(lo, x, zero).astype(BF16), jnp.where(lo, zero, swapped).astype(BF16))
    kv1 = (jnp.where(lo, swapped, zero).astype(BF16), jnp.where(lo, zero, x).astype(BF16))
    return kv0, kv1


def _attn_prompt_kernel(sinks_ref, q_ref, kc_ref, kp_ref, vc_ref, vp_ref, o_ref):
    n = pl.program_id(1)
    L = WINDOW
    k_full = jnp.concatenate([kp_ref[...], kc_ref[...]], axis=0)
    v_full = jnp.concatenate([vp_ref[...], vc_ref[...]], axis=0)
    k_var = _kv_lane_variants(k_full)
    v_var = _kv_lane_variants(v_full)
    own = (lax.broadcasted_iota(jnp.int32, (L, L), 1)
           <= lax.broadcasted_iota(jnp.int32, (L, L), 0))
    scale = HEAD_DIM ** -0.5

    units = [(i, j, par) for i in range(q_ref.shape[0] // L) for j in range(Q_W // LANES) for par in range(2)]

    def scores(unit):
        i, j, par = unit
        q2 = q_ref[i * L:(i + 1) * L, j * LANES:(j + 1) * LANES] * scale
        return _dot_nt(q2, k_var[(2 * j) // GQA_GROUP][par][i * L:(i + 2) * L])

    s_next = scores(units[0])
    out = None
    for idx, (i, j, par) in enumerate(units):
        s = s_next
        if idx + 1 < len(units):
            s_next = scores(units[idx + 1])
        s_prev = s[:, :L]
        if i == 0:
            s_prev = jnp.where(n > 0, s_prev, NEG_INF)
        sw = jnp.where(own, s[:, L:], s_prev)
        sink = sinks_ref[2 * j + par]
        m = jnp.maximum(jnp.max(sw, axis=-1, keepdims=True), sink)
        e = jnp.exp(sw - m)
        denom = jnp.sum(e, axis=-1, keepdims=True) + jnp.exp(sink - m)
        e_band = jnp.concatenate([jnp.where(own, 0.0, e), jnp.where(own, e, 0.0)], axis=1)
        r = _dot(e_band.astype(BF16), v_var[(2 * j) // GQA_GROUP][par][i * L:(i + 2) * L]) / denom
        if par == 0:
            out = r
        else:
            o_ref[i * L:(i + 1) * L, j * LANES:(j + 1) * LANES] = (out + r).astype(o_ref.dtype)


def _attn_prompt(qa, ka, va, sinks, batch, seq):
    tq = ATTN_TILE
    nq = seq // tq
    per = tq // WINDOW
    cur = lambda b, n: (b * nq + n, 0)
    prev = lambda b, n: ((b * nq + n) * per - jnp.minimum(n, 1), 0)
    return pl.pallas_call(
        _attn_prompt_kernel,
        grid=(batch, nq),
        in_specs=[pl.BlockSpec(memory_space=pltpu.SMEM),
                  pl.BlockSpec((tq, Q_W), cur),
                  pl.BlockSpec((tq, KV_W), cur), pl.BlockSpec((WINDOW, KV_W), prev),
                  pl.BlockSpec((tq, KV_W), cur), pl.BlockSpec((WINDOW, KV_W), prev)],
        out_specs=pl.BlockSpec((tq, Q_W), cur),
        out_shape=jax.ShapeDtypeStruct(qa.shape, BF16),
        compiler_params=_cparams(2),
        name="attn_prompt",
    )(sinks, qa, ka, ka, va, va)


def _attn_sample_stages(sink_rows_ref, q_ref, kn_ref, vn_ref, ck_ref, cv_ref,
                        o_ref, nk_ref, nv_ref):
    bt, t_new, _ = q_ref.shape
    assert t_new == SUBLANES
    w = ck_ref.shape[3]
    n_rows = N_Q_HEADS * t_new
    lane_lo = lax.broadcasted_iota(jnp.int32, (bt * t_new, LANES), 1) < HEAD_DIM
    t_c = lax.broadcasted_iota(jnp.int32, (n_rows, w), 0) % t_new
    mask_c = lax.broadcasted_iota(jnp.int32, (n_rows, w), 1) > t_c + (w - WINDOW)
    t_n = lax.broadcasted_iota(jnp.int32, (n_rows, t_new), 0) % t_new
    mask_n = lax.broadcasted_iota(jnp.int32, (n_rows, t_new), 1) <= t_n
    sink = sink_rows_ref[...]
    scale = HEAD_DIM ** -0.5
    kn, vn = kn_ref[...], vn_ref[...]
    kc_t = ck_ref[...].reshape(bt, KV_W, w)
    vc_t = cv_ref[...].reshape(bt, KV_W, w)
    q = q_ref[...].reshape(bt * t_new, Q_W)
    pieces = []
    for h in range(N_Q_HEADS):
        q2 = q[:, (h // 2) * LANES:(h // 2 + 1) * LANES]
        src_lo = h % 2 == 0
        dst_lo = h // GQA_GROUP == 0
        if src_lo != dst_lo:
            q2 = pltpu.roll(q2, HEAD_DIM, 1)
        pieces.append(jnp.where(lane_lo == dst_lo, q2, 0.0).reshape(bt, t_new, LANES))
    q_rows = (jnp.concatenate(pieces, axis=1) * scale).astype(BF16)
    s_c = jnp.einsum("bqd,bdk->bqk", q_rows, kc_t.astype(BF16), preferred_element_type=F32)
    s_n = jnp.einsum("bqd,bkd->bqk", q_rows, kn.astype(BF16), preferred_element_type=F32)
    yield
    s_c = jnp.where(mask_c, s_c, NEG_INF)
    s_n = jnp.where(mask_n, s_n, NEG_INF)
    m = jnp.maximum(jnp.maximum(jnp.max(s_c, axis=-1, keepdims=True), jnp.max(s_n, axis=-1, keepdims=True)), sink)
    e_c, e_n = jnp.exp(s_c - m), jnp.exp(s_n - m)
    denom = jnp.sum(e_c, axis=-1, keepdims=True) + jnp.sum(e_n, axis=-1, keepdims=True) + jnp.exp(sink - m)
    pv = (jnp.einsum("bqk,bdk->bqd", e_c.astype(BF16), vc_t.astype(BF16), preferred_element_type=F32)
          + jnp.einsum("bqk,bkd->bqd", e_n.astype(BF16), vn.astype(BF16), preferred_element_type=F32)) / denom
    for j in range(Q_W // LANES):
        out = None
        for par in range(2):
            h = 2 * j + par
            dst_lo = h // GQA_GROUP == 0
            piece = pv[:, h * t_new:(h + 1) * t_new, :].reshape(bt * t_new, LANES)
            piece = jnp.where(lane_lo == dst_lo, piece, 0.0)
            if dst_lo != (par == 0):
                piece = pltpu.roll(piece, HEAD_DIM, 1)
            out = piece if out is None else out + piece
        o_ref[:, :, j * LANES:(j + 1) * LANES] = out.reshape(bt, t_new, LANES).astype(o_ref.dtype)
    keep = lax.broadcasted_iota(jnp.int32, (bt * KV_W, w), 1) < w - t_new
    for old_t, new, out_ref in ((kc_t, kn, nk_ref), (vc_t, vn, nv_ref)):
        shifted = pltpu.roll(old_t.reshape(bt * KV_W, w), w - t_new, 1)
        placed = jnp.concatenate([jnp.zeros((bt, w - t_new, KV_W), F32), new], axis=1)
        placed_t = jnp.swapaxes(placed, 1, 2).reshape(bt * KV_W, w)
        out_ref[...] = jnp.where(keep, shifted, placed_t).reshape(out_ref.shape)


def _ret_log_gamma(h):
    return math.log(1.0 - 2.0 ** (-5.0 - h))


def _ret_decays(chunk, h):
    lg = _ret_log_gamma(h)
    i = lax.broadcasted_iota(jnp.int32, (chunk, chunk), 0)
    j = lax.broadcasted_iota(jnp.int32, (chunk, chunk), 1)
    rel_mask = jnp.where(i >= j, jnp.exp((j + 1).astype(F32) * -lg), 0.0)
    col = lax.broadcasted_iota(jnp.int32, (chunk, 1), 0).astype(F32)
    gn_eps = GN_EPS * jnp.exp((col + 1.0) * (-2.0 * lg))
    k_decay = jnp.exp((chunk - 1.0 - col) * lg)
    return rel_mask, gn_eps, k_decay, math.exp(chunk * lg)


def _ret_chunk_scores(q, k, v, decays):
    heads = range(N_RET_HEADS)
    batched = q.ndim == 3
    mm = (lambda a, b: jnp.einsum("bmk,bkn->bmn", a, b, preferred_element_type=F32)) if batched else _dot
    mm_nt = (lambda a, b: jnp.einsum("bmk,bnk->bmn", a, b, preferred_element_type=F32)) if batched else _dot_nt
    qh = [q[..., h * RET_KEY_DIM:(h + 1) * RET_KEY_DIM] for h in heads]
    kh = [k[..., h * RET_KEY_DIM:(h + 1) * RET_KEY_DIM] for h in heads]
    vh = [v[..., h * RET_VAL_DIM:(h + 1) * RET_VAL_DIM] for h in heads]
    qb = [x.astype(BF16) for x in qh]
    vb = [x.astype(BF16) for x in vh]
    inner = [mm_nt(qb[h], kh[h].astype(BF16)) * decays[h][0] for h in heads]
    return batched, mm, qh, kh, vh, qb, vb, inner


def _ret_chunk_finish(scores, g, states, decays):
    batched, mm, qh, kh, vh, qb, vb, inner = scores
    heads = range(N_RET_HEADS)
    if batched:
        lhs = [jnp.concatenate([qh[h], inner[h]], axis=-1).astype(BF16) for h in heads]
        rhs = [jnp.concatenate([states[h], vh[h]], axis=-2).astype(BF16) for h in heads]
    else:
        lhs = [jnp.concatenate([qb[h], inner[h].astype(BF16)], axis=-1) for h in heads]
        rhs = [jnp.concatenate([states[h].astype(BF16), vb[h]], axis=-2) for h in heads]
    p = [mm(lhs[h], rhs[h]) for h in heads]
    kd_t = [jnp.swapaxes(kh[h].astype(F32) * decays[h][2], -1, -2).astype(BF16) for h in heads]
    new_states = [states[h] * decays[h][3] + mm(kd_t[h], vb[h]) for h in heads]
    outs = []
    for h in heads:
        mu = jnp.mean(p[h], axis=-1, keepdims=True)
        pc = p[h] - mu
        var = jnp.mean(pc * pc, axis=-1, keepdims=True)
        gate = g[..., h * RET_VAL_DIM:(h + 1) * RET_VAL_DIM].astype(F32)
        outs.append(pc * lax.rsqrt(var + decays[h][1]) * jax.nn.silu(gate))
    return jnp.concatenate(outs, axis=-1), new_states


def _ret_chunk_all_heads(q, k, v, g, states, decays):
    return _ret_chunk_finish(_ret_chunk_scores(q, k, v, decays), g, states, decays)


def _ret_prompt_kernel(q_ref, k_ref, v_ref, g_ref, o_ref, st_ref, state_ref):
    n = pl.program_id(1)
    C = RET_CHUNK

    @pl.when(n == 0)
    def _():
        state_ref[...] = jnp.zeros_like(state_ref)

    decays = [_ret_decays(C, h) for h in range(N_RET_HEADS)]
    states = [state_ref[h] for h in range(N_RET_HEADS)]
    for c in range(q_ref.shape[0] // C):
        rows = slice(c * C, (c + 1) * C)
        out, states = _ret_chunk_all_heads(q_ref[rows, :], k_ref[rows, :], v_ref[rows, :], g_ref[rows, :],
                                           states, decays)
        o_ref[rows, :] = out.astype(o_ref.dtype)
    for h in range(N_RET_HEADS):
        state_ref[h] = states[h]

    @pl.when(n == pl.num_programs(1) - 1)
    def _():
        st_ref[0] = state_ref[...]


def _ret_prompt(qr, kr, vr, gr, batch, seq):
    ts = RET_TILE
    ns = seq // ts
    row = lambda b, n: (b * ns + n, 0)
    return pl.pallas_call(
        _ret_prompt_kernel,
        grid=(batch, ns),
        in_specs=[pl.BlockSpec((ts, RQ_W), row), pl.BlockSpec((ts, RQ_W), row),
                  pl.BlockSpec((ts, RV_W), row), pl.BlockSpec((ts, RV_W), row)],
        out_specs=[pl.BlockSpec((ts, RV_W), row),
                   pl.BlockSpec((1, N_RET_HEADS, RET_KEY_DIM, RET_VAL_DIM), lambda b, n: (b, 0, 0, 0))],
        out_shape=[jax.ShapeDtypeStruct((batch * seq, RV_W), BF16),
                   jax.ShapeDtypeStruct((batch, N_RET_HEADS, RET_KEY_DIM, RET_VAL_DIM), F32)],
        scratch_shapes=[pltpu.VMEM((N_RET_HEADS, RET_KEY_DIM, RET_VAL_DIM), F32)],
        compiler_params=_cparams(2),
        name="ret_prompt",
    )(qr, kr, vr, gr)


def _ret_sample_stages(q_ref, k_ref, v_ref, g_ref, st_ref, o_ref, nst_ref):
    t_new = q_ref.shape[1]
    decays = [_ret_decays(t_new, h) for h in range(N_RET_HEADS)]
    scores = _ret_chunk_scores(q_ref[...], k_ref[...], v_ref[...], decays)
    yield
    states = [st_ref[:, h] for h in range(N_RET_HEADS)]
    out, states = _ret_chunk_finish(scores, g_ref[...], states, decays)
    o_ref[...] = out.astype(o_ref.dtype)
    for h in range(N_RET_HEADS):
        nst_ref[:, h] = states[h]


def _sample_mixers_stages(sink_rows_ref, qa_ref, kn_ref, vn_ref, ck_ref, cv_ref,
                          qr_ref, kr_ref, vr_ref, gr_ref, st_ref,
                          oa_ref, nk_ref, nv_ref, or_ref, nst_ref):
    attn = _attn_sample_stages(sink_rows_ref, qa_ref, kn_ref, vn_ref, ck_ref, cv_ref, oa_ref, nk_ref, nv_ref)
    ret = _ret_sample_stages(qr_ref, kr_ref, vr_ref, gr_ref, st_ref, or_ref, nst_ref)
    next(attn)
    next(ret)
    yield
    _run(attn)
    _run(ret)


def _sample_mixers_parts(qa, ka, va, sinks, cache_k, cache_v, qr, kr, vr, gr, state):
    db, t_new, _ = qa.shape
    w = cache_k.shape[3]
    cache = (N_KV_HEADS, HEAD_DIM, w)
    sink_rows = jnp.repeat(sinks.astype(F32), t_new).reshape(N_Q_HEADS * t_new, 1)

    def parts(n_steps):
        assert db % n_steps == 0
        bt = db // n_steps
        blk = lambda *s: pl.BlockSpec((bt,) + s, lambda i: (i,) + (0,) * len(s))
        st = blk(N_RET_HEADS, RET_KEY_DIM, RET_VAL_DIM)
        in_specs = [_const_spec(sink_rows.shape), blk(t_new, Q_W), blk(t_new, KV_W), blk(t_new, KV_W),
                    blk(*cache), blk(*cache),
                    blk(t_new, RQ_W), blk(t_new, RQ_W), blk(t_new, RV_W), blk(t_new, RV_W), st]
        out_specs = [blk(t_new, Q_W), blk(*cache), blk(*cache), blk(t_new, RV_W), st]
        out_shape = [jax.ShapeDtypeStruct((db, t_new, Q_W), F32),
                     jax.ShapeDtypeStruct((db,) + cache, F32),
                     jax.ShapeDtypeStruct((db,) + cache, F32),
                     jax.ShapeDtypeStruct((db, t_new, RV_W), F32),
                     jax.ShapeDtypeStruct(state.shape, state.dtype)]
        args = [sink_rows, qa, ka, va, cache_k, cache_v, qr, kr, vr, gr, state]
        return _sample_mixers_stages, in_specs, out_specs, out_shape, args

    return parts


def _merge_ln_kernel(alpha, ln_row, x_ref, oa_ref, orr_ref, ga_ref, gb_ref, wba_ref, wbr_ref, wo_ref,
                     g_ref, b_ref, o_ref):
    n_blocks = max(1, x_ref.shape[0] // MERGE_ROWS)
    rb = x_ref.shape[0] // n_blocks
    blocks = [slice(r * rb, (r + 1) * rb) for r in range(n_blocks)]
    ya = [_dot(oa_ref[rows, :].astype(BF16), wba_ref[...]) for rows in blocks]
    yr = [_dot(orr_ref[rows, :].astype(BF16), wbr_ref[...]) for rows in blocks]
    merged = [(jax.nn.sigmoid(ga_ref[rows, :].astype(F32)) * ya[r]
               + jax.nn.sigmoid(gb_ref[rows, :].astype(F32)) * yr[r]).astype(BF16)
              for r, rows in enumerate(blocks)]
    z = [_dot(merged[r], wo_ref[...]) for r in range(n_blocks)]
    for r, rows in enumerate(blocks):
        o_ref[rows, :] = _layer_norm(alpha * x_ref[rows, :] + z[r],
                                     g_ref[ln_row:ln_row + 1, :], b_ref[ln_row:ln_row + 1, :])


def _merge_ln(x1, oa, orr, ga, gb, wba, wbr, wo, ln_g, ln_b, ln_row, alpha):
    t, d = x1.shape
    tm = _dense_tile(t)
    row = lambda i: (i, 0)
    return pl.pallas_call(
        functools.partial(_merge_ln_kernel, alpha, ln_row),
        grid=(t // tm,),
        in_specs=[pl.BlockSpec((tm, d), row), pl.BlockSpec((tm, Q_W), row), pl.BlockSpec((tm, RV_W), row),
                  pl.BlockSpec((tm, d), row), pl.BlockSpec((tm, d), row),
                  _const_spec(wba.shape), _const_spec(wbr.shape), _const_spec(wo.shape),
                  _const_spec(ln_g.shape), _const_spec(ln_b.shape)],
        out_specs=pl.BlockSpec((tm, d), row),
        out_shape=jax.ShapeDtypeStruct((t, d), F32),
        compiler_params=_cparams(1),
        name="merge_ln",
    )(x1, oa, orr, ga, gb, wba, wbr, wo, ln_g, ln_b)


def _rope_tables_for(pos):
    tables = _rope_tables(pos, HEAD_DIM, _attn_inv_freq) + _rope_tables(pos, RET_KEY_DIM, _ret_inv_freq)
    period = pos.shape[0]
    if period < DENSE_TILE:
        tables = tuple(jnp.tile(tb, (DENSE_TILE // period, 1)) for tb in tables)
    return tables


def kernel(x_prompt, x_sample, cache_k, cache_v, state_ret, ln_g, ln_b, ffn_wi, ffn_wo, w_in,
           attn_sinks, w_br_attn, w_br_ret, w_o):
    B, S, D = x_prompt.shape
    DB, T, _ = x_sample.shape
    depth = ln_g.shape[0]
    alpha = (2.0 * depth) ** 0.25
    W = cache_k.shape[2]
    tables_p = _rope_tables_for(jnp.arange(S, dtype=jnp.int32))
    tables_s = _rope_tables_for(PAST_LEN + jnp.arange(T, dtype=jnp.int32))

    y_p = x_prompt.reshape(B * S, D)
    y_s = x_sample.reshape(DB * T, D)
    outs = [[] for _ in range(6)]
    for l in range(depth):
        wi, wo = ffn_wi[l], ffn_wo[l]
        sinks = attn_sinks[l].astype(F32)
        g_l, b_l = ln_g[l], ln_b[l]

        x1_p, x1_s, w_in_l, wba, wbr, w_o_l = _ffn_ln(
            y_p, y_s, wi, wo, 0, g_l, b_l, 0, alpha,
            cast_weights=(w_in[l], w_br_attn[l], w_br_ret[l], w_o[l]))
        qa_s, ka_s, va_s, qr_s, kr_s, vr_s, gr_s, ga_s, gb_s = _proj_rope(
            x1_s, w_in_l, tables_s, F32)

        r3 = lambda a: a.reshape(DB, T, a.shape[-1])
        to_lane_major = lambda c: jnp.transpose(c, (0, 2, 3, 1))
        sample_mixers = _sample_mixers_parts(
            r3(qa_s), r3(ka_s), r3(va_s), sinks, to_lane_major(cache_k[l]), to_lane_major(cache_v[l]),
            r3(qr_s), r3(kr_s), r3(vr_s), r3(gr_s), state_ret[l])
        w = min(WINDOW, S)
        (qa_p, ka_p, va_p, qr_p, kr_p, vr_p, gr_p, ga_p, gb_p, nk_p, nv_p), (oa_s, nk_s, nv_s, or_s, st_s) = \
            _proj_rope(x1_p, w_in_l, tables_p, BF16, rider=(RIDER_PROJ_TILE, sample_mixers), tail=(S, w))
        oa_s, or_s = oa_s.reshape(DB * T, Q_W), or_s.reshape(DB * T, RV_W)
        from_lane_major = lambda c: jnp.transpose(c, (0, 3, 1, 2))
        new_caches = [from_lane_major(nk_p.reshape(B, N_KV_HEADS, HEAD_DIM, w)),
                      from_lane_major(nv_p.reshape(B, N_KV_HEADS, HEAD_DIM, w)),
                      from_lane_major(nk_s), from_lane_major(nv_s)]

        oa_p = _attn_prompt(qa_p, ka_p, va_p, sinks, B, S)
        or_p, st_p = _ret_prompt(qr_p, kr_p, vr_p, gr_p, B, S)

        x2_p = _merge_ln(x1_p, oa_p, or_p, ga_p, gb_p, wba, wbr, w_o_l, g_l, b_l, 1, alpha)
        x2_s = _merge_ln(x1_s, oa_s, or_s, ga_s, gb_s, wba, wbr, w_o_l, g_l, b_l, 1, alpha)
        y_p, y_s = _ffn_ln(x2_p, x2_s, wi, wo, 1, g_l, b_l, 2, alpha)
        for lst, val in zip(outs, (new_caches[0], new_caches[1], st_p, new_caches[2], new_caches[3], st_s)):
            lst.append(val)
    return (y_p.reshape(B, S, D), y_s.reshape(DB, T, D)) + tuple(jnp.stack(o) for o in outs)
```

```python
import functools
import math

import jax
import jax.numpy as jnp
from jax import lax
from jax.experimental import pallas as pl
from jax.experimental.pallas import tpu as pltpu

F32 = jnp.float32
BF16 = jnp.bfloat16

PAST_LEN = 8192
N_Q_HEADS = 8
N_KV_HEADS = 2
HEAD_DIM = 64
GQA_GROUP = N_Q_HEADS // N_KV_HEADS
WINDOW = 128
ROPE_THETA = 10000.0
N_RET_HEADS = 4
RET_KEY_DIM = 128
RET_VAL_DIM = 256
RET_CHUNK = 128
RET_THETA = 10000.0
LN_EPS = 1e-5
GN_EPS = 1e-6
NEG_INF = -1e30

Q_W = N_Q_HEADS * HEAD_DIM
KV_W = N_KV_HEADS * HEAD_DIM
RQ_W = N_RET_HEADS * RET_KEY_DIM
RV_W = N_RET_HEADS * RET_VAL_DIM

LANES = 128
SUBLANES = 8
VMEM_LIMIT_BYTES = 56 * 1024 * 1024

TOKEN_TILE = 512
DENSE_TILE = 1024
DENSE_MIN_STEPS = 4
FF_CHUNK = 256
ATTN_TILE = 2048
RET_TILE = 1024
LN_ANCHOR_CHUNK = 8
MERGE_ROWS = 256
RIDER_PROJ_TILE = 512
RIDER_LEAD_GROUPS = 2
CAST_STEPS = 16


def _dot(a, b):
    return jnp.dot(a, b, preferred_element_type=F32)


def _dot_nt(a, b):
    return lax.dot_general(a, b, (((1,), (1,)), ((), ())), preferred_element_type=F32)


def _layer_norm(y, g, b):
    mu = jnp.mean(y, axis=-1, keepdims=True)
    yc = y - mu
    var = jnp.mean(yc * yc, axis=-1, keepdims=True)
    return yc * lax.rsqrt(var + LN_EPS) * g + b


def _cparams(n_axes, semantics=None):
    return pltpu.CompilerParams(
        dimension_semantics=semantics or ("arbitrary",) * n_axes,
        vmem_limit_bytes=VMEM_LIMIT_BYTES)


def _const_spec(shape):
    nd = len(shape)
    return pl.BlockSpec(shape, lambda *_: (0,) * nd, pipeline_mode=pl.Buffered(1))


def _segment_specs(n_prompt_tiles, tm, width, tile_of):
    prompt = pl.BlockSpec((tm, width), lambda i: (jnp.minimum(tile_of(i), n_prompt_tiles - 1), 0))
    sample = pl.BlockSpec((tm, width), lambda i: (jnp.maximum(tile_of(i) - n_prompt_tiles, 0), 0))
    return prompt, sample


def _store_segment(is_prompt, prompt_ref, sample_ref, value):
    @pl.when(is_prompt)
    def _():
        prompt_ref[...] = value.astype(prompt_ref.dtype)

    @pl.when(jnp.logical_not(is_prompt))
    def _():
        sample_ref[...] = value.astype(sample_ref.dtype)


def _ffn_chunk(xb, w_gate, w_up, w_out):
    act = (jax.nn.silu(_dot(xb, w_gate)) * _dot(xb, w_up)).astype(BF16)
    return _dot(act, w_out)


def _ffn_tile_maps(n_chunks, n_tiles):
    matmul_tile = lambda i: jnp.clip(i - (n_chunks - 1), 0, n_tiles - 1)
    norm_tile = lambda i: jnp.clip(i - n_chunks, 0, n_tiles - 1)
    return matmul_tile, norm_tile


def _ffn_ln_kernel(alpha, n_chunks, n_tiles, ln_row, n_prompt_tiles, n_cast, *refs):
    xp_ref, xs_ref, wg_ref, wu_ref, wo_ref, g_ref, b_ref = refs[:7]
    cast_in = refs[7:7 + n_cast]
    op_ref, os_ref = refs[7 + n_cast:9 + n_cast]
    cast_out = refs[9 + n_cast:9 + 2 * n_cast]
    acc_ref, y_ref, wi_bf_ref, wo_bf_ref = refs[9 + 2 * n_cast:]
    step = pl.program_id(0)
    matmul_tile, norm_tile = _ffn_tile_maps(n_chunks, n_tiles)
    last_step = n_chunks + n_tiles - 1
    fc = FF_CHUNK

    norm_is_prompt = norm_tile(step) < n_prompt_tiles

    def norm_previous():
        return _layer_norm(y_ref[...], g_ref[ln_row:ln_row + 1, :], b_ref[ln_row:ln_row + 1, :])

    @pl.when(step < n_chunks)
    def _():
        w_gate, w_up, w_out = wg_ref[...].astype(BF16), wu_ref[...].astype(BF16), wo_ref[...].astype(BF16)
        wi_bf_ref[step, :, :fc] = w_gate
        wi_bf_ref[step, :, fc:] = w_up
        wo_bf_ref[step] = w_out
        x = xp_ref[...]
        part = _ffn_chunk(x.astype(BF16), w_gate, w_up, w_out)

        @pl.when(step == 0)
        def _():
            acc_ref[...] = part

        @pl.when(step > 0)
        def _():
            acc_ref[...] += part

        @pl.when(step == n_chunks - 1)
        def _():
            y_ref[...] = alpha * x + 0.5 * acc_ref[...]

    def full_tile_step(out_ref):
        normed = norm_previous()
        out_ref[...] = normed
        rows, width = normed.shape
        anchor = jnp.max(normed.reshape(rows // SUBLANES, SUBLANES, width), axis=0)
        anchor = functools.reduce(jnp.maximum, [anchor[:, j * LANES:(j + 1) * LANES]
                                                for j in range(width // LANES)])
        zero = jnp.minimum(jnp.abs(anchor), 0.0)
        x = jnp.where(matmul_tile(step) < n_prompt_tiles, xp_ref[...], xs_ref[...])
        xb = x.astype(BF16)
        for c in range(n_chunks):
            part = _ffn_chunk(xb, wi_bf_ref[c, :, :fc], wi_bf_ref[c, :, fc:], wo_bf_ref[c])
            if c == 0:
                acc_ref[...] = part
            else:
                acc_ref[...] += part
            if c == min(LN_ANCHOR_CHUNK, n_chunks - 1):
                acc_ref[0:SUBLANES, 0:LANES] += zero
        y_ref[...] = alpha * x + 0.5 * acc_ref[...]

    in_full_steps = (step >= n_chunks) & (step < last_step)

    if n_cast:
        @pl.when((step >= n_chunks) & (step < n_chunks + CAST_STEPS))
        def _():
            for src, dst in zip(cast_in, cast_out):
                dst[...] = src[...].astype(dst.dtype)

    pl.when(in_full_steps & norm_is_prompt)(functools.partial(full_tile_step, op_ref))
    pl.when(in_full_steps & jnp.logical_not(norm_is_prompt))(functools.partial(full_tile_step, os_ref))

    @pl.when(step == last_step)
    def _():
        _store_segment(norm_is_prompt, op_ref, os_ref, norm_previous())


def _ffn_ln(xp, xs, wi, wo, half, ln_g, ln_b, ln_row, alpha, cast_weights=()):
    d = xp.shape[1]
    d_ff = wo.shape[1]
    tm, fc = TOKEN_TILE, FF_CHUNK
    assert xp.shape[0] % tm == 0 and xs.shape[0] % tm == 0 and d_ff % fc == 0
    n_p, n_s = xp.shape[0] // tm, xs.shape[0] // tm
    n_chunks = d_ff // fc
    n_tiles = n_p + n_s
    assert n_tiles - 1 >= CAST_STEPS
    matmul_tile, norm_tile = _ffn_tile_maps(n_chunks, n_tiles)
    chunk_of = lambda i: jnp.minimum(i, n_chunks - 1)
    cast_specs = []
    for w in cast_weights:
        rows = w.shape[0] // CAST_STEPS
        assert w.shape[0] % CAST_STEPS == 0 and rows % 16 == 0
        cast_specs.append(pl.BlockSpec((rows, w.shape[1]),
                                       lambda i: (jnp.clip(i - n_chunks, 0, CAST_STEPS - 1), 0)))
    return pl.pallas_call(
        functools.partial(_ffn_ln_kernel, alpha, n_chunks, n_tiles, ln_row, n_p, len(cast_weights)),
        grid=(n_chunks + n_tiles,),
        in_specs=[*_segment_specs(n_p, tm, d, matmul_tile),
                  pl.BlockSpec((None, d, fc), lambda i: (half, 0, chunk_of(i))),
                  pl.BlockSpec((None, d, fc), lambda i: (half, 0, n_chunks + chunk_of(i))),
                  pl.BlockSpec((None, fc, d), lambda i: (half, chunk_of(i), 0)),
                  _const_spec(ln_g.shape), _const_spec(ln_b.shape), *cast_specs],
        out_specs=[*_segment_specs(n_p, tm, d, norm_tile), *cast_specs],
        out_shape=[jax.ShapeDtypeStruct(xp.shape, F32), jax.ShapeDtypeStruct(xs.shape, F32)]
                  + [jax.ShapeDtypeStruct(w.shape, BF16) for w in cast_weights],
        scratch_shapes=[pltpu.VMEM((tm, d), F32), pltpu.VMEM((tm, d), F32),
                        pltpu.VMEM((n_chunks, d, 2 * fc), BF16),
                        pltpu.VMEM((n_chunks, fc, d), BF16)],
        compiler_params=_cparams(1),
        name="ffn_ln",
    )(xp, xs, wi, wi, wo, ln_g, ln_b, *cast_weights)


def _rope_tables(pos, dim, theta_fn):
    half = dim // 2
    lane = jnp.arange(LANES)
    inv_freq = theta_fn(half)[lane % half]
    sign = jnp.where((lane % dim) < half, -1.0, 1.0).astype(F32)
    ang = pos.astype(F32)[:, None] * inv_freq[None, :]
    return jnp.cos(ang), jnp.sin(ang) * sign[None, :]


def _attn_inv_freq(half):
    return 1.0 / (ROPE_THETA ** (jnp.arange(0, 2 * half, 2, dtype=F32) / (2 * half)))


def _ret_inv_freq(half):
    return 1.0 / (RET_THETA ** jnp.linspace(0.0, 1.0, half, dtype=F32))


def _rope_block(x, cos, sin, dim):
    half = dim // 2
    if dim == LANES:
        rot = pltpu.roll(x, half, 1)
    else:
        lane = lax.broadcasted_iota(jnp.int32, x.shape, 1)
        first_half = (lane % dim) < half
        rot = jnp.where(first_half, pltpu.roll(x, LANES - half, 1), pltpu.roll(x, half, 1))
    return x * cos + rot * sin


def _run(stages):
    for _ in stages:
        pass


def _proj_rope_stages(x_ref, w_ref, cosa_ref, sina_ref, cosr_ref, sinr_ref,
                      qa_ref, ka_ref, va_ref, qr_ref, kr_ref, vr_ref, gr_ref, ga_ref, gb_ref,
                      k_tail_ref=None, v_tail_ref=None):
    xb = x_ref[...].astype(BF16)
    cosa, sina = cosa_ref[...], sina_ref[...]
    cosr, sinr = cosr_ref[...], sinr_ref[...]
    off = 0

    y = _dot(xb, w_ref[:, off:off + Q_W])
    for j in range(Q_W // LANES):
        qa_ref[:, j * LANES:(j + 1) * LANES] = _rope_block(
            y[:, j * LANES:(j + 1) * LANES], cosa, sina, HEAD_DIM).astype(qa_ref.dtype)
    off += Q_W
    yield
    y = _dot(xb, w_ref[:, off:off + 2 * KV_W])
    k_rot = _rope_block(y[:, :KV_W], cosa, sina, HEAD_DIM)
    ka_ref[...] = k_rot
    va_ref[...] = y[:, KV_W:]
    if k_tail_ref is not None:
        tail = k_tail_ref.shape[1]
        k_tail_ref[...] = k_rot[-tail:, :].T
        v_tail_ref[...] = y[-tail:, KV_W:].T
    off += 2 * KV_W
    yield
    y = _dot(xb, w_ref[:, off:off + RQ_W])
    for j in range(RQ_W // LANES):
        qr_ref[:, j * LANES:(j + 1) * LANES] = _rope_block(
            y[:, j * LANES:(j + 1) * LANES], cosr, sinr, RET_KEY_DIM).astype(qr_ref.dtype)
    off += RQ_W
    yield
    y = _dot(xb, w_ref[:, off:off + RQ_W])
    for j in range(RQ_W // LANES):
        kr = _rope_block(y[:, j * LANES:(j + 1) * LANES], cosr, sinr, RET_KEY_DIM)
        kr_ref[:, j * LANES:(j + 1) * LANES] = (kr * (RET_KEY_DIM ** -0.5)).astype(kr_ref.dtype)
    off += RQ_W
    yield
    for ref in (vr_ref, gr_ref, ga_ref, gb_ref):
        width = ref.shape[1]
        for c in range(width // 512):
            ref[:, c * 512:(c + 1) * 512] = _dot(
                xb, w_ref[:, off + c * 512:off + (c + 1) * 512]).astype(ref.dtype)
            yield
        off += width


def _proj_rope_kernel(*refs):
    _run(_proj_rope_stages(*refs))


def _dense_tile(tokens):
    return min(DENSE_TILE, tokens // DENSE_MIN_STEPS)


N_PROJ_IN = 6


def _proj_with_rider_kernel(rider_stages, n_rider_in, n_proj_out, *refs):
    proj_in, rest = refs[:N_PROJ_IN], refs[N_PROJ_IN:]
    rider_in, rest = rest[:n_rider_in], rest[n_rider_in:]
    proj_out, rider_out = rest[:n_proj_out], rest[n_proj_out:]
    rider = rider_stages(*rider_in, *rider_out)
    proj = _proj_rope_stages(*proj_in, *proj_out)
    next(rider)
    for _ in range(RIDER_LEAD_GROUPS):
        next(proj)
    _run(rider)
    _run(proj)


def _proj_rope(x1, w_in, tables, mixer_dtype, rider=None, tail=None):
    t, d = x1.shape
    n_in = w_in.shape[1]
    d_model = (n_in - Q_W - 2 * KV_W - 2 * RQ_W - 2 * RV_W) // 2
    tm = _dense_tile(t) if rider is None else rider[0]
    row = lambda i: (i, 0)
    table_blocks = tables[0].shape[0] // tm
    tab = lambda i: (i % table_blocks, 0)
    widths = [(Q_W, mixer_dtype), (KV_W, F32), (KV_W, F32), (RQ_W, mixer_dtype), (RQ_W, mixer_dtype),
              (RV_W, mixer_dtype), (RV_W, mixer_dtype), (d_model, BF16), (d_model, BF16)]
    in_specs = [pl.BlockSpec((tm, d), row), _const_spec(w_in.shape)] + [pl.BlockSpec((tm, LANES), tab)] * 4
    out_specs = [pl.BlockSpec((tm, w), row) for w, _ in widths]
    out_shape = [jax.ShapeDtypeStruct((t, w), dt) for w, dt in widths]
    if tail is not None:
        seq_len, tail_rows = tail
        assert seq_len % tm == 0 and tail_rows <= tm
        tiles_per_seq = seq_len // tm
        out_specs += [pl.BlockSpec((KV_W, tail_rows), lambda i: (i // tiles_per_seq, 0))] * 2
        out_shape += [jax.ShapeDtypeStruct((t // seq_len * KV_W, tail_rows), F32)] * 2
    n_proj_out = len(out_specs)
    args = [x1, w_in, *tables]
    body = _proj_rope_kernel
    if rider is not None:
        r_stages, r_in_specs, r_out_specs, r_out_shape, r_args = rider[1](t // tm)
        body = functools.partial(_proj_with_rider_kernel, r_stages, len(r_args), n_proj_out)
        in_specs, out_specs = in_specs + r_in_specs, out_specs + r_out_specs
        out_shape, args = out_shape + r_out_shape, args + r_args
    outs = pl.pallas_call(
        body,
        grid=(t // tm,),
        in_specs=in_specs,
        out_specs=out_specs,
        out_shape=out_shape,
        compiler_params=_cparams(1),
        name="proj_rope",
    )(*args)
    return outs if rider is None else (outs[:n_proj_out], outs[n_proj_out:])


def _kv_lane_variants(x):
    lane = lax.broadcasted_iota(jnp.int32, x.shape, 1)
    lo = lane < HEAD_DIM
    swapped = pltpu.roll(x, HEAD_DIM, 1)
    zero = jnp.zeros_like(x)
    kv0 = (jnp.where(lo, x, zero).astype(BF16), jnp.where(lo, zero, swapped).astype(BF16))
    kv1 = (jnp.where(lo, swapped, zero).astype(BF16), jnp.where(lo, zero, x).astype(BF16))
    return kv0, kv1


def _attn_prompt_kernel(sinks_ref, q_ref, kc_ref, kp_ref, vc_ref, vp_ref, o_ref):
    n = pl.program_id(1)
    L = WINDOW
    k_full = jnp.concatenate([kp_ref[...], kc_ref[...]], axis=0)
    v_full = jnp.concatenate([vp_ref[...], vc_ref[...]], axis=0)
    k_var = _kv_lane_variants(k_full)
    v_var = _kv_lane_variants(v_full)
    own = (lax.broadcasted_iota(jnp.int32, (L, L), 1)
           <= lax.broadcasted_iota(jnp.int32, (L, L), 0))
    scale = HEAD_DIM ** -0.5

    units = [(i, j, par) for i in range(q_ref.shape[0] // L) for j in range(Q_W // LANES) for par in range(2)]

    def scores(unit):
        i, j, par = unit
        q2 = q_ref[i * L:(i + 1) * L, j * LANES:(j + 1) * LANES] * scale
        return _dot_nt(q2, k_var[(2 * j) // GQA_GROUP][par][i * L:(i + 2) * L])

    s_next = scores(units[0])
    out = None
    for idx, (i, j, par) in enumerate(units):
        s = s_next
        if idx + 1 < len(units):
            s_next = scores(units[idx + 1])
        s_prev = s[:, :L]
        if i == 0:
            s_prev = jnp.where(n > 0, s_prev, NEG_INF)
        sw = jnp.where(own, s[:, L:], s_prev)
        sink = sinks_ref[2 * j + par]
        m = jnp.maximum(jnp.max(sw, axis=-1, keepdims=True), sink)
        e = jnp.exp(sw - m)
        denom = jnp.sum(e, axis=-1, keepdims=True) + jnp.exp(sink - m)
        e_band = jnp.concatenate([jnp.where(own, 0.0, e), jnp.where(own, e, 0.0)], axis=1)
        r = _dot(e_band.astype(BF16), v_var[(2 * j) // GQA_GROUP][par][i * L:(i + 2) * L]) / denom
        if par == 0:
            out = r
        else:
            o_ref[i * L:(i + 1) * L, j * LANES:(j + 1) * LANES] = (out + r).astype(o_ref.dtype)


def _attn_prompt(qa, ka, va, sinks, batch, seq):
    tq = ATTN_TILE
    nq = seq // tq
    per = tq // WINDOW
    cur = lambda b, n: (b * nq + n, 0)
    prev = lambda b, n: ((b * nq + n) * per - jnp.minimum(n, 1), 0)
    return pl.pallas_call(
        _attn_prompt_kernel,
        grid=(batch, nq),
        in_specs=[pl.BlockSpec(memory_space=pltpu.SMEM),
                  pl.BlockSpec((tq, Q_W), cur),
                  pl.BlockSpec((tq, KV_W), cur), pl.BlockSpec((WINDOW, KV_W), prev),
                  pl.BlockSpec((tq, KV_W), cur), pl.BlockSpec((WINDOW, KV_W), prev)],
        out_specs=pl.BlockSpec((tq, Q_W), cur),
        out_shape=jax.ShapeDtypeStruct(qa.shape, BF16),
        compiler_params=_cparams(2),
        name="attn_prompt",
    )(sinks, qa, ka, ka, va, va)


def _attn_sample_stages(sink_rows_ref, q_ref, kn_ref, vn_ref, ck_ref, cv_ref,
                        o_ref, nk_ref, nv_ref):
    bt, t_new, _ = q_ref.shape
    assert t_new == SUBLANES
    w = ck_ref.shape[3]
    n_rows = N_Q_HEADS * t_new
    lane_lo = lax.broadcasted_iota(jnp.int32, (bt * t_new, LANES), 1) < HEAD_DIM
    t_c = lax.broadcasted_iota(jnp.int32, (n_rows, w), 0) % t_new
    mask_c = lax.broadcasted_iota(jnp.int32, (n_rows, w), 1) > t_c + (w - WINDOW)
    t_n = lax.broadcasted_iota(jnp.int32, (n_rows, t_new), 0) % t_new
    mask_n = lax.broadcasted_iota(jnp.int32, (n_rows, t_new), 1) <= t_n
    sink = sink_rows_ref[...]
    scale = HEAD_DIM ** -0.5
    kn, vn = kn_ref[...], vn_ref[...]
    kc_t = ck_ref[...].reshape(bt, KV_W, w)
    vc_t = cv_ref[...].reshape(bt, KV_W, w)
    q = q_ref[...].reshape(bt * t_new, Q_W)
    pieces = []
    for h in range(N_Q_HEADS):
        q2 = q[:, (h // 2) * LANES:(h // 2 + 1) * LANES]
        src_lo = h % 2 == 0
        dst_lo = h // GQA_GROUP == 0
        if src_lo != dst_lo:
            q2 = pltpu.roll(q2, HEAD_DIM, 1)
        pieces.append(jnp.where(lane_lo == dst_lo, q2, 0.0).reshape(bt, t_new, LANES))
    q_rows = (jnp.concatenate(pieces, axis=1) * scale).astype(BF16)
    s_c = jnp.einsum("bqd,bdk->bqk", q_rows, kc_t.astype(BF16), preferred_element_type=F32)
    s_n = jnp.einsum("bqd,bkd->bqk", q_rows, kn.astype(BF16), preferred_element_type=F32)
    yield
    s_c = jnp.where(mask_c, s_c, NEG_INF)
    s_n = jnp.where(mask_n, s_n, NEG_INF)
    m = jnp.maximum(jnp.maximum(jnp.max(s_c, axis=-1, keepdims=True), jnp.max(s_n, axis=-1, keepdims=True)), sink)
    e_c, e_n = jnp.exp(s_c - m), jnp.exp(s_n - m)
    denom = jnp.sum(e_c, axis=-1, keepdims=True) + jnp.sum(e_n, axis=-1, keepdims=True) + jnp.exp(sink - m)
    pv = (jnp.einsum("bqk,bdk->bqd", e_c.astype(BF16), vc_t.astype(BF16), preferred_element_type=F32)
          + jnp.einsum("bqk,bkd->bqd", e_n.astype(BF16), vn.astype(BF16), preferred_element_type=F32)) / denom
    for j in range(Q_W // LANES):
        out = None
        for par in range(2):
            h = 2 * j + par
            dst_lo = h // GQA_GROUP == 0
            piece = pv[:, h * t_new:(h + 1) * t_new, :].reshape(bt * t_new, LANES)
            piece = jnp.where(lane_lo == dst_lo, piece, 0.0)
            if dst_lo != (par == 0):
                piece = pltpu.roll(piece, HEAD_DIM, 1)
            out = piece if out is None else out + piece
        o_ref[:, :, j * LANES:(j + 1) * LANES] = out.reshape(bt, t_new, LANES).astype(o_ref.dtype)
    keep = lax.broadcasted_iota(jnp.int32, (bt * KV_W, w), 1) < w - t_new
    for old_t, new, out_ref in ((kc_t, kn, nk_ref), (vc_t, vn, nv_ref)):
        shifted = pltpu.roll(old_t.reshape(bt * KV_W, w), w - t_new, 1)
        placed = jnp.concatenate([jnp.zeros((bt, w - t_new, KV_W), F32), new], axis=1)
        placed_t = jnp.swapaxes(placed, 1, 2).reshape(bt * KV_W, w)
        out_ref[...] = jnp.where(keep, shifted, placed_t).reshape(out_ref.shape)


def _ret_log_gamma(h):
    return math.log(1.0 - 2.0 ** (-5.0 - h))


def _ret_decays(chunk, h):
    lg = _ret_log_gamma(h)
    i = lax.broadcasted_iota(jnp.int32, (chunk, chunk), 0)
    j = lax.broadcasted_iota(jnp.int32, (chunk, chunk), 1)
    rel_mask = jnp.where(i >= j, jnp.exp((j + 1).astype(F32) * -lg), 0.0)
    col = lax.broadcasted_iota(jnp.int32, (chunk, 1), 0).astype(F32)
    gn_eps = GN_EPS * jnp.exp((col + 1.0) * (-2.0 * lg))
    k_decay = jnp.exp((chunk - 1.0 - col) * lg)
    return rel_mask, gn_eps, k_decay, math.exp(chunk * lg)


def _ret_chunk_scores(q, k, v, decays):
    heads = range(N_RET_HEADS)
    batched = q.ndim == 3
    mm = (lambda a, b: jnp.einsum("bmk,bkn->bmn", a, b, preferred_element_type=F32)) if batched else _dot
    mm_nt = (lambda a, b: jnp.einsum("bmk,bnk->bmn", a, b, preferred_element_type=F32)) if batched else _dot_nt
    qh = [q[..., h * RET_KEY_DIM:(h + 1) * RET_KEY_DIM] for h in heads]
    kh = [k[..., h * RET_KEY_DIM:(h + 1) * RET_KEY_DIM] for h in heads]
    vh = [v[..., h * RET_VAL_DIM:(h + 1) * RET_VAL_DIM] for h in heads]
    qb = [x.astype(BF16) for x in qh]
    vb = [x.astype(BF16) for x in vh]
    inner = [mm_nt(qb[h], kh[h].astype(BF16)) * decays[h][0] for h in heads]
    return batched, mm, qh, kh, vh, qb, vb, inner


def _ret_chunk_finish(scores, g, states, decays):
    batched, mm, qh, kh, vh, qb, vb, inner = scores
    heads = range(N_RET_HEADS)
    if batched:
        lhs = [jnp.concatenate([qh[h], inner[h]], axis=-1).astype(BF16) for h in heads]
        rhs = [jnp.concatenate([states[h], vh[h]], axis=-2).astype(BF16) for h in heads]
    else:
        lhs = [jnp.concatenate([qb[h], inner[h].astype(BF16)], axis=-1) for h in heads]
        rhs = [jnp.concatenate([states[h].astype(BF16), vb[h]], axis=-2) for h in heads]
    p = [mm(lhs[h], rhs[h]) for h in heads]
    kd_t = [jnp.swapaxes(kh[h].astype(F32) * decays[h][2], -1, -2).astype(BF16) for h in heads]
    new_states = [states[h] * decays[h][3] + mm(kd_t[h], vb[h]) for h in heads]
    outs = []
    for h in heads:
        mu = jnp.mean(p[h], axis=-1, keepdims=True)
        pc = p[h] - mu
        var = jnp.mean(pc * pc, axis=-1, keepdims=True)
        gate = g[..., h * RET_VAL_DIM:(h + 1) * RET_VAL_DIM].astype(F32)
        outs.append(pc * lax.rsqrt(var + decays[h][1]) * jax.nn.silu(gate))
    return jnp.concatenate(outs, axis=-1), new_states


def _ret_chunk_all_heads(q, k, v, g, states, decays):
    return _ret_chunk_finish(_ret_chunk_scores(q, k, v, decays), g, states, decays)


def _ret_prompt_kernel(q_ref, k_ref, v_ref, g_ref, o_ref, st_ref, state_ref):
    n = pl.program_id(1)
    C = RET_CHUNK

    @pl.when(n == 0)
    def _():
        state_ref[...] = jnp.zeros_like(state_ref)

    decays = [_ret_decays(C, h) for h in range(N_RET_HEADS)]
    states = [state_ref[h] for h in range(N_RET_HEADS)]
    for c in range(q_ref.shape[0] // C):
        rows = slice(c * C, (c + 1) * C)
        out, states = _ret_chunk_all_heads(q_ref[rows, :], k_ref[rows, :], v_ref[rows, :], g_ref[rows, :],
                                           states, decays)
        o_ref[rows, :] = out.astype(o_ref.dtype)
    for h in range(N_RET_HEADS):
        state_ref[h] = states[h]

    @pl.when(n == pl.num_programs(1) - 1)
    def _():
        st_ref[0] = state_ref[...]


def _ret_prompt(qr, kr, vr, gr, batch, seq):
    ts = RET_TILE
    ns = seq // ts
    row = lambda b, n: (b * ns + n, 0)
    return pl.pallas_call(
        _ret_prompt_kernel,
        grid=(batch, ns),
        in_specs=[pl.BlockSpec((ts, RQ_W), row), pl.BlockSpec((ts, RQ_W), row),
                  pl.BlockSpec((ts, RV_W), row), pl.BlockSpec((ts, RV_W), row)],
        out_specs=[pl.BlockSpec((ts, RV_W), row),
                   pl.BlockSpec((1, N_RET_HEADS, RET_KEY_DIM, RET_VAL_DIM), lambda b, n: (b, 0, 0, 0))],
        out_shape=[jax.ShapeDtypeStruct((batch * seq, RV_W), BF16),
                   jax.ShapeDtypeStruct((batch, N_RET_HEADS, RET_KEY_DIM, RET_VAL_DIM), F32)],
        scratch_shapes=[pltpu.VMEM((N_RET_HEADS, RET_KEY_DIM, RET_VAL_DIM), F32)],
        compiler_params=_cparams(2),
        name="ret_prompt",
    )(qr, kr, vr, gr)


def _ret_sample_stages(q_ref, k_ref, v_ref, g_ref, st_ref, o_ref, nst_ref):
    t_new = q_ref.shape[1]
    decays = [_ret_decays(t_new, h) for h in range(N_RET_HEADS)]
    scores = _ret_chunk_scores(q_ref[...], k_ref[...], v_ref[...], decays)
    yield
    states = [st_ref[:, h] for h in range(N_RET_HEADS)]
    out, states = _ret_chunk_finish(scores, g_ref[...], states, decays)
    o_ref[...] = out.astype(o_ref.dtype)
    for h in range(N_RET_HEADS):
        nst_ref[:, h] = states[h]


def _sample_mixers_stages(sink_rows_ref, qa_ref, kn_ref, vn_ref, ck_ref, cv_ref,
                          qr_ref, kr_ref, vr_ref, gr_ref, st_ref,
                          oa_ref, nk_ref, nv_ref, or_ref, nst_ref):
    attn = _attn_sample_stages(sink_rows_ref, qa_ref, kn_ref, vn_ref, ck_ref, cv_ref, oa_ref, nk_ref, nv_ref)
    ret = _ret_sample_stages(qr_ref, kr_ref, vr_ref, gr_ref, st_ref, or_ref, nst_ref)
    next(attn)
    next(ret)
    yield
    _run(attn)
    _run(ret)


def _sample_mixers_parts(qa, ka, va, sinks, cache_k, cache_v, qr, kr, vr, gr, state):
    db, t_new, _ = qa.shape
    w = cache_k.shape[3]
    cache = (N_KV_HEADS, HEAD_DIM, w)
    sink_rows = jnp.repeat(sinks.astype(F32), t_new).reshape(N_Q_HEADS * t_new, 1)

    def parts(n_steps):
        assert db % n_steps == 0
        bt = db // n_steps
        blk = lambda *s: pl.BlockSpec((bt,) + s, lambda i: (i,) + (0,) * len(s))
        st = blk(N_RET_HEADS, RET_KEY_DIM, RET_VAL_DIM)
        in_specs = [_const_spec(sink_rows.shape), blk(t_new, Q_W), blk(t_new, KV_W), blk(t_new, KV_W),
                    blk(*cache), blk(*cache),
                    blk(t_new, RQ_W), blk(t_new, RQ_W), blk(t_new, RV_W), blk(t_new, RV_W), st]
        out_specs = [blk(t_new, Q_W), blk(*cache), blk(*cache), blk(t_new, RV_W), st]
        out_shape = [jax.ShapeDtypeStruct((db, t_new, Q_W), F32),
                     jax.ShapeDtypeStruct((db,) + cache, F32),
                     jax.ShapeDtypeStruct((db,) + cache, F32),
                     jax.ShapeDtypeStruct((db, t_new, RV_W), F32),
                     jax.ShapeDtypeStruct(state.shape, state.dtype)]
        args = [sink_rows, qa, ka, va, cache_k, cache_v, qr, kr, vr, gr, state]
        return _sample_mixers_stages, in_specs, out_specs, out_shape, args

    return parts


def _merge_ln_kernel(alpha, ln_row, x_ref, oa_ref, orr_ref, ga_ref, gb_ref, wba_ref, wbr_ref, wo_ref,
                     g_ref, b_ref, o_ref):
    n_blocks = max(1, x_ref.shape[0] // MERGE_ROWS)
    rb = x_ref.shape[0] // n_blocks
    blocks = [slice(r * rb, (r + 1) * rb) for r in range(n_blocks)]
    ya = [_dot(oa_ref[rows, :].astype(BF16), wba_ref[...]) for rows in blocks]
    yr = [_dot(orr_ref[rows, :].astype(BF16), wbr_ref[...]) for rows in blocks]
    merged = [(jax.nn.sigmoid(ga_ref[rows, :].astype(F32)) * ya[r]
               + jax.nn.sigmoid(gb_ref[rows, :].astype(F32)) * yr[r]).astype(BF16)
              for r, rows in enumerate(blocks)]
    z = [_dot(merged[r], wo_ref[...]) for r in range(n_blocks)]
    for r, rows in enumerate(blocks):
        o_ref[rows, :] = _layer_norm(alpha * x_ref[rows, :] + z[r],
                                     g_ref[ln_row:ln_row + 1, :], b_ref[ln_row:ln_row + 1, :])


def _merge_ln(x1, oa, orr, ga, gb, wba, wbr, wo, ln_g, ln_b, ln_row, alpha):
    t, d = x1.shape
    tm = _dense_tile(t)
    row = lambda i: (i, 0)
    return pl.pallas_call(
        functools.partial(_merge_ln_kernel, alpha, ln_row),
        grid=(t // tm,),
        in_specs=[pl.BlockSpec((tm, d), row), pl.BlockSpec((tm, Q_W), row), pl.BlockSpec((tm, RV_W), row),
                  pl.BlockSpec((tm, d), row), pl.BlockSpec((tm, d), row),
                  _const_spec(wba.shape), _const_spec(wbr.shape), _const_spec(wo.shape),
                  _const_spec(ln_g.shape), _const_spec(ln_b.shape)],
        out_specs=pl.BlockSpec((tm, d), row),
        out_shape=jax.ShapeDtypeStruct((t, d), F32),
        compiler_params=_cparams(1),
        name="merge_ln",
    )(x1, oa, orr, ga, gb, wba, wbr, wo, ln_g, ln_b)


def _rope_tables_for(pos):
    period = pos.shape[0]
    if period < DENSE_TILE:
        pos = pos[jnp.arange(DENSE_TILE) % period]
    return _rope_tables(pos, HEAD_DIM, _attn_inv_freq) + _rope_tables(pos, RET_KEY_DIM, _ret_inv_freq)


def kernel(x_prompt, x_sample, cache_k, cache_v, state_ret, ln_g, ln_b, ffn_wi, ffn_wo, w_in,
           attn_sinks, w_br_attn, w_br_ret, w_o):
    B, S, D = x_prompt.shape
    DB, T, _ = x_sample.shape
    depth = ln_g.shape[0]
    alpha = (2.0 * depth) ** 0.25
    W = cache_k.shape[2]
    tables_p = _rope_tables_for(jnp.arange(S, dtype=jnp.int32))
    tables_s = _rope_tables_for(PAST_LEN + jnp.arange(T, dtype=jnp.int32))

    y_p = x_prompt.reshape(B * S, D)
    y_s = x_sample.reshape(DB * T, D)
    outs = [[] for _ in range(6)]
    for l in range(depth):
        wi, wo = ffn_wi[l], ffn_wo[l]
        sinks = attn_sinks[l].astype(F32)
        g_l, b_l = ln_g[l], ln_b[l]

        x1_p, x1_s, w_in_l, wba, wbr, w_o_l = _ffn_ln(
            y_p, y_s, wi, wo, 0, g_l, b_l, 0, alpha,
            cast_weights=(w_in[l], w_br_attn[l], w_br_ret[l], w_o[l]))
        qa_s, ka_s, va_s, qr_s, kr_s, vr_s, gr_s, ga_s, gb_s = _proj_rope(
            x1_s, w_in_l, tables_s, F32)

        r3 = lambda a: a.reshape(DB, T, a.shape[-1])
        to_lane_major = lambda c: jnp.transpose(c, (0, 2, 3, 1))
        sample_mixers = _sample_mixers_parts(
            r3(qa_s), r3(ka_s), r3(va_s), sinks, to_lane_major(cache_k[l]), to_lane_major(cache_v[l]),
            r3(qr_s), r3(kr_s), r3(vr_s), r3(gr_s), state_ret[l])
        w = min(WINDOW, S)
        (qa_p, ka_p, va_p, qr_p, kr_p, vr_p, gr_p, ga_p, gb_p, nk_p, nv_p), (oa_s, nk_s, nv_s, or_s, st_s) = \
            _proj_rope(x1_p, w_in_l, tables_p, BF16, rider=(RIDER_PROJ_TILE, sample_mixers), tail=(S, w))
        oa_s, or_s = oa_s.reshape(DB * T, Q_W), or_s.reshape(DB * T, RV_W)
        from_lane_major = lambda c: jnp.transpose(c, (0, 3, 1, 2))
        new_caches = [from_lane_major(nk_p.reshape(B, N_KV_HEADS, HEAD_DIM, w)),
                      from_lane_major(nv_p.reshape(B, N_KV_HEADS, HEAD_DIM, w)),
                      from_lane_major(nk_s), from_lane_major(nv_s)]

        oa_p = _attn_prompt(qa_p, ka_p, va_p, sinks, B, S)
        or_p, st_p = _ret_prompt(qr_p, kr_p, vr_p, gr_p, B, S)

        x2_p = _merge_ln(x1_p, oa_p, or_p, ga_p, gb_p, wba, wbr, w_o_l, g_l, b_l, 1, alpha)
        x2_s = _merge_ln(x1_s, oa_s, or_s, ga_s, gb_s, wba, wbr, w_o_l, g_l, b_l, 1, alpha)
        y_p, y_s = _ffn_ln(x2_p, x2_s, wi, wo, 1, g_l, b_l, 2, alpha)
        for lst, val in zip(outs, (new_caches[0], new_caches[1], st_p, new_caches[2], new_caches[3], st_s)):
            lst.append(val)
    return (y_p.reshape(B, S, D), y_s.reshape(DB, T, D)) + tuple(jnp.stack(o) for o in outs)
```

```python
import functools
import math

import jax
import jax.numpy as jnp
from jax import lax
from jax.experimental import pallas as pl
from jax.experimental.pallas import tpu as pltpu

F32 = jnp.float32
BF16 = jnp.bfloat16

PAST_LEN = 8192
N_Q_HEADS = 8
N_KV_HEADS = 2
HEAD_DIM = 64
GQA_GROUP = N_Q_HEADS // N_KV_HEADS
WINDOW = 128
ROPE_THETA = 10000.0
N_RET_HEADS = 4
RET_KEY_DIM = 128
RET_VAL_DIM = 256
RET_CHUNK = 128
RET_THETA = 10000.0
LN_EPS = 1e-5
GN_EPS = 1e-6
NEG_INF = -1e30

Q_W = N_Q_HEADS * HEAD_DIM
KV_W = N_KV_HEADS * HEAD_DIM
RQ_W = N_RET_HEADS * RET_KEY_DIM
RV_W = N_RET_HEADS * RET_VAL_DIM

LANES = 128
SUBLANES = 8
VMEM_LIMIT_BYTES = 56 * 1024 * 1024

TOKEN_TILE = 512
DENSE_TILE = 1024
DENSE_MIN_STEPS = 4
FF_CHUNK = 256
ATTN_TILE = 2048
RET_TILE = 1024
LN_ANCHOR_CHUNK = 8
MERGE_ROWS = 256
RIDER_PROJ_TILE = 512
RIDER_LEAD_GROUPS = 2
CAST_STEPS = 16


def _dot(a, b):
    return jnp.dot(a, b, preferred_element_type=F32)


def _dot_nt(a, b):
    return lax.dot_general(a, b, (((1,), (1,)), ((), ())), preferred_element_type=F32)


def _layer_norm(y, g, b):
    mu = jnp.mean(y, axis=-1, keepdims=True)
    yc = y - mu
    var = jnp.mean(yc * yc, axis=-1, keepdims=True)
    return yc * lax.rsqrt(var + LN_EPS) * g + b


def _cparams(n_axes, semantics=None):
    return pltpu.CompilerParams(
        dimension_semantics=semantics or ("arbitrary",) * n_axes,
        vmem_limit_bytes=VMEM_LIMIT_BYTES)


def _const_spec(shape):
    nd = len(shape)
    return pl.BlockSpec(shape, lambda *_: (0,) * nd, pipeline_mode=pl.Buffered(1))


def _segment_specs(n_prompt_tiles, tm, width, tile_of):
    prompt = pl.BlockSpec((tm, width), lambda i: (jnp.minimum(tile_of(i), n_prompt_tiles - 1), 0))
    sample = pl.BlockSpec((tm, width), lambda i: (jnp.maximum(tile_of(i) - n_prompt_tiles, 0), 0))
    return prompt, sample


def _store_segment(is_prompt, prompt_ref, sample_ref, value):
    @pl.when(is_prompt)
    def _():
        prompt_ref[...] = value.astype(prompt_ref.dtype)

    @pl.when(jnp.logical_not(is_prompt))
    def _():
        sample_ref[...] = value.astype(sample_ref.dtype)


def _ffn_chunk(xb, w_gate, w_up, w_out):
    act = (jax.nn.silu(_dot(xb, w_gate)) * _dot(xb, w_up)).astype(BF16)
    return _dot(act, w_out)


def _ffn_tile_maps(n_chunks, n_tiles):
    matmul_tile = lambda i: jnp.clip(i - (n_chunks - 1), 0, n_tiles - 1)
    norm_tile = lambda i: jnp.clip(i - n_chunks, 0, n_tiles - 1)
    return matmul_tile, norm_tile


def _ffn_ln_kernel(alpha, n_chunks, n_tiles, ln_row, n_prompt_tiles, n_cast, *refs):
    xp_ref, xs_ref, wg_ref, wu_ref, wo_ref, g_ref, b_ref = refs[:7]
    cast_in = refs[7:7 + n_cast]
    op_ref, os_ref = refs[7 + n_cast:9 + n_cast]
    cast_out = refs[9 + n_cast:9 + 2 * n_cast]
    acc_ref, y_ref, wi_bf_ref, wo_bf_ref = refs[9 + 2 * n_cast:]
    step = pl.program_id(0)
    matmul_tile, norm_tile = _ffn_tile_maps(n_chunks, n_tiles)
    last_step = n_chunks + n_tiles - 1
    fc = FF_CHUNK

    norm_is_prompt = norm_tile(step) < n_prompt_tiles

    def norm_previous():
        return _layer_norm(y_ref[...], g_ref[ln_row:ln_row + 1, :], b_ref[ln_row:ln_row + 1, :])

    @pl.when(step < n_chunks)
    def _():
        w_gate, w_up, w_out = wg_ref[...].astype(BF16), wu_ref[...].astype(BF16), wo_ref[...].astype(BF16)
        wi_bf_ref[step, :, :fc] = w_gate
        wi_bf_ref[step, :, fc:] = w_up
        wo_bf_ref[step] = w_out
        x = xp_ref[...]
        part = _ffn_chunk(x.astype(BF16), w_gate, w_up, w_out)

        @pl.when(step == 0)
        def _():
            acc_ref[...] = part

        @pl.when(step > 0)
        def _():
            acc_ref[...] += part

        @pl.when(step == n_chunks - 1)
        def _():
            y_ref[...] = alpha * x + 0.5 * acc_ref[...]

    def full_tile_step(out_ref):
        normed = norm_previous()
        out_ref[...] = normed
        rows, width = normed.shape
        anchor = jnp.max(normed.reshape(rows // SUBLANES, SUBLANES, width), axis=0)
        anchor = functools.reduce(jnp.maximum, [anchor[:, j * LANES:(j + 1) * LANES]
                                                for j in range(width // LANES)])
        zero = jnp.minimum(jnp.abs(anchor), 0.0)
        x = jnp.where(matmul_tile(step) < n_prompt_tiles, xp_ref[...], xs_ref[...])
        xb = x.astype(BF16)
        for c in range(n_chunks):
            part = _ffn_chunk(xb, wi_bf_ref[c, :, :fc], wi_bf_ref[c, :, fc:], wo_bf_ref[c])
            if c == 0:
                acc_ref[...] = part
            else:
                acc_ref[...] += part
            if c == min(LN_ANCHOR_CHUNK, n_chunks - 1):
                acc_ref[0:SUBLANES, 0:LANES] += zero
        y_ref[...] = alpha * x + 0.5 * acc_ref[...]

    in_full_steps = (step >= n_chunks) & (step < last_step)

    if n_cast:
        @pl.when((step >= n_chunks) & (step < n_chunks + CAST_STEPS))
        def _():
            for src, dst in zip(cast_in, cast_out):
                dst[...] = src[...].astype(dst.dtype)

    pl.when(in_full_steps & norm_is_prompt)(functools.partial(full_tile_step, op_ref))
    pl.when(in_full_steps & jnp.logical_not(norm_is_prompt))(functools.partial(full_tile_step, os_ref))

    @pl.when(step == last_step)
    def _():
        _store_segment(norm_is_prompt, op_ref, os_ref, norm_previous())


def _ffn_ln(xp, xs, wi, wo, half, ln_g, ln_b, ln_row, alpha, cast_weights=()):
    d = xp.shape[1]
    d_ff = wo.shape[1]
    tm, fc = TOKEN_TILE, FF_CHUNK
    assert xp.shape[0] % tm == 0 and xs.shape[0] % tm == 0 and d_ff % fc == 0
    n_p, n_s = xp.shape[0] // tm, xs.shape[0] // tm
    n_chunks = d_ff // fc
    n_tiles = n_p + n_s
    assert n_tiles - 1 >= CAST_STEPS
    matmul_tile, norm_tile = _ffn_tile_maps(n_chunks, n_tiles)
    chunk_of = lambda i: jnp.minimum(i, n_chunks - 1)
    cast_specs = []
    for w in cast_weights:
        rows = w.shape[0] // CAST_STEPS
        assert w.shape[0] % CAST_STEPS == 0 and rows % 16 == 0
        cast_specs.append(pl.BlockSpec((rows, w.shape[1]),
                                       lambda i: (jnp.clip(i - n_chunks, 0, CAST_STEPS - 1), 0)))
    return pl.pallas_call(
        functools.partial(_ffn_ln_kernel, alpha, n_chunks, n_tiles, ln_row, n_p, len(cast_weights)),
        grid=(n_chunks + n_tiles,),
        in_specs=[*_segment_specs(n_p, tm, d, matmul_tile),
                  pl.BlockSpec((None, d, fc), lambda i: (half, 0, chunk_of(i))),
                  pl.BlockSpec((None, d, fc), lambda i: (half, 0, n_chunks + chunk_of(i))),
                  pl.BlockSpec((None, fc, d), lambda i: (half, chunk_of(i), 0)),
                  _const_spec(ln_g.shape), _const_spec(ln_b.shape), *cast_specs],
        out_specs=[*_segment_specs(n_p, tm, d, norm_tile), *cast_specs],
        out_shape=[jax.ShapeDtypeStruct(xp.shape, F32), jax.ShapeDtypeStruct(xs.shape, F32)]
                  + [jax.ShapeDtypeStruct(w.shape, BF16) for w in cast_weights],
        scratch_shapes=[pltpu.VMEM((tm, d), F32), pltpu.VMEM((tm, d), F32),
                        pltpu.VMEM((n_chunks, d, 2 * fc), BF16),
                        pltpu.VMEM((n_chunks, fc, d), BF16)],
        compiler_params=_cparams(1),
        name="ffn_ln",
    )(xp, xs, wi, wi, wo, ln_g, ln_b, *cast_weights)


def _rope_tables(pos, dim, theta_fn):
    half = dim // 2
    inv_freq = theta_fn(half)
    ang = pos.astype(F32)[:, None] * inv_freq[None, :]
    reps = LANES // half
    cos = jnp.tile(jnp.cos(ang), (1, reps))
    sign = jnp.tile(jnp.concatenate([-jnp.ones((half,), F32), jnp.ones((half,), F32)]), LANES // dim)
    sin = jnp.tile(jnp.sin(ang), (1, reps)) * sign[None, :]
    return cos, sin


def _attn_inv_freq(half):
    return 1.0 / (ROPE_THETA ** (jnp.arange(0, 2 * half, 2, dtype=F32) / (2 * half)))


def _ret_inv_freq(half):
    return 1.0 / (RET_THETA ** jnp.linspace(0.0, 1.0, half, dtype=F32))


def _rope_block(x, cos, sin, dim):
    half = dim // 2
    if dim == LANES:
        rot = pltpu.roll(x, half, 1)
    else:
        lane = lax.broadcasted_iota(jnp.int32, x.shape, 1)
        first_half = (lane % dim) < half
        rot = jnp.where(first_half, pltpu.roll(x, LANES - half, 1), pltpu.roll(x, half, 1))
    return x * cos + rot * sin


def _run(stages):
    for _ in stages:
        pass


def _proj_rope_stages(x_ref, w_ref, cosa_ref, sina_ref, cosr_ref, sinr_ref,
                      qa_ref, ka_ref, va_ref, qr_ref, kr_ref, vr_ref, gr_ref, ga_ref, gb_ref,
                      k_tail_ref=None, v_tail_ref=None):
    xb = x_ref[...].astype(BF16)
    cosa, sina = cosa_ref[...], sina_ref[...]
    cosr, sinr = cosr_ref[...], sinr_ref[...]
    off = 0

    y = _dot(xb, w_ref[:, off:off + Q_W])
    for j in range(Q_W // LANES):
        qa_ref[:, j * LANES:(j + 1) * LANES] = _rope_block(
            y[:, j * LANES:(j + 1) * LANES], cosa, sina, HEAD_DIM).astype(qa_ref.dtype)
    off += Q_W
    yield
    y = _dot(xb, w_ref[:, off:off + 2 * KV_W])
    k_rot = _rope_block(y[:, :KV_W], cosa, sina, HEAD_DIM)
    ka_ref[...] = k_rot
    va_ref[...] = y[:, KV_W:]
    if k_tail_ref is not None:
        tail = k_tail_ref.shape[1]
        k_tail_ref[...] = k_rot[-tail:, :].T
        v_tail_ref[...] = y[-tail:, KV_W:].T
    off += 2 * KV_W
    yield
    y = _dot(xb, w_ref[:, off:off + RQ_W])
    for j in range(RQ_W // LANES):
        qr_ref[:, j * LANES:(j + 1) * LANES] = _rope_block(
            y[:, j * LANES:(j + 1) * LANES], cosr, sinr, RET_KEY_DIM).astype(qr_ref.dtype)
    off += RQ_W
    yield
    y = _dot(xb, w_ref[:, off:off + RQ_W])
    for j in range(RQ_W // LANES):
        kr = _rope_block(y[:, j * LANES:(j + 1) * LANES], cosr, sinr, RET_KEY_DIM)
        kr_ref[:, j * LANES:(j + 1) * LANES] = (kr * (RET_KEY_DIM ** -0.5)).astype(kr_ref.dtype)
    off += RQ_W
    yield
    for ref in (vr_ref, gr_ref, ga_ref, gb_ref):
        width = ref.shape[1]
        for c in range(width // 512):
            ref[:, c * 512:(c + 1) * 512] = _dot(
                xb, w_ref[:, off + c * 512:off + (c + 1) * 512]).astype(ref.dtype)
            yield
        off += width


def _proj_rope_kernel(*refs):
    _run(_proj_rope_stages(*refs))


def _dense_tile(tokens):
    return min(DENSE_TILE, tokens // DENSE_MIN_STEPS)


N_PROJ_IN = 6


def _proj_with_rider_kernel(rider_stages, n_rider_in, n_proj_out, *refs):
    proj_in, rest = refs[:N_PROJ_IN], refs[N_PROJ_IN:]
    rider_in, rest = rest[:n_rider_in], rest[n_rider_in:]
    proj_out, rider_out = rest[:n_proj_out], rest[n_proj_out:]
    rider = rider_stages(*rider_in, *rider_out)
    proj = _proj_rope_stages(*proj_in, *proj_out)
    next(rider)
    for _ in range(RIDER_LEAD_GROUPS):
        next(proj)
    _run(rider)
    _run(proj)


def _proj_rope(x1, w_in, tables, mixer_dtype, rider=None, tail=None):
    t, d = x1.shape
    n_in = w_in.shape[1]
    d_model = (n_in - Q_W - 2 * KV_W - 2 * RQ_W - 2 * RV_W) // 2
    tm = _dense_tile(t) if rider is None else rider[0]
    row = lambda i: (i, 0)
    table_blocks = tables[0].shape[0] // tm
    tab = lambda i: (i % table_blocks, 0)
    widths = [(Q_W, mixer_dtype), (KV_W, F32), (KV_W, F32), (RQ_W, mixer_dtype), (RQ_W, mixer_dtype),
              (RV_W, mixer_dtype), (RV_W, mixer_dtype), (d_model, BF16), (d_model, BF16)]
    in_specs = [pl.BlockSpec((tm, d), row), _const_spec(w_in.shape)] + [pl.BlockSpec((tm, LANES), tab)] * 4
    out_specs = [pl.BlockSpec((tm, w), row) for w, _ in widths]
    out_shape = [jax.ShapeDtypeStruct((t, w), dt) for w, dt in widths]
    if tail is not None:
        seq_len, tail_rows = tail
        assert seq_len % tm == 0 and tail_rows <= tm
        tiles_per_seq = seq_len // tm
        out_specs += [pl.BlockSpec((KV_W, tail_rows), lambda i: (i // tiles_per_seq, 0))] * 2
        out_shape += [jax.ShapeDtypeStruct((t // seq_len * KV_W, tail_rows), F32)] * 2
    n_proj_out = len(out_specs)
    args = [x1, w_in, *tables]
    body = _proj_rope_kernel
    if rider is not None:
        r_stages, r_in_specs, r_out_specs, r_out_shape, r_args = rider[1](t // tm)
        body = functools.partial(_proj_with_rider_kernel, r_stages, len(r_args), n_proj_out)
        in_specs, out_specs = in_specs + r_in_specs, out_specs + r_out_specs
        out_shape, args = out_shape + r_out_shape, args + r_args
    outs = pl.pallas_call(
        body,
        grid=(t // tm,),
        in_specs=in_specs,
        out_specs=out_specs,
        out_shape=out_shape,
        compiler_params=_cparams(1),
        name="proj_rope",
    )(*args)
    return outs if rider is None else (outs[:n_proj_out], outs[n_proj_out:])


def _kv_lane_variants(x):
    lane = lax.broadcasted_iota(jnp.int32, x.shape, 1)
    lo = lane < HEAD_DIM
    swapped = pltpu.roll(x, HEAD_DIM, 1)
    zero = jnp.zeros_like(x)
    kv0 = (jnp.where(lo, x, zero).astype(BF16), jnp.where(lo, zero, swapped).astype(BF16))
    kv1 = (jnp.where(lo, swapped, zero).astype(BF16), jnp.where(lo, zero, x).astype(BF16))
    return kv0, kv1


def _attn_prompt_kernel(sinks_ref, q_ref, kc_ref, kp_ref, vc_ref, vp_ref, o_ref):
    n = pl.program_id(1)
    L = WINDOW
    k_full = jnp.concatenate([kp_ref[...], kc_ref[...]], axis=0)
    v_full = jnp.concatenate([vp_ref[...], vc_ref[...]], axis=0)
    v_var = _kv_lane_variants(v_full)
    k_t = k_full.T
    k_t_swapped = jnp.concatenate([k_t[HEAD_DIM:], k_t[:HEAD_DIM]], axis=0)
    row_lo = lax.broadcasted_iota(jnp.int32, k_t.shape, 0) < HEAD_DIM
    zero_t = jnp.zeros_like(k_t)
    k_var = ((jnp.where(row_lo, k_t, zero_t).astype(BF16), jnp.where(row_lo, zero_t, k_t_swapped).astype(BF16)),
             (jnp.where(row_lo, k_t_swapped, zero_t).astype(BF16), jnp.where(row_lo, zero_t, k_t).astype(BF16)))
    own = (lax.broadcasted_iota(jnp.int32, (L, L), 1)
           <= lax.broadcasted_iota(jnp.int32, (L, L), 0))
    scale = HEAD_DIM ** -0.5

    units = [(i, j, par) for i in range(q_ref.shape[0] // L) for j in range(Q_W // LANES) for par in range(2)]

    def scores(unit):
        i, j, par = unit
        q2 = q_ref[i * L:(i + 1) * L, j * LANES:(j + 1) * LANES] * scale
        return _dot(q2, k_var[(2 * j) // GQA_GROUP][par][:, i * L:(i + 2) * L])

    s_next = scores(units[0])
    out = None
    for idx, (i, j, par) in enumerate(units):
        s = s_next
        if idx + 1 < len(units):
            s_next = scores(units[idx + 1])
        s_prev = s[:, :L]
        if i == 0:
            s_prev = jnp.where(n > 0, s_prev, NEG_INF)
        sw = jnp.where(own, s[:, L:], s_prev)
        sink = sinks_ref[2 * j + par]
        m = jnp.maximum(jnp.max(sw, axis=-1, keepdims=True), sink)
        e = jnp.exp(sw - m)
        denom = jnp.sum(e, axis=-1, keepdims=True) + jnp.exp(sink - m)
        e_band = jnp.concatenate([jnp.where(own, 0.0, e), jnp.where(own, e, 0.0)], axis=1)
        r = _dot(e_band.astype(BF16), v_var[(2 * j) // GQA_GROUP][par][i * L:(i + 2) * L]) / denom
        if par == 0:
            out = r
        else:
            o_ref[i * L:(i + 1) * L, j * LANES:(j + 1) * LANES] = (out + r).astype(o_ref.dtype)


def _attn_prompt(qa, ka, va, sinks, batch, seq):
    tq = ATTN_TILE
    nq = seq // tq
    per = tq // WINDOW
    cur = lambda b, n: (b * nq + n, 0)
    prev = lambda b, n: ((b * nq + n) * per - jnp.minimum(n, 1), 0)
    return pl.pallas_call(
        _attn_prompt_kernel,
        grid=(batch, nq),
        in_specs=[pl.BlockSpec(memory_space=pltpu.SMEM),
                  pl.BlockSpec((tq, Q_W), cur),
                  pl.BlockSpec((tq, KV_W), cur), pl.BlockSpec((WINDOW, KV_W), prev),
                  pl.BlockSpec((tq, KV_W), cur), pl.BlockSpec((WINDOW, KV_W), prev)],
        out_specs=pl.BlockSpec((tq, Q_W), cur),
        out_shape=jax.ShapeDtypeStruct(qa.shape, BF16),
        compiler_params=_cparams(2),
        name="attn_prompt",
    )(sinks, qa, ka, ka, va, va)


def _attn_sample_stages(sink_rows_ref, q_ref, kn_ref, vn_ref, ck_ref, cv_ref,
                        o_ref, nk_ref, nv_ref):
    bt, t_new, _ = q_ref.shape
    assert t_new == SUBLANES
    w = ck_ref.shape[3]
    n_rows = N_Q_HEADS * t_new
    lane_lo = lax.broadcasted_iota(jnp.int32, (bt * t_new, LANES), 1) < HEAD_DIM
    t_c = lax.broadcasted_iota(jnp.int32, (n_rows, w), 0) % t_new
    mask_c = lax.broadcasted_iota(jnp.int32, (n_rows, w), 1) > t_c + (w - WINDOW)
    t_n = lax.broadcasted_iota(jnp.int32, (n_rows, t_new), 0) % t_new
    mask_n = lax.broadcasted_iota(jnp.int32, (n_rows, t_new), 1) <= t_n
    sink = sink_rows_ref[...]
    scale = HEAD_DIM ** -0.5
    kn, vn = kn_ref[...], vn_ref[...]
    kc_t = ck_ref[...].reshape(bt, KV_W, w)
    vc_t = cv_ref[...].reshape(bt, KV_W, w)
    q = q_ref[...].reshape(bt * t_new, Q_W)
    pieces = []
    for h in range(N_Q_HEADS):
        q2 = q[:, (h // 2) * LANES:(h // 2 + 1) * LANES]
        src_lo = h % 2 == 0
        dst_lo = h // GQA_GROUP == 0
        if src_lo != dst_lo:
            q2 = pltpu.roll(q2, HEAD_DIM, 1)
        pieces.append(jnp.where(lane_lo == dst_lo, q2, 0.0).reshape(bt, t_new, LANES))
    q_rows = (jnp.concatenate(pieces, axis=1) * scale).astype(BF16)
    s_c = jnp.einsum("bqd,bdk->bqk", q_rows, kc_t.astype(BF16), preferred_element_type=F32)
    s_n = jnp.einsum("bqd,bkd->bqk", q_rows, kn.astype(BF16), preferred_element_type=F32)
    yield
    s_c = jnp.where(mask_c, s_c, NEG_INF)
    s_n = jnp.where(mask_n, s_n, NEG_INF)
    m = jnp.maximum(jnp.maximum(jnp.max(s_c, axis=-1, keepdims=True), jnp.max(s_n, axis=-1, keepdims=True)), sink)
    e_c, e_n = jnp.exp(s_c - m), jnp.exp(s_n - m)
    denom = jnp.sum(e_c, axis=-1, keepdims=True) + jnp.sum(e_n, axis=-1, keepdims=True) + jnp.exp(sink - m)
    pv = (jnp.einsum("bqk,bdk->bqd", e_c.astype(BF16), vc_t.astype(BF16), preferred_element_type=F32)
          + jnp.einsum("bqk,bkd->bqd", e_n.astype(BF16), vn.astype(BF16), preferred_element_type=F32)) / denom
    for j in range(Q_W // LANES):
        out = None
        for par in range(2):
            h = 2 * j + par
            dst_lo = h // GQA_GROUP == 0
            piece = pv[:, h * t_new:(h + 1) * t_new, :].reshape(bt * t_new, LANES)
            piece = jnp.where(lane_lo == dst_lo, piece, 0.0)
            if dst_lo != (par == 0):
                piece = pltpu.roll(piece, HEAD_DIM, 1)
            out = piece if out is None else out + piece
        o_ref[:, :, j * LANES:(j + 1) * LANES] = out.reshape(bt, t_new, LANES).astype(o_ref.dtype)
    keep = lax.broadcasted_iota(jnp.int32, (bt * KV_W, w), 1) < w - t_new
    for old_t, new, out_ref in ((kc_t, kn, nk_ref), (vc_t, vn, nv_ref)):
        shifted = pltpu.roll(old_t.reshape(bt * KV_W, w), w - t_new, 1)
        placed = jnp.concatenate([jnp.zeros((bt, w - t_new, KV_W), F32), new], axis=1)
        placed_t = jnp.swapaxes(placed, 1, 2).reshape(bt * KV_W, w)
        out_ref[...] = jnp.where(keep, shifted, placed_t).reshape(out_ref.shape)


def _ret_log_gamma(h):
    return math.log(1.0 - 2.0 ** (-5.0 - h))


def _ret_decays(chunk, h):
    lg = _ret_log_gamma(h)
    i = lax.broadcasted_iota(jnp.int32, (chunk, chunk), 0)
    j = lax.broadcasted_iota(jnp.int32, (chunk, chunk), 1)
    rel_mask = jnp.where(i >= j, jnp.exp((j + 1).astype(F32) * -lg), 0.0)
    col = lax.broadcasted_iota(jnp.int32, (chunk, 1), 0).astype(F32)
    gn_eps = GN_EPS * jnp.exp((col + 1.0) * (-2.0 * lg))
    k_decay = jnp.exp((chunk - 1.0 - col) * lg)
    return rel_mask, gn_eps, k_decay, math.exp(chunk * lg)


def _ret_chunk_scores(q, k, v, decays):
    heads = range(N_RET_HEADS)
    batched = q.ndim == 3
    mm = (lambda a, b: jnp.einsum("bmk,bkn->bmn", a, b, preferred_element_type=F32)) if batched else _dot
    mm_nt = (lambda a, b: jnp.einsum("bmk,bnk->bmn", a, b, preferred_element_type=F32)) if batched else _dot_nt
    qh = [q[..., h * RET_KEY_DIM:(h + 1) * RET_KEY_DIM] for h in heads]
    kh = [k[..., h * RET_KEY_DIM:(h + 1) * RET_KEY_DIM] for h in heads]
    vh = [v[..., h * RET_VAL_DIM:(h + 1) * RET_VAL_DIM] for h in heads]
    qb = [x.astype(BF16) for x in qh]
    vb = [x.astype(BF16) for x in vh]
    inner = [mm_nt(qb[h], kh[h].astype(BF16)) * decays[h][0] for h in heads]
    return batched, mm, qh, kh, vh, qb, vb, inner


def _ret_chunk_finish(scores, g, states, decays):
    batched, mm, qh, kh, vh, qb, vb, inner = scores
    heads = range(N_RET_HEADS)
    if batched:
        lhs = [jnp.concatenate([qh[h], inner[h]], axis=-1).astype(BF16) for h in heads]
        rhs = [jnp.concatenate([states[h], vh[h]], axis=-2).astype(BF16) for h in heads]
    else:
        lhs = [jnp.concatenate([qb[h], inner[h].astype(BF16)], axis=-1) for h in heads]
        rhs = [jnp.concatenate([states[h].astype(BF16), vb[h]], axis=-2) for h in heads]
    p = [mm(lhs[h], rhs[h]) for h in heads]
    kd_t = [jnp.swapaxes(kh[h].astype(F32) * decays[h][2], -1, -2).astype(BF16) for h in heads]
    new_states = [states[h] * decays[h][3] + mm(kd_t[h], vb[h]) for h in heads]
    outs = []
    for h in heads:
        mu = jnp.mean(p[h], axis=-1, keepdims=True)
        pc = p[h] - mu
        var = jnp.mean(pc * pc, axis=-1, keepdims=True)
        gate = g[..., h * RET_VAL_DIM:(h + 1) * RET_VAL_DIM].astype(F32)
        outs.append(pc * lax.rsqrt(var + decays[h][1]) * jax.nn.silu(gate))
    return jnp.concatenate(outs, axis=-1), new_states


def _ret_chunk_all_heads(q, k, v, g, states, decays):
    return _ret_chunk_finish(_ret_chunk_scores(q, k, v, decays), g, states, decays)


def _ret_prompt_kernel(q_ref, k_ref, v_ref, g_ref, o_ref, st_ref, state_ref):
    n = pl.program_id(1)
    C = RET_CHUNK

    @pl.when(n == 0)
    def _():
        state_ref[...] = jnp.zeros_like(state_ref)

    decays = [_ret_decays(C, h) for h in range(N_RET_HEADS)]
    states = [state_ref[h] for h in range(N_RET_HEADS)]
    for c in range(q_ref.shape[0] // C):
        rows = slice(c * C, (c + 1) * C)
        out, states = _ret_chunk_all_heads(q_ref[rows, :], k_ref[rows, :], v_ref[rows, :], g_ref[rows, :],
                                           states, decays)
        o_ref[rows, :] = out.astype(o_ref.dtype)
    for h in range(N_RET_HEADS):
        state_ref[h] = states[h]

    @pl.when(n == pl.num_programs(1) - 1)
    def _():
        st_ref[0] = state_ref[...]


def _ret_prompt(qr, kr, vr, gr, batch, seq):
    ts = RET_TILE
    ns = seq // ts
    row = lambda b, n: (b * ns + n, 0)
    return pl.pallas_call(
        _ret_prompt_kernel,
        grid=(batch, ns),
        in_specs=[pl.BlockSpec((ts, RQ_W), row), pl.BlockSpec((ts, RQ_W), row),
                  pl.BlockSpec((ts, RV_W), row), pl.BlockSpec((ts, RV_W), row)],
        out_specs=[pl.BlockSpec((ts, RV_W), row),
                   pl.BlockSpec((1, N_RET_HEADS, RET_KEY_DIM, RET_VAL_DIM), lambda b, n: (b, 0, 0, 0))],
        out_shape=[jax.ShapeDtypeStruct((batch * seq, RV_W), BF16),
                   jax.ShapeDtypeStruct((batch, N_RET_HEADS, RET_KEY_DIM, RET_VAL_DIM), F32)],
        scratch_shapes=[pltpu.VMEM((N_RET_HEADS, RET_KEY_DIM, RET_VAL_DIM), F32)],
        compiler_params=_cparams(2),
        name="ret_prompt",
    )(qr, kr, vr, gr)


def _ret_sample_stages(q_ref, k_ref, v_ref, g_ref, st_ref, o_ref, nst_ref):
    t_new = q_ref.shape[1]
    decays = [_ret_decays(t_new, h) for h in range(N_RET_HEADS)]
    scores = _ret_chunk_scores(q_ref[...], k_ref[...], v_ref[...], decays)
    yield
    states = [st_ref[:, h] for h in range(N_RET_HEADS)]
    out, states = _ret_chunk_finish(scores, g_ref[...], states, decays)
    o_ref[...] = out.astype(o_ref.dtype)
    for h in range(N_RET_HEADS):
        nst_ref[:, h] = states[h]


def _sample_mixers_stages(sink_rows_ref, qa_ref, kn_ref, vn_ref, ck_ref, cv_ref,
                          qr_ref, kr_ref, vr_ref, gr_ref, st_ref,
                          oa_ref, nk_ref, nv_ref, or_ref, nst_ref):
    attn = _attn_sample_stages(sink_rows_ref, qa_ref, kn_ref, vn_ref, ck_ref, cv_ref, oa_ref, nk_ref, nv_ref)
    ret = _ret_sample_stages(qr_ref, kr_ref, vr_ref, gr_ref, st_ref, or_ref, nst_ref)
    next(attn)
    next(ret)
    yield
    _run(attn)
    _run(ret)


def _sample_mixers_parts(qa, ka, va, sinks, cache_k, cache_v, qr, kr, vr, gr, state):
    db, t_new, _ = qa.shape
    w = cache_k.shape[3]
    cache = (N_KV_HEADS, HEAD_DIM, w)
    sink_rows = jnp.repeat(sinks.astype(F32), t_new).reshape(N_Q_HEADS * t_new, 1)

    def parts(n_steps):
        assert db % n_steps == 0
        bt = db // n_steps
        blk = lambda *s: pl.BlockSpec((bt,) + s, lambda i: (i,) + (0,) * len(s))
        st = blk(N_RET_HEADS, RET_KEY_DIM, RET_VAL_DIM)
        in_specs = [_const_spec(sink_rows.shape), blk(t_new, Q_W), blk(t_new, KV_W), blk(t_new, KV_W),
                    blk(*cache), blk(*cache),
                    blk(t_new, RQ_W), blk(t_new, RQ_W), blk(t_new, RV_W), blk(t_new, RV_W), st]
        out_specs = [blk(t_new, Q_W), blk(*cache), blk(*cache), blk(t_new, RV_W), st]
        out_shape = [jax.ShapeDtypeStruct((db, t_new, Q_W), F32),
                     jax.ShapeDtypeStruct((db,) + cache, F32),
                     jax.ShapeDtypeStruct((db,) + cache, F32),
                     jax.ShapeDtypeStruct((db, t_new, RV_W), F32),
                     jax.ShapeDtypeStruct(state.shape, state.dtype)]
        args = [sink_rows, qa, ka, va, cache_k, cache_v, qr, kr, vr, gr, state]
        return _sample_mixers_stages, in_specs, out_specs, out_shape, args

    return parts


def _merge_ln_kernel(alpha, ln_row, x_ref, oa_ref, orr_ref, ga_ref, gb_ref, wba_ref, wbr_ref, wo_ref,
                     g_ref, b_ref, o_ref):
    n_blocks = max(1, x_ref.shape[0] // MERGE_ROWS)
    rb = x_ref.shape[0] // n_blocks
    blocks = [slice(r * rb, (r + 1) * rb) for r in range(n_blocks)]
    ya = [_dot(oa_ref[rows, :].astype(BF16), wba_ref[...]) for rows in blocks]
    yr = [_dot(orr_ref[rows, :].astype(BF16), wbr_ref[...]) for rows in blocks]
    merged = [(jax.nn.sigmoid(ga_ref[rows, :].astype(F32)) * ya[r]
               + jax.nn.sigmoid(gb_ref[rows, :].astype(F32)) * yr[r]).astype(BF16)
              for r, rows in enumerate(blocks)]
    z = [_dot(merged[r], wo_ref[...]) for r in range(n_blocks)]
    for r, rows in enumerate(blocks):
        o_ref[rows, :] = _layer_norm(alpha * x_ref[rows, :] + z[r],
                                     g_ref[ln_row:ln_row + 1, :], b_ref[ln_row:ln_row + 1, :])


def _merge_ln(x1, oa, orr, ga, gb, wba, wbr, wo, ln_g, ln_b, ln_row, alpha):
    t, d = x1.shape
    tm = _dense_tile(t)
    row = lambda i: (i, 0)
    return pl.pallas_call(
        functools.partial(_merge_ln_kernel, alpha, ln_row),
        grid=(t // tm,),
        in_specs=[pl.BlockSpec((tm, d), row), pl.BlockSpec((tm, Q_W), row), pl.BlockSpec((tm, RV_W), row),
                  pl.BlockSpec((tm, d), row), pl.BlockSpec((tm, d), row),
                  _const_spec(wba.shape), _const_spec(wbr.shape), _const_spec(wo.shape),
                  _const_spec(ln_g.shape), _const_spec(ln_b.shape)],
        out_specs=pl.BlockSpec((tm, d), row),
        out_shape=jax.ShapeDtypeStruct((t, d), F32),
        compiler_params=_cparams(1),
        name="merge_ln",
    )(x1, oa, orr, ga, gb, wba, wbr, wo, ln_g, ln_b)


def _rope_tables_for(pos):
    tables = _rope_tables(pos, HEAD_DIM, _attn_inv_freq) + _rope_tables(pos, RET_KEY_DIM, _ret_inv_freq)
    period = pos.shape[0]
    if period < DENSE_TILE:
        tables = tuple(jnp.tile(tb, (DENSE_TILE // period, 1)) for tb in tables)
    return tables


def kernel(x_prompt, x_sample, cache_k, cache_v, state_ret, ln_g, ln_b, ffn_wi, ffn_wo, w_in,
           attn_sinks, w_br_attn, w_br_ret, w_o):
    B, S, D = x_prompt.shape
    DB, T, _ = x_sample.shape
    depth = ln_g.shape[0]
    alpha = (2.0 * depth) ** 0.25
    W = cache_k.shape[2]
    tables_p = _rope_tables_for(jnp.arange(S, dtype=jnp.int32))
    tables_s = _rope_tables_for(PAST_LEN + jnp.arange(T, dtype=jnp.int32))

    y_p = x_prompt.reshape(B * S, D)
    y_s = x_sample.reshape(DB * T, D)
    outs = [[] for _ in range(6)]
    for l in range(depth):
        wi, wo = ffn_wi[l], ffn_wo[l]
        sinks = attn_sinks[l].astype(F32)
        g_l, b_l = ln_g[l], ln_b[l]

        x1_p, x1_s, w_in_l, wba, wbr, w_o_l = _ffn_ln(
            y_p, y_s, wi, wo, 0, g_l, b_l, 0, alpha,
            cast_weights=(w_in[l], w_br_attn[l], w_br_ret[l], w_o[l]))
        qa_s, ka_s, va_s, qr_s, kr_s, vr_s, gr_s, ga_s, gb_s = _proj_rope(
            x1_s, w_in_l, tables_s, F32)

        r3 = lambda a: a.reshape(DB, T, a.shape[-1])
        to_lane_major = lambda c: jnp.transpose(c, (0, 2, 3, 1))
        sample_mixers = _sample_mixers_parts(
            r3(qa_s), r3(ka_s), r3(va_s), sinks, to_lane_major(cache_k[l]), to_lane_major(cache_v[l]),
            r3(qr_s), r3(kr_s), r3(vr_s), r3(gr_s), state_ret[l])
        w = min(WINDOW, S)
        (qa_p, ka_p, va_p, qr_p, kr_p, vr_p, gr_p, ga_p, gb_p, nk_p, nv_p), (oa_s, nk_s, nv_s, or_s, st_s) = \
            _proj_rope(x1_p, w_in_l, tables_p, BF16, rider=(RIDER_PROJ_TILE, sample_mixers), tail=(S, w))
        oa_s, or_s = oa_s.reshape(DB * T, Q_W), or_s.reshape(DB * T, RV_W)
        from_lane_major = lambda c: jnp.transpose(c, (0, 3, 1, 2))
        new_caches = [from_lane_major(nk_p.reshape(B, N_KV_HEADS, HEAD_DIM, w)),
                      from_lane_major(nv_p.reshape(B, N_KV_HEADS, HEAD_DIM, w)),
                      from_lane_major(nk_s), from_lane_major(nv_s)]

        oa_p = _attn_prompt(qa_p, ka_p, va_p, sinks, B, S)
        or_p, st_p = _ret_prompt(qr_p, kr_p, vr_p, gr_p, B, S)

        x2_p = _merge_ln(x1_p, oa_p, or_p, ga_p, gb_p, wba, wbr, w_o_l, g_l, b_l, 1, alpha)
        x2_s = _merge_ln(x1_s, oa_s, or_s, ga_s, gb_s, wba, wbr, w_o_l, g_l, b_l, 1, alpha)
        y_p, y_s = _ffn_ln(x2_p, x2_s, wi, wo, 1, g_l, b_l, 2, alpha)
        for lst, val in zip(outs, (new_caches[0], new_caches[1], st_p, new_caches[2], new_caches[3], st_s)):
            lst.append(val)
    return (y_p.reshape(B, S, D), y_s.reshape(DB, T, D)) + tuple(jnp.stack(o) for o in outs)
```

```python
import functools
import math

import jax
import jax.numpy as jnp
from jax import lax
from jax.experimental import pallas as pl
from jax.experimental.pallas import tpu as pltpu

F32 = jnp.float32
BF16 = jnp.bfloat16

PAST_LEN = 8192
N_Q_HEADS = 8
N_KV_HEADS = 2
HEAD_DIM = 64
GQA_GROUP = N_Q_HEADS // N_KV_HEADS
WINDOW = 128
ROPE_THETA = 10000.0
N_RET_HEADS = 4
RET_KEY_DIM = 128
RET_VAL_DIM = 256
RET_CHUNK = 128
RET_THETA = 10000.0
LN_EPS = 1e-5
GN_EPS = 1e-6
NEG_INF = -1e30

Q_W = N_Q_HEADS * HEAD_DIM
KV_W = N_KV_HEADS * HEAD_DIM
RQ_W = N_RET_HEADS * RET_KEY_DIM
RV_W = N_RET_HEADS * RET_VAL_DIM

LANES = 128
SUBLANES = 8
VMEM_LIMIT_BYTES = 56 * 1024 * 1024

TOKEN_TILE = 512
DENSE_TILE = 1024
DENSE_MIN_STEPS = 4
FF_CHUNK = 256
ATTN_TILE = 2048
RET_TILE = 1024
LN_ANCHOR_CHUNK = 8
MERGE_ROWS = 256
RIDER_PROJ_TILE = 512
RIDER_LEAD_GROUPS = 2
CAST_STEPS = 16


def _dot(a, b):
    return jnp.dot(a, b, preferred_element_type=F32)


def _dot_nt(a, b):
    return lax.dot_general(a, b, (((1,), (1,)), ((), ())), preferred_element_type=F32)


def _layer_norm(y, g, b):
    mu = jnp.mean(y, axis=-1, keepdims=True)
    yc = y - mu
    var = jnp.mean(yc * yc, axis=-1, keepdims=True)
    return yc * lax.rsqrt(var + LN_EPS) * g + b


def _cparams(n_axes, semantics=None):
    return pltpu.CompilerParams(
        dimension_semantics=semantics or ("arbitrary",) * n_axes,
        vmem_limit_bytes=VMEM_LIMIT_BYTES)


def _const_spec(shape):
    nd = len(shape)
    return pl.BlockSpec(shape, lambda *_: (0,) * nd, pipeline_mode=pl.Buffered(1))


def _segment_specs(n_prompt_tiles, tm, width, tile_of):
    prompt = pl.BlockSpec((tm, width), lambda i: (jnp.minimum(tile_of(i), n_prompt_tiles - 1), 0))
    sample = pl.BlockSpec((tm, width), lambda i: (jnp.maximum(tile_of(i) - n_prompt_tiles, 0), 0))
    return prompt, sample


def _store_segment(is_prompt, prompt_ref, sample_ref, value):
    @pl.when(is_prompt)
    def _():
        prompt_ref[...] = value.astype(prompt_ref.dtype)

    @pl.when(jnp.logical_not(is_prompt))
    def _():
        sample_ref[...] = value.astype(sample_ref.dtype)


def _ffn_chunk(xb, w_gate, w_up, w_out):
    act = (jax.nn.silu(_dot(xb, w_gate)) * _dot(xb, w_up)).astype(BF16)
    return _dot(act, w_out)


def _ffn_tile_maps(n_chunks, n_tiles):
    matmul_tile = lambda i: jnp.clip(i - (n_chunks - 1), 0, n_tiles - 1)
    norm_tile = lambda i: jnp.clip(i - n_chunks, 0, n_tiles - 1)
    return matmul_tile, norm_tile


def _ffn_ln_kernel(alpha, n_chunks, n_tiles, ln_row, n_prompt_tiles, n_cast, *refs):
    xp_ref, xs_ref, wg_ref, wu_ref, wo_ref, g_ref, b_ref = refs[:7]
    cast_in = refs[7:7 + n_cast]
    op_ref, os_ref = refs[7 + n_cast:9 + n_cast]
    cast_out = refs[9 + n_cast:9 + 2 * n_cast]
    acc_ref, y_ref, wi_bf_ref, wo_bf_ref = refs[9 + 2 * n_cast:]
    step = pl.program_id(0)
    matmul_tile, norm_tile = _ffn_tile_maps(n_chunks, n_tiles)
    last_step = n_chunks + n_tiles - 1
    fc = FF_CHUNK

    norm_is_prompt = norm_tile(step) < n_prompt_tiles

    def norm_previous():
        return _layer_norm(y_ref[...], g_ref[ln_row:ln_row + 1, :], b_ref[ln_row:ln_row + 1, :])

    @pl.when(step < n_chunks)
    def _():
        w_gate, w_up, w_out = wg_ref[...].astype(BF16), wu_ref[...].astype(BF16), wo_ref[...].astype(BF16)
        wi_bf_ref[step, :, :fc] = w_gate
        wi_bf_ref[step, :, fc:] = w_up
        wo_bf_ref[step] = w_out
        x = xp_ref[...]
        part = _ffn_chunk(x.astype(BF16), w_gate, w_up, w_out)

        @pl.when(step == 0)
        def _():
            acc_ref[...] = part

        @pl.when(step > 0)
        def _():
            acc_ref[...] += part

        @pl.when(step == n_chunks - 1)
        def _():
            y_ref[...] = alpha * x + 0.5 * acc_ref[...]

    def full_tile_step(out_ref):
        normed = norm_previous()
        out_ref[...] = normed
        rows, width = normed.shape
        anchor = jnp.max(normed.reshape(rows // SUBLANES, SUBLANES, width), axis=0)
        anchor = functools.reduce(jnp.maximum, [anchor[:, j * LANES:(j + 1) * LANES]
                                                for j in range(width // LANES)])
        zero = jnp.minimum(jnp.abs(anchor), 0.0)
        x = jnp.where(matmul_tile(step) < n_prompt_tiles, xp_ref[...], xs_ref[...])
        xb = x.astype(BF16)
        for c in range(n_chunks):
            part = _ffn_chunk(xb, wi_bf_ref[c, :, :fc], wi_bf_ref[c, :, fc:], wo_bf_ref[c])
            if c == 0:
                acc_ref[...] = part
            else:
                acc_ref[...] += part
            if c == min(LN_ANCHOR_CHUNK, n_chunks - 1):
                acc_ref[0:SUBLANES, 0:LANES] += zero
        y_ref[...] = alpha * x + 0.5 * acc_ref[...]

    in_full_steps = (step >= n_chunks) & (step < last_step)

    if n_cast:
        @pl.when((step >= n_chunks) & (step < n_chunks + CAST_STEPS))
        def _():
            for src, dst in zip(cast_in, cast_out):
                dst[...] = src[...].astype(dst.dtype)

    pl.when(in_full_steps & norm_is_prompt)(functools.partial(full_tile_step, op_ref))
    pl.when(in_full_steps & jnp.logical_not(norm_is_prompt))(functools.partial(full_tile_step, os_ref))

    @pl.when(step == last_step)
    def _():
        _store_segment(norm_is_prompt, op_ref, os_ref, norm_previous())


def _ffn_ln(xp, xs, wi, wo, half, ln_g, ln_b, ln_row, alpha, cast_weights=()):
    d = xp.shape[1]
    d_ff = wo.shape[1]
    tm, fc = TOKEN_TILE, FF_CHUNK
    assert xp.shape[0] % tm == 0 and xs.shape[0] % tm == 0 and d_ff % fc == 0
    n_p, n_s = xp.shape[0] // tm, xs.shape[0] // tm
    n_chunks = d_ff // fc
    n_tiles = n_p + n_s
    assert n_tiles - 1 >= CAST_STEPS
    matmul_tile, norm_tile = _ffn_tile_maps(n_chunks, n_tiles)
    chunk_of = lambda i: jnp.minimum(i, n_chunks - 1)
    cast_specs = []
    for w in cast_weights:
        rows = w.shape[0] // CAST_STEPS
        assert w.shape[0] % CAST_STEPS == 0 and rows % 16 == 0
        cast_specs.append(pl.BlockSpec((rows, w.shape[1]),
                                       lambda i: (jnp.clip(i - n_chunks, 0, CAST_STEPS - 1), 0)))
    return pl.pallas_call(
        functools.partial(_ffn_ln_kernel, alpha, n_chunks, n_tiles, ln_row, n_p, len(cast_weights)),
        grid=(n_chunks + n_tiles,),
        in_specs=[*_segment_specs(n_p, tm, d, matmul_tile),
                  pl.BlockSpec((None, d, fc), lambda i: (half, 0, chunk_of(i))),
                  pl.BlockSpec((None, d, fc), lambda i: (half, 0, n_chunks + chunk_of(i))),
                  pl.BlockSpec((None, fc, d), lambda i: (half, chunk_of(i), 0)),
                  _const_spec(ln_g.shape), _const_spec(ln_b.shape), *cast_specs],
        out_specs=[*_segment_specs(n_p, tm, d, norm_tile), *cast_specs],
        out_shape=[jax.ShapeDtypeStruct(xp.shape, F32), jax.ShapeDtypeStruct(xs.shape, F32)]
                  + [jax.ShapeDtypeStruct(w.shape, BF16) for w in cast_weights],
        scratch_shapes=[pltpu.VMEM((tm, d), F32), pltpu.VMEM((tm, d), F32),
                        pltpu.VMEM((n_chunks, d, 2 * fc), BF16),
                        pltpu.VMEM((n_chunks, fc, d), BF16)],
        compiler_params=_cparams(1),
        name="ffn_ln",
    )(xp, xs, wi, wi, wo, ln_g, ln_b, *cast_weights)


def _rope_tables(pos, dim, theta_fn):
    half = dim // 2
    inv_freq = theta_fn(half)
    ang = pos.astype(F32)[:, None] * inv_freq[None, :]
    reps = LANES // half
    cos = jnp.tile(jnp.cos(ang), (1, reps))
    sign = jnp.tile(jnp.concatenate([-jnp.ones((half,), F32), jnp.ones((half,), F32)]), LANES // dim)
    sin = jnp.tile(jnp.sin(ang), (1, reps)) * sign[None, :]
    return cos, sin


def _attn_inv_freq(half):
    return 1.0 / (ROPE_THETA ** (jnp.arange(0, 2 * half, 2, dtype=F32) / (2 * half)))


def _ret_inv_freq(half):
    return 1.0 / (RET_THETA ** jnp.linspace(0.0, 1.0, half, dtype=F32))


def _rope_block(x, cos, sin, dim):
    half = dim // 2
    if dim == LANES:
        rot = pltpu.roll(x, half, 1)
    else:
        lane = lax.broadcasted_iota(jnp.int32, x.shape, 1)
        first_half = (lane % dim) < half
        rot = jnp.where(first_half, pltpu.roll(x, LANES - half, 1), pltpu.roll(x, half, 1))
    return x * cos + rot * sin


def _run(stages):
    for _ in stages:
        pass


def _proj_rope_stages(x_ref, w_ref, cosa_ref, sina_ref, cosr_ref, sinr_ref,
                      qa_ref, ka_ref, va_ref, qr_ref, kr_ref, vr_ref, gr_ref, ga_ref, gb_ref,
                      k_tail_ref=None, v_tail_ref=None):
    xb = x_ref[...].astype(BF16)
    cosa, sina = cosa_ref[...], sina_ref[...]
    cosr, sinr = cosr_ref[...], sinr_ref[...]
    off = 0

    y = _dot(xb, w_ref[:, off:off + Q_W])
    for j in range(Q_W // LANES):
        qa_ref[:, j * LANES:(j + 1) * LANES] = _rope_block(
            y[:, j * LANES:(j + 1) * LANES], cosa, sina, HEAD_DIM).astype(qa_ref.dtype)
    off += Q_W
    yield
    y = _dot(xb, w_ref[:, off:off + 2 * KV_W])
    k_rot = _rope_block(y[:, :KV_W], cosa, sina, HEAD_DIM)
    ka_ref[...] = k_rot
    va_ref[...] = y[:, KV_W:]
    if k_tail_ref is not None:
        tail = k_tail_ref.shape[1]
        k_tail_ref[...] = k_rot[-tail:, :].T
        v_tail_ref[...] = y[-tail:, KV_W:].T
    off += 2 * KV_W
    yield
    y = _dot(xb, w_ref[:, off:off + RQ_W])
    for j in range(RQ_W // LANES):
        qr_ref[:, j * LANES:(j + 1) * LANES] = _rope_block(
            y[:, j * LANES:(j + 1) * LANES], cosr, sinr, RET_KEY_DIM).astype(qr_ref.dtype)
    off += RQ_W
    yield
    y = _dot(xb, w_ref[:, off:off + RQ_W])
    for j in range(RQ_W // LANES):
        kr = _rope_block(y[:, j * LANES:(j + 1) * LANES], cosr, sinr, RET_KEY_DIM)
        kr_ref[:, j * LANES:(j + 1) * LANES] = (kr * (RET_KEY_DIM ** -0.5)).astype(kr_ref.dtype)
    off += RQ_W
    yield
    for ref in (vr_ref, gr_ref, ga_ref, gb_ref):
        width = ref.shape[1]
        for c in range(width // 512):
            ref[:, c * 512:(c + 1) * 512] = _dot(
                xb, w_ref[:, off + c * 512:off + (c + 1) * 512]).astype(ref.dtype)
            yield
        off += width


def _proj_rope_kernel(*refs):
    _run(_proj_rope_stages(*refs))


def _dense_tile(tokens):
    return min(DENSE_TILE, tokens // DENSE_MIN_STEPS)


N_PROJ_IN = 6


def _proj_with_rider_kernel(rider_stages, n_rider_in, n_proj_out, *refs):
    proj_in, rest = refs[:N_PROJ_IN], refs[N_PROJ_IN:]
    rider_in, rest = rest[:n_rider_in], rest[n_rider_in:]
    proj_out, rider_out = rest[:n_proj_out], rest[n_proj_out:]
    rider = rider_stages(*rider_in, *rider_out)
    proj = _proj_rope_stages(*proj_in, *proj_out)
    next(rider)
    for _ in range(RIDER_LEAD_GROUPS):
        next(proj)
    _run(rider)
    _run(proj)


def _proj_rope(x1, w_in, tables, mixer_dtype, rider=None, tail=None):
    t, d = x1.shape
    n_in = w_in.shape[1]
    d_model = (n_in - Q_W - 2 * KV_W - 2 * RQ_W - 2 * RV_W) // 2
    tm = _dense_tile(t) if rider is None else rider[0]
    row = lambda i: (i, 0)
    table_blocks = tables[0].shape[0] // tm
    tab = lambda i: (i % table_blocks, 0)
    widths = [(Q_W, mixer_dtype), (KV_W, F32), (KV_W, F32), (RQ_W, mixer_dtype), (RQ_W, mixer_dtype),
              (RV_W, mixer_dtype), (RV_W, mixer_dtype), (d_model, BF16), (d_model, BF16)]
    in_specs = [pl.BlockSpec((tm, d), row), _const_spec(w_in.shape)] + [pl.BlockSpec((tm, LANES), tab)] * 4
    out_specs = [pl.BlockSpec((tm, w), row) for w, _ in widths]
    out_shape = [jax.ShapeDtypeStruct((t, w), dt) for w, dt in widths]
    if tail is not None:
        seq_len, tail_rows = tail
        assert seq_len % tm == 0 and tail_rows <= tm
        tiles_per_seq = seq_len // tm
        out_specs += [pl.BlockSpec((KV_W, tail_rows), lambda i: (i // tiles_per_seq, 0))] * 2
        out_shape += [jax.ShapeDtypeStruct((t // seq_len * KV_W, tail_rows), F32)] * 2
    n_proj_out = len(out_specs)
    args = [x1, w_in, *tables]
    body = _proj_rope_kernel
    if rider is not None:
        r_stages, r_in_specs, r_out_specs, r_out_shape, r_args = rider[1](t // tm)
        body = functools.partial(_proj_with_rider_kernel, r_stages, len(r_args), n_proj_out)
        in_specs, out_specs = in_specs + r_in_specs, out_specs + r_out_specs
        out_shape, args = out_shape + r_out_shape, args + r_args
    outs = pl.pallas_call(
        body,
        grid=(t // tm,),
        in_specs=in_specs,
        out_specs=out_specs,
        out_shape=out_shape,
        compiler_params=_cparams(1),
        name="proj_rope",
    )(*args)
    return outs if rider is None else (outs[:n_proj_out], outs[n_proj_out:])


def _kv_lane_variants(x):
    lane = lax.broadcasted_iota(jnp.int32, x.shape, 1)
    lo = lane < HEAD_DIM
    swapped = pltpu.roll(x, HEAD_DIM, 1)
    zero = jnp.zeros_like(x)
    kv0 = (jnp.where(lo, x, zero).astype(BF16), jnp.where(lo, zero, swapped).astype(BF16))
    kv1 = (jnp.where(lo, swapped, zero).astype(BF16), jnp.where(lo, zero, x).astype(BF16))
    return kv0, kv1


def _attn_prompt_kernel(sinks_ref, q_ref, kc_ref, kp_ref, vc_ref, vp_ref, o_ref):
    n = pl.program_id(1)
    L = WINDOW
    k_full = jnp.concatenate([kp_ref[...], kc_ref[...]], axis=0)
    v_full = jnp.concatenate([vp_ref[...], vc_ref[...]], axis=0)
    v_var = _kv_lane_variants(v_full)
    k_t = k_full.T
    k_t_swapped = jnp.concatenate([k_t[HEAD_DIM:], k_t[:HEAD_DIM]], axis=0)
    row_lo = lax.broadcasted_iota(jnp.int32, k_t.shape, 0) < HEAD_DIM
    zero_t = jnp.zeros_like(k_t)
    k_var = ((jnp.where(row_lo, k_t, zero_t).astype(BF16), jnp.where(row_lo, zero_t, k_t_swapped).astype(BF16)),
             (jnp.where(row_lo, k_t_swapped, zero_t).astype(BF16), jnp.where(row_lo, zero_t, k_t).astype(BF16)))
    own = (lax.broadcasted_iota(jnp.int32, (L, L), 1)
           <= lax.broadcasted_iota(jnp.int32, (L, L), 0))
    scale = HEAD_DIM ** -0.5

    units = [(i, j, par) for i in range(q_ref.shape[0] // L) for j in range(Q_W // LANES) for par in range(2)]

    def scores(unit):
        i, j, par = unit
        q2 = q_ref[i * L:(i + 1) * L, j * LANES:(j + 1) * LANES] * scale
        return _dot(q2, k_var[(2 * j) // GQA_GROUP][par][:, i * L:(i + 2) * L])

    s_next = scores(units[0])
    out = None
    for idx, (i, j, par) in enumerate(units):
        s = s_next
        if idx + 1 < len(units):
            s_next = scores(units[idx + 1])
        s_prev = s[:, :L]
        if i == 0:
            s_prev = jnp.where(n > 0, s_prev, NEG_INF)
        sw = jnp.where(own, s[:, L:], s_prev)
        sink = sinks_ref[2 * j + par]
        m = jnp.maximum(jnp.max(sw, axis=-1, keepdims=True), sink)
        e = jnp.exp(sw - m)
        denom = jnp.sum(e, axis=-1, keepdims=True) + jnp.exp(sink - m)
        e_band = jnp.concatenate([jnp.where(own, 0.0, e), jnp.where(own, e, 0.0)], axis=1)
        r = _dot(e_band.astype(BF16), v_var[(2 * j) // GQA_GROUP][par][i * L:(i + 2) * L]) / denom
        if par == 0:
            out = r
        else:
            o_ref[i * L:(i + 1) * L, j * LANES:(j + 1) * LANES] = (out + r).astype(o_ref.dtype)


def _attn_prompt(qa, ka, va, sinks, batch, seq):
    tq = ATTN_TILE
    nq = seq // tq
    per = tq // WINDOW
    cur = lambda b, n: (b * nq + n, 0)
    prev = lambda b, n: ((b * nq + n) * per - jnp.minimum(n, 1), 0)
    return pl.pallas_call(
        _attn_prompt_kernel,
        grid=(batch, nq),
        in_specs=[pl.BlockSpec(memory_space=pltpu.SMEM),
                  pl.BlockSpec((tq, Q_W), cur),
                  pl.BlockSpec((tq, KV_W), cur), pl.BlockSpec((WINDOW, KV_W), prev),
                  pl.BlockSpec((tq, KV_W), cur), pl.BlockSpec((WINDOW, KV_W), prev)],
        out_specs=pl.BlockSpec((tq, Q_W), cur),
        out_shape=jax.ShapeDtypeStruct(qa.shape, BF16),
        compiler_params=_cparams(2),
        name="attn_prompt",
    )(sinks, qa, ka, ka, va, va)


def _attn_sample_stages(sink_rows_ref, q_ref, kn_ref, vn_ref, ck_ref, cv_ref,
                        o_ref, nk_ref, nv_ref):
    bt, t_new, _ = q_ref.shape
    assert t_new == SUBLANES
    w = ck_ref.shape[3]
    n_rows = N_Q_HEADS * t_new
    lane_lo = lax.broadcasted_iota(jnp.int32, (bt * t_new, LANES), 1) < HEAD_DIM
    t_c = lax.broadcasted_iota(jnp.int32, (n_rows, w), 0) % t_new
    mask_c = lax.broadcasted_iota(jnp.int32, (n_rows, w), 1) > t_c + (w - WINDOW)
    t_n = lax.broadcasted_iota(jnp.int32, (n_rows, t_new), 0) % t_new
    mask_n = lax.broadcasted_iota(jnp.int32, (n_rows, t_new), 1) <= t_n
    sink = sink_rows_ref[...]
    scale = HEAD_DIM ** -0.5
    kn, vn = kn_ref[...], vn_ref[...]
    kc_t = ck_ref[...].reshape(bt, KV_W, w)
    vc_t = cv_ref[...].reshape(bt, KV_W, w)
    q = q_ref[...].reshape(bt * t_new, Q_W)
    pieces = []
    for h in range(N_Q_HEADS):
        q2 = q[:, (h // 2) * LANES:(h // 2 + 1) * LANES]
        src_lo = h % 2 == 0
        dst_lo = h // GQA_GROUP == 0
        if src_lo != dst_lo:
            q2 = pltpu.roll(q2, HEAD_DIM, 1)
        pieces.append(jnp.where(lane_lo == dst_lo, q2, 0.0).reshape(bt, t_new, LANES))
    q_rows = (jnp.concatenate(pieces, axis=1) * scale).astype(BF16)
    s_c = jnp.einsum("bqd,bdk->bqk", q_rows, kc_t.astype(BF16), preferred_element_type=F32)
    s_n = jnp.einsum("bqd,bkd->bqk", q_rows, kn.astype(BF16), preferred_element_type=F32)
    yield
    s_c = jnp.where(mask_c, s_c, NEG_INF)
    s_n = jnp.where(mask_n, s_n, NEG_INF)
    m = jnp.maximum(jnp.maximum(jnp.max(s_c, axis=-1, keepdims=True), jnp.max(s_n, axis=-1, keepdims=True)), sink)
    e_c, e_n = jnp.exp(s_c - m), jnp.exp(s_n - m)
    denom = jnp.sum(e_c, axis=-1, keepdims=True) + jnp.sum(e_n, axis=-1, keepdims=True) + jnp.exp(sink - m)
    pv = (jnp.einsum("bqk,bdk->bqd", e_c.astype(BF16), vc_t.astype(BF16), preferred_element_type=F32)
          + jnp.einsum("bqk,bkd->bqd", e_n.astype(BF16), vn.astype(BF16), preferred_element_type=F32)) / denom
    for j in range(Q_W // LANES):
        out = None
        for par in range(2):
            h = 2 * j + par
            dst_lo = h // GQA_GROUP == 0
            piece = pv[:, h * t_new:(h + 1) * t_new, :].reshape(bt * t_new, LANES)
            piece = jnp.where(lane_lo == dst_lo, piece, 0.0)
            if dst_lo != (par == 0):
                piece = pltpu.roll(piece, HEAD_DIM, 1)
            out = piece if out is None else out + piece
        o_ref[:, :, j * LANES:(j + 1) * LANES] = out.reshape(bt, t_new, LANES).astype(o_ref.dtype)
    keep = lax.broadcasted_iota(jnp.int32, (bt * KV_W, w), 1) < w - t_new
    for old_t, new, out_ref in ((kc_t, kn, nk_ref), (vc_t, vn, nv_ref)):
        shifted = pltpu.roll(old_t.reshape(bt * KV_W, w), w - t_new, 1)
        placed = jnp.concatenate([jnp.zeros((bt, w - t_new, KV_W), F32), new], axis=1)
        placed_t = jnp.swapaxes(placed, 1, 2).reshape(bt * KV_W, w)
        out_ref[...] = jnp.where(keep, shifted, placed_t).reshape(out_ref.shape)


def _ret_log_gamma(h):
    return math.log(1.0 - 2.0 ** (-5.0 - h))


def _ret_decays(chunk, h):
    lg = _ret_log_gamma(h)
    i = lax.broadcasted_iota(jnp.int32, (chunk, chunk), 0)
    j = lax.broadcasted_iota(jnp.int32, (chunk, chunk), 1)
    rel_mask = jnp.where(i >= j, jnp.exp((j + 1).astype(F32) * -lg), 0.0)
    col = lax.broadcasted_iota(jnp.int32, (chunk, 1), 0).astype(F32)
    gn_eps = GN_EPS * jnp.exp((col + 1.0) * (-2.0 * lg))
    k_decay = jnp.exp((chunk - 1.0 - col) * lg)
    return rel_mask, gn_eps, k_decay, math.exp(chunk * lg)


def _ret_chunk_scores(q, k, v, decays):
    heads = range(N_RET_HEADS)
    batched = q.ndim == 3
    mm = (lambda a, b: jnp.einsum("bmk,bkn->bmn", a, b, preferred_element_type=F32)) if batched else _dot
    mm_nt = (lambda a, b: jnp.einsum("bmk,bnk->bmn", a, b, preferred_element_type=F32)) if batched else _dot_nt
    qh = [q[..., h * RET_KEY_DIM:(h + 1) * RET_KEY_DIM] for h in heads]
    kh = [k[..., h * RET_KEY_DIM:(h + 1) * RET_KEY_DIM] for h in heads]
    vh = [v[..., h * RET_VAL_DIM:(h + 1) * RET_VAL_DIM] for h in heads]
    qb = [x.astype(BF16) for x in qh]
    vb = [x.astype(BF16) for x in vh]
    inner = [mm_nt(qb[h], kh[h].astype(BF16)) * decays[h][0] for h in heads]
    return batched, mm, qh, kh, vh, qb, vb, inner


def _ret_chunk_finish(scores, g, states, decays):
    batched, mm, qh, kh, vh, qb, vb, inner = scores
    heads = range(N_RET_HEADS)
    if batched:
        lhs = [jnp.concatenate([qh[h], inner[h]], axis=-1).astype(BF16) for h in heads]
        rhs = [jnp.concatenate([states[h], vh[h]], axis=-2).astype(BF16) for h in heads]
    else:
        lhs = [jnp.concatenate([qb[h], inner[h].astype(BF16)], axis=-1) for h in heads]
        rhs = [jnp.concatenate([states[h].astype(BF16), vb[h]], axis=-2) for h in heads]
    p = [mm(lhs[h], rhs[h]) for h in heads]
    kd_t = [jnp.swapaxes(kh[h].astype(F32) * decays[h][2], -1, -2).astype(BF16) for h in heads]
    new_states = [states[h] * decays[h][3] + mm(kd_t[h], vb[h]) for h in heads]
    outs = []
    for h in heads:
        mu = jnp.mean(p[h], axis=-1, keepdims=True)
        pc = p[h] - mu
        var = jnp.mean(pc * pc, axis=-1, keepdims=True)
        normed = pc * lax.rsqrt(var + decays[h][1])
        if g is not None:
            normed = normed * jax.nn.silu(g[..., h * RET_VAL_DIM:(h + 1) * RET_VAL_DIM].astype(F32))
        outs.append(normed)
    return jnp.concatenate(outs, axis=-1), new_states


def _ret_chunk_all_heads(q, k, v, g, states, decays):
    return _ret_chunk_finish(_ret_chunk_scores(q, k, v, decays), g, states, decays)


def _ret_prompt_kernel(q_ref, k_ref, v_ref, o_ref, st_ref, state_ref):
    n = pl.program_id(1)
    C = RET_CHUNK

    @pl.when(n == 0)
    def _():
        state_ref[...] = jnp.zeros_like(state_ref)

    decays = [_ret_decays(C, h) for h in range(N_RET_HEADS)]
    states = [state_ref[h] for h in range(N_RET_HEADS)]
    for c in range(q_ref.shape[0] // C):
        rows = slice(c * C, (c + 1) * C)
        out, states = _ret_chunk_all_heads(q_ref[rows, :], k_ref[rows, :], v_ref[rows, :], None, states, decays)
        o_ref[rows, :] = out.astype(o_ref.dtype)
    for h in range(N_RET_HEADS):
        state_ref[h] = states[h]

    @pl.when(n == pl.num_programs(1) - 1)
    def _():
        st_ref[0] = state_ref[...]


def _ret_prompt(qr, kr, vr, batch, seq):
    ts = RET_TILE
    ns = seq // ts
    row = lambda b, n: (b * ns + n, 0)
    return pl.pallas_call(
        _ret_prompt_kernel,
        grid=(batch, ns),
        in_specs=[pl.BlockSpec((ts, RQ_W), row), pl.BlockSpec((ts, RQ_W), row), pl.BlockSpec((ts, RV_W), row)],
        out_specs=[pl.BlockSpec((ts, RV_W), row),
                   pl.BlockSpec((1, N_RET_HEADS, RET_KEY_DIM, RET_VAL_DIM), lambda b, n: (b, 0, 0, 0))],
        out_shape=[jax.ShapeDtypeStruct((batch * seq, RV_W), BF16),
                   jax.ShapeDtypeStruct((batch, N_RET_HEADS, RET_KEY_DIM, RET_VAL_DIM), F32)],
        scratch_shapes=[pltpu.VMEM((N_RET_HEADS, RET_KEY_DIM, RET_VAL_DIM), F32)],
        compiler_params=_cparams(2),
        name="ret_prompt",
    )(qr, kr, vr)


def _ret_sample_stages(q_ref, k_ref, v_ref, g_ref, st_ref, o_ref, nst_ref):
    t_new = q_ref.shape[1]
    decays = [_ret_decays(t_new, h) for h in range(N_RET_HEADS)]
    scores = _ret_chunk_scores(q_ref[...], k_ref[...], v_ref[...], decays)
    yield
    states = [st_ref[:, h] for h in range(N_RET_HEADS)]
    out, states = _ret_chunk_finish(scores, g_ref[...], states, decays)
    o_ref[...] = out.astype(o_ref.dtype)
    for h in range(N_RET_HEADS):
        nst_ref[:, h] = states[h]


def _sample_mixers_stages(sink_rows_ref, qa_ref, kn_ref, vn_ref, ck_ref, cv_ref,
                          qr_ref, kr_ref, vr_ref, gr_ref, st_ref,
                          oa_ref, nk_ref, nv_ref, or_ref, nst_ref):
    attn = _attn_sample_stages(sink_rows_ref, qa_ref, kn_ref, vn_ref, ck_ref, cv_ref, oa_ref, nk_ref, nv_ref)
    ret = _ret_sample_stages(qr_ref, kr_ref, vr_ref, gr_ref, st_ref, or_ref, nst_ref)
    next(attn)
    next(ret)
    yield
    _run(attn)
    _run(ret)


def _sample_mixers_parts(qa, ka, va, sinks, cache_k, cache_v, qr, kr, vr, gr, state):
    db, t_new, _ = qa.shape
    w = cache_k.shape[3]
    cache = (N_KV_HEADS, HEAD_DIM, w)
    sink_rows = jnp.repeat(sinks.astype(F32), t_new).reshape(N_Q_HEADS * t_new, 1)

    def parts(n_steps):
        assert db % n_steps == 0
        bt = db // n_steps
        blk = lambda *s: pl.BlockSpec((bt,) + s, lambda i: (i,) + (0,) * len(s))
        st = blk(N_RET_HEADS, RET_KEY_DIM, RET_VAL_DIM)
        in_specs = [_const_spec(sink_rows.shape), blk(t_new, Q_W), blk(t_new, KV_W), blk(t_new, KV_W),
                    blk(*cache), blk(*cache),
                    blk(t_new, RQ_W), blk(t_new, RQ_W), blk(t_new, RV_W), blk(t_new, RV_W), st]
        out_specs = [blk(t_new, Q_W), blk(*cache), blk(*cache), blk(t_new, RV_W), st]
        out_shape = [jax.ShapeDtypeStruct((db, t_new, Q_W), F32),
                     jax.ShapeDtypeStruct((db,) + cache, F32),
                     jax.ShapeDtypeStruct((db,) + cache, F32),
                     jax.ShapeDtypeStruct((db, t_new, RV_W), F32),
                     jax.ShapeDtypeStruct(state.shape, state.dtype)]
        args = [sink_rows, qa, ka, va, cache_k, cache_v, qr, kr, vr, gr, state]
        return _sample_mixers_stages, in_specs, out_specs, out_shape, args

    return parts


def _merge_ln_kernel(alpha, ln_row, x_ref, oa_ref, orr_ref, gr_ref, ga_ref, gb_ref, wba_ref, wbr_ref, wo_ref,
                     g_ref, b_ref, o_ref):
    n_blocks = max(1, x_ref.shape[0] // MERGE_ROWS)
    rb = x_ref.shape[0] // n_blocks
    blocks = [slice(r * rb, (r + 1) * rb) for r in range(n_blocks)]

    def ret_branch(rows):
        if gr_ref is None:
            return orr_ref[rows, :].astype(BF16)
        return (orr_ref[rows, :].astype(F32) * jax.nn.silu(gr_ref[rows, :].astype(F32))).astype(BF16)

    ya = [_dot(oa_ref[rows, :].astype(BF16), wba_ref[...]) for rows in blocks]
    yr = [_dot(ret_branch(rows), wbr_ref[...]) for rows in blocks]
    merged = [(jax.nn.sigmoid(ga_ref[rows, :].astype(F32)) * ya[r]
               + jax.nn.sigmoid(gb_ref[rows, :].astype(F32)) * yr[r]).astype(BF16)
              for r, rows in enumerate(blocks)]
    z = [_dot(merged[r], wo_ref[...]) for r in range(n_blocks)]
    for r, rows in enumerate(blocks):
        o_ref[rows, :] = _layer_norm(alpha * x_ref[rows, :] + z[r],
                                     g_ref[ln_row:ln_row + 1, :], b_ref[ln_row:ln_row + 1, :])


def _merge_ln_gated_kernel(alpha, ln_row, *refs):
    _merge_ln_kernel(alpha, ln_row, *refs)


def _merge_ln_ungated_kernel(alpha, ln_row, x_ref, oa_ref, orr_ref, *refs):
    _merge_ln_kernel(alpha, ln_row, x_ref, oa_ref, orr_ref, None, *refs)


def _merge_ln(x1, oa, orr, gr, ga, gb, wba, wbr, wo, ln_g, ln_b, ln_row, alpha):
    t, d = x1.shape
    tm = _dense_tile(t)
    row = lambda i: (i, 0)
    gate_specs = [] if gr is None else [pl.BlockSpec((tm, RV_W), row)]
    gate_args = [] if gr is None else [gr]
    body = _merge_ln_ungated_kernel if gr is None else _merge_ln_gated_kernel
    return pl.pallas_call(
        functools.partial(body, alpha, ln_row),
        grid=(t // tm,),
        in_specs=[pl.BlockSpec((tm, d), row), pl.BlockSpec((tm, Q_W), row), pl.BlockSpec((tm, RV_W), row),
                  *gate_specs, pl.BlockSpec((tm, d), row), pl.BlockSpec((tm, d), row),
                  _const_spec(wba.shape), _const_spec(wbr.shape), _const_spec(wo.shape),
                  _const_spec(ln_g.shape), _const_spec(ln_b.shape)],
        out_specs=pl.BlockSpec((tm, d), row),
        out_shape=jax.ShapeDtypeStruct((t, d), F32),
        compiler_params=_cparams(1),
        name="merge_ln",
    )(x1, oa, orr, *gate_args, ga, gb, wba, wbr, wo, ln_g, ln_b)


def _rope_tables_for(pos):
    tables = _rope_tables(pos, HEAD_DIM, _attn_inv_freq) + _rope_tables(pos, RET_KEY_DIM, _ret_inv_freq)
    period = pos.shape[0]
    if period < DENSE_TILE:
        tables = tuple(jnp.tile(tb, (DENSE_TILE // period, 1)) for tb in tables)
    return tables


def kernel(x_prompt, x_sample, cache_k, cache_v, state_ret, ln_g, ln_b, ffn_wi, ffn_wo, w_in,
           attn_sinks, w_br_attn, w_br_ret, w_o):
    B, S, D = x_prompt.shape
    DB, T, _ = x_sample.shape
    depth = ln_g.shape[0]
    alpha = (2.0 * depth) ** 0.25
    W = cache_k.shape[2]
    tables_p = _rope_tables_for(jnp.arange(S, dtype=jnp.int32))
    tables_s = _rope_tables_for(PAST_LEN + jnp.arange(T, dtype=jnp.int32))

    y_p = x_prompt.reshape(B * S, D)
    y_s = x_sample.reshape(DB * T, D)
    outs = [[] for _ in range(6)]
    for l in range(depth):
        wi, wo = ffn_wi[l], ffn_wo[l]
        sinks = attn_sinks[l].astype(F32)
        g_l, b_l = ln_g[l], ln_b[l]

        x1_p, x1_s, w_in_l, wba, wbr, w_o_l = _ffn_ln(
            y_p, y_s, wi, wo, 0, g_l, b_l, 0, alpha,
            cast_weights=(w_in[l], w_br_attn[l], w_br_ret[l], w_o[l]))
        qa_s, ka_s, va_s, qr_s, kr_s, vr_s, gr_s, ga_s, gb_s = _proj_rope(
            x1_s, w_in_l, tables_s, F32)

        r3 = lambda a: a.reshape(DB, T, a.shape[-1])
        to_lane_major = lambda c: jnp.transpose(c, (0, 2, 3, 1))
        sample_mixers = _sample_mixers_parts(
            r3(qa_s), r3(ka_s), r3(va_s), sinks, to_lane_major(cache_k[l]), to_lane_major(cache_v[l]),
            r3(qr_s), r3(kr_s), r3(vr_s), r3(gr_s), state_ret[l])
        w = min(WINDOW, S)
        (qa_p, ka_p, va_p, qr_p, kr_p, vr_p, gr_p, ga_p, gb_p, nk_p, nv_p), (oa_s, nk_s, nv_s, or_s, st_s) = \
            _proj_rope(x1_p, w_in_l, tables_p, BF16, rider=(RIDER_PROJ_TILE, sample_mixers), tail=(S, w))
        oa_s, or_s = oa_s.reshape(DB * T, Q_W), or_s.reshape(DB * T, RV_W)
        from_lane_major = lambda c: jnp.transpose(c, (0, 3, 1, 2))
        new_caches = [from_lane_major(nk_p.reshape(B, N_KV_HEADS, HEAD_DIM, w)),
                      from_lane_major(nv_p.reshape(B, N_KV_HEADS, HEAD_DIM, w)),
                      from_lane_major(nk_s), from_lane_major(nv_s)]

        oa_p = _attn_prompt(qa_p, ka_p, va_p, sinks, B, S)
        or_p, st_p = _ret_prompt(qr_p, kr_p, vr_p, B, S)

        x2_p = _merge_ln(x1_p, oa_p, or_p, gr_p, ga_p, gb_p, wba, wbr, w_o_l, g_l, b_l, 1, alpha)
        x2_s = _merge_ln(x1_s, oa_s, or_s, None, ga_s, gb_s, wba, wbr, w_o_l, g_l, b_l, 1, alpha)
        y_p, y_s = _ffn_ln(x2_p, x2_s, wi, wo, 1, g_l, b_l, 2, alpha)
        for lst, val in zip(outs, (new_caches[0], new_caches[1], st_p, new_caches[2], new_caches[3], st_s)):
            lst.append(val)
    return (y_p.reshape(B, S, D), y_s.reshape(DB, T, D)) + tuple(jnp.stack(o) for o in outs)
```

```python
import functools
import math

import jax
import jax.numpy as jnp
from jax import lax
from jax.experimental import pallas as pl
from jax.experimental.pallas import tpu as pltpu

F32 = jnp.float32
BF16 = jnp.bfloat16

PAST_LEN = 8192
N_Q_HEADS = 8
N_KV_HEADS = 2
HEAD_DIM = 64
GQA_GROUP = N_Q_HEADS // N_KV_HEADS
WINDOW = 128
ROPE_THETA = 10000.0
N_RET_HEADS = 4
RET_KEY_DIM = 128
RET_VAL_DIM = 256
RET_CHUNK = 128
RET_THETA = 10000.0
LN_EPS = 1e-5
GN_EPS = 1e-6
NEG_INF = -1e30

Q_W = N_Q_HEADS * HEAD_DIM
KV_W = N_KV_HEADS * HEAD_DIM
RQ_W = N_RET_HEADS * RET_KEY_DIM
RV_W = N_RET_HEADS * RET_VAL_DIM

LANES = 128
SUBLANES = 8
VMEM_LIMIT_BYTES = 56 * 1024 * 1024

TOKEN_TILE = 512
DENSE_TILE = 1024
DENSE_MIN_STEPS = 4
FF_CHUNK = 256
ATTN_TILE = 2048
RET_TILE = 2048
LN_ANCHOR_CHUNK = 8
MERGE_ROWS = 128
RIDER_PROJ_TILE = 512
RIDER_LEAD_GROUPS = 2
CAST_STEPS = 16


def _dot(a, b):
    return jnp.dot(a, b, preferred_element_type=F32)


def _dot_nt(a, b):
    return lax.dot_general(a, b, (((1,), (1,)), ((), ())), preferred_element_type=F32)


def _layer_norm(y, g, b):
    mu = jnp.mean(y, axis=-1, keepdims=True)
    yc = y - mu
    var = jnp.mean(yc * yc, axis=-1, keepdims=True)
    return yc * lax.rsqrt(var + LN_EPS) * g + b


def _cparams(n_axes, semantics=None):
    return pltpu.CompilerParams(
        dimension_semantics=semantics or ("arbitrary",) * n_axes,
        vmem_limit_bytes=VMEM_LIMIT_BYTES)


def _const_spec(shape):
    nd = len(shape)
    return pl.BlockSpec(shape, lambda *_: (0,) * nd, pipeline_mode=pl.Buffered(1))


def _segment_specs(n_prompt_tiles, tm, width, tile_of):
    prompt = pl.BlockSpec((tm, width), lambda i: (jnp.minimum(tile_of(i), n_prompt_tiles - 1), 0))
    sample = pl.BlockSpec((tm, width), lambda i: (jnp.maximum(tile_of(i) - n_prompt_tiles, 0), 0))
    return prompt, sample


def _store_segment(is_prompt, prompt_ref, sample_ref, value):
    @pl.when(is_prompt)
    def _():
        prompt_ref[...] = value.astype(prompt_ref.dtype)

    @pl.when(jnp.logical_not(is_prompt))
    def _():
        sample_ref[...] = value.astype(sample_ref.dtype)


def _ffn_chunk(xb, w_gate, w_up, w_out):
    act = (jax.nn.silu(_dot(xb, w_gate)) * _dot(xb, w_up)).astype(BF16)
    return _dot(act, w_out)


def _ffn_tile_maps(n_chunks, n_tiles):
    matmul_tile = lambda i: jnp.clip(i - (n_chunks - 1), 0, n_tiles - 1)
    norm_tile = lambda i: jnp.clip(i - n_chunks, 0, n_tiles - 1)
    return matmul_tile, norm_tile


def _ffn_ln_kernel(alpha, n_chunks, n_tiles, ln_row, n_prompt_tiles, n_cast, *refs):
    xp_ref, xs_ref, wg_ref, wu_ref, wo_ref, g_ref, b_ref = refs[:7]
    cast_in = refs[7:7 + n_cast]
    op_ref, os_ref = refs[7 + n_cast:9 + n_cast]
    cast_out = refs[9 + n_cast:9 + 2 * n_cast]
    acc_ref, y_ref, wi_bf_ref, wo_bf_ref = refs[9 + 2 * n_cast:]
    step = pl.program_id(0)
    matmul_tile, norm_tile = _ffn_tile_maps(n_chunks, n_tiles)
    last_step = n_chunks + n_tiles - 1
    fc = FF_CHUNK

    norm_is_prompt = norm_tile(step) < n_prompt_tiles

    def norm_previous():
        return _layer_norm(y_ref[...], g_ref[ln_row:ln_row + 1, :], b_ref[ln_row:ln_row + 1, :])

    @pl.when(step < n_chunks)
    def _():
        w_gate, w_up, w_out = wg_ref[...].astype(BF16), wu_ref[...].astype(BF16), wo_ref[...].astype(BF16)
        wi_bf_ref[step, :, :fc] = w_gate
        wi_bf_ref[step, :, fc:] = w_up
        wo_bf_ref[step] = w_out
        x = xp_ref[...]
        part = _ffn_chunk(x.astype(BF16), w_gate, w_up, w_out)

        @pl.when(step == 0)
        def _():
            acc_ref[...] = part

        @pl.when(step > 0)
        def _():
            acc_ref[...] += part

        @pl.when(step == n_chunks - 1)
        def _():
            y_ref[...] = alpha * x + 0.5 * acc_ref[...]

    def full_tile_step(out_ref):
        normed = norm_previous()
        out_ref[...] = normed
        rows, width = normed.shape
        anchor = jnp.max(normed.reshape(rows // SUBLANES, SUBLANES, width), axis=0)
        anchor = functools.reduce(jnp.maximum, [anchor[:, j * LANES:(j + 1) * LANES]
                                                for j in range(width // LANES)])
        zero = jnp.minimum(jnp.abs(anchor), 0.0)
        x = jnp.where(matmul_tile(step) < n_prompt_tiles, xp_ref[...], xs_ref[...])
        xb = x.astype(BF16)
        for c in range(n_chunks):
            part = _ffn_chunk(xb, wi_bf_ref[c, :, :fc], wi_bf_ref[c, :, fc:], wo_bf_ref[c])
            if c == 0:
                acc_ref[...] = part
            else:
                acc_ref[...] += part
            if c == min(LN_ANCHOR_CHUNK, n_chunks - 1):
                acc_ref[0:SUBLANES, 0:LANES] += zero
        y_ref[...] = alpha * x + 0.5 * acc_ref[...]

    in_full_steps = (step >= n_chunks) & (step < last_step)

    if n_cast:
        @pl.when((step >= n_chunks) & (step < n_chunks + CAST_STEPS))
        def _():
            for src, dst in zip(cast_in, cast_out):
                dst[...] = src[...].astype(dst.dtype)

    pl.when(in_full_steps & norm_is_prompt)(functools.partial(full_tile_step, op_ref))
    pl.when(in_full_steps & jnp.logical_not(norm_is_prompt))(functools.partial(full_tile_step, os_ref))

    @pl.when(step == last_step)
    def _():
        _store_segment(norm_is_prompt, op_ref, os_ref, norm_previous())


def _ffn_ln(xp, xs, wi, wo, half, ln_g, ln_b, ln_row, alpha, cast_weights=()):
    d = xp.shape[1]
    d_ff = wo.shape[1]
    tm, fc = TOKEN_TILE, FF_CHUNK
    assert xp.shape[0] % tm == 0 and xs.shape[0] % tm == 0 and d_ff % fc == 0
    n_p, n_s = xp.shape[0] // tm, xs.shape[0] // tm
    n_chunks = d_ff // fc
    n_tiles = n_p + n_s
    assert n_tiles - 1 >= CAST_STEPS
    matmul_tile, norm_tile = _ffn_tile_maps(n_chunks, n_tiles)
    chunk_of = lambda i: jnp.minimum(i, n_chunks - 1)
    cast_specs = []
    for w in cast_weights:
        rows = w.shape[0] // CAST_STEPS
        assert w.shape[0] % CAST_STEPS == 0 and rows % 16 == 0
        cast_specs.append(pl.BlockSpec((rows, w.shape[1]),
                                       lambda i: (jnp.clip(i - n_chunks, 0, CAST_STEPS - 1), 0)))
    return pl.pallas_call(
        functools.partial(_ffn_ln_kernel, alpha, n_chunks, n_tiles, ln_row, n_p, len(cast_weights)),
        grid=(n_chunks + n_tiles,),
        in_specs=[*_segment_specs(n_p, tm, d, matmul_tile),
                  pl.BlockSpec((None, d, fc), lambda i: (half, 0, chunk_of(i))),
                  pl.BlockSpec((None, d, fc), lambda i: (half, 0, n_chunks + chunk_of(i))),
                  pl.BlockSpec((None, fc, d), lambda i: (half, chunk_of(i), 0)),
                  _const_spec(ln_g.shape), _const_spec(ln_b.shape), *cast_specs],
        out_specs=[*_segment_specs(n_p, tm, d, norm_tile), *cast_specs],
        out_shape=[jax.ShapeDtypeStruct(xp.shape, F32), jax.ShapeDtypeStruct(xs.shape, F32)]
                  + [jax.ShapeDtypeStruct(w.shape, BF16) for w in cast_weights],
        scratch_shapes=[pltpu.VMEM((tm, d), F32), pltpu.VMEM((tm, d), F32),
                        pltpu.VMEM((n_chunks, d, 2 * fc), BF16),
                        pltpu.VMEM((n_chunks, fc, d), BF16)],
        compiler_params=_cparams(1),
        name="ffn_ln",
    )(xp, xs, wi, wi, wo, ln_g, ln_b, *cast_weights)


def _rope_tables(pos, dim, theta_fn):
    half = dim // 2
    inv_freq = theta_fn(half)
    ang = pos.astype(F32)[:, None] * inv_freq[None, :]
    reps = LANES // half
    cos = jnp.tile(jnp.cos(ang), (1, reps))
    sign = jnp.tile(jnp.concatenate([-jnp.ones((half,), F32), jnp.ones((half,), F32)]), LANES // dim)
    sin = jnp.tile(jnp.sin(ang), (1, reps)) * sign[None, :]
    return cos, sin


def _attn_inv_freq(half):
    return 1.0 / (ROPE_THETA ** (jnp.arange(0, 2 * half, 2, dtype=F32) / (2 * half)))


def _ret_inv_freq(half):
    return 1.0 / (RET_THETA ** jnp.linspace(0.0, 1.0, half, dtype=F32))


def _rope_block(x, cos, sin, dim):
    half = dim // 2
    if dim == LANES:
        rot = pltpu.roll(x, half, 1)
    else:
        lane = lax.broadcasted_iota(jnp.int32, x.shape, 1)
        first_half = (lane % dim) < half
        rot = jnp.where(first_half, pltpu.roll(x, LANES - half, 1), pltpu.roll(x, half, 1))
    return x * cos + rot * sin


def _run(stages):
    for _ in stages:
        pass


def _proj_rope_stages(x_ref, w_ref, cosa_ref, sina_ref, cosr_ref, sinr_ref,
                      qa_ref, ka_ref, va_ref, qr_ref, kr_ref, vr_ref, gr_ref, ga_ref, gb_ref,
                      k_tail_ref=None, v_tail_ref=None):
    xb = x_ref[...].astype(BF16)
    cosa, sina = cosa_ref[...], sina_ref[...]
    cosr, sinr = cosr_ref[...], sinr_ref[...]
    off = 0

    y = _dot(xb, w_ref[:, off:off + Q_W])
    for j in range(Q_W // LANES):
        qa_ref[:, j * LANES:(j + 1) * LANES] = _rope_block(
            y[:, j * LANES:(j + 1) * LANES], cosa, sina, HEAD_DIM).astype(qa_ref.dtype)
    off += Q_W
    yield
    y = _dot(xb, w_ref[:, off:off + 2 * KV_W])
    k_rot = _rope_block(y[:, :KV_W], cosa, sina, HEAD_DIM)
    ka_ref[...] = k_rot
    va_ref[...] = y[:, KV_W:]
    if k_tail_ref is not None:
        tail = k_tail_ref.shape[1]
        k_tail_ref[...] = k_rot[-tail:, :].T
        v_tail_ref[...] = y[-tail:, KV_W:].T
    off += 2 * KV_W
    yield
    y = _dot(xb, w_ref[:, off:off + RQ_W])
    for j in range(RQ_W // LANES):
        qr_ref[:, j * LANES:(j + 1) * LANES] = _rope_block(
            y[:, j * LANES:(j + 1) * LANES], cosr, sinr, RET_KEY_DIM).astype(qr_ref.dtype)
    off += RQ_W
    yield
    y = _dot(xb, w_ref[:, off:off + RQ_W])
    for j in range(RQ_W // LANES):
        kr = _rope_block(y[:, j * LANES:(j + 1) * LANES], cosr, sinr, RET_KEY_DIM)
        kr_ref[:, j * LANES:(j + 1) * LANES] = (kr * (RET_KEY_DIM ** -0.5)).astype(kr_ref.dtype)
    off += RQ_W
    yield
    for ref in (vr_ref, gr_ref, ga_ref, gb_ref):
        width = ref.shape[1]
        for c in range(width // 512):
            ref[:, c * 512:(c + 1) * 512] = _dot(
                xb, w_ref[:, off + c * 512:off + (c + 1) * 512]).astype(ref.dtype)
            yield
        off += width


def _proj_rope_kernel(*refs):
    _run(_proj_rope_stages(*refs))


def _dense_tile(tokens):
    return min(DENSE_TILE, tokens // DENSE_MIN_STEPS)


N_PROJ_IN = 6


def _proj_with_rider_kernel(rider_stages, n_rider_in, n_proj_out, *refs):
    proj_in, rest = refs[:N_PROJ_IN], refs[N_PROJ_IN:]
    rider_in, rest = rest[:n_rider_in], rest[n_rider_in:]
    proj_out, rider_out = rest[:n_proj_out], rest[n_proj_out:]
    rider = rider_stages(*rider_in, *rider_out)
    proj = _proj_rope_stages(*proj_in, *proj_out)
    next(rider)
    for _ in range(RIDER_LEAD_GROUPS):
        next(proj)
    _run(rider)
    _run(proj)


def _proj_rope(x1, w_in, tables, mixer_dtype, rider=None, tail=None):
    t, d = x1.shape
    n_in = w_in.shape[1]
    d_model = (n_in - Q_W - 2 * KV_W - 2 * RQ_W - 2 * RV_W) // 2
    tm = _dense_tile(t) if rider is None else rider[0]
    row = lambda i: (i, 0)
    table_blocks = tables[0].shape[0] // tm
    tab = lambda i: (i % table_blocks, 0)
    widths = [(Q_W, mixer_dtype), (KV_W, F32), (KV_W, F32), (RQ_W, mixer_dtype), (RQ_W, mixer_dtype),
              (RV_W, mixer_dtype), (RV_W, mixer_dtype), (d_model, BF16), (d_model, BF16)]
    in_specs = [pl.BlockSpec((tm, d), row), _const_spec(w_in.shape)] + [pl.BlockSpec((tm, LANES), tab)] * 4
    out_specs = [pl.BlockSpec((tm, w), row) for w, _ in widths]
    out_shape = [jax.ShapeDtypeStruct((t, w), dt) for w, dt in widths]
    if tail is not None:
        seq_len, tail_rows = tail
        assert seq_len % tm == 0 and tail_rows <= tm
        tiles_per_seq = seq_len // tm
        out_specs += [pl.BlockSpec((KV_W, tail_rows), lambda i: (i // tiles_per_seq, 0))] * 2
        out_shape += [jax.ShapeDtypeStruct((t // seq_len * KV_W, tail_rows), F32)] * 2
    n_proj_out = len(out_specs)
    args = [x1, w_in, *tables]
    body = _proj_rope_kernel
    if rider is not None:
        r_stages, r_in_specs, r_out_specs, r_out_shape, r_args = rider[1](t // tm)
        body = functools.partial(_proj_with_rider_kernel, r_stages, len(r_args), n_proj_out)
        in_specs, out_specs = in_specs + r_in_specs, out_specs + r_out_specs
        out_shape, args = out_shape + r_out_shape, args + r_args
    outs = pl.pallas_call(
        body,
        grid=(t // tm,),
        in_specs=in_specs,
        out_specs=out_specs,
        out_shape=out_shape,
        compiler_params=_cparams(1),
        name="proj_rope",
    )(*args)
    return outs if rider is None else (outs[:n_proj_out], outs[n_proj_out:])


def _kv_lane_variants(x):
    lane = lax.broadcasted_iota(jnp.int32, x.shape, 1)
    lo = lane < HEAD_DIM
    swapped = pltpu.roll(x, HEAD_DIM, 1)
    zero = jnp.zeros_like(x)
    kv0 = (jnp.where(lo, x, zero).astype(BF16), jnp.where(lo, zero, swapped).astype(BF16))
    kv1 = (jnp.where(lo, swapped, zero).astype(BF16), jnp.where(lo, zero, x).astype(BF16))
    return kv0, kv1


def _attn_prompt_kernel(sinks_ref, q_ref, kc_ref, kp_ref, vc_ref, vp_ref, o_ref):
    n = pl.program_id(1)
    L = WINDOW
    k_full = jnp.concatenate([kp_ref[...], kc_ref[...]], axis=0)
    v_full = jnp.concatenate([vp_ref[...], vc_ref[...]], axis=0)
    v_var = _kv_lane_variants(v_full)
    k_t = k_full.T
    k_t_swapped = jnp.concatenate([k_t[HEAD_DIM:], k_t[:HEAD_DIM]], axis=0)
    row_lo = lax.broadcasted_iota(jnp.int32, k_t.shape, 0) < HEAD_DIM
    zero_t = jnp.zeros_like(k_t)
    k_var = ((jnp.where(row_lo, k_t, zero_t).astype(BF16), jnp.where(row_lo, zero_t, k_t_swapped).astype(BF16)),
             (jnp.where(row_lo, k_t_swapped, zero_t).astype(BF16), jnp.where(row_lo, zero_t, k_t).astype(BF16)))
    own = (lax.broadcasted_iota(jnp.int32, (L, L), 1)
           <= lax.broadcasted_iota(jnp.int32, (L, L), 0))
    scale = HEAD_DIM ** -0.5

    units = [(i, j, par) for i in range(q_ref.shape[0] // L) for j in range(Q_W // LANES) for par in range(2)]

    def scores(unit):
        i, j, par = unit
        q2 = q_ref[i * L:(i + 1) * L, j * LANES:(j + 1) * LANES] * scale
        return _dot(q2, k_var[(2 * j) // GQA_GROUP][par][:, i * L:(i + 2) * L])

    s_next = scores(units[0])
    out = None
    for idx, (i, j, par) in enumerate(units):
        s = s_next
        if idx + 1 < len(units):
            s_next = scores(units[idx + 1])
        s_prev = s[:, :L]
        if i == 0:
            s_prev = jnp.where(n > 0, s_prev, NEG_INF)
        sw = jnp.where(own, s[:, L:], s_prev)
        sink = sinks_ref[2 * j + par]
        m = jnp.maximum(jnp.max(sw, axis=-1, keepdims=True), sink)
        e = jnp.exp(sw - m)
        denom = jnp.sum(e, axis=-1, keepdims=True) + jnp.exp(sink - m)
        e_band = jnp.concatenate([jnp.where(own, 0.0, e), jnp.where(own, e, 0.0)], axis=1)
        r = _dot(e_band.astype(BF16), v_var[(2 * j) // GQA_GROUP][par][i * L:(i + 2) * L]) / denom
        if par == 0:
            out = r
        else:
            o_ref[i * L:(i + 1) * L, j * LANES:(j + 1) * LANES] = (out + r).astype(o_ref.dtype)


def _attn_prompt(qa, ka, va, sinks, batch, seq):
    tq = ATTN_TILE
    nq = seq // tq
    per = tq // WINDOW
    cur = lambda b, n: (b * nq + n, 0)
    prev = lambda b, n: ((b * nq + n) * per - jnp.minimum(n, 1), 0)
    return pl.pallas_call(
        _attn_prompt_kernel,
        grid=(batch, nq),
        in_specs=[pl.BlockSpec(memory_space=pltpu.SMEM),
                  pl.BlockSpec((tq, Q_W), cur),
                  pl.BlockSpec((tq, KV_W), cur), pl.BlockSpec((WINDOW, KV_W), prev),
                  pl.BlockSpec((tq, KV_W), cur), pl.BlockSpec((WINDOW, KV_W), prev)],
        out_specs=pl.BlockSpec((tq, Q_W), cur),
        out_shape=jax.ShapeDtypeStruct(qa.shape, BF16),
        compiler_params=_cparams(2),
        name="attn_prompt",
    )(sinks, qa, ka, ka, va, va)


def _attn_sample_stages(sink_rows_ref, q_ref, kn_ref, vn_ref, ck_ref, cv_ref,
                        o_ref, nk_ref, nv_ref):
    bt, t_new, _ = q_ref.shape
    assert t_new == SUBLANES
    w = ck_ref.shape[3]
    n_rows = N_Q_HEADS * t_new
    lane_lo = lax.broadcasted_iota(jnp.int32, (bt * t_new, LANES), 1) < HEAD_DIM
    t_c = lax.broadcasted_iota(jnp.int32, (n_rows, w), 0) % t_new
    mask_c = lax.broadcasted_iota(jnp.int32, (n_rows, w), 1) > t_c + (w - WINDOW)
    t_n = lax.broadcasted_iota(jnp.int32, (n_rows, t_new), 0) % t_new
    mask_n = lax.broadcasted_iota(jnp.int32, (n_rows, t_new), 1) <= t_n
    sink = sink_rows_ref[...]
    scale = HEAD_DIM ** -0.5
    kn, vn = kn_ref[...], vn_ref[...]
    kc_t = ck_ref[...].reshape(bt, KV_W, w)
    vc_t = cv_ref[...].reshape(bt, KV_W, w)
    q = q_ref[...].reshape(bt * t_new, Q_W)
    pieces = []
    for h in range(N_Q_HEADS):
        q2 = q[:, (h // 2) * LANES:(h // 2 + 1) * LANES]
        src_lo = h % 2 == 0
        dst_lo = h // GQA_GROUP == 0
        if src_lo != dst_lo:
            q2 = pltpu.roll(q2, HEAD_DIM, 1)
        pieces.append(jnp.where(lane_lo == dst_lo, q2, 0.0).reshape(bt, t_new, LANES))
    q_rows = (jnp.concatenate(pieces, axis=1) * scale).astype(BF16)
    s_c = jnp.einsum("bqd,bdk->bqk", q_rows, kc_t.astype(BF16), preferred_element_type=F32)
    s_n = jnp.einsum("bqd,bkd->bqk", q_rows, kn.astype(BF16), preferred_element_type=F32)
    yield
    s_c = jnp.where(mask_c, s_c, NEG_INF)
    s_n = jnp.where(mask_n, s_n, NEG_INF)
    m = jnp.maximum(jnp.maximum(jnp.max(s_c, axis=-1, keepdims=True), jnp.max(s_n, axis=-1, keepdims=True)), sink)
    e_c, e_n = jnp.exp(s_c - m), jnp.exp(s_n - m)
    denom = jnp.sum(e_c, axis=-1, keepdims=True) + jnp.sum(e_n, axis=-1, keepdims=True) + jnp.exp(sink - m)
    pv = (jnp.einsum("bqk,bdk->bqd", e_c.astype(BF16), vc_t.astype(BF16), preferred_element_type=F32)
          + jnp.einsum("bqk,bkd->bqd", e_n.astype(BF16), vn.astype(BF16), preferred_element_type=F32)) / denom
    for j in range(Q_W // LANES):
        out = None
        for par in range(2):
            h = 2 * j + par
            dst_lo = h // GQA_GROUP == 0
            piece = pv[:, h * t_new:(h + 1) * t_new, :].reshape(bt * t_new, LANES)
            piece = jnp.where(lane_lo == dst_lo, piece, 0.0)
            if dst_lo != (par == 0):
                piece = pltpu.roll(piece, HEAD_DIM, 1)
            out = piece if out is None else out + piece
        o_ref[:, :, j * LANES:(j + 1) * LANES] = out.reshape(bt, t_new, LANES).astype(o_ref.dtype)
    keep = lax.broadcasted_iota(jnp.int32, (bt * KV_W, w), 1) < w - t_new
    for old_t, new, out_ref in ((kc_t, kn, nk_ref), (vc_t, vn, nv_ref)):
        shifted = pltpu.roll(old_t.reshape(bt * KV_W, w), w - t_new, 1)
        placed = jnp.concatenate([jnp.zeros((bt, w - t_new, KV_W), F32), new], axis=1)
        placed_t = jnp.swapaxes(placed, 1, 2).reshape(bt * KV_W, w)
        out_ref[...] = jnp.where(keep, shifted, placed_t).reshape(out_ref.shape)


def _ret_log_gamma(h):
    return math.log(1.0 - 2.0 ** (-5.0 - h))


def _ret_decays(chunk, h):
    lg = _ret_log_gamma(h)
    i = lax.broadcasted_iota(jnp.int32, (chunk, chunk), 0)
    j = lax.broadcasted_iota(jnp.int32, (chunk, chunk), 1)
    rel_mask = jnp.where(i >= j, jnp.exp((j + 1).astype(F32) * -lg), 0.0)
    col = lax.broadcasted_iota(jnp.int32, (chunk, 1), 0).astype(F32)
    gn_eps = GN_EPS * jnp.exp((col + 1.0) * (-2.0 * lg))
    k_decay = jnp.exp((chunk - 1.0 - col) * lg)
    return rel_mask, gn_eps, k_decay, math.exp(chunk * lg)


def _ret_chunk_scores(q, k, v, decays):
    heads = range(N_RET_HEADS)
    batched = q.ndim == 3
    mm = (lambda a, b: jnp.einsum("bmk,bkn->bmn", a, b, preferred_element_type=F32)) if batched else _dot
    mm_nt = (lambda a, b: jnp.einsum("bmk,bnk->bmn", a, b, preferred_element_type=F32)) if batched else _dot_nt
    qh = [q[..., h * RET_KEY_DIM:(h + 1) * RET_KEY_DIM] for h in heads]
    kh = [k[..., h * RET_KEY_DIM:(h + 1) * RET_KEY_DIM] for h in heads]
    vh = [v[..., h * RET_VAL_DIM:(h + 1) * RET_VAL_DIM] for h in heads]
    qb = [x.astype(BF16) for x in qh]
    vb = [x.astype(BF16) for x in vh]
    inner = [mm_nt(qb[h], kh[h].astype(BF16)) * decays[h][0] for h in heads]
    return batched, mm, qh, kh, vh, qb, vb, inner


def _ret_chunk_finish(scores, g, states, decays):
    batched, mm, qh, kh, vh, qb, vb, inner = scores
    heads = range(N_RET_HEADS)
    if batched:
        lhs = [jnp.concatenate([qh[h], inner[h]], axis=-1).astype(BF16) for h in heads]
        rhs = [jnp.concatenate([states[h], vh[h]], axis=-2).astype(BF16) for h in heads]
    else:
        lhs = [jnp.concatenate([qb[h], inner[h].astype(BF16)], axis=-1) for h in heads]
        rhs = [jnp.concatenate([states[h].astype(BF16), vb[h]], axis=-2) for h in heads]
    p = [mm(lhs[h], rhs[h]) for h in heads]
    kd_t = [jnp.swapaxes(kh[h].astype(F32) * decays[h][2], -1, -2).astype(BF16) for h in heads]
    new_states = [states[h] * decays[h][3] + mm(kd_t[h], vb[h]) for h in heads]
    outs = []
    for h in heads:
        mu = jnp.mean(p[h], axis=-1, keepdims=True)
        pc = p[h] - mu
        var = jnp.mean(pc * pc, axis=-1, keepdims=True)
        normed = pc * lax.rsqrt(var + decays[h][1])
        if g is not None:
            normed = normed * jax.nn.silu(g[..., h * RET_VAL_DIM:(h + 1) * RET_VAL_DIM].astype(F32))
        outs.append(normed)
    return jnp.concatenate(outs, axis=-1), new_states


def _ret_chunk_all_heads(q, k, v, g, states, decays):
    return _ret_chunk_finish(_ret_chunk_scores(q, k, v, decays), g, states, decays)


def _ret_prompt_kernel(q_ref, k_ref, v_ref, o_ref, st_ref, state_ref):
    n = pl.program_id(1)
    C = RET_CHUNK

    @pl.when(n == 0)
    def _():
        state_ref[...] = jnp.zeros_like(state_ref)

    decays = [_ret_decays(C, h) for h in range(N_RET_HEADS)]
    states = [state_ref[h] for h in range(N_RET_HEADS)]
    for c in range(q_ref.shape[0] // C):
        rows = slice(c * C, (c + 1) * C)
        out, states = _ret_chunk_all_heads(q_ref[rows, :], k_ref[rows, :], v_ref[rows, :], None, states, decays)
        o_ref[rows, :] = out.astype(o_ref.dtype)
    for h in range(N_RET_HEADS):
        state_ref[h] = states[h]

    @pl.when(n == pl.num_programs(1) - 1)
    def _():
        st_ref[0] = state_ref[...]


def _ret_prompt(qr, kr, vr, batch, seq):
    ts = RET_TILE
    ns = seq // ts
    row = lambda b, n: (b * ns + n, 0)
    return pl.pallas_call(
        _ret_prompt_kernel,
        grid=(batch, ns),
        in_specs=[pl.BlockSpec((ts, RQ_W), row), pl.BlockSpec((ts, RQ_W), row), pl.BlockSpec((ts, RV_W), row)],
        out_specs=[pl.BlockSpec((ts, RV_W), row),
                   pl.BlockSpec((1, N_RET_HEADS, RET_KEY_DIM, RET_VAL_DIM), lambda b, n: (b, 0, 0, 0))],
        out_shape=[jax.ShapeDtypeStruct((batch * seq, RV_W), BF16),
                   jax.ShapeDtypeStruct((batch, N_RET_HEADS, RET_KEY_DIM, RET_VAL_DIM), F32)],
        scratch_shapes=[pltpu.VMEM((N_RET_HEADS, RET_KEY_DIM, RET_VAL_DIM), F32)],
        compiler_params=_cparams(2),
        name="ret_prompt",
    )(qr, kr, vr)


def _ret_sample_stages(q_ref, k_ref, v_ref, g_ref, st_ref, o_ref, nst_ref):
    t_new = q_ref.shape[1]
    decays = [_ret_decays(t_new, h) for h in range(N_RET_HEADS)]
    scores = _ret_chunk_scores(q_ref[...], k_ref[...], v_ref[...], decays)
    yield
    states = [st_ref[:, h] for h in range(N_RET_HEADS)]
    out, states = _ret_chunk_finish(scores, g_ref[...], states, decays)
    o_ref[...] = out.astype(o_ref.dtype)
    for h in range(N_RET_HEADS):
        nst_ref[:, h] = states[h]


def _sample_mixers_stages(sink_rows_ref, qa_ref, kn_ref, vn_ref, ck_ref, cv_ref,
                          qr_ref, kr_ref, vr_ref, gr_ref, st_ref,
                          oa_ref, nk_ref, nv_ref, or_ref, nst_ref):
    attn = _attn_sample_stages(sink_rows_ref, qa_ref, kn_ref, vn_ref, ck_ref, cv_ref, oa_ref, nk_ref, nv_ref)
    ret = _ret_sample_stages(qr_ref, kr_ref, vr_ref, gr_ref, st_ref, or_ref, nst_ref)
    next(attn)
    next(ret)
    yield
    _run(attn)
    _run(ret)


def _sample_mixers_parts(qa, ka, va, sinks, cache_k, cache_v, qr, kr, vr, gr, state):
    db, t_new, _ = qa.shape
    w = cache_k.shape[3]
    cache = (N_KV_HEADS, HEAD_DIM, w)
    sink_rows = jnp.repeat(sinks.astype(F32), t_new).reshape(N_Q_HEADS * t_new, 1)

    def parts(n_steps):
        assert db % n_steps == 0
        bt = db // n_steps
        blk = lambda *s: pl.BlockSpec((bt,) + s, lambda i: (i,) + (0,) * len(s))
        st = blk(N_RET_HEADS, RET_KEY_DIM, RET_VAL_DIM)
        in_specs = [_const_spec(sink_rows.shape), blk(t_new, Q_W), blk(t_new, KV_W), blk(t_new, KV_W),
                    blk(*cache), blk(*cache),
                    blk(t_new, RQ_W), blk(t_new, RQ_W), blk(t_new, RV_W), blk(t_new, RV_W), st]
        out_specs = [blk(t_new, Q_W), blk(*cache), blk(*cache), blk(t_new, RV_W), st]
        out_shape = [jax.ShapeDtypeStruct((db, t_new, Q_W), F32),
                     jax.ShapeDtypeStruct((db,) + cache, F32),
                     jax.ShapeDtypeStruct((db,) + cache, F32),
                     jax.ShapeDtypeStruct((db, t_new, RV_W), F32),
                     jax.ShapeDtypeStruct(state.shape, state.dtype)]
        args = [sink_rows, qa, ka, va, cache_k, cache_v, qr, kr, vr, gr, state]
        return _sample_mixers_stages, in_specs, out_specs, out_shape, args

    return parts


def _merge_ln_kernel(alpha, ln_row, x_ref, oa_ref, orr_ref, gr_ref, ga_ref, gb_ref, wba_ref, wbr_ref, wo_ref,
                     g_ref, b_ref, o_ref):
    n_blocks = max(1, x_ref.shape[0] // MERGE_ROWS)
    rb = x_ref.shape[0] // n_blocks
    blocks = [slice(r * rb, (r + 1) * rb) for r in range(n_blocks)]

    def ret_branch(rows):
        if gr_ref is None:
            return orr_ref[rows, :].astype(BF16)
        return (orr_ref[rows, :].astype(F32) * jax.nn.silu(gr_ref[rows, :].astype(F32))).astype(BF16)

    ya = [_dot(oa_ref[rows, :].astype(BF16), wba_ref[...]) for rows in blocks]
    yr = [_dot(ret_branch(rows), wbr_ref[...]) for rows in blocks]
    merged = [(jax.nn.sigmoid(ga_ref[rows, :].astype(F32)) * ya[r]
               + jax.nn.sigmoid(gb_ref[rows, :].astype(F32)) * yr[r]).astype(BF16)
              for r, rows in enumerate(blocks)]
    z = [_dot(merged[r], wo_ref[...]) for r in range(n_blocks)]
    for r, rows in enumerate(blocks):
        o_ref[rows, :] = _layer_norm(alpha * x_ref[rows, :] + z[r],
                                     g_ref[ln_row:ln_row + 1, :], b_ref[ln_row:ln_row + 1, :])


def _merge_ln_gated_kernel(alpha, ln_row, *refs):
    _merge_ln_kernel(alpha, ln_row, *refs)


def _merge_ln_ungated_kernel(alpha, ln_row, x_ref, oa_ref, orr_ref, *refs):
    _merge_ln_kernel(alpha, ln_row, x_ref, oa_ref, orr_ref, None, *refs)


def _merge_ln(x1, oa, orr, gr, ga, gb, wba, wbr, wo, ln_g, ln_b, ln_row, alpha):
    t, d = x1.shape
    tm = _dense_tile(t)
    row = lambda i: (i, 0)
    gate_specs = [] if gr is None else [pl.BlockSpec((tm, RV_W), row)]
    gate_args = [] if gr is None else [gr]
    body = _merge_ln_ungated_kernel if gr is None else _merge_ln_gated_kernel
    return pl.pallas_call(
        functools.partial(body, alpha, ln_row),
        grid=(t // tm,),
        in_specs=[pl.BlockSpec((tm, d), row), pl.BlockSpec((tm, Q_W), row), pl.BlockSpec((tm, RV_W), row),
                  *gate_specs, pl.BlockSpec((tm, d), row), pl.BlockSpec((tm, d), row),
                  _const_spec(wba.shape), _const_spec(wbr.shape), _const_spec(wo.shape),
                  _const_spec(ln_g.shape), _const_spec(ln_b.shape)],
        out_specs=pl.BlockSpec((tm, d), row),
        out_shape=jax.ShapeDtypeStruct((t, d), F32),
        compiler_params=_cparams(1),
        name="merge_ln",
    )(x1, oa, orr, *gate_args, ga, gb, wba, wbr, wo, ln_g, ln_b)


def _rope_tables_for(pos):
    tables = _rope_tables(pos, HEAD_DIM, _attn_inv_freq) + _rope_tables(pos, RET_KEY_DIM, _ret_inv_freq)
    period = pos.shape[0]
    if period < DENSE_TILE:
        tables = tuple(jnp.tile(tb, (DENSE_TILE // period, 1)) for tb in tables)
    return tables


def kernel(x_prompt, x_sample, cache_k, cache_v, state_ret, ln_g, ln_b, ffn_wi, ffn_wo, w_in,
           attn_sinks, w_br_attn, w_br_ret, w_o):
    B, S, D = x_prompt.shape
    DB, T, _ = x_sample.shape
    depth = ln_g.shape[0]
    alpha = (2.0 * depth) ** 0.25
    W = cache_k.shape[2]
    tables_p = _rope_tables_for(jnp.arange(S, dtype=jnp.int32))
    tables_s = _rope_tables_for(PAST_LEN + jnp.arange(T, dtype=jnp.int32))

    y_p = x_prompt.reshape(B * S, D)
    y_s = x_sample.reshape(DB * T, D)
    outs = [[] for _ in range(6)]
    for l in range(depth):
        wi, wo = ffn_wi[l], ffn_wo[l]
        sinks = attn_sinks[l].astype(F32)
        g_l, b_l = ln_g[l], ln_b[l]

        x1_p, x1_s, w_in_l, wba, wbr, w_o_l = _ffn_ln(
            y_p, y_s, wi, wo, 0, g_l, b_l, 0, alpha,
            cast_weights=(w_in[l], w_br_attn[l], w_br_ret[l], w_o[l]))
        qa_s, ka_s, va_s, qr_s, kr_s, vr_s, gr_s, ga_s, gb_s = _proj_rope(
            x1_s, w_in_l, tables_s, F32)

        r3 = lambda a: a.reshape(DB, T, a.shape[-1])
        to_lane_major = lambda c: jnp.transpose(c, (0, 2, 3, 1))
        sample_mixers = _sample_mixers_parts(
            r3(qa_s), r3(ka_s), r3(va_s), sinks, to_lane_major(cache_k[l]), to_lane_major(cache_v[l]),
            r3(qr_s), r3(kr_s), r3(vr_s), r3(gr_s), state_ret[l])
        w = min(WINDOW, S)
        (qa_p, ka_p, va_p, qr_p, kr_p, vr_p, gr_p, ga_p, gb_p, nk_p, nv_p), (oa_s, nk_s, nv_s, or_s, st_s) = \
            _proj_rope(x1_p, w_in_l, tables_p, BF16, rider=(RIDER_PROJ_TILE, sample_mixers), tail=(S, w))
        oa_s, or_s = oa_s.reshape(DB * T, Q_W), or_s.reshape(DB * T, RV_W)
        from_lane_major = lambda c: jnp.transpose(c, (0, 3, 1, 2))
        new_caches = [from_lane_major(nk_p.reshape(B, N_KV_HEADS, HEAD_DIM, w)),
                      from_lane_major(nv_p.reshape(B, N_KV_HEADS, HEAD_DIM, w)),
                      from_lane_major(nk_s), from_lane_major(nv_s)]

        oa_p = _attn_prompt(qa_p, ka_p, va_p, sinks, B, S)
        or_p, st_p = _ret_prompt(qr_p, kr_p, vr_p, B, S)

        x2_p = _merge_ln(x1_p, oa_p, or_p, gr_p, ga_p, gb_p, wba, wbr, w_o_l, g_l, b_l, 1, alpha)
        x2_s = _merge_ln(x1_s, oa_s, or_s, None, ga_s, gb_s, wba, wbr, w_o_l, g_l, b_l, 1, alpha)
        y_p, y_s = _ffn_ln(x2_p, x2_s, wi, wo, 1, g_l, b_l, 2, alpha)
        for lst, val in zip(outs, (new_caches[0], new_caches[1], st_p, new_caches[2], new_caches[3], st_s)):
            lst.append(val)
    return (y_p.reshape(B, S, D), y_s.reshape(DB, T, D)) + tuple(jnp.stack(o) for o in outs)
```

```python
import functools
import math

import jax
import jax.numpy as jnp
from jax import lax
from jax.experimental import pallas as pl
from jax.experimental.pallas import tpu as pltpu

F32 = jnp.float32
BF16 = jnp.bfloat16

PAST_LEN = 8192
N_Q_HEADS = 8
N_KV_HEADS = 2
HEAD_DIM = 64
GQA_GROUP = N_Q_HEADS // N_KV_HEADS
WINDOW = 128
ROPE_THETA = 10000.0
N_RET_HEADS = 4
RET_KEY_DIM = 128
RET_VAL_DIM = 256
RET_CHUNK = 128
RET_THETA = 10000.0
LN_EPS = 1e-5
GN_EPS = 1e-6
NEG_INF = -1e30

Q_W = N_Q_HEADS * HEAD_DIM
KV_W = N_KV_HEADS * HEAD_DIM
RQ_W = N_RET_HEADS * RET_KEY_DIM
RV_W = N_RET_HEADS * RET_VAL_DIM

LANES = 128
SUBLANES = 8
VMEM_LIMIT_BYTES = 56 * 1024 * 1024

TOKEN_TILE = 512
DENSE_TILE = 1024
DENSE_MIN_STEPS = 4
FF_CHUNK = 256
ATTN_TILE = 2048
RET_TILE = 2048
LN_ANCHOR_CHUNK = 8
MERGE_ROWS = 128
RIDER_PROJ_TILE = 512
RIDER_LEAD_GROUPS = 2
CAST_STEPS = 16


def _dot(a, b):
    return jnp.dot(a, b, preferred_element_type=F32)


def _dot_nt(a, b):
    return lax.dot_general(a, b, (((1,), (1,)), ((), ())), preferred_element_type=F32)


def _layer_norm(y, g, b):
    mu = jnp.mean(y, axis=-1, keepdims=True)
    yc = y - mu
    var = jnp.mean(yc * yc, axis=-1, keepdims=True)
    return yc * lax.rsqrt(var + LN_EPS) * g + b


def _cparams(n_axes, semantics=None):
    return pltpu.CompilerParams(
        dimension_semantics=semantics or ("arbitrary",) * n_axes,
        vmem_limit_bytes=VMEM_LIMIT_BYTES)


def _const_spec(shape):
    nd = len(shape)
    return pl.BlockSpec(shape, lambda *_: (0,) * nd, pipeline_mode=pl.Buffered(1))


def _segment_specs(n_prompt_tiles, tm, width, tile_of):
    prompt = pl.BlockSpec((tm, width), lambda i: (jnp.minimum(tile_of(i), n_prompt_tiles - 1), 0))
    sample = pl.BlockSpec((tm, width), lambda i: (jnp.maximum(tile_of(i) - n_prompt_tiles, 0), 0))
    return prompt, sample


def _store_segment(is_prompt, prompt_ref, sample_ref, value):
    @pl.when(is_prompt)
    def _():
        prompt_ref[...] = value.astype(prompt_ref.dtype)

    @pl.when(jnp.logical_not(is_prompt))
    def _():
        sample_ref[...] = value.astype(sample_ref.dtype)


def _ffn_chunk(xb, w_gate, w_up, w_out):
    act = (jax.nn.silu(_dot(xb, w_gate)) * _dot(xb, w_up)).astype(BF16)
    return _dot(act, w_out)


def _ffn_tile_maps(n_chunks, n_tiles):
    matmul_tile = lambda i: jnp.clip(i - (n_chunks - 1), 0, n_tiles - 1)
    norm_tile = lambda i: jnp.clip(i - n_chunks, 0, n_tiles - 1)
    return matmul_tile, norm_tile


def _ffn_ln_kernel(alpha, n_chunks, n_tiles, ln_row, n_prompt_tiles, n_cast, *refs):
    xp_ref, xs_ref, wg_ref, wu_ref, wo_ref, g_ref, b_ref = refs[:7]
    cast_in = refs[7:7 + n_cast]
    op_ref, os_ref = refs[7 + n_cast:9 + n_cast]
    cast_out = refs[9 + n_cast:9 + 2 * n_cast]
    acc_ref, y_ref, wi_bf_ref, wo_bf_ref = refs[9 + 2 * n_cast:]
    step = pl.program_id(0)
    matmul_tile, norm_tile = _ffn_tile_maps(n_chunks, n_tiles)
    last_step = n_chunks + n_tiles - 1
    fc = FF_CHUNK

    norm_is_prompt = norm_tile(step) < n_prompt_tiles

    def norm_previous():
        return _layer_norm(y_ref[...], g_ref[ln_row:ln_row + 1, :], b_ref[ln_row:ln_row + 1, :])

    @pl.when(step < n_chunks)
    def _():
        w_gate, w_up, w_out = wg_ref[...].astype(BF16), wu_ref[...].astype(BF16), wo_ref[...].astype(BF16)
        wi_bf_ref[step, :, :fc] = w_gate
        wi_bf_ref[step, :, fc:] = w_up
        wo_bf_ref[step] = w_out
        x = xp_ref[...]
        part = _ffn_chunk(x.astype(BF16), w_gate, w_up, w_out)

        @pl.when(step == 0)
        def _():
            acc_ref[...] = part

        @pl.when(step > 0)
        def _():
            acc_ref[...] += part

        @pl.when(step == n_chunks - 1)
        def _():
            y_ref[...] = alpha * x + 0.5 * acc_ref[...]

    def full_tile_step(out_ref):
        normed = norm_previous()
        out_ref[...] = normed
        rows, width = normed.shape
        anchor = jnp.max(normed.reshape(rows // SUBLANES, SUBLANES, width), axis=0)
        anchor = functools.reduce(jnp.maximum, [anchor[:, j * LANES:(j + 1) * LANES]
                                                for j in range(width // LANES)])
        zero = jnp.minimum(jnp.abs(anchor), 0.0)
        x = jnp.where(matmul_tile(step) < n_prompt_tiles, xp_ref[...], xs_ref[...])
        xb = x.astype(BF16)
        for c in range(n_chunks):
            part = _ffn_chunk(xb, wi_bf_ref[c, :, :fc], wi_bf_ref[c, :, fc:], wo_bf_ref[c])
            if c == 0:
                acc_ref[...] = part
            else:
                acc_ref[...] += part
            if c == min(LN_ANCHOR_CHUNK, n_chunks - 1):
                acc_ref[0:SUBLANES, 0:LANES] += zero
        y_ref[...] = alpha * x + 0.5 * acc_ref[...]

    in_full_steps = (step >= n_chunks) & (step < last_step)

    if n_cast:
        @pl.when((step >= n_chunks) & (step < n_chunks + CAST_STEPS))
        def _():
            for src, dst in zip(cast_in, cast_out):
                dst[...] = src[...].astype(dst.dtype)

    pl.when(in_full_steps & norm_is_prompt)(functools.partial(full_tile_step, op_ref))
    pl.when(in_full_steps & jnp.logical_not(norm_is_prompt))(functools.partial(full_tile_step, os_ref))

    @pl.when(step == last_step)
    def _():
        _store_segment(norm_is_prompt, op_ref, os_ref, norm_previous())


def _ffn_ln(xp, xs, wi, wo, half, ln_g, ln_b, ln_row, alpha, cast_weights=()):
    d = xp.shape[1]
    d_ff = wo.shape[1]
    tm, fc = TOKEN_TILE, FF_CHUNK
    assert xp.shape[0] % tm == 0 and xs.shape[0] % tm == 0 and d_ff % fc == 0
    n_p, n_s = xp.shape[0] // tm, xs.shape[0] // tm
    n_chunks = d_ff // fc
    n_tiles = n_p + n_s
    assert n_tiles - 1 >= CAST_STEPS
    matmul_tile, norm_tile = _ffn_tile_maps(n_chunks, n_tiles)
    chunk_of = lambda i: jnp.minimum(i, n_chunks - 1)
    cast_specs = []
    for w in cast_weights:
        rows = w.shape[0] // CAST_STEPS
        assert w.shape[0] % CAST_STEPS == 0 and rows % 16 == 0
        cast_specs.append(pl.BlockSpec((rows, w.shape[1]),
                                       lambda i: (jnp.clip(i - n_chunks, 0, CAST_STEPS - 1), 0)))
    return pl.pallas_call(
        functools.partial(_ffn_ln_kernel, alpha, n_chunks, n_tiles, ln_row, n_p, len(cast_weights)),
        grid=(n_chunks + n_tiles,),
        in_specs=[*_segment_specs(n_p, tm, d, matmul_tile),
                  pl.BlockSpec((None, d, fc), lambda i: (half, 0, chunk_of(i))),
                  pl.BlockSpec((None, d, fc), lambda i: (half, 0, n_chunks + chunk_of(i))),
                  pl.BlockSpec((None, fc, d), lambda i: (half, chunk_of(i), 0)),
                  _const_spec(ln_g.shape), _const_spec(ln_b.shape), *cast_specs],
        out_specs=[*_segment_specs(n_p, tm, d, norm_tile), *cast_specs],
        out_shape=[jax.ShapeDtypeStruct(xp.shape, F32), jax.ShapeDtypeStruct(xs.shape, F32)]
                  + [jax.ShapeDtypeStruct(w.shape, BF16) for w in cast_weights],
        scratch_shapes=[pltpu.VMEM((tm, d), F32), pltpu.VMEM((tm, d), F32),
                        pltpu.VMEM((n_chunks, d, 2 * fc), BF16),
                        pltpu.VMEM((n_chunks, fc, d), BF16)],
        compiler_params=_cparams(1),
        name="ffn_ln",
    )(xp, xs, wi, wi, wo, ln_g, ln_b, *cast_weights)


def _rope_tables(pos, dim, theta_fn):
    half = dim // 2
    inv_freq = theta_fn(half)
    ang = pos.astype(F32)[:, None] * inv_freq[None, :]
    reps = LANES // half
    cos = jnp.tile(jnp.cos(ang), (1, reps))
    sign = jnp.tile(jnp.concatenate([-jnp.ones((half,), F32), jnp.ones((half,), F32)]), LANES // dim)
    sin = jnp.tile(jnp.sin(ang), (1, reps)) * sign[None, :]
    return cos, sin


def _attn_inv_freq(half):
    return 1.0 / (ROPE_THETA ** (jnp.arange(0, 2 * half, 2, dtype=F32) / (2 * half)))


def _ret_inv_freq(half):
    return 1.0 / (RET_THETA ** jnp.linspace(0.0, 1.0, half, dtype=F32))


def _rope_block(x, cos, sin, dim):
    half = dim // 2
    if dim == LANES:
        rot = pltpu.roll(x, half, 1)
    else:
        lane = lax.broadcasted_iota(jnp.int32, x.shape, 1)
        first_half = (lane % dim) < half
        rot = jnp.where(first_half, pltpu.roll(x, LANES - half, 1), pltpu.roll(x, half, 1))
    return x * cos + rot * sin


def _run(stages):
    for _ in stages:
        pass


def _proj_rope_stages(x_ref, w_ref, cosa_ref, sina_ref, cosr_ref, sinr_ref,
                      qa_ref, ka_ref, va_ref, qr_ref, kr_ref, vr_ref, gr_ref, ga_ref, gb_ref,
                      k_tail_ref=None, v_tail_ref=None):
    xb = x_ref[...].astype(BF16)
    cosa, sina = cosa_ref[...], sina_ref[...]
    cosr, sinr = cosr_ref[...], sinr_ref[...]
    off = 0

    y = _dot(xb, w_ref[:, off:off + Q_W])
    for j in range(Q_W // LANES):
        qa_ref[:, j * LANES:(j + 1) * LANES] = _rope_block(
            y[:, j * LANES:(j + 1) * LANES], cosa, sina, HEAD_DIM).astype(qa_ref.dtype)
    off += Q_W
    yield
    y = _dot(xb, w_ref[:, off:off + 2 * KV_W])
    k_rot = _rope_block(y[:, :KV_W], cosa, sina, HEAD_DIM)
    ka_ref[...] = k_rot
    va_ref[...] = y[:, KV_W:]
    if k_tail_ref is not None:
        tail = k_tail_ref.shape[1]
        k_tail_ref[...] = k_rot[-tail:, :].T
        v_tail_ref[...] = y[-tail:, KV_W:].T
    off += 2 * KV_W
    yield
    y = _dot(xb, w_ref[:, off:off + RQ_W])
    for j in range(RQ_W // LANES):
        qr_ref[:, j * LANES:(j + 1) * LANES] = _rope_block(
            y[:, j * LANES:(j + 1) * LANES], cosr, sinr, RET_KEY_DIM).astype(qr_ref.dtype)
    off += RQ_W
    yield
    y = _dot(xb, w_ref[:, off:off + RQ_W])
    for j in range(RQ_W // LANES):
        kr = _rope_block(y[:, j * LANES:(j + 1) * LANES], cosr, sinr, RET_KEY_DIM)
        kr_ref[:, j * LANES:(j + 1) * LANES] = (kr * (RET_KEY_DIM ** -0.5)).astype(kr_ref.dtype)
    off += RQ_W
    yield
    for ref in (vr_ref, gr_ref, ga_ref, gb_ref):
        width = ref.shape[1]
        for c in range(width // 512):
            ref[:, c * 512:(c + 1) * 512] = _dot(
                xb, w_ref[:, off + c * 512:off + (c + 1) * 512]).astype(ref.dtype)
            yield
        off += width


def _proj_rope_kernel(*refs):
    _run(_proj_rope_stages(*refs))


def _dense_tile(tokens):
    return min(DENSE_TILE, tokens // DENSE_MIN_STEPS)


N_PROJ_IN = 6


def _proj_with_rider_kernel(rider_stages, n_rider_in, n_proj_out, *refs):
    proj_in, rest = refs[:N_PROJ_IN], refs[N_PROJ_IN:]
    rider_in, rest = rest[:n_rider_in], rest[n_rider_in:]
    proj_out, rider_out = rest[:n_proj_out], rest[n_proj_out:]
    rider = rider_stages(*rider_in, *rider_out)
    proj = _proj_rope_stages(*proj_in, *proj_out)
    next(rider)
    for _ in range(RIDER_LEAD_GROUPS):
        next(proj)
    _run(rider)
    _run(proj)


def _proj_rope(x1, w_in, tables, mixer_dtype, rider=None, tail=None):
    t, d = x1.shape
    n_in = w_in.shape[1]
    d_model = (n_in - Q_W - 2 * KV_W - 2 * RQ_W - 2 * RV_W) // 2
    tm = _dense_tile(t) if rider is None else rider[0]
    row = lambda i: (i, 0)
    table_blocks = tables[0].shape[0] // tm
    tab = lambda i: (i % table_blocks, 0)
    widths = [(Q_W, mixer_dtype), (KV_W, F32), (KV_W, F32), (RQ_W, mixer_dtype), (RQ_W, mixer_dtype),
              (RV_W, mixer_dtype), (RV_W, mixer_dtype), (d_model, BF16), (d_model, BF16)]
    in_specs = [pl.BlockSpec((tm, d), row), _const_spec(w_in.shape)] + [pl.BlockSpec((tm, LANES), tab)] * 4
    out_specs = [pl.BlockSpec((tm, w), row) for w, _ in widths]
    out_shape = [jax.ShapeDtypeStruct((t, w), dt) for w, dt in widths]
    if tail is not None:
        seq_len, tail_rows = tail
        assert seq_len % tm == 0 and tail_rows <= tm
        tiles_per_seq = seq_len // tm
        out_specs += [pl.BlockSpec((KV_W, tail_rows), lambda i: (i // tiles_per_seq, 0))] * 2
        out_shape += [jax.ShapeDtypeStruct((t // seq_len * KV_W, tail_rows), F32)] * 2
    n_proj_out = len(out_specs)
    args = [x1, w_in, *tables]
    body = _proj_rope_kernel
    if rider is not None:
        r_stages, r_in_specs, r_out_specs, r_out_shape, r_args = rider[1](t // tm)
        body = functools.partial(_proj_with_rider_kernel, r_stages, len(r_args), n_proj_out)
        in_specs, out_specs = in_specs + r_in_specs, out_specs + r_out_specs
        out_shape, args = out_shape + r_out_shape, args + r_args
    outs = pl.pallas_call(
        body,
        grid=(t // tm,),
        in_specs=in_specs,
        out_specs=out_specs,
        out_shape=out_shape,
        compiler_params=_cparams(1),
        name="proj_rope",
    )(*args)
    return outs if rider is None else (outs[:n_proj_out], outs[n_proj_out:])


def _kv_lane_variants(x):
    lane = lax.broadcasted_iota(jnp.int32, x.shape, 1)
    lo = lane < HEAD_DIM
    swapped = pltpu.roll(x, HEAD_DIM, 1)
    zero = jnp.zeros_like(x)
    kv0 = (jnp.where(lo, x, zero).astype(BF16), jnp.where(lo, zero, swapped).astype(BF16))
    kv1 = (jnp.where(lo, swapped, zero).astype(BF16), jnp.where(lo, zero, x).astype(BF16))
    return kv0, kv1


def _attn_prompt_kernel(sinks_ref, q_ref, kc_ref, kp_ref, vc_ref, vp_ref, o_ref):
    n = pl.program_id(1)
    L = WINDOW
    k_full = jnp.concatenate([kp_ref[...], kc_ref[...]], axis=0)
    v_full = jnp.concatenate([vp_ref[...], vc_ref[...]], axis=0)
    v_var = _kv_lane_variants(v_full)
    k_t = k_full.T
    k_t_swapped = jnp.concatenate([k_t[HEAD_DIM:], k_t[:HEAD_DIM]], axis=0)
    row_lo = lax.broadcasted_iota(jnp.int32, k_t.shape, 0) < HEAD_DIM
    zero_t = jnp.zeros_like(k_t)
    k_var = ((jnp.where(row_lo, k_t, zero_t).astype(BF16), jnp.where(row_lo, zero_t, k_t_swapped).astype(BF16)),
             (jnp.where(row_lo, k_t_swapped, zero_t).astype(BF16), jnp.where(row_lo, zero_t, k_t).astype(BF16)))
    own = (lax.broadcasted_iota(jnp.int32, (L, L), 1)
           <= lax.broadcasted_iota(jnp.int32, (L, L), 0))
    scale = HEAD_DIM ** -0.5

    units = [(i, j, par) for i in range(q_ref.shape[0] // L) for j in range(Q_W // LANES) for par in range(2)]

    def scores(unit):
        i, j, par = unit
        q2 = q_ref[i * L:(i + 1) * L, j * LANES:(j + 1) * LANES] * scale
        return _dot(q2, k_var[(2 * j) // GQA_GROUP][par][:, i * L:(i + 2) * L])

    s_next = scores(units[0])
    out = None
    for idx, (i, j, par) in enumerate(units):
        s = s_next
        if idx + 1 < len(units):
            s_next = scores(units[idx + 1])
        s_prev = s[:, :L]
        if i == 0:
            s_prev = jnp.where(n > 0, s_prev, NEG_INF)
        sw = jnp.where(own, s[:, L:], s_prev)
        sink = sinks_ref[2 * j + par]
        m = jnp.maximum(jnp.max(sw, axis=-1, keepdims=True), sink)
        e = jnp.exp(sw - m)
        denom = jnp.sum(e, axis=-1, keepdims=True) + jnp.exp(sink - m)
        e_band = jnp.concatenate([jnp.where(own, 0.0, e), jnp.where(own, e, 0.0)], axis=1)
        r = _dot(e_band.astype(BF16), v_var[(2 * j) // GQA_GROUP][par][i * L:(i + 2) * L]) / denom
        if par == 0:
            out = r
        else:
            o_ref[i * L:(i + 1) * L, j * LANES:(j + 1) * LANES] = (out + r).astype(o_ref.dtype)


def _attn_prompt(qa, ka, va, sinks, batch, seq):
    tq = ATTN_TILE
    nq = seq // tq
    per = tq // WINDOW
    cur = lambda b, n: (b * nq + n, 0)
    prev = lambda b, n: ((b * nq + n) * per - jnp.minimum(n, 1), 0)
    return pl.pallas_call(
        _attn_prompt_kernel,
        grid=(batch, nq),
        in_specs=[pl.BlockSpec(memory_space=pltpu.SMEM),
                  pl.BlockSpec((tq, Q_W), cur),
                  pl.BlockSpec((tq, KV_W), cur), pl.BlockSpec((WINDOW, KV_W), prev),
                  pl.BlockSpec((tq, KV_W), cur), pl.BlockSpec((WINDOW, KV_W), prev)],
        out_specs=pl.BlockSpec((tq, Q_W), cur),
        out_shape=jax.ShapeDtypeStruct(qa.shape, BF16),
        compiler_params=_cparams(2),
        name="attn_prompt",
    )(sinks, qa, ka, ka, va, va)


def _attn_sample_stages(sink_rows_ref, q_ref, kn_ref, vn_ref, ck_ref, cv_ref,
                        o_ref, nk_ref, nv_ref):
    bt, t_new, _ = q_ref.shape
    assert t_new == SUBLANES
    w = ck_ref.shape[3]
    n_rows = N_Q_HEADS * t_new
    lane_lo = lax.broadcasted_iota(jnp.int32, (bt * t_new, LANES), 1) < HEAD_DIM
    t_c = lax.broadcasted_iota(jnp.int32, (n_rows, w), 0) % t_new
    mask_c = lax.broadcasted_iota(jnp.int32, (n_rows, w), 1) > t_c + (w - WINDOW)
    t_n = lax.broadcasted_iota(jnp.int32, (n_rows, t_new), 0) % t_new
    mask_n = lax.broadcasted_iota(jnp.int32, (n_rows, t_new), 1) <= t_n
    sink = sink_rows_ref[...]
    scale = HEAD_DIM ** -0.5
    kn, vn = kn_ref[...], vn_ref[...]
    kc_t = ck_ref[...].reshape(bt, KV_W, w)
    vc_t = cv_ref[...].reshape(bt, KV_W, w)
    q = q_ref[...].reshape(bt * t_new, Q_W)
    pieces = []
    for h in range(N_Q_HEADS):
        q2 = q[:, (h // 2) * LANES:(h // 2 + 1) * LANES]
        src_lo = h % 2 == 0
        dst_lo = h // GQA_GROUP == 0
        if src_lo != dst_lo:
            q2 = pltpu.roll(q2, HEAD_DIM, 1)
        pieces.append(jnp.where(lane_lo == dst_lo, q2, 0.0).reshape(bt, t_new, LANES))
    q_rows = (jnp.concatenate(pieces, axis=1) * scale).astype(BF16)
    s_c = jnp.einsum("bqd,bdk->bqk", q_rows, kc_t.astype(BF16), preferred_element_type=F32)
    s_n = jnp.einsum("bqd,bkd->bqk", q_rows, kn.astype(BF16), preferred_element_type=F32)
    yield
    s_c = jnp.where(mask_c, s_c, NEG_INF)
    s_n = jnp.where(mask_n, s_n, NEG_INF)
    m = jnp.maximum(jnp.maximum(jnp.max(s_c, axis=-1, keepdims=True), jnp.max(s_n, axis=-1, keepdims=True)), sink)
    e_c, e_n = jnp.exp(s_c - m), jnp.exp(s_n - m)
    denom = jnp.sum(e_c, axis=-1, keepdims=True) + jnp.sum(e_n, axis=-1, keepdims=True) + jnp.exp(sink - m)
    pv = (jnp.einsum("bqk,bdk->bqd", e_c.astype(BF16), vc_t.astype(BF16), preferred_element_type=F32)
          + jnp.einsum("bqk,bkd->bqd", e_n.astype(BF16), vn.astype(BF16), preferred_element_type=F32)) / denom
    for j in range(Q_W // LANES):
        out = None
        for par in range(2):
            h = 2 * j + par
            dst_lo = h // GQA_GROUP == 0
            piece = pv[:, h * t_new:(h + 1) * t_new, :].reshape(bt * t_new, LANES)
            piece = jnp.where(lane_lo == dst_lo, piece, 0.0)
            if dst_lo != (par == 0):
                piece = pltpu.roll(piece, HEAD_DIM, 1)
            out = piece if out is None else out + piece
        o_ref[:, :, j * LANES:(j + 1) * LANES] = out.reshape(bt, t_new, LANES).astype(o_ref.dtype)
    keep = lax.broadcasted_iota(jnp.int32, (bt * KV_W, w), 1) < w - t_new
    for old_t, new, out_ref in ((kc_t, kn, nk_ref), (vc_t, vn, nv_ref)):
        shifted = pltpu.roll(old_t.reshape(bt * KV_W, w), w - t_new, 1)
        placed = jnp.concatenate([jnp.zeros((bt, w - t_new, KV_W), F32), new], axis=1)
        placed_t = jnp.swapaxes(placed, 1, 2).reshape(bt * KV_W, w)
        out_ref[...] = jnp.where(keep, shifted, placed_t).reshape(out_ref.shape)


def _ret_log_gamma(h):
    return math.log(1.0 - 2.0 ** (-5.0 - h))


def _ret_decays(chunk, h):
    lg = _ret_log_gamma(h)
    i = lax.broadcasted_iota(jnp.int32, (chunk, chunk), 0)
    j = lax.broadcasted_iota(jnp.int32, (chunk, chunk), 1)
    rel_mask = jnp.where(i >= j, jnp.exp((j + 1).astype(F32) * -lg), 0.0)
    col = lax.broadcasted_iota(jnp.int32, (chunk, 1), 0).astype(F32)
    gn_eps = GN_EPS * jnp.exp((col + 1.0) * (-2.0 * lg))
    k_decay = jnp.exp((chunk - 1.0 - col) * lg)
    return rel_mask, gn_eps, k_decay, math.exp(chunk * lg)


def _ret_chunk_scores(q, k, v, decays):
    heads = range(N_RET_HEADS)
    batched = q.ndim == 3
    mm = (lambda a, b: jnp.einsum("bmk,bkn->bmn", a, b, preferred_element_type=F32)) if batched else _dot
    mm_nt = (lambda a, b: jnp.einsum("bmk,bnk->bmn", a, b, preferred_element_type=F32)) if batched else _dot_nt
    qh = [q[..., h * RET_KEY_DIM:(h + 1) * RET_KEY_DIM] for h in heads]
    kh = [k[..., h * RET_KEY_DIM:(h + 1) * RET_KEY_DIM] for h in heads]
    vh = [v[..., h * RET_VAL_DIM:(h + 1) * RET_VAL_DIM] for h in heads]
    qb = [x.astype(BF16) for x in qh]
    vb = [x.astype(BF16) for x in vh]
    inner = [mm_nt(qb[h], kh[h].astype(BF16)) * decays[h][0] for h in heads]
    return batched, mm, qh, kh, vh, qb, vb, inner


def _ret_chunk_finish(scores, g, states, decays):
    batched, mm, qh, kh, vh, qb, vb, inner = scores
    heads = range(N_RET_HEADS)
    if batched:
        lhs = [jnp.concatenate([qh[h], inner[h]], axis=-1).astype(BF16) for h in heads]
        rhs = [jnp.concatenate([states[h], vh[h]], axis=-2).astype(BF16) for h in heads]
    else:
        lhs = [jnp.concatenate([qb[h], inner[h].astype(BF16)], axis=-1) for h in heads]
        rhs = [jnp.concatenate([states[h].astype(BF16), vb[h]], axis=-2) for h in heads]
    p = [mm(lhs[h], rhs[h]) for h in heads]
    kd_t = [jnp.swapaxes(kh[h].astype(F32) * decays[h][2], -1, -2).astype(BF16) for h in heads]
    new_states = [states[h] * decays[h][3] + mm(kd_t[h], vb[h]) for h in heads]
    outs = []
    for h in heads:
        mu = jnp.mean(p[h], axis=-1, keepdims=True)
        pc = p[h] - mu
        var = jnp.mean(pc * pc, axis=-1, keepdims=True)
        normed = pc * lax.rsqrt(var + decays[h][1])
        if g is not None:
            normed = normed * jax.nn.silu(g[..., h * RET_VAL_DIM:(h + 1) * RET_VAL_DIM].astype(F32))
        outs.append(normed)
    return jnp.concatenate(outs, axis=-1), new_states


def _ret_chunk_all_heads(q, k, v, g, states, decays):
    return _ret_chunk_finish(_ret_chunk_scores(q, k, v, decays), g, states, decays)


def _ret_prompt_kernel(q_ref, k_ref, v_ref, o_ref, st_ref, state_ref):
    n = pl.program_id(1)
    C = RET_CHUNK

    @pl.when(n == 0)
    def _():
        state_ref[...] = jnp.zeros_like(state_ref)

    decays = [_ret_decays(C, h) for h in range(N_RET_HEADS)]
    states = [state_ref[h] for h in range(N_RET_HEADS)]
    for c in range(q_ref.shape[0] // C):
        rows = slice(c * C, (c + 1) * C)
        out, states = _ret_chunk_all_heads(q_ref[rows, :], k_ref[rows, :], v_ref[rows, :], None, states, decays)
        o_ref[rows, :] = out.astype(o_ref.dtype)
    for h in range(N_RET_HEADS):
        state_ref[h] = states[h]

    @pl.when(n == pl.num_programs(1) - 1)
    def _():
        st_ref[0] = state_ref[...]


def _ret_prompt(qr, kr, vr, batch, seq):
    ts = RET_TILE
    ns = seq // ts
    row = lambda b, n: (b * ns + n, 0)
    return pl.pallas_call(
        _ret_prompt_kernel,
        grid=(batch, ns),
        in_specs=[pl.BlockSpec((ts, RQ_W), row), pl.BlockSpec((ts, RQ_W), row), pl.BlockSpec((ts, RV_W), row)],
        out_specs=[pl.BlockSpec((ts, RV_W), row),
                   pl.BlockSpec((1, N_RET_HEADS, RET_KEY_DIM, RET_VAL_DIM), lambda b, n: (b, 0, 0, 0))],
        out_shape=[jax.ShapeDtypeStruct((batch * seq, RV_W), BF16),
                   jax.ShapeDtypeStruct((batch, N_RET_HEADS, RET_KEY_DIM, RET_VAL_DIM), F32)],
        scratch_shapes=[pltpu.VMEM((N_RET_HEADS, RET_KEY_DIM, RET_VAL_DIM), F32)],
        compiler_params=_cparams(2),
        name="ret_prompt",
    )(qr, kr, vr)


def _ret_sample_stages(q_ref, k_ref, v_ref, g_ref, st_ref, o_ref, nst_ref):
    t_new = q_ref.shape[1]
    decays = [_ret_decays(t_new, h) for h in range(N_RET_HEADS)]
    scores = _ret_chunk_scores(q_ref[...], k_ref[...], v_ref[...], decays)
    yield
    states = [st_ref[:, h] for h in range(N_RET_HEADS)]
    out, states = _ret_chunk_finish(scores, g_ref[...], states, decays)
    o_ref[...] = out.astype(o_ref.dtype)
    for h in range(N_RET_HEADS):
        nst_ref[:, h] = states[h]


def _sample_mixers_stages(sink_rows_ref, qa_ref, kn_ref, vn_ref, ck_ref, cv_ref,
                          qr_ref, kr_ref, vr_ref, gr_ref, st_ref,
                          oa_ref, nk_ref, nv_ref, or_ref, nst_ref):
    attn = _attn_sample_stages(sink_rows_ref, qa_ref, kn_ref, vn_ref, ck_ref, cv_ref, oa_ref, nk_ref, nv_ref)
    ret = _ret_sample_stages(qr_ref, kr_ref, vr_ref, gr_ref, st_ref, or_ref, nst_ref)
    next(attn)
    next(ret)
    yield
    _run(attn)
    _run(ret)


def _sample_mixers_parts(qa, ka, va, sinks, cache_k, cache_v, qr, kr, vr, gr, state):
    db, t_new, _ = qa.shape
    w = cache_k.shape[3]
    cache = (N_KV_HEADS, HEAD_DIM, w)
    sink_rows = jnp.repeat(sinks.astype(F32), t_new).reshape(N_Q_HEADS * t_new, 1)

    def parts(n_steps):
        assert db % n_steps == 0
        bt = db // n_steps
        blk = lambda *s: pl.BlockSpec((bt,) + s, lambda i: (i,) + (0,) * len(s))
        st = blk(N_RET_HEADS, RET_KEY_DIM, RET_VAL_DIM)
        in_specs = [_const_spec(sink_rows.shape), blk(t_new, Q_W), blk(t_new, KV_W), blk(t_new, KV_W),
                    blk(*cache), blk(*cache),
                    blk(t_new, RQ_W), blk(t_new, RQ_W), blk(t_new, RV_W), blk(t_new, RV_W), st]
        out_specs = [blk(t_new, Q_W), blk(*cache), blk(*cache), blk(t_new, RV_W), st]
        out_shape = [jax.ShapeDtypeStruct((db, t_new, Q_W), F32),
                     jax.ShapeDtypeStruct((db,) + cache, F32),
                     jax.ShapeDtypeStruct((db,) + cache, F32),
                     jax.ShapeDtypeStruct((db, t_new, RV_W), F32),
                     jax.ShapeDtypeStruct(state.shape, state.dtype)]
        args = [sink_rows, qa, ka, va, cache_k, cache_v, qr, kr, vr, gr, state]
        return _sample_mixers_stages, in_specs, out_specs, out_shape, args

    return parts


def _merge_ln_kernel(alpha, ln_row, x_ref, oa_ref, orr_ref, gr_ref, ga_ref, gb_ref, wba_ref, wbr_ref, wo_ref,
                     g_ref, b_ref, o_ref):
    n_blocks = max(1, x_ref.shape[0] // MERGE_ROWS)
    rb = x_ref.shape[0] // n_blocks
    blocks = [slice(r * rb, (r + 1) * rb) for r in range(n_blocks)]

    def ret_branch(rows):
        if gr_ref is None:
            return orr_ref[rows, :].astype(BF16)
        return (orr_ref[rows, :].astype(F32) * jax.nn.silu(gr_ref[rows, :].astype(F32))).astype(BF16)

    y, z = {}, {}
    for r in range(n_blocks + 2):
        if r < n_blocks:
            rows = blocks[r]
            y[r] = (_dot(oa_ref[rows, :].astype(BF16), wba_ref[...]), _dot(ret_branch(rows), wbr_ref[...]))
        if 0 <= r - 1 < n_blocks:
            rows = blocks[r - 1]
            ya, yr = y.pop(r - 1)
            merged = (jax.nn.sigmoid(ga_ref[rows, :].astype(F32)) * ya
                      + jax.nn.sigmoid(gb_ref[rows, :].astype(F32)) * yr).astype(BF16)
            z[r - 1] = _dot(merged, wo_ref[...])
        if 0 <= r - 2 < n_blocks:
            rows = blocks[r - 2]
            o_ref[rows, :] = _layer_norm(alpha * x_ref[rows, :] + z.pop(r - 2),
                                         g_ref[ln_row:ln_row + 1, :], b_ref[ln_row:ln_row + 1, :])


def _merge_ln_gated_kernel(alpha, ln_row, *refs):
    _merge_ln_kernel(alpha, ln_row, *refs)


def _merge_ln_ungated_kernel(alpha, ln_row, x_ref, oa_ref, orr_ref, *refs):
    _merge_ln_kernel(alpha, ln_row, x_ref, oa_ref, orr_ref, None, *refs)


def _merge_ln(x1, oa, orr, gr, ga, gb, wba, wbr, wo, ln_g, ln_b, ln_row, alpha):
    t, d = x1.shape
    tm = _dense_tile(t)
    row = lambda i: (i, 0)
    gate_specs = [] if gr is None else [pl.BlockSpec((tm, RV_W), row)]
    gate_args = [] if gr is None else [gr]
    body = _merge_ln_ungated_kernel if gr is None else _merge_ln_gated_kernel
    return pl.pallas_call(
        functools.partial(body, alpha, ln_row),
        grid=(t // tm,),
        in_specs=[pl.BlockSpec((tm, d), row), pl.BlockSpec((tm, Q_W), row), pl.BlockSpec((tm, RV_W), row),
                  *gate_specs, pl.BlockSpec((tm, d), row), pl.BlockSpec((tm, d), row),
                  _const_spec(wba.shape), _const_spec(wbr.shape), _const_spec(wo.shape),
                  _const_spec(ln_g.shape), _const_spec(ln_b.shape)],
        out_specs=pl.BlockSpec((tm, d), row),
        out_shape=jax.ShapeDtypeStruct((t, d), F32),
        compiler_params=_cparams(1),
        name="merge_ln",
    )(x1, oa, orr, *gate_args, ga, gb, wba, wbr, wo, ln_g, ln_b)


def _rope_tables_for(pos):
    tables = _rope_tables(pos, HEAD_DIM, _attn_inv_freq) + _rope_tables(pos, RET_KEY_DIM, _ret_inv_freq)
    period = pos.shape[0]
    if period < DENSE_TILE:
        tables = tuple(jnp.tile(tb, (DENSE_TILE // period, 1)) for tb in tables)
    return tables


def kernel(x_prompt, x_sample, cache_k, cache_v, state_ret, ln_g, ln_b, ffn_wi, ffn_wo, w_in,
           attn_sinks, w_br_attn, w_br_ret, w_o):
    B, S, D = x_prompt.shape
    DB, T, _ = x_sample.shape
    depth = ln_g.shape[0]
    alpha = (2.0 * depth) ** 0.25
    W = cache_k.shape[2]
    tables_p = _rope_tables_for(jnp.arange(S, dtype=jnp.int32))
    tables_s = _rope_tables_for(PAST_LEN + jnp.arange(T, dtype=jnp.int32))

    y_p = x_prompt.reshape(B * S, D)
    y_s = x_sample.reshape(DB * T, D)
    outs = [[] for _ in range(6)]
    for l in range(depth):
        wi, wo = ffn_wi[l], ffn_wo[l]
        sinks = attn_sinks[l].astype(F32)
        g_l, b_l = ln_g[l], ln_b[l]

        x1_p, x1_s, w_in_l, wba, wbr, w_o_l = _ffn_ln(
            y_p, y_s, wi, wo, 0, g_l, b_l, 0, alpha,
            cast_weights=(w_in[l], w_br_attn[l], w_br_ret[l], w_o[l]))
        qa_s, ka_s, va_s, qr_s, kr_s, vr_s, gr_s, ga_s, gb_s = _proj_rope(
            x1_s, w_in_l, tables_s, F32)

        r3 = lambda a: a.reshape(DB, T, a.shape[-1])
        to_lane_major = lambda c: jnp.transpose(c, (0, 2, 3, 1))
        sample_mixers = _sample_mixers_parts(
            r3(qa_s), r3(ka_s), r3(va_s), sinks, to_lane_major(cache_k[l]), to_lane_major(cache_v[l]),
            r3(qr_s), r3(kr_s), r3(vr_s), r3(gr_s), state_ret[l])
        w = min(WINDOW, S)
        (qa_p, ka_p, va_p, qr_p, kr_p, vr_p, gr_p, ga_p, gb_p, nk_p, nv_p), (oa_s, nk_s, nv_s, or_s, st_s) = \
            _proj_rope(x1_p, w_in_l, tables_p, BF16, rider=(RIDER_PROJ_TILE, sample_mixers), tail=(S, w))
        oa_s, or_s = oa_s.reshape(DB * T, Q_W), or_s.reshape(DB * T, RV_W)
        from_lane_major = lambda c: jnp.transpose(c, (0, 3, 1, 2))
        new_caches = [from_lane_major(nk_p.reshape(B, N_KV_HEADS, HEAD_DIM, w)),
                      from_lane_major(nv_p.reshape(B, N_KV_HEADS, HEAD_DIM, w)),
                      from_lane_major(nk_s), from_lane_major(nv_s)]

        oa_p = _attn_prompt(qa_p, ka_p, va_p, sinks, B, S)
        or_p, st_p = _ret_prompt(qr_p, kr_p, vr_p, B, S)

        x2_p = _merge_ln(x1_p, oa_p, or_p, gr_p, ga_p, gb_p, wba, wbr, w_o_l, g_l, b_l, 1, alpha)
        x2_s = _merge_ln(x1_s, oa_s, or_s, None, ga_s, gb_s, wba, wbr, w_o_l, g_l, b_l, 1, alpha)
        y_p, y_s = _ffn_ln(x2_p, x2_s, wi, wo, 1, g_l, b_l, 2, alpha)
        for lst, val in zip(outs, (new_caches[0], new_caches[1], st_p, new_caches[2], new_caches[3], st_s)):
            lst.append(val)
    return (y_p.reshape(B, S, D), y_s.reshape(DB, T, D)) + tuple(jnp.stack(o) for o in outs)
```
